```python
import math
import jax, jax.numpy as jnp
from jax import lax
import numpy as np

D_MODEL = 1024
BATCH = 8
SEQ = 16384
DEPTH = 4

N_EVEN = (DEPTH + 1) // 2
N_ODD = DEPTH // 2
EPS = 1e-6
ROPE_THETA = 10000.0
BLOCK = 128

A_HEADS = 8
A_KV_HEADS = 2
A_GROUP = A_HEADS // A_KV_HEADS
A_HEAD_DIM = D_MODEL // 16
A_WINDOW = 128
A_WIDTH = A_HEADS * A_HEAD_DIM
A_KV_WIDTH = A_KV_HEADS * A_HEAD_DIM
B_WIDTH = D_MODEL // 2
POOL_WINDOWS = (2, 4, 8, 16)
B_GROUPS = len(POOL_WINDOWS)
B_GROUP_DIM = B_WIDTH // B_GROUPS
EVEN_IN = A_WIDTH + 2 * A_KV_WIDTH + B_WIDTH
C_HEADS = 8
C_NOPE = 64
C_ROPE = 32
C_VDIM = 64
C_Q_RANK = D_MODEL // 4
C_KV_RANK = D_MODEL // 8
C_WIDTH = C_HEADS * C_VDIM
D_WIDTH = D_MODEL // 2
D_BLOCKS = 8
D_BLOCK_DIM = D_WIDTH // D_BLOCKS
CONV_WIDTH = 4
LRU_C = 8.0
ODD_IN = C_Q_RANK + C_KV_RANK + C_ROPE + 2 * D_WIDTH
D_FF = 4 * D_MODEL

kernel_name = "hybrid_bidir_swa_pool_mla_rglru"

F32 = jnp.float32


def rmsnorm(x, g):
    xf = x.astype(F32)
    y = xf * lax.rsqrt(jnp.mean(xf * xf, axis=-1, keepdims=True) + EPS)
    return (y * g.astype(F32)).astype(x.dtype)


def rope(x):
    S, d = x.shape[1], x.shape[-1]
    half = d // 2
    inv = ROPE_THETA ** (-jnp.arange(half, dtype=F32) / half)
    ang = jnp.arange(S, dtype=F32)[:, None] * inv[None, :]
    shape = (1, S) + (1,) * (x.ndim - 3) + (half,)
    cos = jnp.cos(ang).reshape(shape)
    sin = jnp.sin(ang).reshape(shape)
    xf = x.astype(F32)
    x1, x2 = xf[..., :half], xf[..., half:]
    return jnp.concatenate([x1 * cos - x2 * sin, x2 * cos + x1 * sin], axis=-1).astype(x.dtype)


def windowed_gqa(q, k, v, sink):
    Bsz, S = q.shape[0], q.shape[1]
    nb = S // BLOCK
    qb = q.reshape(Bsz, nb, BLOCK, A_KV_HEADS, A_GROUP, A_HEAD_DIM)

    def band(t):
        tp = jnp.pad(t, ((0, 0), (BLOCK, BLOCK), (0, 0), (0, 0)))
        tp = tp.reshape(Bsz, nb + 2, BLOCK, A_KV_HEADS, A_HEAD_DIM)
        return jnp.concatenate([tp[:, :-2], tp[:, 1:-1], tp[:, 2:]], axis=2)

    kb, vb = band(k), band(v)
    s = jnp.einsum('bnqhgd,bnjhd->bnhgqj', qb, kb).astype(F32) * (A_HEAD_DIM ** -0.5)
    blk = jnp.arange(nb)[:, None, None]
    qpos = blk * BLOCK + jnp.arange(BLOCK)[None, :, None]
    kpos = (blk - 1) * BLOCK + jnp.arange(3 * BLOCK)[None, None, :]
    valid = (jnp.abs(kpos - qpos) <= A_WINDOW) & (kpos >= 0) & (kpos < S)
    s = jnp.where(valid[None, :, None, None], s, -jnp.inf)
    sk = sink.astype(F32).reshape(1, 1, A_KV_HEADS, A_GROUP, 1, 1)
    m = jnp.maximum(jnp.max(s, axis=-1, keepdims=True), sk)
    p = jnp.exp(s - m)
    p = p / (jnp.sum(p, axis=-1, keepdims=True) + jnp.exp(sk - m))
    out = jnp.einsum('bnhgqj,bnjhd->bnqhgd', p.astype(v.dtype), vb)
    return out.reshape(Bsz, S, A_WIDTH)


def multiscale_pool(u, w_pool, pool_scale):
    Bsz, S = u.shape[0], u.shape[1]
    uf = u.astype(F32).reshape(Bsz, S, B_GROUPS, B_GROUP_DIM)
    cs = jnp.pad(jnp.cumsum(uf, axis=1), ((0, 0), (1, 0), (0, 0), (0, 0)))
    t = jnp.arange(S)
    outs = []
    for g, w in enumerate(POOL_WINDOWS):
        half = w // 2
        lo = jnp.clip(t - half, 0, S)
        hi = jnp.clip(t + half, 0, S)
        win_sum = cs[:, hi, g] - cs[:, lo, g]
        cnt = (hi - lo).astype(F32)[None, :, None]
        outs.append(win_sum / cnt - uf[:, :, g])
    d = jnp.stack(outs, axis=2)
    y = jnp.einsum('bsgi,gij->bsgj', d, w_pool.astype(F32)).reshape(Bsz, S, B_WIDTH)
    return (y * pool_scale.astype(F32)).astype(u.dtype)


def dense_mla(qn, qr, kn, kr, v):
    Bsz, S = qn.shape[0], qn.shape[1]
    nb = S // BLOCK
    scale = (C_NOPE + C_ROPE) ** -0.5

    def blocks(t):
        return jnp.moveaxis(t.reshape((Bsz, nb, BLOCK) + t.shape[2:]), 1, 0)

    def one(args):
        qn_b, qr_b = args
        s = (jnp.einsum('bqhd,bkhd->bhqk', qn_b, kn) + jnp.einsum('bqhr,bkr->bhqk', qr_b, kr)).astype(F32) * scale
        p = jax.nn.softmax(s, axis=-1).astype(v.dtype)
        return jnp.einsum('bhqk,bkhd->bqhd', p, v)

    out = lax.map(one, (blocks(qn), blocks(qr)))
    return jnp.moveaxis(out, 0, 1).reshape(Bsz, S, C_WIDTH)


def _lin_combine(left, right):
    a1, b1 = left
    a2, b2 = right
    return a1 * a2, a2 * b1 + b2


def rglru_block(xr, xg, conv_w, conv_b, wa, ba, wx, bx, lam):
    Bsz, S = xr.shape[0], xr.shape[1]
    left = CONV_WIDTH // 2
    xp = jnp.pad(xr, ((0, 0), (left, CONV_WIDTH - 1 - left), (0, 0)))
    xc = conv_b + conv_w[0] * xp[:, 0:S]
    for j in range(1, CONV_WIDTH):
        xc = xc + conv_w[j] * xp[:, j:j + S]
    xcf = xc.astype(F32)
    xblk = xcf.reshape(Bsz, S, D_BLOCKS, D_BLOCK_DIM)
    hs = []
    for dirn in range(2):
        r = jax.nn.sigmoid(jnp.einsum('bsni,nij->bsnj', xblk, wa[dirn].astype(F32)).reshape(Bsz, S, D_WIDTH) + ba[dirn].astype(F32))
        i = jax.nn.sigmoid(jnp.einsum('bsni,nij->bsnj', xblk, wx[dirn].astype(F32)).reshape(Bsz, S, D_WIDTH) + bx[dirn].astype(F32))
        log_a = -LRU_C * r * jax.nn.softplus(-lam[dirn].astype(F32))
        a = jnp.exp(log_a)
        b = jnp.sqrt(-jnp.expm1(2.0 * log_a)) * (i * xcf)
        _, h = lax.associative_scan(_lin_combine, (a, b), axis=1, reverse=(dirn == 1))
        hs.append(h)
    y = (hs[0] + hs[1]) * jax.nn.gelu(xg.astype(F32))
    return y.astype(xr.dtype)


def even_mixer(h, w_in, sink, w_pool, pool_scale, w_out):
    Bsz, S = h.shape[0], h.shape[1]
    z = h @ w_in
    q, k, v, u = jnp.split(z, [A_WIDTH, A_WIDTH + A_KV_WIDTH, A_WIDTH + 2 * A_KV_WIDTH], axis=-1)
    q = rope(q.reshape(Bsz, S, A_HEADS, A_HEAD_DIM))
    k = rope(k.reshape(Bsz, S, A_KV_HEADS, A_HEAD_DIM))
    v = v.reshape(Bsz, S, A_KV_HEADS, A_HEAD_DIM)
    ya = windowed_gqa(q, k, v, sink)
    yb = multiscale_pool(u, w_pool, pool_scale)
    return jnp.concatenate([ya, yb], axis=-1) @ w_out


def odd_mixer(h, w_in, g_cq, w_uq, g_ckv, w_ukv, conv_w, conv_b, wa, ba, wx, bx, lam, w_out):
    Bsz, S = h.shape[0], h.shape[1]
    z = h @ w_in
    i1 = C_Q_RANK
    i2 = i1 + C_KV_RANK
    i3 = i2 + C_ROPE
    i4 = i3 + D_WIDTH
    cq, ckv, kr, xr, xg = jnp.split(z, [i1, i2, i3, i4], axis=-1)
    q = (rmsnorm(cq, g_cq) @ w_uq).reshape(Bsz, S, C_HEADS, C_NOPE + C_ROPE)
    qn, qr = q[..., :C_NOPE], rope(q[..., C_NOPE:])
    kv = (rmsnorm(ckv, g_ckv) @ w_ukv).reshape(Bsz, S, C_HEADS, C_NOPE + C_VDIM)
    kn, v = kv[..., :C_NOPE], kv[..., C_NOPE:]
    yc = dense_mla(qn, qr, kn, rope(kr), v)
    yd = rglru_block(xr, xg, conv_w, conv_b, wa, ba, wx, bx, lam)
    return jnp.concatenate([yc, yd], axis=-1) @ w_out


def sq_relu_mlp(h, w1, w2):
    u = jax.nn.relu(h @ w1)
    return (u * u) @ w2


def _fwd_setup_inputs(seed: int = 0) -> dict:
    key = jax.random.key(seed)
    ks = iter(jax.random.split(key, 40))

    def nrm(shape, fan_in):
        return jax.random.normal(next(ks), shape, F32) * (fan_in ** -0.5)

    def gain(shape):
        return 1.0 + 0.02 * jax.random.normal(next(ks), shape, F32)

    def bias(shape):
        return 0.01 * jax.random.normal(next(ks), shape, F32)

    x = jax.random.normal(next(ks), (BATCH, SEQ, D_MODEL), F32)
    u = jax.random.uniform(next(ks), (N_ODD, 2, D_WIDTH), F32, 0.9, 0.999)
    a0 = u ** (1.0 / LRU_C)
    lam = jnp.log(a0) - jnp.log1p(-a0)
    return {
        "x": x,
        "e_norm_mix": gain((N_EVEN, D_MODEL)),
        "e_w_in": nrm((N_EVEN, D_MODEL, EVEN_IN), D_MODEL),
        "e_sink": 0.5 * jax.random.normal(next(ks), (N_EVEN, A_HEADS), F32),
        "e_w_pool": nrm((N_EVEN, B_GROUPS, B_GROUP_DIM, B_GROUP_DIM), B_GROUP_DIM),
        "e_pool_scale": gain((N_EVEN, B_WIDTH)),
        "e_w_out": nrm((N_EVEN, D_MODEL, D_MODEL), D_MODEL),
        "o_norm_mix": gain((N_ODD, D_MODEL)),
        "o_w_in": nrm((N_ODD, D_MODEL, ODD_IN), D_MODEL),
        "o_g_cq": gain((N_ODD, C_Q_RANK)),
        "o_w_uq": nrm((N_ODD, C_Q_RANK, C_HEADS * (C_NOPE + C_ROPE)), C_Q_RANK),
        "o_g_ckv": gain((N_ODD, C_KV_RANK)),
        "o_w_ukv": nrm((N_ODD, C_KV_RANK, C_HEADS * (C_NOPE + C_VDIM)), C_KV_RANK),
        "o_conv_w": nrm((N_ODD, CONV_WIDTH, D_WIDTH), CONV_WIDTH),
        "o_conv_b": bias((N_ODD, D_WIDTH)),
        "o_lru_wa": nrm((N_ODD, 2, D_BLOCKS, D_BLOCK_DIM, D_BLOCK_DIM), D_BLOCK_DIM),
        "o_lru_ba": bias((N_ODD, 2, D_WIDTH)),
        "o_lru_wx": nrm((N_ODD, 2, D_BLOCKS, D_BLOCK_DIM, D_BLOCK_DIM), D_BLOCK_DIM),
        "o_lru_bx": bias((N_ODD, 2, D_WIDTH)),
        "o_lru_lambda": lam,
        "o_w_out": nrm((N_ODD, D_MODEL, D_MODEL), D_MODEL),
        "norm_mlp": gain((DEPTH, D_MODEL)),
        "w_mlp1": nrm((DEPTH, D_MODEL, D_FF), D_MODEL),
        "w_mlp2": nrm((DEPTH, D_FF, D_MODEL), D_FF),
        "final_norm": gain((D_MODEL,)),
    }


def _fwd_reference(x, e_norm_mix, e_w_in, e_sink, e_w_pool, e_pool_scale, e_w_out,
              o_norm_mix, o_w_in, o_g_cq, o_w_uq, o_g_ckv, o_w_ukv, o_conv_w, o_conv_b,
              o_lru_wa, o_lru_ba, o_lru_wx, o_lru_bx, o_lru_lambda, o_w_out,
              norm_mlp, w_mlp1, w_mlp2, final_norm):
    for layer in range(DEPTH):
        if layer % 2 == 0:
            e = layer // 2
            h = rmsnorm(x, e_norm_mix[e])
            x = x + even_mixer(h, e_w_in[e], e_sink[e], e_w_pool[e], e_pool_scale[e], e_w_out[e])
        else:
            o = layer // 2
            h = rmsnorm(x, o_norm_mix[o])
            x = x + odd_mixer(h, o_w_in[o], o_g_cq[o], o_w_uq[o], o_g_ckv[o], o_w_ukv[o],
                              o_conv_w[o], o_conv_b[o], o_lru_wa[o], o_lru_ba[o], o_lru_wx[o],
                              o_lru_bx[o], o_lru_lambda[o], o_w_out[o])
        x = x + sq_relu_mlp(rmsnorm(x, norm_mlp[layer]), w_mlp1[layer], w_mlp2[layer])
    return rmsnorm(x, final_norm)


import jax as _jax
import jax.numpy as _jnp

TWIN_FORMAT = 'train_step'
FWD_PARAMS = ['x', 'e_norm_mix', 'e_w_in', 'e_sink', 'e_w_pool', 'e_pool_scale', 'e_w_out', 'o_norm_mix', 'o_w_in', 'o_g_cq', 'o_w_uq', 'o_g_ckv', 'o_w_ukv', 'o_conv_w', 'o_conv_b', 'o_lru_wa', 'o_lru_ba', 'o_lru_wx', 'o_lru_bx', 'o_lru_lambda', 'o_w_out', 'norm_mlp', 'w_mlp1', 'w_mlp2', 'final_norm']
TWIN_WEIGHTS = ['e_norm_mix', 'e_w_in', 'e_sink', 'e_w_pool', 'e_pool_scale', 'e_w_out', 'o_norm_mix', 'o_w_in', 'o_g_cq', 'o_w_uq', 'o_g_ckv', 'o_w_ukv', 'o_conv_w', 'o_conv_b', 'o_lru_wa', 'o_lru_ba', 'o_lru_wx', 'o_lru_bx', 'o_lru_lambda', 'o_w_out', 'norm_mlp', 'w_mlp1', 'w_mlp2', 'final_norm']
TWIN_DIFF_INPUT = 'x'
TWIN_INPUTS = ['x', 'e_norm_mix', 'e_w_in', 'e_sink', 'e_w_pool', 'e_pool_scale', 'e_w_out', 'o_norm_mix', 'o_w_in', 'o_g_cq', 'o_w_uq', 'o_g_ckv', 'o_w_ukv', 'o_conv_w', 'o_conv_b', 'o_lru_wa', 'o_lru_ba', 'o_lru_wx', 'o_lru_bx', 'o_lru_lambda', 'o_w_out', 'norm_mlp', 'w_mlp1', 'w_mlp2', 'final_norm', 'loss_target', 'm_e_norm_mix', 'm_e_w_in', 'm_e_sink', 'm_e_w_pool', 'm_e_pool_scale', 'm_e_w_out', 'm_o_norm_mix', 'm_o_w_in', 'm_o_g_cq', 'm_o_w_uq', 'm_o_g_ckv', 'm_o_w_ukv', 'm_o_conv_w', 'm_o_conv_b', 'm_o_lru_wa', 'm_o_lru_ba', 'm_o_lru_wx', 'm_o_lru_bx', 'm_o_lru_lambda', 'm_o_w_out', 'm_norm_mlp', 'm_w_mlp1', 'm_w_mlp2', 'm_final_norm', 'v_e_norm_mix', 'v_e_w_in', 'v_e_sink', 'v_e_w_pool', 'v_e_pool_scale', 'v_e_w_out', 'v_o_norm_mix', 'v_o_w_in', 'v_o_g_cq', 'v_o_w_uq', 'v_o_g_ckv', 'v_o_w_ukv', 'v_o_conv_w', 'v_o_conv_b', 'v_o_lru_wa', 'v_o_lru_ba', 'v_o_lru_wx', 'v_o_lru_bx', 'v_o_lru_lambda', 'v_o_w_out', 'v_norm_mlp', 'v_w_mlp1', 'v_w_mlp2', 'v_final_norm']
TWIN_OUTPUTS = ['loss', 'grad_x', 'grad_e_norm_mix', 'grad_e_w_in', 'grad_e_sink', 'grad_e_w_pool', 'grad_e_pool_scale', 'grad_e_w_out', 'grad_o_norm_mix', 'grad_o_w_in', 'grad_o_g_cq', 'grad_o_w_uq', 'grad_o_g_ckv', 'grad_o_w_ukv', 'grad_o_conv_w', 'grad_o_conv_b', 'grad_o_lru_wa', 'grad_o_lru_ba', 'grad_o_lru_wx', 'grad_o_lru_bx', 'grad_o_lru_lambda', 'grad_o_w_out', 'grad_norm_mlp', 'grad_w_mlp1', 'grad_w_mlp2', 'grad_final_norm', 'delta_e_norm_mix', 'delta_e_w_in', 'delta_e_sink', 'delta_e_w_pool', 'delta_e_pool_scale', 'delta_e_w_out', 'delta_o_norm_mix', 'delta_o_w_in', 'delta_o_g_cq', 'delta_o_w_uq', 'delta_o_g_ckv', 'delta_o_w_ukv', 'delta_o_conv_w', 'delta_o_conv_b', 'delta_o_lru_wa', 'delta_o_lru_ba', 'delta_o_lru_wx', 'delta_o_lru_bx', 'delta_o_lru_lambda', 'delta_o_w_out', 'delta_norm_mlp', 'delta_w_mlp1', 'delta_w_mlp2', 'delta_final_norm', 'new_m_e_norm_mix', 'new_m_e_w_in', 'new_m_e_sink', 'new_m_e_w_pool', 'new_m_e_pool_scale', 'new_m_e_w_out', 'new_m_o_norm_mix', 'new_m_o_w_in', 'new_m_o_g_cq', 'new_m_o_w_uq', 'new_m_o_g_ckv', 'new_m_o_w_ukv', 'new_m_o_conv_w', 'new_m_o_conv_b', 'new_m_o_lru_wa', 'new_m_o_lru_ba', 'new_m_o_lru_wx', 'new_m_o_lru_bx', 'new_m_o_lru_lambda', 'new_m_o_w_out', 'new_m_norm_mlp', 'new_m_w_mlp1', 'new_m_w_mlp2', 'new_m_final_norm', 'new_v_e_norm_mix', 'new_v_e_w_in', 'new_v_e_sink', 'new_v_e_w_pool', 'new_v_e_pool_scale', 'new_v_e_w_out', 'new_v_o_norm_mix', 'new_v_o_w_in', 'new_v_o_g_cq', 'new_v_o_w_uq', 'new_v_o_g_ckv', 'new_v_o_w_ukv', 'new_v_o_conv_w', 'new_v_o_conv_b', 'new_v_o_lru_wa', 'new_v_o_lru_ba', 'new_v_o_lru_wx', 'new_v_o_lru_bx', 'new_v_o_lru_lambda', 'new_v_o_w_out', 'new_v_norm_mlp', 'new_v_w_mlp1', 'new_v_w_mlp2', 'new_v_final_norm']
TWIN_LEAF_KINDS = {'loss': 'loss', 'grad_x': 'grad_x', 'grad_e_norm_mix': 'grad_w', 'grad_e_w_in': 'grad_w', 'grad_e_sink': 'grad_w', 'grad_e_w_pool': 'grad_w', 'grad_e_pool_scale': 'grad_w', 'grad_e_w_out': 'grad_w', 'grad_o_norm_mix': 'grad_w', 'grad_o_w_in': 'grad_w', 'grad_o_g_cq': 'grad_w', 'grad_o_w_uq': 'grad_w', 'grad_o_g_ckv': 'grad_w', 'grad_o_w_ukv': 'grad_w', 'grad_o_conv_w': 'grad_w', 'grad_o_conv_b': 'grad_w', 'grad_o_lru_wa': 'grad_w', 'grad_o_lru_ba': 'grad_w', 'grad_o_lru_wx': 'grad_w', 'grad_o_lru_bx': 'grad_w', 'grad_o_lru_lambda': 'grad_w', 'grad_o_w_out': 'grad_w', 'grad_norm_mlp': 'grad_w', 'grad_w_mlp1': 'grad_w', 'grad_w_mlp2': 'grad_w', 'grad_final_norm': 'grad_w', 'delta_e_norm_mix': 'delta_w', 'delta_e_w_in': 'delta_w', 'delta_e_sink': 'delta_w', 'delta_e_w_pool': 'delta_w', 'delta_e_pool_scale': 'delta_w', 'delta_e_w_out': 'delta_w', 'delta_o_norm_mix': 'delta_w', 'delta_o_w_in': 'delta_w', 'delta_o_g_cq': 'delta_w', 'delta_o_w_uq': 'delta_w', 'delta_o_g_ckv': 'delta_w', 'delta_o_w_ukv': 'delta_w', 'delta_o_conv_w': 'delta_w', 'delta_o_conv_b': 'delta_w', 'delta_o_lru_wa': 'delta_w', 'delta_o_lru_ba': 'delta_w', 'delta_o_lru_wx': 'delta_w', 'delta_o_lru_bx': 'delta_w', 'delta_o_lru_lambda': 'delta_w', 'delta_o_w_out': 'delta_w', 'delta_norm_mlp': 'delta_w', 'delta_w_mlp1': 'delta_w', 'delta_w_mlp2': 'delta_w', 'delta_final_norm': 'delta_w', 'new_m_e_norm_mix': 'new_m', 'new_m_e_w_in': 'new_m', 'new_m_e_sink': 'new_m', 'new_m_e_w_pool': 'new_m', 'new_m_e_pool_scale': 'new_m', 'new_m_e_w_out': 'new_m', 'new_m_o_norm_mix': 'new_m', 'new_m_o_w_in': 'new_m', 'new_m_o_g_cq': 'new_m', 'new_m_o_w_uq': 'new_m', 'new_m_o_g_ckv': 'new_m', 'new_m_o_w_ukv': 'new_m', 'new_m_o_conv_w': 'new_m', 'new_m_o_conv_b': 'new_m', 'new_m_o_lru_wa': 'new_m', 'new_m_o_lru_ba': 'new_m', 'new_m_o_lru_wx': 'new_m', 'new_m_o_lru_bx': 'new_m', 'new_m_o_lru_lambda': 'new_m', 'new_m_o_w_out': 'new_m', 'new_m_norm_mlp': 'new_m', 'new_m_w_mlp1': 'new_m', 'new_m_w_mlp2': 'new_m', 'new_m_final_norm': 'new_m', 'new_v_e_norm_mix': 'new_v', 'new_v_e_w_in': 'new_v', 'new_v_e_sink': 'new_v', 'new_v_e_w_pool': 'new_v', 'new_v_e_pool_scale': 'new_v', 'new_v_e_w_out': 'new_v', 'new_v_o_norm_mix': 'new_v', 'new_v_o_w_in': 'new_v', 'new_v_o_g_cq': 'new_v', 'new_v_o_w_uq': 'new_v', 'new_v_o_g_ckv': 'new_v', 'new_v_o_w_ukv': 'new_v', 'new_v_o_conv_w': 'new_v', 'new_v_o_conv_b': 'new_v', 'new_v_o_lru_wa': 'new_v', 'new_v_o_lru_ba': 'new_v', 'new_v_o_lru_wx': 'new_v', 'new_v_o_lru_bx': 'new_v', 'new_v_o_lru_lambda': 'new_v', 'new_v_o_w_out': 'new_v', 'new_v_norm_mlp': 'new_v', 'new_v_w_mlp1': 'new_v', 'new_v_w_mlp2': 'new_v', 'new_v_final_norm': 'new_v'}


def _forward(args):
    return _fwd_reference(*[args[k] for k in FWD_PARAMS])


def _output_shape():
    def fwd():
        inp = _fwd_setup_inputs(0)
        return _fwd_reference(*[inp[k] for k in FWD_PARAMS])
    out = _jax.eval_shape(fwd)
    return out.shape, out.dtype

N_MICROBATCH = 1
ADAM_LR = 0.001
ADAM_B1 = 0.9
ADAM_B2 = 0.999
ADAM_EPS = 1e-08
ADAM_WD = 0.01
ADAM_STEP = 10
PER_EXAMPLE_BATCH_AXIS = {'x': 0, 'loss_target': 0}
SHARED_INPUTS = []
_WEIGHT_DTYPES = {'e_norm_mix': _jnp.float32, 'e_w_in': _jnp.float32, 'e_sink': _jnp.float32, 'e_w_pool': _jnp.float32, 'e_pool_scale': _jnp.float32, 'e_w_out': _jnp.float32, 'o_norm_mix': _jnp.float32, 'o_w_in': _jnp.float32, 'o_g_cq': _jnp.float32, 'o_w_uq': _jnp.float32, 'o_g_ckv': _jnp.float32, 'o_w_ukv': _jnp.float32, 'o_conv_w': _jnp.float32, 'o_conv_b': _jnp.float32, 'o_lru_wa': _jnp.float32, 'o_lru_ba': _jnp.float32, 'o_lru_wx': _jnp.float32, 'o_lru_bx': _jnp.float32, 'o_lru_lambda': _jnp.float32, 'o_w_out': _jnp.float32, 'norm_mlp': _jnp.float32, 'w_mlp1': _jnp.float32, 'w_mlp2': _jnp.float32, 'final_norm': _jnp.float32}
MOMENT_SCALE = {'e_norm_mix': 2.263224e-01, 'e_w_in': 1.805064e-01, 'e_sink': 2.461563e-03, 'e_w_pool': 2.495448e-01, 'e_pool_scale': 2.543648e-01, 'e_w_out': 2.072605e-01, 'o_norm_mix': 8.450868e-01, 'o_w_in': 7.597833e-01, 'o_g_cq': 2.636712e-02, 'o_w_uq': 1.522424e-02, 'o_g_ckv': 3.301035e-01, 'o_w_ukv': 1.172372e-01, 'o_conv_w': 9.066219e-01, 'o_conv_b': 3.177204e+00, 'o_lru_wa': 4.788217e-02, 'o_lru_ba': 1.106583e-01, 'o_lru_wx': 1.114844e-01, 'o_lru_bx': 2.063000e-01, 'o_lru_lambda': 2.763924e-01, 'o_w_out': 7.151054e-01, 'norm_mlp': 2.512452e-01, 'w_mlp1': 1.297890e-01, 'w_mlp2': 5.196420e-01, 'final_norm': 1.330674e+02}


def _to_microbatches(a, axis):
    t = _jnp.moveaxis(a, axis, 0)
    t = t.reshape((N_MICROBATCH, t.shape[0] // N_MICROBATCH) + t.shape[1:])
    return _jnp.moveaxis(t, 1, axis + 1)


def setup_inputs(seed: int = 0) -> dict:
    inp = _fwd_setup_inputs(seed)
    key = _jax.random.fold_in(_jax.random.key(seed), 7919)
    shape, _ = _output_shape()
    out = dict(inp)
    out["loss_target"] = _jax.random.normal(_jax.random.fold_in(key, 0), shape, _jnp.float32)
    for i, name in enumerate(TWIN_WEIGHTS):
        w = inp[name].astype(_jnp.float32)
        if MOMENT_SCALE is None:
            s = _jnp.sqrt(_jnp.mean(_jnp.square(w)) + 1e-30)
        else:
            s = MOMENT_SCALE[name]
        km, kv = _jax.random.split(_jax.random.fold_in(key, i + 1))
        out[name] = w
        out["m_" + name] = s * _jax.random.normal(km, w.shape, _jnp.float32)
        out["v_" + name] = (s * s) * _jax.random.uniform(kv, w.shape, _jnp.float32, 0.5, 1.5)
    if N_MICROBATCH > 1:
        for name, axis in PER_EXAMPLE_BATCH_AXIS.items():
            out[name] = _to_microbatches(out[name], axis)
    return {'x': out['x'], 'e_norm_mix': out['e_norm_mix'], 'e_w_in': out['e_w_in'], 'e_sink': out['e_sink'], 'e_w_pool': out['e_w_pool'], 'e_pool_scale': out['e_pool_scale'], 'e_w_out': out['e_w_out'], 'o_norm_mix': out['o_norm_mix'], 'o_w_in': out['o_w_in'], 'o_g_cq': out['o_g_cq'], 'o_w_uq': out['o_w_uq'], 'o_g_ckv': out['o_g_ckv'], 'o_w_ukv': out['o_w_ukv'], 'o_conv_w': out['o_conv_w'], 'o_conv_b': out['o_conv_b'], 'o_lru_wa': out['o_lru_wa'], 'o_lru_ba': out['o_lru_ba'], 'o_lru_wx': out['o_lru_wx'], 'o_lru_bx': out['o_lru_bx'], 'o_lru_lambda': out['o_lru_lambda'], 'o_w_out': out['o_w_out'], 'norm_mlp': out['norm_mlp'], 'w_mlp1': out['w_mlp1'], 'w_mlp2': out['w_mlp2'], 'final_norm': out['final_norm'], 'loss_target': out['loss_target'], 'm_e_norm_mix': out['m_e_norm_mix'], 'm_e_w_in': out['m_e_w_in'], 'm_e_sink': out['m_e_sink'], 'm_e_w_pool': out['m_e_w_pool'], 'm_e_pool_scale': out['m_e_pool_scale'], 'm_e_w_out': out['m_e_w_out'], 'm_o_norm_mix': out['m_o_norm_mix'], 'm_o_w_in': out['m_o_w_in'], 'm_o_g_cq': out['m_o_g_cq'], 'm_o_w_uq': out['m_o_w_uq'], 'm_o_g_ckv': out['m_o_g_ckv'], 'm_o_w_ukv': out['m_o_w_ukv'], 'm_o_conv_w': out['m_o_conv_w'], 'm_o_conv_b': out['m_o_conv_b'], 'm_o_lru_wa': out['m_o_lru_wa'], 'm_o_lru_ba': out['m_o_lru_ba'], 'm_o_lru_wx': out['m_o_lru_wx'], 'm_o_lru_bx': out['m_o_lru_bx'], 'm_o_lru_lambda': out['m_o_lru_lambda'], 'm_o_w_out': out['m_o_w_out'], 'm_norm_mlp': out['m_norm_mlp'], 'm_w_mlp1': out['m_w_mlp1'], 'm_w_mlp2': out['m_w_mlp2'], 'm_final_norm': out['m_final_norm'], 'v_e_norm_mix': out['v_e_norm_mix'], 'v_e_w_in': out['v_e_w_in'], 'v_e_sink': out['v_e_sink'], 'v_e_w_pool': out['v_e_w_pool'], 'v_e_pool_scale': out['v_e_pool_scale'], 'v_e_w_out': out['v_e_w_out'], 'v_o_norm_mix': out['v_o_norm_mix'], 'v_o_w_in': out['v_o_w_in'], 'v_o_g_cq': out['v_o_g_cq'], 'v_o_w_uq': out['v_o_w_uq'], 'v_o_g_ckv': out['v_o_g_ckv'], 'v_o_w_ukv': out['v_o_w_ukv'], 'v_o_conv_w': out['v_o_conv_w'], 'v_o_conv_b': out['v_o_conv_b'], 'v_o_lru_wa': out['v_o_lru_wa'], 'v_o_lru_ba': out['v_o_lru_ba'], 'v_o_lru_wx': out['v_o_lru_wx'], 'v_o_lru_bx': out['v_o_lru_bx'], 'v_o_lru_lambda': out['v_o_lru_lambda'], 'v_o_w_out': out['v_o_w_out'], 'v_norm_mlp': out['v_norm_mlp'], 'v_w_mlp1': out['v_w_mlp1'], 'v_w_mlp2': out['v_w_mlp2'], 'v_final_norm': out['v_final_norm']}


def _loss(weights, diff, rest, loss_target):
    with _jax.named_scope("forward"):
        args = {**rest, TWIN_DIFF_INPUT: diff, **{k: w.astype(_WEIGHT_DTYPES[k]) for k, w in weights.items()}}
        y = _forward(args)
    with _jax.named_scope("loss_head"):
        err = _jnp.square(y.astype(_jnp.float32) - loss_target)
        return 0.5 * _jnp.sum(_jnp.mean(err, axis=-1)) if err.ndim else 0.5 * err


def _adamw(w, g, m, v):
    m = ADAM_B1 * m + (1.0 - ADAM_B1) * g
    v = ADAM_B2 * v + (1.0 - ADAM_B2) * _jnp.square(g)
    m_hat = m / (1.0 - ADAM_B1 ** ADAM_STEP)
    v_hat = v / (1.0 - ADAM_B2 ** ADAM_STEP)
    delta = -ADAM_LR * (m_hat / (_jnp.sqrt(v_hat) + ADAM_EPS) + ADAM_WD * w)
    return delta, m, v


def reference(x, e_norm_mix, e_w_in, e_sink, e_w_pool, e_pool_scale, e_w_out, o_norm_mix, o_w_in, o_g_cq, o_w_uq, o_g_ckv, o_w_ukv, o_conv_w, o_conv_b, o_lru_wa, o_lru_ba, o_lru_wx, o_lru_bx, o_lru_lambda, o_w_out, norm_mlp, w_mlp1, w_mlp2, final_norm, loss_target, m_e_norm_mix, m_e_w_in, m_e_sink, m_e_w_pool, m_e_pool_scale, m_e_w_out, m_o_norm_mix, m_o_w_in, m_o_g_cq, m_o_w_uq, m_o_g_ckv, m_o_w_ukv, m_o_conv_w, m_o_conv_b, m_o_lru_wa, m_o_lru_ba, m_o_lru_wx, m_o_lru_bx, m_o_lru_lambda, m_o_w_out, m_norm_mlp, m_w_mlp1, m_w_mlp2, m_final_norm, v_e_norm_mix, v_e_w_in, v_e_sink, v_e_w_pool, v_e_pool_scale, v_e_w_out, v_o_norm_mix, v_o_w_in, v_o_g_cq, v_o_w_uq, v_o_g_ckv, v_o_w_ukv, v_o_conv_w, v_o_conv_b, v_o_lru_wa, v_o_lru_ba, v_o_lru_wx, v_o_lru_bx, v_o_lru_lambda, v_o_w_out, v_norm_mlp, v_w_mlp1, v_w_mlp2, v_final_norm):
    given = dict(x=x, e_norm_mix=e_norm_mix, e_w_in=e_w_in, e_sink=e_sink, e_w_pool=e_w_pool, e_pool_scale=e_pool_scale, e_w_out=e_w_out, o_norm_mix=o_norm_mix, o_w_in=o_w_in, o_g_cq=o_g_cq, o_w_uq=o_w_uq, o_g_ckv=o_g_ckv, o_w_ukv=o_w_ukv, o_conv_w=o_conv_w, o_conv_b=o_conv_b, o_lru_wa=o_lru_wa, o_lru_ba=o_lru_ba, o_lru_wx=o_lru_wx, o_lru_bx=o_lru_bx, o_lru_lambda=o_lru_lambda, o_w_out=o_w_out, norm_mlp=norm_mlp, w_mlp1=w_mlp1, w_mlp2=w_mlp2, final_norm=final_norm, loss_target=loss_target, m_e_norm_mix=m_e_norm_mix, m_e_w_in=m_e_w_in, m_e_sink=m_e_sink, m_e_w_pool=m_e_w_pool, m_e_pool_scale=m_e_pool_scale, m_e_w_out=m_e_w_out, m_o_norm_mix=m_o_norm_mix, m_o_w_in=m_o_w_in, m_o_g_cq=m_o_g_cq, m_o_w_uq=m_o_w_uq, m_o_g_ckv=m_o_g_ckv, m_o_w_ukv=m_o_w_ukv, m_o_conv_w=m_o_conv_w, m_o_conv_b=m_o_conv_b, m_o_lru_wa=m_o_lru_wa, m_o_lru_ba=m_o_lru_ba, m_o_lru_wx=m_o_lru_wx, m_o_lru_bx=m_o_lru_bx, m_o_lru_lambda=m_o_lru_lambda, m_o_w_out=m_o_w_out, m_norm_mlp=m_norm_mlp, m_w_mlp1=m_w_mlp1, m_w_mlp2=m_w_mlp2, m_final_norm=m_final_norm, v_e_norm_mix=v_e_norm_mix, v_e_w_in=v_e_w_in, v_e_sink=v_e_sink, v_e_w_pool=v_e_w_pool, v_e_pool_scale=v_e_pool_scale, v_e_w_out=v_e_w_out, v_o_norm_mix=v_o_norm_mix, v_o_w_in=v_o_w_in, v_o_g_cq=v_o_g_cq, v_o_w_uq=v_o_w_uq, v_o_g_ckv=v_o_g_ckv, v_o_w_ukv=v_o_w_ukv, v_o_conv_w=v_o_conv_w, v_o_conv_b=v_o_conv_b, v_o_lru_wa=v_o_lru_wa, v_o_lru_ba=v_o_lru_ba, v_o_lru_wx=v_o_lru_wx, v_o_lru_bx=v_o_lru_bx, v_o_lru_lambda=v_o_lru_lambda, v_o_w_out=v_o_w_out, v_norm_mlp=v_norm_mlp, v_w_mlp1=v_w_mlp1, v_w_mlp2=v_w_mlp2, v_final_norm=v_final_norm)
    weights = {n: given[n] for n in TWIN_WEIGHTS}
    shared = {n: given[n] for n in SHARED_INPUTS}
    per_example = {n: given[n] for n in ['x']}
    grad_fn = _jax.value_and_grad(_loss, argnums=(0, 1))

    def one_microbatch(ex, loss_target):
        ex = dict(ex)
        diff = ex.pop(TWIN_DIFF_INPUT)
        return grad_fn(weights, diff, {**shared, **ex}, loss_target)

    if N_MICROBATCH == 1:
        loss, (grad_w, grad_x) = one_microbatch(per_example, given["loss_target"])
    else:
        def body(carry, xs):
            loss_sum, grad_sum = carry
            l_k, (gw_k, gx_k) = one_microbatch(xs[0], xs[1])
            with _jax.named_scope("update"):
                return (loss_sum + l_k, _jax.tree.map(_jnp.add, grad_sum, gw_k)), gx_k

        init = (_jnp.zeros((), _jnp.float32), _jax.tree.map(_jnp.zeros_like, weights))
        (loss, grad_w), grad_x = _jax.lax.scan(body, init, (per_example, given["loss_target"]))
    with _jax.named_scope("update"):
        delta_w, new_m, new_v = {}, {}, {}
        for n in TWIN_WEIGHTS:
            delta_w[n], new_m[n], new_v[n] = _adamw(weights[n], grad_w[n], given["m_" + n], given["v_" + n])
    return (loss, grad_x, *[grad_w[n] for n in TWIN_WEIGHTS], *[delta_w[n] for n in TWIN_WEIGHTS],
            *[new_m[n] for n in TWIN_WEIGHTS], *[new_v[n] for n in TWIN_WEIGHTS])
```

```python
import functools

import jax
import jax.numpy as jnp
from jax import lax
from jax.experimental import pallas as pl
from jax.experimental.pallas import tpu as pltpu

F32 = jnp.float32
MXU_DTYPE = jnp.bfloat16
EPS = 1e-6
ROPE_THETA = 10000.0
NDEV = 8
LANES = 128
VMEM_LIMIT = 48 * 1024 * 1024
MESH_AXES = ("x", "y", "c")

D_MODEL = 1024
D_FF = 4096
LRU_C = 8.0
POOL_WINDOWS = (2, 4, 8, 16)
HALO = 8
MLA_SCALE = 96.0 ** -0.5

ADAM_LR, ADAM_B1, ADAM_B2, ADAM_EPS, ADAM_WD, ADAM_STEP = 0.001, 0.9, 0.999, 1e-08, 0.01, 10


def _mx(v):
    return v.astype(MXU_DTYPE)


def _call(body, *, name, grid, in_specs, out_specs, out_shape, scratch=()):
    return pl.pallas_call(
        body, name=name, grid=grid, in_specs=in_specs, out_specs=out_specs, out_shape=out_shape,
        scratch_shapes=list(scratch),
        compiler_params=pltpu.CompilerParams(
            dimension_semantics=("arbitrary",) * len(grid), vmem_limit_bytes=VMEM_LIMIT),
    )


def _dot(a, b):
    return jnp.dot(a, b, preferred_element_type=F32)


def _dot_nt(a, b):
    return lax.dot_general(a, b, (((1,), (1,)), ((), ())), preferred_element_type=F32)


def _dot_tn(a, b):
    return lax.dot_general(a, b, (((0,), (0,)), ((), ())), preferred_element_type=F32)


def _rms(x, g):
    r = lax.rsqrt(jnp.mean(x * x, axis=-1, keepdims=True) + EPS)
    return (x * r) * g


def _rms_bwd(x, g, dy):
    r = lax.rsqrt(jnp.mean(x * x, axis=-1, keepdims=True) + EPS)
    xh = x * r
    dyg = dy * g
    dx = r * (dyg - xh * jnp.mean(dyg * xh, axis=-1, keepdims=True))
    return dx, jnp.sum(dy * xh, axis=0, keepdims=True)


def _sigmoid(x):
    return 1.0 / (1.0 + jnp.exp(-x))


def _log1p(e):
    u = 1.0 + e
    d = u - 1.0
    return jnp.where(d == 0.0, e, jnp.log(u) * (e / jnp.where(d == 0.0, 1.0, d)))


def _softplus(x):
    return jnp.maximum(x, 0.0) + _log1p(jnp.exp(-jnp.abs(x)))


def _expm1(x):
    u = jnp.exp(x)
    lu = jnp.log(u)
    safe = jnp.where((lu == 0.0) | (u == 0.0), 1.0, lu)
    return jnp.where(u == 1.0, x, jnp.where(u == 0.0, -1.0, (u - 1.0) * x / safe))


_GELU_K = 0.7978845608028654


def _gelu(x):
    return 0.5 * x * (1.0 + jnp.tanh(_GELU_K * (x + 0.044715 * x * x * x)))


def _gelu_grad(x):
    t = jnp.tanh(_GELU_K * (x + 0.044715 * x * x * x))
    return 0.5 * (1.0 + t) + 0.5 * x * (1.0 - t * t) * _GELU_K * (1.0 + 3.0 * 0.044715 * x * x)


def _rope(x, c, sa, sb, half):
    return x * c + pltpu.roll(x, LANES - half, 1) * sa + pltpu.roll(x, half, 1) * sb


def _rope_t(d, c, sa, sb, half):
    return d * c - pltpu.roll(d, LANES - half, 1) * sa - pltpu.roll(d, half, 1) * sb


def _as_cols(a):
    return a if isinstance(a, tuple) else (a, 0, a.shape[1])


def mm_nn(name, a_list, b_list, *, res=None, act=None, norm_g=None, emit_norm=False,
          out_dtype=F32, tm=256, tn=512):
    a_list = [_as_cols(a) for a in a_list]
    M, N = a_list[0][0].shape[0], b_list[0].shape[1]
    tm, tn = min(tm, M), min(tn, N)
    na = len(a_list)

    def body(*refs):
        a_refs, b_refs = refs[:na], refs[na:2 * na]
        k = 2 * na
        g_ref = res_ref = hn_ref = None
        if norm_g is not None:
            g_ref = refs[k]
            k += 1
        if res is not None:
            res_ref = refs[k]
            k += 1
        o_ref = refs[k]
        if emit_norm:
            hn_ref = refs[k + 1]
        acc = None
        for a_ref, b_ref in zip(a_refs, b_refs):
            a = a_ref[...]
            if g_ref is not None:
                a = _rms(a.astype(F32), g_ref[...])
                if hn_ref is not None:
                    hn_ref[...] = a.astype(hn_ref.dtype)
            if act == "relu2":
                a = jnp.maximum(a, 0.0)
                a = a * a
            d = _dot(_mx(a), _mx(b_ref[...]))
            acc = d if acc is None else acc + d
        if res_ref is not None:
            acc = acc + res_ref[...]
        o_ref[...] = acc.astype(o_ref.dtype)

    in_specs = [pl.BlockSpec((tm, w), functools.partial(lambda i, j, cb: (i, cb), cb=cb)) for (_, cb, w) in a_list]
    in_specs += [pl.BlockSpec((b.shape[0], tn), lambda i, j: (0, j)) for b in b_list]
    args = [a for (a, _, _) in a_list] + list(b_list)
    if norm_g is not None:
        in_specs.append(pl.BlockSpec((1, norm_g.shape[1]), lambda i, j: (0, 0)))
        args.append(norm_g)
    if res is not None:
        in_specs.append(pl.BlockSpec((tm, tn), lambda i, j: (i, j)))
        args.append(res)
    out_specs = [pl.BlockSpec((tm, tn), lambda i, j: (i, j))]
    out_shape = [jax.ShapeDtypeStruct((M, N), out_dtype)]
    if emit_norm:
        K = a_list[0][2]
        out_specs.append(pl.BlockSpec((tm, K), lambda i, j: (i, 0)))
        out_shape.append(jax.ShapeDtypeStruct((M, K), MXU_DTYPE))
    out = _call(body, name=name, grid=(M // tm, N // tn), in_specs=in_specs, out_specs=out_specs,
                out_shape=out_shape)(*args)
    return out if emit_norm else out[0]


def mm_nt(name, a_list, b_list, *, relu2_of=None, out_dtype=F32, tm=256, tn=512):
    a_list = [_as_cols(a) for a in a_list]
    b_list = [_as_cols(b) for b in b_list]
    M, N = a_list[0][0].shape[0], b_list[0][0].shape[0]
    tm, tn = min(tm, M), min(tn, N)
    na = len(a_list)

    def body(*refs):
        a_refs, b_refs = refs[:na], refs[na:2 * na]
        u_ref = refs[2 * na] if relu2_of is not None else None
        o_ref = refs[-1]
        acc = None
        for a_ref, b_ref in zip(a_refs, b_refs):
            d = _dot_nt(_mx(a_ref[...]), _mx(b_ref[...]))
            acc = d if acc is None else acc + d
        if u_ref is not None:
            acc = acc * (2.0 * jnp.maximum(u_ref[...], 0.0))
        o_ref[...] = acc.astype(o_ref.dtype)

    in_specs = [pl.BlockSpec((tm, w), functools.partial(lambda i, j, cb: (i, cb), cb=cb)) for (_, cb, w) in a_list]
    in_specs += [pl.BlockSpec((tn, w), functools.partial(lambda i, j, cb: (j, cb), cb=cb)) for (_, cb, w) in b_list]
    args = [a for (a, _, _) in a_list] + [b for (b, _, _) in b_list]
    if relu2_of is not None:
        in_specs.append(pl.BlockSpec((tm, tn), lambda i, j: (i, j)))
        args.append(relu2_of)
    return _call(body, name=name, grid=(M // tm, N // tn), in_specs=in_specs,
                 out_specs=pl.BlockSpec((tm, tn), lambda i, j: (i, j)),
                 out_shape=jax.ShapeDtypeStruct((M, N), out_dtype))(*args)


def mm_tn(name, a, b, *, act=None, tm=512, tn=512, tk=512):
    a, acb, Ma = _as_cols(a)
    b, bcb, Nb = _as_cols(b)
    S = a.shape[0]
    tm, tn, tk = min(tm, Ma), min(tn, Nb), min(tk, S)
    a0, b0 = acb * (Ma // tm), bcb * (Nb // tn)

    def body(a_ref, b_ref, o_ref):
        @pl.when(pl.program_id(2) == 0)
        def _():
            o_ref[...] = jnp.zeros_like(o_ref)

        av = a_ref[...]
        if act == "relu2":
            av = jnp.maximum(av, 0.0)
            av = av * av
        o_ref[...] += _dot_tn(_mx(av), _mx(b_ref[...]))

    return _call(body, name=name, grid=(Ma // tm, Nb // tn, S // tk),
                 in_specs=[pl.BlockSpec((tk, tm), lambda i, j, k: (k, a0 + i)),
                           pl.BlockSpec((tk, tn), lambda i, j, k: (k, b0 + j))],
                 out_specs=pl.BlockSpec((tm, tn), lambda i, j, k: (i, j)),
                 out_shape=jax.ShapeDtypeStruct((Ma, Nb), F32))(a, b)


def rms_bwd(name, x, g, dh, dres, T=256):
    S, D = x.shape
    T = min(T, S)

    def body(x_ref, g_ref, dh_ref, dres_ref, dx_ref, dg_ref):
        @pl.when(pl.program_id(0) == 0)
        def _():
            dg_ref[...] = jnp.zeros_like(dg_ref)

        dx, dg = _rms_bwd(x_ref[...], g_ref[...], dh_ref[...])
        dx_ref[...] = dres_ref[...] + dx
        dg_ref[...] += dg

    row = pl.BlockSpec((T, D), lambda i: (i, 0))
    vec = pl.BlockSpec((1, D), lambda i: (0, 0))
    return _call(body, name=name, grid=(S // T,), in_specs=[row, vec, row, row], out_specs=[row, vec],
                 out_shape=[jax.ShapeDtypeStruct((S, D), F32), jax.ShapeDtypeStruct((1, D), F32)])(x, g, dh, dres)


def loss_head(x, g, target, T=256):
    S, D = x.shape
    T = min(T, S)

    def body(x_ref, g_ref, t_ref, loss_ref, dx_ref, dg_ref):
        @pl.when(pl.program_id(0) == 0)
        def _():
            dg_ref[...] = jnp.zeros_like(dg_ref)
            loss_ref[...] = jnp.zeros_like(loss_ref)

        x = x_ref[...]
        err = _rms(x, g_ref[...]) - t_ref[...]
        loss_ref[...] += 0.5 * jnp.sum(jnp.sum(err * err, axis=-1, keepdims=True) / D, axis=0, keepdims=True)
        dx, dg = _rms_bwd(x, g_ref[...], err / D)
        dx_ref[...] = dx
        dg_ref[...] += dg

    row = pl.BlockSpec((T, D), lambda i: (i, 0))
    vec = pl.BlockSpec((1, D), lambda i: (0, 0))
    return _call(body, name="loss_head", grid=(S // T,), in_specs=[row, vec, row],
                 out_specs=[pl.BlockSpec((1, LANES), lambda i: (0, 0)), row, vec],
                 out_shape=[jax.ShapeDtypeStruct((1, LANES), F32), jax.ShapeDtypeStruct((S, D), F32),
                            jax.ShapeDtypeStruct((1, D), F32)])(x, g, target)


def rope_tables(S, lo, half):
    inv = ROPE_THETA ** (-jnp.arange(half, dtype=F32) / half)
    ang = jnp.arange(S, dtype=F32)[:, None] * inv[None, :]
    cos, sin = jnp.cos(ang), jnp.sin(ang)
    one = lambda n: jnp.ones((S, n), F32)
    zero = lambda n: jnp.zeros((S, n), F32)
    hi = LANES - lo - 2 * half
    c = jnp.concatenate([one(lo), cos, cos, one(hi)], axis=1)
    sa = jnp.concatenate([zero(lo), -sin, zero(half), zero(hi)], axis=1)
    sb = jnp.concatenate([zero(lo), zero(half), sin, zero(hi)], axis=1)
    return c, sa, sb


SWA_BLOCK = 128
SWA_HALF = 32


def swa_prep(z, tabs, T=256):
    S = z.shape[0]
    T = min(T, S)

    def body(z_ref, c_ref, sa_ref, sb_ref, o_ref):
        c, sa, sb = c_ref[...], sa_ref[...], sb_ref[...]
        for s in range(10):
            sl = slice(s * LANES, (s + 1) * LANES)
            y = _rope(z_ref[:, sl], c, sa, sb, SWA_HALF)
            if s < 8:
                y = y * 0.125
            o_ref[:, sl] = y.astype(o_ref.dtype)
        o_ref[:, 1280:1536] = z_ref[:, 1280:1536].astype(o_ref.dtype)

    tab = pl.BlockSpec((T, LANES), lambda i: (i, 0))
    return _call(body, name="swa_prep", grid=(S // T,),
                 in_specs=[pl.BlockSpec((T, 1536), lambda i: (i, 0)), tab, tab, tab],
                 out_specs=pl.BlockSpec((T, 1536), lambda i: (i, 0)),
                 out_shape=jax.ShapeDtypeStruct((S, 1536), MXU_DTYPE))(z, *tabs)


def _swa_valid(n, S):
    B = SWA_BLOCK
    i = lax.broadcasted_iota(jnp.int32, (B, 3 * B), 0)
    j = lax.broadcasted_iota(jnp.int32, (B, 3 * B), 1)
    kpos = j + (n - 1) * B
    return (jnp.abs(j - B - i) <= B) & (kpos >= 0) & (kpos < S)


def swa_fwd(qkv, sink):
    S = qkv.shape[0]
    B = SWA_BLOCK
    nb = S // B

    def body(sink_ref, q_ref, kp_ref, kc_ref, kn_ref, vp_ref, vc_ref, vn_ref, o_ref, st_ref):
        n = pl.program_id(0)
        valid = _swa_valid(n, S)
        lane = lax.broadcasted_iota(jnp.int32, (B, LANES), 1)
        st = jnp.zeros((B, LANES), F32)
        for hk in range(2):
            sl = slice(hk * LANES, (hk + 1) * LANES)
            k3 = jnp.concatenate([kp_ref[:, sl], kc_ref[:, sl], kn_ref[:, sl]], axis=0)
            v3 = jnp.concatenate([vp_ref[:, sl], vc_ref[:, sl], vn_ref[:, sl]], axis=0)
            for g in range(4):
                h = hk * 4 + g
                hs = slice(h * LANES, (h + 1) * LANES)
                s = jnp.where(valid, _dot_nt(q_ref[:, hs], k3), -jnp.inf)
                sk = sink_ref[h]
                m = jnp.maximum(jnp.max(s, axis=-1, keepdims=True), sk)
                p = jnp.exp(s - m)
                den = jnp.sum(p, axis=-1, keepdims=True) + jnp.exp(sk - m)
                p = p / den
                o_ref[:, hs] = _dot(_mx(p), v3).astype(o_ref.dtype)
                st = jnp.where(lane == h, m + jnp.log(den), st)
        st_ref[...] = st

    kv = lambda cb, d: pl.BlockSpec((B, 2 * LANES), lambda n: (jnp.clip(n + d, 0, nb - 1), cb))
    return _call(body, name="swa_fwd", grid=(nb,),
                 in_specs=[pl.BlockSpec(memory_space=pltpu.SMEM),
                           pl.BlockSpec((B, 1024), lambda n: (n, 0)),
                           kv(4, -1), kv(4, 0), kv(4, 1), kv(5, -1), kv(5, 0), kv(5, 1)],
                 out_specs=[pl.BlockSpec((B, 1024), lambda n: (n, 0)), pl.BlockSpec((B, LANES), lambda n: (n, 0))],
                 out_shape=[jax.ShapeDtypeStruct((S, 1024), MXU_DTYPE), jax.ShapeDtypeStruct((S, LANES), F32)],
                 )(sink, qkv, qkv, qkv, qkv, qkv, qkv, qkv)


def swa_bwd(qkv, sink, lse, dycat, tabs):
    S = qkv.shape[0]
    B = SWA_BLOCK
    nb = S // B

    def body(sink_ref, q_ref, kp_ref, kc_ref, kn_ref, vp_ref, vc_ref, vn_ref, do_ref, st_ref,
             cq_ref, saq_ref, sbq_ref, ck_ref, sak_ref, sbk_ref,
             dq_ref, dk_ref, dv_ref, dsink_ref, dk_acc, dv_acc):
        n = pl.program_id(0)

        @pl.when(n == 0)
        def _():
            dk_acc[...] = jnp.zeros_like(dk_acc)
            dv_acc[...] = jnp.zeros_like(dv_acc)
            dsink_ref[...] = jnp.zeros_like(dsink_ref)

        @pl.when(n < nb)
        def _():
            valid = _swa_valid(n, S)
            lane = lax.broadcasted_iota(jnp.int32, (B, LANES), 1)
            lane1 = lax.broadcasted_iota(jnp.int32, (1, LANES), 1)
            st = st_ref[...]
            cq, saq, sbq = cq_ref[...], saq_ref[...], sbq_ref[...]
            dsink = jnp.zeros((1, LANES), F32)
            for hk in range(2):
                sl = slice(hk * LANES, (hk + 1) * LANES)
                k3 = jnp.concatenate([kp_ref[:, sl], kc_ref[:, sl], kn_ref[:, sl]], axis=0)
                v3 = jnp.concatenate([vp_ref[:, sl], vc_ref[:, sl], vn_ref[:, sl]], axis=0)
                dk3 = jnp.zeros((3 * B, LANES), F32)
                dv3 = jnp.zeros((3 * B, LANES), F32)
                for g in range(4):
                    h = hk * 4 + g
                    hs = slice(h * LANES, (h + 1) * LANES)
                    q = q_ref[:, hs]
                    do = _mx(do_ref[:, hs])
                    lse_h = jnp.sum(jnp.where(lane == h, st, 0.0), axis=-1, keepdims=True)
                    p = jnp.where(valid, jnp.exp(_dot_nt(q, k3) - lse_h), 0.0)
                    dp = _dot_nt(do, v3)
                    dsum = jnp.sum(p * dp, axis=-1, keepdims=True)
                    ds = _mx(p * (dp - dsum))
                    dq_ref[:, hs] = _rope_t(_dot(ds, k3) * 0.125, cq, saq, sbq, SWA_HALF)
                    dk3 = dk3 + _dot_tn(ds, q)
                    dv3 = dv3 + _dot_tn(_mx(p), do)
                    dsk = -jnp.sum(jnp.exp(sink_ref[h] - lse_h) * dsum, axis=0, keepdims=True)
                    dsink = jnp.where(lane1 == h, dsk, dsink)
                dk_acc[:, sl] += dk3
                dv_acc[:, sl] += dv3
            dsink_ref[...] += dsink

        ck, sak, sbk = ck_ref[...], sak_ref[...], sbk_ref[...]
        for hk in range(2):
            sl = slice(hk * LANES, (hk + 1) * LANES)
            dk_ref[:, sl] = _rope_t(dk_acc[0:B, sl], ck, sak, sbk, SWA_HALF)
        dv_ref[...] = dv_acc[0:B, :]
        for acc in (dk_acc, dv_acc):
            acc[0:B, :] = acc[B:2 * B, :]
            acc[B:2 * B, :] = acc[2 * B:3 * B, :]
            acc[2 * B:3 * B, :] = jnp.zeros((B, 2 * LANES), F32)

    qn = lambda n: jnp.minimum(n, nb - 1)
    kv = lambda cb, d: pl.BlockSpec((B, 2 * LANES), lambda n: (jnp.clip(qn(n) + d, 0, nb - 1), cb))
    qrow = lambda w: pl.BlockSpec((B, w), lambda n: (qn(n), 0))
    krow = lambda w: pl.BlockSpec((B, w), lambda n: (jnp.maximum(n - 1, 0), 0))
    return _call(body, name="swa_bwd", grid=(nb + 1,),
                 in_specs=[pl.BlockSpec(memory_space=pltpu.SMEM), qrow(1024),
                           kv(4, -1), kv(4, 0), kv(4, 1), kv(5, -1), kv(5, 0), kv(5, 1),
                           qrow(1024), qrow(LANES),
                           qrow(LANES), qrow(LANES), qrow(LANES), krow(LANES), krow(LANES), krow(LANES)],
                 out_specs=[qrow(1024), krow(2 * LANES), krow(2 * LANES), pl.BlockSpec((1, LANES), lambda n: (0, 0))],
                 out_shape=[jax.ShapeDtypeStruct((S, 1024), F32), jax.ShapeDtypeStruct((S, 2 * LANES), F32),
                            jax.ShapeDtypeStruct((S, 2 * LANES), F32), jax.ShapeDtypeStruct((1, LANES), F32)],
                 scratch=[pltpu.VMEM((3 * B, 2 * LANES), F32), pltpu.VMEM((3 * B, 2 * LANES), F32)],
                 )(sink, qkv, qkv, qkv, qkv, qkv, qkv, qkv, dycat, lse, *tabs, *tabs)


def _halo_specs(T, S, w, cb):
    r = T // HALO
    last = S // HALO - 1
    return [pl.BlockSpec((HALO, w), lambda i: (jnp.maximum(i * r - 1, 0), cb)),
            pl.BlockSpec((T, w), lambda i: (i, cb)),
            pl.BlockSpec((HALO, w), lambda i: (jnp.minimum((i + 1) * r, last), cb))]


def _fill_ext(ext, prev_ref, cur_ref, next_ref, i, nt, T):
    ext[0:HALO, :] = jnp.where(i > 0, prev_ref[...], 0.0).astype(F32)
    ext[HALO:HALO + T, :] = cur_ref[...].astype(F32)
    ext[HALO + T:2 * HALO + T, :] = jnp.where(i < nt - 1, next_ref[...], 0.0).astype(F32)


def _pool_cnt(t, half, S):
    return (jnp.clip(t + half, 0, S) - jnp.clip(t - half, 0, S)).astype(F32)


def pool_fwd(z, w_pool, scale, T=256):
    S = z.shape[0]
    T = min(T, S)
    nt = S // T

    def body(up_ref, uc_ref, un_ref, w_ref, sc_ref, o_ref, ext):
        i = pl.program_id(0)
        _fill_ext(ext, up_ref, uc_ref, un_ref, i, nt, T)
        t = i * T + lax.broadcasted_iota(jnp.int32, (T, 1), 0)
        for g, win in enumerate(POOL_WINDOWS):
            half = win // 2
            sl = slice(g * LANES, (g + 1) * LANES)
            acc = ext[pl.ds(HALO - half, T), sl]
            for off in range(-half + 1, half):
                acc = acc + ext[pl.ds(HALO + off, T), sl]
            d = acc / _pool_cnt(t, half, S) - ext[pl.ds(HALO, T), sl]
            o_ref[:, sl] = (_dot(_mx(d), w_ref[g]) * sc_ref[:, sl]).astype(o_ref.dtype)

    return _call(body, name="pool_fwd", grid=(nt,),
                 in_specs=_halo_specs(T, S, 512, 3) + [pl.BlockSpec((4, LANES, LANES), lambda i: (0, 0, 0)),
                                                      pl.BlockSpec((1, 512), lambda i: (0, 0))],
                 out_specs=pl.BlockSpec((T, 512), lambda i: (i, 0)),
                 out_shape=jax.ShapeDtypeStruct((S, 512), MXU_DTYPE),
                 scratch=[pltpu.VMEM((T + 2 * HALO, 512), F32)])(z, z, z, w_pool, scale)


def pool_bwd(z, dycat, w_pool, scale, T=256):
    S = z.shape[0]
    T = min(T, S)
    nt = S // T
    TE = T + 2 * HALO

    def body(up_ref, uc_ref, un_ref, yp_ref, yc_ref, yn_ref, w_ref, sc_ref, du_ref, dw_ref, dsc_ref, extu, exty, exte):
        i = pl.program_id(0)

        @pl.when(i == 0)
        def _():
            dw_ref[...] = jnp.zeros_like(dw_ref)
            dsc_ref[...] = jnp.zeros_like(dsc_ref)

        _fill_ext(extu, up_ref, uc_ref, un_ref, i, nt, T)
        _fill_ext(exty, yp_ref, yc_ref, yn_ref, i, nt, T)
        t = i * T + lax.broadcasted_iota(jnp.int32, (T, 1), 0)
        te = i * T - HALO + lax.broadcasted_iota(jnp.int32, (TE, 1), 0)
        for g, win in enumerate(POOL_WINDOWS):
            half = win // 2
            sl = slice(g * LANES, (g + 1) * LANES)
            w = w_ref[g]
            acc = extu[pl.ds(HALO - half, T), sl]
            for off in range(-half + 1, half):
                acc = acc + extu[pl.ds(HALO + off, T), sl]
            d = _mx(acc / _pool_cnt(t, half, S) - extu[pl.ds(HALO, T), sl])
            dy = exty[pl.ds(HALO, T), sl]
            dsc_ref[:, sl] += jnp.sum(dy * _dot(d, w), axis=0, keepdims=True)
            dw_ref[g] += _dot_tn(d, _mx(dy * sc_ref[:, sl]))
            dd = _dot_nt(_mx(exty[:, sl] * sc_ref[:, sl]), w)
            exte[:, sl] = dd / jnp.maximum(_pool_cnt(te, half, S), 1.0)
            acc = exte[pl.ds(HALO - half + 1, T), sl]
            for off in range(-half + 2, half + 1):
                acc = acc + exte[pl.ds(HALO + off, T), sl]
            du_ref[:, sl] = acc - dd[HALO:HALO + T, :]

    return _call(body, name="pool_bwd", grid=(nt,),
                 in_specs=_halo_specs(T, S, 512, 3) + _halo_specs(T, S, 512, 2)
                 + [pl.BlockSpec((4, LANES, LANES), lambda i: (0, 0, 0)), pl.BlockSpec((1, 512), lambda i: (0, 0))],
                 out_specs=[pl.BlockSpec((T, 512), lambda i: (i, 0)), pl.BlockSpec((4, LANES, LANES), lambda i: (0, 0, 0)),
                            pl.BlockSpec((1, 512), lambda i: (0, 0))],
                 out_shape=[jax.ShapeDtypeStruct((S, 512), F32), jax.ShapeDtypeStruct((4, LANES, LANES), F32),
                            jax.ShapeDtypeStruct((1, 512), F32)],
                 scratch=[pltpu.VMEM((TE, 512), F32)] * 3)(z, z, z, dycat, dycat, dycat, w_pool, scale)


MLA_HALF = 16


def mla_prep(z, g_cq, g_ckv, w_uq, w_k, w_v, tabs_q, tabs_k, T=256):
    S = z.shape[0]
    T = min(T, S)

    def body(z_ref, gq_ref, gkv_ref, wq_ref, wk_ref, wv_ref, cq_ref, saq_ref, sbq_ref, ck_ref, sak_ref, sbk_ref,
             q_ref, k_ref, v_ref, nq_ref, nkv_ref):
        nq = _mx(_rms(z_ref[:, 0:256], gq_ref[...]))
        nkv = _mx(_rms(z_ref[:, 256:384], gkv_ref[...]))
        nq_ref[...] = nq
        nkv_ref[...] = nkv
        q = _dot(nq, wq_ref[...])
        kn = _dot(nkv, wk_ref[...])
        v_ref[...] = _dot(nkv, wv_ref[...]).astype(v_ref.dtype)
        kr = pltpu.roll(_rope(z_ref[:, 384:512], ck_ref[...], sak_ref[...], sbk_ref[...], MLA_HALF), 64, 1)
        cq, saq, sbq = cq_ref[...], saq_ref[...], sbq_ref[...]
        for h in range(8):
            hs = slice(h * LANES, (h + 1) * LANES)
            q_ref[:, hs] = _rope(q[:, hs], cq, saq, sbq, MLA_HALF).astype(q_ref.dtype)
            k_ref[:, hs] = (kn[:, hs] + kr).astype(k_ref.dtype)

    tab = pl.BlockSpec((T, LANES), lambda i: (i, 0))
    full = lambda a: pl.BlockSpec(a.shape, lambda i: (0, 0))
    row = lambda w: pl.BlockSpec((T, w), lambda i: (i, 0))
    sd = lambda w: jax.ShapeDtypeStruct((S, w), MXU_DTYPE)
    return _call(body, name="mla_prep", grid=(S // T,),
                 in_specs=[row(512), full(g_cq), full(g_ckv), full(w_uq), full(w_k), full(w_v)] + [tab] * 6,
                 out_specs=[row(1024), row(1024), row(1024), row(256), row(128)],
                 out_shape=[sd(1024), sd(1024), sd(1024), sd(256), sd(128)],
                 )(z, g_cq, g_ckv, w_uq, w_k, w_v, *tabs_q, *tabs_k)


def _col_to_row(c):
    return jnp.transpose(jnp.broadcast_to(c, (c.shape[0], LANES)))[0:1, :]


def _row_to_col(r):
    return jnp.transpose(jnp.broadcast_to(r, (LANES, r.shape[1])))[:, 0:1]


def mla_fwd(q, k, v, TQ=512, TK=512):
    S = q.shape[0]
    TQ, TK = min(TQ, S), min(TK, S)
    nk = S // TK

    def body(q_ref, k_ref, v_ref, o_ref, lse_ref, m_s, l_s, acc_s):
        ki = pl.program_id(2)

        @pl.when(ki == 0)
        def _():
            m_s[...] = jnp.full_like(m_s, -jnp.inf)
            l_s[...] = jnp.zeros_like(l_s)
            acc_s[...] = jnp.zeros_like(acc_s)

        s = _dot_nt(q_ref[...], k_ref[...]) * MLA_SCALE
        m_old = m_s[...]
        m_new = jnp.maximum(m_old, jnp.max(s, axis=-1, keepdims=True))
        alpha = jnp.exp(m_old - m_new)
        p = jnp.exp(s - m_new)
        l_s[...] = alpha * l_s[...] + jnp.sum(p, axis=-1, keepdims=True)
        acc_s[...] = alpha * acc_s[...] + _dot(_mx(p), v_ref[...])
        m_s[...] = m_new

        @pl.when(ki == nk - 1)
        def _():
            o_ref[...] = acc_s[...] / l_s[...]
            lse_ref[0] = _col_to_row(m_s[...] + jnp.log(l_s[...]))

    qs = pl.BlockSpec((TQ, LANES), lambda h, i, j: (i, h))
    ks = pl.BlockSpec((TK, LANES), lambda h, i, j: (j, h))
    return _call(body, name="mla_fwd", grid=(8, S // TQ, nk), in_specs=[qs, ks, ks],
                 out_specs=[qs, pl.BlockSpec((1, 1, TQ), lambda h, i, j: (h, 0, i))],
                 out_shape=[jax.ShapeDtypeStruct((S, 1024), F32), jax.ShapeDtypeStruct((8, 1, S), F32)],
                 scratch=[pltpu.VMEM((TQ, 1), F32), pltpu.VMEM((TQ, 1), F32), pltpu.VMEM((TQ, LANES), F32)],
                 )(q, k, v)


def mla_delta(o, dycat, TQ=512):
    S = o.shape[0]
    TQ = min(TQ, S)

    def body(o_ref, do_ref, d_ref):
        d_ref[0] = _col_to_row(jnp.sum(o_ref[...] * do_ref[...], axis=-1, keepdims=True))

    qs = pl.BlockSpec((TQ, LANES), lambda h, i: (i, h))
    return _call(body, name="mla_delta", grid=(8, S // TQ), in_specs=[qs, qs],
                 out_specs=pl.BlockSpec((1, 1, TQ), lambda h, i: (h, 0, i)),
                 out_shape=jax.ShapeDtypeStruct((8, 1, S), F32))(o, dycat)


def mla_bwd_q(q, k, v, dycat, lse, delta, TQ=512, TK=512):
    S = q.shape[0]
    TQ, TK = min(TQ, S), min(TK, S)
    nk = S // TK

    def body(q_ref, k_ref, v_ref, do_ref, lse_ref, d_ref, dq_ref, acc_s, lse_c, d_c):
        ki = pl.program_id(2)

        @pl.when(ki == 0)
        def _():
            acc_s[...] = jnp.zeros_like(acc_s)
            lse_c[...] = _row_to_col(lse_ref[0])
            d_c[...] = _row_to_col(d_ref[0])

        kk = k_ref[...]
        p = jnp.exp(_dot_nt(q_ref[...], kk) * MLA_SCALE - lse_c[...])
        dp = _dot_nt(_mx(do_ref[...]), v_ref[...])
        acc_s[...] += _dot(_mx(p * (dp - d_c[...])), kk)

        @pl.when(ki == nk - 1)
        def _():
            dq_ref[...] = acc_s[...] * MLA_SCALE

    qs = pl.BlockSpec((TQ, LANES), lambda h, i, j: (i, h))
    ks = pl.BlockSpec((TK, LANES), lambda h, i, j: (j, h))
    st = pl.BlockSpec((1, 1, TQ), lambda h, i, j: (h, 0, i))
    return _call(body, name="mla_bwd_q", grid=(8, S // TQ, nk), in_specs=[qs, ks, ks, qs, st, st],
                 out_specs=qs, out_shape=jax.ShapeDtypeStruct((S, 1024), F32),
                 scratch=[pltpu.VMEM((TQ, LANES), F32), pltpu.VMEM((TQ, 1), F32), pltpu.VMEM((TQ, 1), F32)],
                 )(q, k, v, dycat, lse, delta)


def mla_bwd_kv(q, k, v, dycat, lse, delta, TQ=512, TK=512):
    S = q.shape[0]
    TQ, TK = min(TQ, S), min(TK, S)
    nq = S // TQ

    def body(q_ref, k_ref, v_ref, do_ref, lse_ref, d_ref, dk_ref, dv_ref, dk_s, dv_s):
        qi = pl.program_id(2)

        @pl.when(qi == 0)
        def _():
            dk_s[...] = jnp.zeros_like(dk_s)
            dv_s[...] = jnp.zeros_like(dv_s)

        qq = q_ref[...]
        do = _mx(do_ref[...])
        pt = jnp.exp(_dot_nt(k_ref[...], qq) * MLA_SCALE - lse_ref[0])
        dv_s[...] += _dot(_mx(pt), do)
        dpt = _dot_nt(v_ref[...], do)
        dk_s[...] += _dot(_mx(pt * (dpt - d_ref[0])), qq)

        @pl.when(qi == nq - 1)
        def _():
            dk_ref[...] = dk_s[...] * MLA_SCALE
            dv_ref[...] = dv_s[...]

    qs = pl.BlockSpec((TQ, LANES), lambda h, j, i: (i, h))
    ks = pl.BlockSpec((TK, LANES), lambda h, j, i: (j, h))
    st = pl.BlockSpec((1, 1, TQ), lambda h, j, i: (h, 0, i))
    sd = jax.ShapeDtypeStruct((S, 1024), F32)
    return _call(body, name="mla_bwd_kv", grid=(8, S // TK, nq), in_specs=[qs, ks, ks, qs, st, st],
                 out_specs=[ks, ks], out_shape=[sd, sd],
                 scratch=[pltpu.VMEM((TK, LANES), F32), pltpu.VMEM((TK, LANES), F32)],
                 )(q, k, v, dycat, lse, delta)


def mla_prep_bwd(z, g_cq, g_ckv, w_uq, w_k, w_v, dq, dk, dv, tabs_q, tabs_k, T=256):
    S = z.shape[0]
    T = min(T, S)

    def body(z_ref, gq_ref, gkv_ref, wq_ref, wk_ref, wv_ref, dq_ref, dk_ref, dv_ref,
             cq_ref, saq_ref, sbq_ref, ck_ref, sak_ref, sbk_ref, dz_ref, dqp_ref, dgq_ref, dgkv_ref):
        @pl.when(pl.program_id(0) == 0)
        def _():
            dgq_ref[...] = jnp.zeros_like(dgq_ref)
            dgkv_ref[...] = jnp.zeros_like(dgkv_ref)

        cq, saq, sbq = cq_ref[...], saq_ref[...], sbq_ref[...]
        dkr = jnp.zeros((T, LANES), F32)
        for h in range(8):
            hs = slice(h * LANES, (h + 1) * LANES)
            dqp_ref[:, hs] = _rope_t(dq_ref[:, hs], cq, saq, sbq, MLA_HALF).astype(dqp_ref.dtype)
            dkr = dkr + dk_ref[:, hs]
        lane = lax.broadcasted_iota(jnp.int32, (T, LANES), 1)
        dkr = jnp.where(lane < 2 * MLA_HALF, pltpu.roll(dkr, 64, 1), 0.0)
        dz_ref[:, 384:512] = _rope_t(dkr, ck_ref[...], sak_ref[...], sbk_ref[...], MLA_HALF)
        dnq = _dot_nt(dqp_ref[...], wq_ref[...])
        dx, dg = _rms_bwd(z_ref[:, 0:256], gq_ref[...], dnq)
        dz_ref[:, 0:256] = dx
        dgq_ref[...] += dg
        dnkv = _dot_nt(_mx(dk_ref[...]), wk_ref[...]) + _dot_nt(_mx(dv_ref[...]), wv_ref[...])
        dx, dg = _rms_bwd(z_ref[:, 256:384], gkv_ref[...], dnkv)
        dz_ref[:, 256:384] = dx
        dgkv_ref[...] += dg

    tab = pl.BlockSpec((T, LANES), lambda i: (i, 0))
    full = lambda a: pl.BlockSpec(a.shape, lambda i: (0, 0))
    row = lambda w: pl.BlockSpec((T, w), lambda i: (i, 0))
    return _call(body, name="mla_prep_bwd", grid=(S // T,),
                 in_specs=[row(512), full(g_cq), full(g_ckv), full(w_uq), full(w_k), full(w_v),
                           row(1024), row(1024), row(1024)] + [tab] * 6,
                 out_specs=[row(512), row(1024), full(g_cq), full(g_ckv)],
                 out_shape=[jax.ShapeDtypeStruct((S, 512), F32), jax.ShapeDtypeStruct((S, 1024), MXU_DTYPE),
                            jax.ShapeDtypeStruct(g_cq.shape, F32), jax.ShapeDtypeStruct(g_ckv.shape, F32)],
                 )(z, g_cq, g_ckv, w_uq, w_k, w_v, dq, dk, dv, *tabs_q, *tabs_k)


def _lru_gates(xc, w_ref, bias_ref, lam_ref):
    pre = _dot(_mx(xc), w_ref[...]) + bias_ref[...]
    out = []
    for d in range(2):
        r = _sigmoid(pre[:, d * 1024:d * 1024 + 512])
        ig = _sigmoid(pre[:, d * 1024 + 512:(d + 1) * 1024])
        log_a = -LRU_C * r * _softplus(-lam_ref[:, d * 512:(d + 1) * 512])
        out.append((r, ig, jnp.exp(log_a), jnp.sqrt(-_expm1(2.0 * log_a))))
    return out


def lru_pre(z, conv_w, conv_b, w_gate, b_gate, lam, T=256):
    S = z.shape[0]
    T = min(T, S)
    nt = S // T

    def body(xp_ref, xcur_ref, xn_ref, cw_ref, cb_ref, w_ref, bias_ref, lam_ref, xc_ref, a0_ref, b0_ref, a1_ref, b1_ref, ext):
        i = pl.program_id(0)
        _fill_ext(ext, xp_ref, xcur_ref, xn_ref, i, nt, T)
        xc = cb_ref[...] + cw_ref[0:1, :] * ext[pl.ds(HALO - 2, T), :]
        for j in range(1, 4):
            xc = xc + cw_ref[j:j + 1, :] * ext[pl.ds(HALO - 2 + j, T), :]
        xc_ref[...] = xc
        (_, i0, a0, m0), (_, i1, a1, m1) = _lru_gates(xc, w_ref, bias_ref, lam_ref)
        a0_ref[...] = a0
        b0_ref[...] = m0 * (i0 * xc)
        a1_ref[...] = a1
        b1_ref[...] = m1 * (i1 * xc)

    full = lambda a: pl.BlockSpec(a.shape, lambda i: (0, 0))
    row = pl.BlockSpec((T, 512), lambda i: (i, 0))
    sd = jax.ShapeDtypeStruct((S, 512), F32)
    return _call(body, name="lru_pre", grid=(nt,),
                 in_specs=_halo_specs(T, S, 512, 1) + [full(conv_w), full(conv_b), full(w_gate), full(b_gate), full(lam)],
                 out_specs=[row] * 5, out_shape=[sd] * 5,
                 scratch=[pltpu.VMEM((T + 2 * HALO, 512), F32)])(z, z, z, conv_w, conv_b, w_gate, b_gate, lam)


def lru_scan(name, af, bf, ar, br, *, adjoint, T=256):
    S, W = af.shape
    T = min(T, S)
    nt = S // T
    nc = T // 8

    def body(af_ref, bf_ref, ar_ref, br_ref, hf_ref, hr_ref, cf, cr):
        @pl.when(pl.program_id(0) == 0)
        def _():
            cf[...] = jnp.zeros_like(cf)
            cr[...] = jnp.zeros_like(cr)

        row = lax.broadcasted_iota(jnp.int32, (8, W), 0)

        def step(a, b, carry):
            if adjoint:
                val = b + carry
                return val, a * val
            val = a * carry + b
            return val, val

        def chunk(c, carry):
            hf, hr = carry
            of = pl.multiple_of(c * 8, 8)
            orv = pl.multiple_of((nc - 1 - c) * 8, 8)
            a8, b8 = af_ref[pl.ds(of, 8), :], bf_ref[pl.ds(of, 8), :]
            ra8, rb8 = ar_ref[pl.ds(orv, 8), :], br_ref[pl.ds(orv, 8), :]
            outf = jnp.zeros((8, W), F32)
            outr = jnp.zeros((8, W), F32)
            for k in range(8):
                val, hf = step(a8[k:k + 1, :], b8[k:k + 1, :], hf)
                outf = jnp.where(row == k, val, outf)
                kr = 7 - k
                val, hr = step(ra8[kr:kr + 1, :], rb8[kr:kr + 1, :], hr)
                outr = jnp.where(row == kr, val, outr)
            hf_ref[pl.ds(of, 8), :] = outf
            hr_ref[pl.ds(orv, 8), :] = outr
            return hf, hr

        hf, hr = lax.fori_loop(0, nc, chunk, (cf[0:1, :], cr[0:1, :]))
        cf[0:1, :] = hf
        cr[0:1, :] = hr

    fw = pl.BlockSpec((T, W), lambda i: (i, 0))
    rv = pl.BlockSpec((T, W), lambda i: (nt - 1 - i, 0))
    sd = jax.ShapeDtypeStruct((S, W), F32)
    return _call(body, name=name, grid=(nt,), in_specs=[fw, fw, rv, rv], out_specs=[fw, rv], out_shape=[sd, sd],
                 scratch=[pltpu.VMEM((8, W), F32), pltpu.VMEM((8, W), F32)])(af, bf, ar, br)


def lru_gate(h0, h1, z, T=256):
    S = z.shape[0]
    T = min(T, S)

    def body(h0_ref, h1_ref, xg_ref, y_ref):
        y_ref[...] = ((h0_ref[...] + h1_ref[...]) * _gelu(xg_ref[...])).astype(y_ref.dtype)

    row = pl.BlockSpec((T, 512), lambda i: (i, 0))
    return _call(body, name="lru_gate", grid=(S // T,), in_specs=[row, row, pl.BlockSpec((T, 512), lambda i: (i, 2))],
                 out_specs=row, out_shape=jax.ShapeDtypeStruct((S, 512), MXU_DTYPE))(h0, h1, z)


def lru_gate_bwd(h0, h1, z, dycat, T=256):
    S = z.shape[0]
    T = min(T, S)

    def body(h0_ref, h1_ref, xg_ref, dy_ref, dxg_ref, dh_ref):
        xg, dy = xg_ref[...], dy_ref[...]
        dxg_ref[...] = dy * (h0_ref[...] + h1_ref[...]) * _gelu_grad(xg)
        dh_ref[...] = dy * _gelu(xg)

    row = pl.BlockSpec((T, 512), lambda i: (i, 0))
    col2 = pl.BlockSpec((T, 512), lambda i: (i, 2))
    sd = jax.ShapeDtypeStruct((S, 512), F32)
    return _call(body, name="lru_gate_bwd", grid=(S // T,), in_specs=[row, row, col2, col2],
                 out_specs=[row, row], out_shape=[sd, sd])(h0, h1, z, dycat)


def lru_bwd_point(xc, h0, h1, g0, g1, w_gate, b_gate, lam, T=256):
    S = xc.shape[0]
    T = min(T, S)
    nt = S // T

    def body(xc_ref, h0p_ref, h0_ref, h0n_ref, h1p_ref, h1_ref, h1n_ref, g0_ref, g1_ref, w_ref, bias_ref, lam_ref,
             dxc_ref, dpre_ref, dbias_ref, dlam_ref, ext0, ext1):
        i = pl.program_id(0)

        @pl.when(i == 0)
        def _():
            dbias_ref[...] = jnp.zeros_like(dbias_ref)
            dlam_ref[...] = jnp.zeros_like(dlam_ref)

        _fill_ext(ext0, h0p_ref, h0_ref, h0n_ref, i, nt, T)
        _fill_ext(ext1, h1p_ref, h1_ref, h1n_ref, i, nt, T)
        xc = xc_ref[...]
        gates = _lru_gates(xc, w_ref, bias_ref, lam_ref)
        hshift = (ext0[pl.ds(HALO - 1, T), :], ext1[pl.ds(HALO + 1, T), :])
        gs = (g0_ref[...], g1_ref[...])
        dxc = jnp.zeros((T, 512), F32)
        for d in range(2):
            r, ig, a, mult = gates[d]
            db = gs[d]
            da = db * hshift[d]
            dmult = db * (ig * xc)
            di = db * (mult * xc)
            dxc = dxc + db * (mult * ig)
            dloga = da * a - dmult * (a * a / mult)
            lam_d = lam_ref[:, d * 512:(d + 1) * 512]
            dr = dloga * (-LRU_C * _softplus(-lam_d))
            dsp = jnp.sum(dloga * (-LRU_C * r), axis=0, keepdims=True)
            dlam_ref[:, d * 512:(d + 1) * 512] += dsp * (-_sigmoid(-lam_d))
            dpre_ref[:, d * 1024:d * 1024 + 512] = (dr * (r * (1.0 - r))).astype(dpre_ref.dtype)
            dpre_ref[:, d * 1024 + 512:(d + 1) * 1024] = (di * (ig * (1.0 - ig))).astype(dpre_ref.dtype)
            dbias_ref[:, d * 1024:d * 1024 + 512] += jnp.sum(dr * (r * (1.0 - r)), axis=0, keepdims=True)
            dbias_ref[:, d * 1024 + 512:(d + 1) * 1024] += jnp.sum(di * (ig * (1.0 - ig)), axis=0, keepdims=True)
        dxc_ref[...] = dxc + _dot_nt(dpre_ref[...], w_ref[...])

    full = lambda a: pl.BlockSpec(a.shape, lambda i: (0, 0))
    row = pl.BlockSpec((T, 512), lambda i: (i, 0))
    return _call(body, name="lru_bwd_point", grid=(nt,),
                 in_specs=[row] + _halo_specs(T, S, 512, 0) + _halo_specs(T, S, 512, 0) + [row, row, full(w_gate), full(b_gate), full(lam)],
                 out_specs=[row, pl.BlockSpec((T, 2048), lambda i: (i, 0)), full(b_gate), full(lam)],
                 out_shape=[jax.ShapeDtypeStruct((S, 512), F32), jax.ShapeDtypeStruct((S, 2048), MXU_DTYPE),
                            jax.ShapeDtypeStruct(b_gate.shape, F32), jax.ShapeDtypeStruct(lam.shape, F32)],
                 scratch=[pltpu.VMEM((T + 2 * HALO, 512), F32)] * 2,
                 )(xc, h0, h0, h0, h1, h1, h1, g0, g1, w_gate, b_gate, lam)


def conv_bwd(z, dxc, conv_w, T=256):
    S = z.shape[0]
    T = min(T, S)
    nt = S // T

    def body(xp_ref, xcur_ref, xn_ref, dp_ref, dcur_ref, dn_ref, cw_ref, dx_ref, dw_ref, db_ref, extx, extd):
        i = pl.program_id(0)

        @pl.when(i == 0)
        def _():
            dw_ref[...] = jnp.zeros_like(dw_ref)
            db_ref[...] = jnp.zeros_like(db_ref)

        _fill_ext(extx, xp_ref, xcur_ref, xn_ref, i, nt, T)
        _fill_ext(extd, dp_ref, dcur_ref, dn_ref, i, nt, T)
        d = extd[pl.ds(HALO, T), :]
        dx = cw_ref[0:1, :] * extd[pl.ds(HALO + 2, T), :]
        for j in range(1, 4):
            dx = dx + cw_ref[j:j + 1, :] * extd[pl.ds(HALO + 2 - j, T), :]
        dx_ref[...] = dx
        for j in range(4):
            dw_ref[j:j + 1, :] += jnp.sum(d * extx[pl.ds(HALO - 2 + j, T), :], axis=0, keepdims=True)
        db_ref[...] += jnp.sum(d, axis=0, keepdims=True)

    full = lambda a: pl.BlockSpec(a.shape, lambda i: (0, 0))
    row = pl.BlockSpec((T, 512), lambda i: (i, 0))
    return _call(body, name="conv_bwd", grid=(nt,),
                 in_specs=_halo_specs(T, S, 512, 1) + _halo_specs(T, S, 512, 0) + [full(conv_w)],
                 out_specs=[row, full(conv_w), pl.BlockSpec((1, 512), lambda i: (0, 0))],
                 out_shape=[jax.ShapeDtypeStruct((S, 512), F32), jax.ShapeDtypeStruct(conv_w.shape, F32),
                            jax.ShapeDtypeStruct((1, 512), F32)],
                 scratch=[pltpu.VMEM((T + 2 * HALO, 512), F32)] * 2)(z, z, z, dxc, dxc, dxc, conv_w)


def _me_and_peers():
    x, y, c = lax.axis_index("x"), lax.axis_index("y"), lax.axis_index("c")
    peers = []
    for k in range(1, NDEV):
        px, py, pc = x ^ (k >> 2), y ^ ((k >> 1) & 1), c ^ (k & 1)
        peers.append(((px, py, pc), 4 * px + 2 * py + pc))
    return 4 * x + 2 * y + c, peers


def all_gather(arrays):
    n = len(arrays)

    def body(*refs):
        ins, outs = refs[:n], refs[n:2 * n]
        send, recv, loc = refs[2 * n:]
        me, peers = _me_and_peers()
        local = [pltpu.make_async_copy(ins[a], outs[a].at[me], loc.at[a]) for a in range(n)]
        for cp in local:
            cp.start()
        sends = []
        for a in range(n):
            for k, (dev, _) in enumerate(peers):
                cp = pltpu.make_async_remote_copy(src_ref=ins[a], dst_ref=outs[a].at[me], send_sem=send.at[a * (NDEV - 1) + k],
                                                  recv_sem=recv.at[a * (NDEV - 1) + k], device_id=dev,
                                                  device_id_type=pl.DeviceIdType.MESH)
                cp.start()
                sends.append(cp)
        for a in range(n):
            for k, (dev, idx) in enumerate(peers):
                pltpu.make_async_remote_copy(src_ref=ins[a], dst_ref=outs[a].at[idx], send_sem=send.at[a * (NDEV - 1) + k],
                                             recv_sem=recv.at[a * (NDEV - 1) + k], device_id=dev,
                                             device_id_type=pl.DeviceIdType.MESH).wait_recv()
        for cp in sends:
            cp.wait_send()
        for cp in local:
            cp.wait()

    any_spec = pl.BlockSpec(memory_space=pl.ANY)
    return pl.pallas_call(
        body, name="all_gather", in_specs=[any_spec] * n, out_specs=[any_spec] * n,
        out_shape=[jax.ShapeDtypeStruct((NDEV,) + a.shape, a.dtype) for a in arrays],
        scratch_shapes=[pltpu.SemaphoreType.DMA((n * (NDEV - 1),)), pltpu.SemaphoreType.DMA((n * (NDEV - 1),)),
                        pltpu.SemaphoreType.DMA((n,))],
    )(*arrays)


def grad_exchange(gs, gr):
    def body(gs_ref, gr_ref, os_ref, or_ref, send, recv, loc):
        me, peers = _me_and_peers()
        local = [pltpu.make_async_copy(gs_ref.at[me], os_ref.at[me], loc.at[0]),
                 pltpu.make_async_copy(gr_ref, or_ref.at[me], loc.at[1])]
        for cp in local:
            cp.start()
        sends = []
        for k, (dev, idx) in enumerate(peers):
            for a, (src, dst) in enumerate(((gs_ref.at[idx], os_ref.at[me]), (gr_ref, or_ref.at[me]))):
                cp = pltpu.make_async_remote_copy(src_ref=src, dst_ref=dst, send_sem=send.at[a * (NDEV - 1) + k], recv_sem=recv.at[a * (NDEV - 1) + k],
                                                  device_id=dev, device_id_type=pl.DeviceIdType.MESH)
                cp.start()
                sends.append(cp)
        for k, (dev, idx) in enumerate(peers):
            for a, (src, dst) in enumerate(((gs_ref.at[idx], os_ref.at[idx]), (gr_ref, or_ref.at[idx]))):
                pltpu.make_async_remote_copy(src_ref=src, dst_ref=dst, send_sem=send.at[a * (NDEV - 1) + k], recv_sem=recv.at[a * (NDEV - 1) + k],
                                             device_id=dev, device_id_type=pl.DeviceIdType.MESH).wait_recv()
        for cp in sends:
            cp.wait_send()
        for cp in local:
            cp.wait()

    any_spec = pl.BlockSpec(memory_space=pl.ANY)
    return pl.pallas_call(
        body, name="grad_exchange", in_specs=[any_spec] * 2, out_specs=[any_spec] * 2,
        out_shape=[jax.ShapeDtypeStruct(gs.shape, gs.dtype), jax.ShapeDtypeStruct((NDEV,) + gr.shape, gr.dtype)],
        scratch_shapes=[pltpu.SemaphoreType.DMA((2 * (NDEV - 1),)), pltpu.SemaphoreType.DMA((2 * (NDEV - 1),)),
                        pltpu.SemaphoreType.DMA((2,))],
    )(gs, gr)


def adamw(name, gparts, w, m, v, T=128):
    R, C = w.shape
    T = min(T, R)

    def body(g_ref, w_ref, m_ref, v_ref, go_ref, d_ref, mo_ref, vo_ref):
        g = g_ref[0]
        for k in range(1, NDEV):
            g = g + g_ref[k]
        mn = ADAM_B1 * m_ref[...] + (1.0 - ADAM_B1) * g
        vn = ADAM_B2 * v_ref[...] + (1.0 - ADAM_B2) * (g * g)
        m_hat = mn / (1.0 - ADAM_B1 ** ADAM_STEP)
        v_hat = vn / (1.0 - ADAM_B2 ** ADAM_STEP)
        go_ref[...] = g
        d_ref[...] = -ADAM_LR * (m_hat / (jnp.sqrt(v_hat) + ADAM_EPS) + ADAM_WD * w_ref[...])
        mo_ref[...] = mn
        vo_ref[...] = vn

    row = pl.BlockSpec((T, C), lambda i: (i, 0))
    sd = jax.ShapeDtypeStruct((R, C), F32)
    return _call(body, name=name, grid=(R // T,), in_specs=[pl.BlockSpec((NDEV, T, C), lambda i: (0, i, 0)), row, row, row],
                 out_specs=[row] * 4, out_shape=[sd] * 4)(gparts, w, m, v)


PACK_W = 1024
PACK_ROWS = 128


def _pack(arrays, dtype):
    flat = jnp.concatenate([a.astype(dtype).reshape(-1) for a in arrays])
    n = flat.shape[0]
    gran = PACK_W * PACK_ROWS
    total = -(-n // gran) * gran
    return jnp.pad(flat, (0, total - n)).reshape(total // PACK_W, PACK_W)


def _unpack(buf, shapes, lead=()):
    flat = buf.reshape(lead + (-1,))
    out, off = [], 0
    for sh in shapes:
        n = 1
        for s in sh:
            n *= s
        out.append(flat[..., off:off + n].reshape(lead + tuple(sh)))
        off += n
    return out


def _unshard(parts, axis):
    return jnp.concatenate([parts[d] for d in range(NDEV)], axis=axis)


def _split(full, axis):
    return jnp.stack(jnp.split(full, NDEV, axis=axis))


def _slab_cols(w, head, used, n_heads, slab=LANES):
    lead = w.shape[:-1]
    w = w.reshape(lead + (n_heads, head))[..., :used]
    w = jnp.pad(w, [(0, 0)] * len(lead) + [(0, 0), (0, slab - used)])
    return w.reshape(lead + (n_heads * slab,))


def _unslab_cols(g, used, n_heads, slab=LANES):
    lead = g.shape[:-1]
    return g.reshape(lead + (n_heads, slab))[..., :used].reshape(lead + (n_heads * used,))


def _block_diag(w):
    eye = jnp.eye(8, dtype=w.dtype)
    return jnp.einsum("nij,nm->nimj", w, eye).reshape(512, 512)


def _block_diag_t(g):
    g = g.reshape(8, 64, 8, 64)
    return jnp.stack([g[n, :, n, :] for n in range(8)])


BIG = (("e_w_in", 2), ("e_w_out", 1), ("o_w_in", 2), ("o_w_uq", 2), ("o_w_ukv", 2), ("o_w_out", 1),
       ("w_mlp1", 2), ("w_mlp2", 1))
SMALL = (("o_norm_mix", 1), ("o_g_cq", 1), ("o_conv_w", 2), ("o_conv_b", 1), ("o_lru_ba", 2), ("o_lru_bx", 2),
         ("o_lru_lambda", 2))
REPL = ("e_norm_mix", "e_sink", "e_w_pool", "e_pool_scale", "o_g_ckv", "o_lru_wa", "o_lru_wx", "norm_mlp",
        "final_norm")
WEIGHTS = ("e_norm_mix", "e_w_in", "e_sink", "e_w_pool", "e_pool_scale", "e_w_out", "o_norm_mix", "o_w_in", "o_g_cq",
           "o_w_uq", "o_g_ckv", "o_w_ukv", "o_conv_w", "o_conv_b", "o_lru_wa", "o_lru_ba", "o_lru_wx", "o_lru_bx",
           "o_lru_lambda", "o_w_out", "norm_mlp", "w_mlp1", "w_mlp2", "final_norm")


def _mlp_fwd(l, x, g, w1, w2):
    u1, hn = mm_nn(f"mlp1_{l}", [x], [w1], norm_g=g, emit_norm=True, tm=256, tn=1024)
    x2 = mm_nn(f"mlp2_{l}", [u1], [w2], act="relu2", res=x, tm=256, tn=512)
    return x2, (x, u1, hn)


def _mlp_bwd(l, saved, g, w1, w2, dx2):
    x, u1, hn = saved
    du1 = mm_nt(f"mlp2_dx_{l}", [dx2], [w2], relu2_of=u1, out_dtype=MXU_DTYPE, tm=256, tn=1024)
    dw2 = mm_tn(f"mlp2_dw_{l}", u1, dx2, act="relu2")
    dhn = mm_nt(f"mlp1_dx_{l}", [du1], [w1], tm=256, tn=512)
    dw1 = mm_tn(f"mlp1_dw_{l}", hn, du1)
    dx, dg = rms_bwd(f"mlp_norm_bwd_{l}", x, g, dhn, dx2)
    return dx, dg, dw1, dw2


def kernel(x, e_norm_mix, e_w_in, e_sink, e_w_pool, e_pool_scale, e_w_out, o_norm_mix, o_w_in, o_g_cq, o_w_uq, o_g_ckv, o_w_ukv, o_conv_w, o_conv_b, o_lru_wa, o_lru_ba, o_lru_wx, o_lru_bx, o_lru_lambda, o_w_out, norm_mlp, w_mlp1, w_mlp2, final_norm, loss_target, m_e_norm_mix, m_e_w_in, m_e_sink, m_e_w_pool, m_e_pool_scale, m_e_w_out, m_o_norm_mix, m_o_w_in, m_o_g_cq, m_o_w_uq, m_o_g_ckv, m_o_w_ukv, m_o_conv_w, m_o_conv_b, m_o_lru_wa, m_o_lru_ba, m_o_lru_wx, m_o_lru_bx, m_o_lru_lambda, m_o_w_out, m_norm_mlp, m_w_mlp1, m_w_mlp2, m_final_norm, v_e_norm_mix, v_e_w_in, v_e_sink, v_e_w_pool, v_e_pool_scale, v_e_w_out, v_o_norm_mix, v_o_w_in, v_o_g_cq, v_o_w_uq, v_o_g_ckv, v_o_w_ukv, v_o_conv_w, v_o_conv_b, v_o_lru_wa, v_o_lru_ba, v_o_lru_wx, v_o_lru_bx, v_o_lru_lambda, v_o_w_out, v_norm_mlp, v_w_mlp1, v_w_mlp2, v_final_norm):
    W = dict(e_norm_mix=e_norm_mix, e_w_in=e_w_in, e_sink=e_sink, e_w_pool=e_w_pool, e_pool_scale=e_pool_scale, e_w_out=e_w_out, o_norm_mix=o_norm_mix, o_w_in=o_w_in, o_g_cq=o_g_cq, o_w_uq=o_w_uq, o_g_ckv=o_g_ckv, o_w_ukv=o_w_ukv, o_conv_w=o_conv_w, o_conv_b=o_conv_b, o_lru_wa=o_lru_wa, o_lru_ba=o_lru_ba, o_lru_wx=o_lru_wx, o_lru_bx=o_lru_bx, o_lru_lambda=o_lru_lambda, o_w_out=o_w_out, norm_mlp=norm_mlp, w_mlp1=w_mlp1, w_mlp2=w_mlp2, final_norm=final_norm)
    Mo = dict(e_norm_mix=m_e_norm_mix, e_w_in=m_e_w_in, e_sink=m_e_sink, e_w_pool=m_e_w_pool, e_pool_scale=m_e_pool_scale, e_w_out=m_e_w_out, o_norm_mix=m_o_norm_mix, o_w_in=m_o_w_in, o_g_cq=m_o_g_cq, o_w_uq=m_o_w_uq, o_g_ckv=m_o_g_ckv, o_w_ukv=m_o_w_ukv, o_conv_w=m_o_conv_w, o_conv_b=m_o_conv_b, o_lru_wa=m_o_lru_wa, o_lru_ba=m_o_lru_ba, o_lru_wx=m_o_lru_wx, o_lru_bx=m_o_lru_bx, o_lru_lambda=m_o_lru_lambda, o_w_out=m_o_w_out, norm_mlp=m_norm_mlp, w_mlp1=m_w_mlp1, w_mlp2=m_w_mlp2, final_norm=m_final_norm)
    Vo = dict(e_norm_mix=v_e_norm_mix, e_w_in=v_e_w_in, e_sink=v_e_sink, e_w_pool=v_e_w_pool, e_pool_scale=v_e_pool_scale, e_w_out=v_e_w_out, o_norm_mix=v_o_norm_mix, o_w_in=v_o_w_in, o_g_cq=v_o_g_cq, o_w_uq=v_o_w_uq, o_g_ckv=v_o_g_ckv, o_w_ukv=v_o_w_ukv, o_conv_w=v_o_conv_w, o_conv_b=v_o_conv_b, o_lru_wa=v_o_lru_wa, o_lru_ba=v_o_lru_ba, o_lru_wx=v_o_lru_wx, o_lru_bx=v_o_lru_bx, o_lru_lambda=v_o_lru_lambda, o_w_out=v_o_w_out, norm_mlp=v_norm_mlp, w_mlp1=v_w_mlp1, w_mlp2=v_w_mlp2, final_norm=v_final_norm)

    S = x.shape[1]
    x0 = x[0]
    target = loss_target[0]

    big_g, small_g = all_gather([_pack([W[n] for n, _ in BIG], MXU_DTYPE), _pack([W[n] for n, _ in SMALL], F32)])
    full = {}
    for (n, ax), parts in zip(BIG, _unpack(big_g, [W[n].shape for n, _ in BIG], (NDEV,))):
        full[n] = _unshard(parts, ax)
    for (n, ax), parts in zip(SMALL, _unpack(small_g, [W[n].shape for n, _ in SMALL], (NDEV,))):
        full[n] = _unshard(parts, ax)

    def even_in(w):
        return jnp.concatenate([_slab_cols(w[:, 0:512], 64, 64, 8), _slab_cols(w[:, 512:640], 64, 64, 2),
                                _slab_cols(w[:, 640:768], 64, 64, 2), w[:, 768:1280]], axis=1)

    def even_in_t(g):
        return jnp.concatenate([_unslab_cols(g[:, 0:1024], 64, 8), _unslab_cols(g[:, 1024:1280], 64, 2),
                                _unslab_cols(g[:, 1280:1536], 64, 2), g[:, 1536:2048]], axis=1)

    def slab_rows(w, n_heads):
        return _slab_cols(w.T, 64, 64, n_heads).T

    def unslab_rows(g, n_heads):
        return _unslab_cols(g.T, 64, n_heads).T

    def odd_in(w):
        return jnp.concatenate([w[:, 0:384], jnp.pad(w[:, 384:416], ((0, 0), (0, 96))), w[:, 416:1440]], axis=1)

    def odd_in_t(g):
        return jnp.concatenate([g[:, 0:416], g[:, 512:1536]], axis=1)

    def uq(w):
        return _slab_cols(w, 96, 96, 8)

    def ukv(w):
        w = w.reshape(128, 8, 128)
        pad = lambda t: jnp.pad(t, ((0, 0), (0, 0), (0, 64))).reshape(128, 1024)
        return pad(w[:, :, :64]), pad(w[:, :, 64:])

    def ukv_t(gk, gv):
        gk = gk.reshape(128, 8, 128)[:, :, :64]
        gv = gv.reshape(128, 8, 128)[:, :, :64]
        return jnp.concatenate([gk, gv], axis=2).reshape(128, 1024)

    tabs_swa = rope_tables(S, 0, SWA_HALF)
    tabs_mq = rope_tables(S, 64, MLA_HALF)
    tabs_mk = rope_tables(S, 0, MLA_HALF)
    row = lambda v: v.reshape(1, -1)

    saved = []
    xcur = x0
    for l in range(4):
        j = l // 2
        if l % 2 == 0:
            w_in = even_in(full["e_w_in"][j])
            w_out = full["e_w_out"][j]
            w_out_a, w_out_b = slab_rows(w_out[0:512], 8), w_out[512:1024]
            w_pool = W["e_w_pool"][j].astype(MXU_DTYPE)
            z, h = mm_nn(f"e_in_{j}", [xcur], [w_in], norm_g=row(W["e_norm_mix"][j]), emit_norm=True, tm=256, tn=1024)
            qkv = swa_prep(z, tabs_swa)
            ya, lse = swa_fwd(qkv, W["e_sink"][j])
            yb = pool_fwd(z, w_pool, row(W["e_pool_scale"][j]))
            x1 = mm_nn(f"e_out_{j}", [ya, yb], [w_out_a, w_out_b], res=xcur)
            mix = (xcur, z, h, qkv, lse, ya, yb, w_in, w_out_a, w_out_b, w_pool)
        else:
            w_in = odd_in(full["o_w_in"][j])
            w_out = full["o_w_out"][j]
            w_out_a, w_out_b = slab_rows(w_out[0:512], 8), w_out[512:1024]
            w_uq = uq(full["o_w_uq"][j])
            w_k, w_v = ukv(full["o_w_ukv"][j])
            g_cq, g_ckv = row(full["o_g_cq"][j]), row(W["o_g_ckv"][j])
            w_gate = jnp.concatenate([_block_diag(W["o_lru_wa"][j, 0]), _block_diag(W["o_lru_wx"][j, 0]),
                                      _block_diag(W["o_lru_wa"][j, 1]), _block_diag(W["o_lru_wx"][j, 1])], axis=1).astype(MXU_DTYPE)
            b_gate = jnp.concatenate([full["o_lru_ba"][j, 0], full["o_lru_bx"][j, 0], full["o_lru_ba"][j, 1],
                                      full["o_lru_bx"][j, 1]]).reshape(1, 2048)
            lam = full["o_lru_lambda"][j].reshape(1, 1024)
            conv_w, conv_b = full["o_conv_w"][j], row(full["o_conv_b"][j])
            z, h = mm_nn(f"o_in_{j}", [xcur], [w_in], norm_g=row(full["o_norm_mix"][j]), emit_norm=True, tm=256, tn=512)
            q, k, v, nq, nkv = mla_prep(z, g_cq, g_ckv, w_uq, w_k, w_v, tabs_mq, tabs_mk)
            yc, lse = mla_fwd(q, k, v)
            xc, a0, b0, a1, b1 = lru_pre(z, conv_w, conv_b, w_gate, b_gate, lam)
            h0, h1 = lru_scan(f"lru_scan_fwd_{j}", a0, b0, a1, b1, adjoint=False)
            yd = lru_gate(h0, h1, z)
            x1 = mm_nn(f"o_out_{j}", [yc, yd], [w_out_a, w_out_b], res=xcur)
            mix = (xcur, z, h, q, k, v, nq, nkv, yc, lse, xc, a0, a1, h0, h1, yd, w_in, w_out_a, w_out_b, w_uq, w_k, w_v,
                   g_cq, g_ckv, w_gate, b_gate, lam, conv_w)
        xcur, mlp = _mlp_fwd(l, x1, row(W["norm_mlp"][l]), full["w_mlp1"][l], full["w_mlp2"][l])
        saved.append((mix, mlp))

    loss_row, dx, dg_final = loss_head(xcur, row(W["final_norm"]), target)
    loss = lax.psum(loss_row[0, 0], MESH_AXES)

    G = {n: [None] * W[n].shape[0] for n in WEIGHTS if n != "final_norm"}
    G["final_norm"] = dg_final.reshape(-1)
    for l in reversed(range(4)):
        j = l // 2
        mix, mlp = saved[l]
        dx, dg, dw1, dw2 = _mlp_bwd(l, mlp, row(W["norm_mlp"][l]), full["w_mlp1"][l], full["w_mlp2"][l], dx)
        G["norm_mlp"][l], G["w_mlp1"][l], G["w_mlp2"][l] = dg.reshape(-1), dw1, dw2
        if l % 2 == 0:
            xin, z, h, qkv, lse, ya, yb, w_in, w_out_a, w_out_b, w_pool = mix
            w_out_cat = jnp.concatenate([w_out_a, w_out_b], axis=0)
            dycat = mm_nt(f"e_out_dx_{j}", [dx], [w_out_cat])
            G["e_w_out"][j] = jnp.concatenate([unslab_rows(mm_tn(f"e_out_dwa_{j}", ya, dx), 8),
                                               mm_tn(f"e_out_dwb_{j}", yb, dx)], axis=0)
            dq, dk, dv, dsink = swa_bwd(qkv, W["e_sink"][j], lse, dycat, tabs_swa)
            du, dwp, dsc = pool_bwd(z, dycat, w_pool, row(W["e_pool_scale"][j]))
            G["e_sink"][j], G["e_w_pool"][j], G["e_pool_scale"][j] = dsink[0, 0:8], dwp, dsc.reshape(-1)
            dh = mm_nt(f"e_in_dx_{j}", [dq, dk, dv, du], [(w_in, 0, 1024), (w_in, 4, 256), (w_in, 5, 256), (w_in, 3, 512)])
            G["e_w_in"][j] = even_in_t(jnp.concatenate(
                [mm_tn(f"e_in_dwq_{j}", h, dq), mm_tn(f"e_in_dwk_{j}", h, dk), mm_tn(f"e_in_dwv_{j}", h, dv),
                 mm_tn(f"e_in_dwu_{j}", h, du)], axis=1))
            dx, dg = rms_bwd(f"e_norm_bwd_{j}", xin, row(W["e_norm_mix"][j]), dh, dx)
            G["e_norm_mix"][j] = dg.reshape(-1)
        else:
            (xin, z, h, q, k, v, nq, nkv, yc, lse, xc, a0, a1, h0, h1, yd, w_in, w_out_a, w_out_b, w_uq, w_k, w_v,
             g_cq, g_ckv, w_gate, b_gate, lam, conv_w) = mix
            w_out_cat = jnp.concatenate([w_out_a, w_out_b], axis=0)
            dycat = mm_nt(f"o_out_dx_{j}", [dx], [w_out_cat])
            G["o_w_out"][j] = jnp.concatenate([unslab_rows(mm_tn(f"o_out_dwa_{j}", yc, dx), 8),
                                               mm_tn(f"o_out_dwb_{j}", yd, dx)], axis=0)
            delta = mla_delta(yc, dycat)
            dq = mla_bwd_q(q, k, v, dycat, lse, delta)
            dk, dv = mla_bwd_kv(q, k, v, dycat, lse, delta)
            dza, dqp, dgq, dgkv = mla_prep_bwd(z, g_cq, g_ckv, w_uq, w_k, w_v, dq, dk, dv, tabs_mq, tabs_mk)
            G["o_g_cq"][j], G["o_g_ckv"][j] = dgq.reshape(-1), dgkv.reshape(-1)
            G["o_w_uq"][j] = _unslab_cols(mm_tn(f"o_uq_dw_{j}", nq, dqp), 96, 8)
            G["o_w_ukv"][j] = ukv_t(mm_tn(f"o_uk_dw_{j}", nkv, dk), mm_tn(f"o_uv_dw_{j}", nkv, dv))
            dxg, dhh = lru_gate_bwd(h0, h1, z, dycat)
            g1, g0 = lru_scan(f"lru_scan_bwd_{j}", a1, dhh, a0, dhh, adjoint=True)
            dxc, dpre, dbias, dlam = lru_bwd_point(xc, h0, h1, g0, g1, w_gate, b_gate, lam)
            dwg = mm_tn(f"o_gate_dw_{j}", xc, dpre)
            G["o_lru_wa"][j] = jnp.stack([_block_diag_t(dwg[:, 0:512]), _block_diag_t(dwg[:, 1024:1536])])
            G["o_lru_wx"][j] = jnp.stack([_block_diag_t(dwg[:, 512:1024]), _block_diag_t(dwg[:, 1536:2048])])
            G["o_lru_ba"][j] = jnp.stack([dbias[0, 0:512], dbias[0, 1024:1536]])
            G["o_lru_bx"][j] = jnp.stack([dbias[0, 512:1024], dbias[0, 1536:2048]])
            G["o_lru_lambda"][j] = dlam.reshape(2, 512)
            dxr, dcw, dcb = conv_bwd(z, dxc, conv_w)
            G["o_conv_w"][j], G["o_conv_b"][j] = dcw, dcb.reshape(-1)
            dh = mm_nt(f"o_in_dx_{j}", [dza, dxr, dxg], [(w_in, 0, 512), (w_in, 1, 512), (w_in, 2, 512)])
            G["o_w_in"][j] = odd_in_t(jnp.concatenate(
                [mm_tn(f"o_in_dwa_{j}", h, dza), mm_tn(f"o_in_dwr_{j}", h, dxr), mm_tn(f"o_in_dwg_{j}", h, dxg)], axis=1))
            dx, dg = rms_bwd(f"o_norm_bwd_{j}", xin, row(full["o_norm_mix"][j]), dh, dx)
            G["o_norm_mix"][j] = dg.reshape(-1)
    grad_x = dx[None]
    G = {n: (g if n == "final_norm" else jnp.stack(g)) for n, g in G.items()}

    sharded = BIG + SMALL
    parts = [_split(G[n], ax) for n, ax in sharded]
    gs = jnp.stack([_pack([p[d] for p in parts], F32) for d in range(NDEV)])
    gr = _pack([G[n] for n in REPL], F32)
    gs_all, gr_all = grad_exchange(gs, gr)
    outs_s = adamw("adamw_sharded", gs_all, *[_pack([T[n] for n, _ in sharded], F32) for T in (W, Mo, Vo)])
    outs_r = adamw("adamw_replicated", gr_all, *[_pack([T[n] for n in REPL], F32) for T in (W, Mo, Vo)])
    res = [dict(), dict(), dict(), dict()]
    for kind in range(4):
        for (n, _), a in zip(sharded, _unpack(outs_s[kind], [W[n].shape for n, _ in sharded])):
            res[kind][n] = a
        for n, a in zip(REPL, _unpack(outs_r[kind], [W[n].shape for n in REPL])):
            res[kind][n] = a
    return (loss, grad_x, *[res[0][n] for n in WEIGHTS], *[res[1][n] for n in WEIGHTS],
            *[res[2][n] for n in WEIGHTS], *[res[3][n] for n in WEIGHTS])
```

```python
import functools

import jax
import jax.numpy as jnp
from jax import lax
from jax.experimental import pallas as pl
from jax.experimental.pallas import tpu as pltpu

F32 = jnp.float32
MXU_DTYPE = jnp.bfloat16
EPS = 1e-6
ROPE_THETA = 10000.0
NDEV = 8
LANES = 128
VMEM_LIMIT = 48 * 1024 * 1024

D_MODEL = 1024
D_FF = 4096
LRU_C = 8.0
POOL_WINDOWS = (2, 4, 8, 16)
HALO = 8
MLA_SCALE = 96.0 ** -0.5

ADAM_LR, ADAM_B1, ADAM_B2, ADAM_EPS, ADAM_WD, ADAM_STEP = 0.001, 0.9, 0.999, 1e-08, 0.01, 10


def _mx(v):
    return v.astype(MXU_DTYPE)


def _call(body, *, name, grid, in_specs, out_specs, out_shape, scratch=()):
    return pl.pallas_call(
        body, name=name, grid=grid, in_specs=in_specs, out_specs=out_specs, out_shape=out_shape,
        scratch_shapes=list(scratch),
        compiler_params=pltpu.CompilerParams(
            dimension_semantics=("arbitrary",) * len(grid), vmem_limit_bytes=VMEM_LIMIT),
    )


def _dot(a, b):
    return jnp.dot(a, b, preferred_element_type=F32)


def _dot_nt(a, b):
    return lax.dot_general(a, b, (((1,), (1,)), ((), ())), preferred_element_type=F32)


def _dot_tn(a, b):
    return lax.dot_general(a, b, (((0,), (0,)), ((), ())), preferred_element_type=F32)


def _rms(x, g):
    r = lax.rsqrt(jnp.mean(x * x, axis=-1, keepdims=True) + EPS)
    return (x * r) * g


def _rms_bwd(x, g, dy):
    r = lax.rsqrt(jnp.mean(x * x, axis=-1, keepdims=True) + EPS)
    xh = x * r
    dyg = dy * g
    dx = r * (dyg - xh * jnp.mean(dyg * xh, axis=-1, keepdims=True))
    return dx, jnp.sum(dy * xh, axis=0, keepdims=True)


def _sigmoid(x):
    return 1.0 / (1.0 + jnp.exp(-x))


def _log1p(e):
    u = 1.0 + e
    d = u - 1.0
    return jnp.where(d == 0.0, e, jnp.log(u) * (e / jnp.where(d == 0.0, 1.0, d)))


def _softplus(x):
    return jnp.maximum(x, 0.0) + _log1p(jnp.exp(-jnp.abs(x)))


def _expm1(x):
    u = jnp.exp(x)
    lu = jnp.log(u)
    safe = jnp.where((lu == 0.0) | (u == 0.0), 1.0, lu)
    return jnp.where(u == 1.0, x, jnp.where(u == 0.0, -1.0, (u - 1.0) * x / safe))


_GELU_K = 0.7978845608028654


def _gelu(x):
    return 0.5 * x * (1.0 + jnp.tanh(_GELU_K * (x + 0.044715 * x * x * x)))


def _gelu_grad(x):
    t = jnp.tanh(_GELU_K * (x + 0.044715 * x * x * x))
    return 0.5 * (1.0 + t) + 0.5 * x * (1.0 - t * t) * _GELU_K * (1.0 + 3.0 * 0.044715 * x * x)


def _rope(x, c, sa, sb, half):
    return x * c + pltpu.roll(x, LANES - half, 1) * sa + pltpu.roll(x, half, 1) * sb


def _rope_t(d, c, sa, sb, half):
    return d * c - pltpu.roll(d, LANES - half, 1) * sa - pltpu.roll(d, half, 1) * sb


def _as_cols(a):
    return a if isinstance(a, tuple) else (a, 0, a.shape[1])


def mm_nn(name, a_list, b_list, *, res=None, act=None, norm_g=None, emit_norm=False,
          out_dtype=F32, tm=256, tn=512):
    a_list = [_as_cols(a) for a in a_list]
    M, N = a_list[0][0].shape[0], b_list[0].shape[1]
    tm, tn = min(tm, M), min(tn, N)
    na = len(a_list)

    def body(*refs):
        a_refs, b_refs = refs[:na], refs[na:2 * na]
        k = 2 * na
        g_ref = res_ref = hn_ref = None
        if norm_g is not None:
            g_ref = refs[k]
            k += 1
        if res is not None:
            res_ref = refs[k]
            k += 1
        o_ref = refs[k]
        if emit_norm:
            hn_ref = refs[k + 1]
        acc = None
        for a_ref, b_ref in zip(a_refs, b_refs):
            a = a_ref[...]
            if g_ref is not None:
                a = _rms(a.astype(F32), g_ref[...])
                if hn_ref is not None:
                    hn_ref[...] = a.astype(hn_ref.dtype)
            if act == "relu2":
                a = jnp.maximum(a, 0.0)
                a = a * a
            d = _dot(_mx(a), _mx(b_ref[...]))
            acc = d if acc is None else acc + d
        if res_ref is not None:
            acc = acc + res_ref[...]
        o_ref[...] = acc.astype(o_ref.dtype)

    in_specs = [pl.BlockSpec((tm, w), functools.partial(lambda i, j, cb: (i, cb), cb=cb)) for (_, cb, w) in a_list]
    in_specs += [pl.BlockSpec((b.shape[0], tn), lambda i, j: (0, j)) for b in b_list]
    args = [a for (a, _, _) in a_list] + list(b_list)
    if norm_g is not None:
        in_specs.append(pl.BlockSpec((1, norm_g.shape[1]), lambda i, j: (0, 0)))
        args.append(norm_g)
    if res is not None:
        in_specs.append(pl.BlockSpec((tm, tn), lambda i, j: (i, j)))
        args.append(res)
    out_specs = [pl.BlockSpec((tm, tn), lambda i, j: (i, j))]
    out_shape = [jax.ShapeDtypeStruct((M, N), out_dtype)]
    if emit_norm:
        K = a_list[0][2]
        out_specs.append(pl.BlockSpec((tm, K), lambda i, j: (i, 0)))
        out_shape.append(jax.ShapeDtypeStruct((M, K), MXU_DTYPE))
    out = _call(body, name=name, grid=(M // tm, N // tn), in_specs=in_specs, out_specs=out_specs,
                out_shape=out_shape)(*args)
    return out if emit_norm else out[0]


def mm_nt(name, a_list, b_list, *, relu2_of=None, out_dtype=F32, tm=256, tn=512):
    a_list = [_as_cols(a) for a in a_list]
    b_list = [_as_cols(b) for b in b_list]
    M, N = a_list[0][0].shape[0], b_list[0][0].shape[0]
    tm, tn = min(tm, M), min(tn, N)
    na = len(a_list)

    def body(*refs):
        a_refs, b_refs = refs[:na], refs[na:2 * na]
        u_ref = refs[2 * na] if relu2_of is not None else None
        o_ref = refs[-1]
        acc = None
        for a_ref, b_ref in zip(a_refs, b_refs):
            d = _dot_nt(_mx(a_ref[...]), _mx(b_ref[...]))
            acc = d if acc is None else acc + d
        if u_ref is not None:
            acc = acc * (2.0 * jnp.maximum(u_ref[...], 0.0))
        o_ref[...] = acc.astype(o_ref.dtype)

    in_specs = [pl.BlockSpec((tm, w), functools.partial(lambda i, j, cb: (i, cb), cb=cb)) for (_, cb, w) in a_list]
    in_specs += [pl.BlockSpec((tn, w), functools.partial(lambda i, j, cb: (j, cb), cb=cb)) for (_, cb, w) in b_list]
    args = [a for (a, _, _) in a_list] + [b for (b, _, _) in b_list]
    if relu2_of is not None:
        in_specs.append(pl.BlockSpec((tm, tn), lambda i, j: (i, j)))
        args.append(relu2_of)
    return _call(body, name=name, grid=(M // tm, N // tn), in_specs=in_specs,
                 out_specs=pl.BlockSpec((tm, tn), lambda i, j: (i, j)),
                 out_shape=jax.ShapeDtypeStruct((M, N), out_dtype))(*args)


def mm_tn(name, a, b, *, act=None, tm=512, tn=512, tk=512):
    a, acb, Ma = _as_cols(a)
    b, bcb, Nb = _as_cols(b)
    S = a.shape[0]
    tm, tn, tk = min(tm, Ma), min(tn, Nb), min(tk, S)
    a0, b0 = acb * (Ma // tm), bcb * (Nb // tn)

    def body(a_ref, b_ref, o_ref):
        @pl.when(pl.program_id(2) == 0)
        def _():
            o_ref[...] = jnp.zeros_like(o_ref)

        av = a_ref[...]
        if act == "relu2":
            av = jnp.maximum(av, 0.0)
            av = av * av
        o_ref[...] += _dot_tn(_mx(av), _mx(b_ref[...]))

    return _call(body, name=name, grid=(Ma // tm, Nb // tn, S // tk),
                 in_specs=[pl.BlockSpec((tk, tm), lambda i, j, k: (k, a0 + i)),
                           pl.BlockSpec((tk, tn), lambda i, j, k: (k, b0 + j))],
                 out_specs=pl.BlockSpec((tm, tn), lambda i, j, k: (i, j)),
                 out_shape=jax.ShapeDtypeStruct((Ma, Nb), F32))(a, b)


def rms_bwd(name, x, g, dh, dres, T=256):
    S, D = x.shape
    T = min(T, S)

    def body(x_ref, g_ref, dh_ref, dres_ref, dx_ref, dg_ref):
        @pl.when(pl.program_id(0) == 0)
        def _():
            dg_ref[...] = jnp.zeros_like(dg_ref)

        dx, dg = _rms_bwd(x_ref[...], g_ref[...], dh_ref[...])
        dx_ref[...] = dres_ref[...] + dx
        dg_ref[...] += dg

    row = pl.BlockSpec((T, D), lambda i: (i, 0))
    vec = pl.BlockSpec((1, D), lambda i: (0, 0))
    return _call(body, name=name, grid=(S // T,), in_specs=[row, vec, row, row], out_specs=[row, vec],
                 out_shape=[jax.ShapeDtypeStruct((S, D), F32), jax.ShapeDtypeStruct((1, D), F32)])(x, g, dh, dres)


def loss_head(x, g, target, T=256):
    S, D = x.shape
    T = min(T, S)

    def body(x_ref, g_ref, t_ref, loss_ref, dx_ref, dg_ref):
        @pl.when(pl.program_id(0) == 0)
        def _():
            dg_ref[...] = jnp.zeros_like(dg_ref)
            loss_ref[...] = jnp.zeros_like(loss_ref)

        x = x_ref[...]
        err = _rms(x, g_ref[...]) - t_ref[...]
        loss_ref[...] += 0.5 * jnp.sum(jnp.sum(err * err, axis=-1, keepdims=True) / D, axis=0, keepdims=True)
        dx, dg = _rms_bwd(x, g_ref[...], err / D)
        dx_ref[...] = dx
        dg_ref[...] += dg

    row = pl.BlockSpec((T, D), lambda i: (i, 0))
    vec = pl.BlockSpec((1, D), lambda i: (0, 0))
    return _call(body, name="loss_head", grid=(S // T,), in_specs=[row, vec, row],
                 out_specs=[pl.BlockSpec((1, LANES), lambda i: (0, 0)), row, vec],
                 out_shape=[jax.ShapeDtypeStruct((1, LANES), F32), jax.ShapeDtypeStruct((S, D), F32),
                            jax.ShapeDtypeStruct((1, D), F32)])(x, g, target)


def rope_tables(S, lo, half):
    inv = ROPE_THETA ** (-jnp.arange(half, dtype=F32) / half)
    ang = jnp.arange(S, dtype=F32)[:, None] * inv[None, :]
    cos, sin = jnp.cos(ang), jnp.sin(ang)
    one = lambda n: jnp.ones((S, n), F32)
    zero = lambda n: jnp.zeros((S, n), F32)
    hi = LANES - lo - 2 * half
    c = jnp.concatenate([one(lo), cos, cos, one(hi)], axis=1)
    sa = jnp.concatenate([zero(lo), -sin, zero(half), zero(hi)], axis=1)
    sb = jnp.concatenate([zero(lo), zero(half), sin, zero(hi)], axis=1)
    return c, sa, sb


SWA_BLOCK = 128
SWA_HALF = 32


def swa_prep(z, tabs, T=256):
    S = z.shape[0]
    T = min(T, S)

    def body(z_ref, c_ref, sa_ref, sb_ref, o_ref):
        c, sa, sb = c_ref[...], sa_ref[...], sb_ref[...]
        for s in range(10):
            sl = slice(s * LANES, (s + 1) * LANES)
            y = _rope(z_ref[:, sl], c, sa, sb, SWA_HALF)
            if s < 8:
                y = y * 0.125
            o_ref[:, sl] = y.astype(o_ref.dtype)
        o_ref[:, 1280:1536] = z_ref[:, 1280:1536].astype(o_ref.dtype)

    tab = pl.BlockSpec((T, LANES), lambda i: (i, 0))
    return _call(body, name="swa_prep", grid=(S // T,),
                 in_specs=[pl.BlockSpec((T, 1536), lambda i: (i, 0)), tab, tab, tab],
                 out_specs=pl.BlockSpec((T, 1536), lambda i: (i, 0)),
                 out_shape=jax.ShapeDtypeStruct((S, 1536), MXU_DTYPE))(z, *tabs)


def _swa_valid(n, S):
    B = SWA_BLOCK
    i = lax.broadcasted_iota(jnp.int32, (B, 3 * B), 0)
    j = lax.broadcasted_iota(jnp.int32, (B, 3 * B), 1)
    kpos = j + (n - 1) * B
    return (jnp.abs(j - B - i) <= B) & (kpos >= 0) & (kpos < S)


def swa_fwd(qkv, sink):
    S = qkv.shape[0]
    B = SWA_BLOCK
    nb = S // B

    def body(sink_ref, q_ref, kp_ref, kc_ref, kn_ref, vp_ref, vc_ref, vn_ref, o_ref, st_ref):
        n = pl.program_id(0)
        valid = _swa_valid(n, S)
        lane = lax.broadcasted_iota(jnp.int32, (B, LANES), 1)
        st = jnp.zeros((B, LANES), F32)
        for hk in range(2):
            sl = slice(hk * LANES, (hk + 1) * LANES)
            k3 = jnp.concatenate([kp_ref[:, sl], kc_ref[:, sl], kn_ref[:, sl]], axis=0)
            v3 = jnp.concatenate([vp_ref[:, sl], vc_ref[:, sl], vn_ref[:, sl]], axis=0)
            for g in range(4):
                h = hk * 4 + g
                hs = slice(h * LANES, (h + 1) * LANES)
                s = jnp.where(valid, _dot_nt(q_ref[:, hs], k3), -jnp.inf)
                sk = sink_ref[h]
                m = jnp.maximum(jnp.max(s, axis=-1, keepdims=True), sk)
                p = jnp.exp(s - m)
                den = jnp.sum(p, axis=-1, keepdims=True) + jnp.exp(sk - m)
                p = p / den
                o_ref[:, hs] = _dot(_mx(p), v3).astype(o_ref.dtype)
                st = jnp.where(lane == h, m + jnp.log(den), st)
        st_ref[...] = st

    kv = lambda cb, d: pl.BlockSpec((B, 2 * LANES), lambda n: (jnp.clip(n + d, 0, nb - 1), cb))
    return _call(body, name="swa_fwd", grid=(nb,),
                 in_specs=[pl.BlockSpec(memory_space=pltpu.SMEM),
                           pl.BlockSpec((B, 1024), lambda n: (n, 0)),
                           kv(4, -1), kv(4, 0), kv(4, 1), kv(5, -1), kv(5, 0), kv(5, 1)],
                 out_specs=[pl.BlockSpec((B, 1024), lambda n: (n, 0)), pl.BlockSpec((B, LANES), lambda n: (n, 0))],
                 out_shape=[jax.ShapeDtypeStruct((S, 1024), MXU_DTYPE), jax.ShapeDtypeStruct((S, LANES), F32)],
                 )(sink, qkv, qkv, qkv, qkv, qkv, qkv, qkv)


def swa_bwd(qkv, sink, lse, dycat, tabs):
    S = qkv.shape[0]
    B = SWA_BLOCK
    nb = S // B

    def body(sink_ref, q_ref, kp_ref, kc_ref, kn_ref, vp_ref, vc_ref, vn_ref, do_ref, st_ref,
             cq_ref, saq_ref, sbq_ref, ck_ref, sak_ref, sbk_ref,
             dq_ref, dk_ref, dv_ref, dsink_ref, dk_acc, dv_acc):
        n = pl.program_id(0)

        @pl.when(n == 0)
        def _():
            dk_acc[...] = jnp.zeros_like(dk_acc)
            dv_acc[...] = jnp.zeros_like(dv_acc)
            dsink_ref[...] = jnp.zeros_like(dsink_ref)

        @pl.when(n < nb)
        def _():
            valid = _swa_valid(n, S)
            lane = lax.broadcasted_iota(jnp.int32, (B, LANES), 1)
            lane1 = lax.broadcasted_iota(jnp.int32, (1, LANES), 1)
            st = st_ref[...]
            cq, saq, sbq = cq_ref[...], saq_ref[...], sbq_ref[...]
            dsink = jnp.zeros((1, LANES), F32)
            for hk in range(2):
                sl = slice(hk * LANES, (hk + 1) * LANES)
                k3 = jnp.concatenate([kp_ref[:, sl], kc_ref[:, sl], kn_ref[:, sl]], axis=0)
                v3 = jnp.concatenate([vp_ref[:, sl], vc_ref[:, sl], vn_ref[:, sl]], axis=0)
                dk3 = jnp.zeros((3 * B, LANES), F32)
                dv3 = jnp.zeros((3 * B, LANES), F32)
                for g in range(4):
                    h = hk * 4 + g
                    hs = slice(h * LANES, (h + 1) * LANES)
                    q = q_ref[:, hs]
                    do = _mx(do_ref[:, hs])
                    lse_h = jnp.sum(jnp.where(lane == h, st, 0.0), axis=-1, keepdims=True)
                    p = jnp.where(valid, jnp.exp(_dot_nt(q, k3) - lse_h), 0.0)
                    dp = _dot_nt(do, v3)
                    dsum = jnp.sum(p * dp, axis=-1, keepdims=True)
                    ds = _mx(p * (dp - dsum))
                    dq_ref[:, hs] = _rope_t(_dot(ds, k3) * 0.125, cq, saq, sbq, SWA_HALF)
                    dk3 = dk3 + _dot_tn(ds, q)
                    dv3 = dv3 + _dot_tn(_mx(p), do)
                    dsk = -jnp.sum(jnp.exp(sink_ref[h] - lse_h) * dsum, axis=0, keepdims=True)
                    dsink = jnp.where(lane1 == h, dsk, dsink)
                dk_acc[:, sl] += dk3
                dv_acc[:, sl] += dv3
            dsink_ref[...] += dsink

        ck, sak, sbk = ck_ref[...], sak_ref[...], sbk_ref[...]
        for hk in range(2):
            sl = slice(hk * LANES, (hk + 1) * LANES)
            dk_ref[:, sl] = _rope_t(dk_acc[0:B, sl], ck, sak, sbk, SWA_HALF)
        dv_ref[...] = dv_acc[0:B, :]
        for acc in (dk_acc, dv_acc):
            acc[0:B, :] = acc[B:2 * B, :]
            acc[B:2 * B, :] = acc[2 * B:3 * B, :]
            acc[2 * B:3 * B, :] = jnp.zeros((B, 2 * LANES), F32)

    qn = lambda n: jnp.minimum(n, nb - 1)
    kv = lambda cb, d: pl.BlockSpec((B, 2 * LANES), lambda n: (jnp.clip(qn(n) + d, 0, nb - 1), cb))
    qrow = lambda w: pl.BlockSpec((B, w), lambda n: (qn(n), 0))
    krow = lambda w: pl.BlockSpec((B, w), lambda n: (jnp.maximum(n - 1, 0), 0))
    return _call(body, name="swa_bwd", grid=(nb + 1,),
                 in_specs=[pl.BlockSpec(memory_space=pltpu.SMEM), qrow(1024),
                           kv(4, -1), kv(4, 0), kv(4, 1), kv(5, -1), kv(5, 0), kv(5, 1),
                           qrow(1024), qrow(LANES),
                           qrow(LANES), qrow(LANES), qrow(LANES), krow(LANES), krow(LANES), krow(LANES)],
                 out_specs=[qrow(1024), krow(2 * LANES), krow(2 * LANES), pl.BlockSpec((1, LANES), lambda n: (0, 0))],
                 out_shape=[jax.ShapeDtypeStruct((S, 1024), F32), jax.ShapeDtypeStruct((S, 2 * LANES), F32),
                            jax.ShapeDtypeStruct((S, 2 * LANES), F32), jax.ShapeDtypeStruct((1, LANES), F32)],
                 scratch=[pltpu.VMEM((3 * B, 2 * LANES), F32), pltpu.VMEM((3 * B, 2 * LANES), F32)],
                 )(sink, qkv, qkv, qkv, qkv, qkv, qkv, qkv, dycat, lse, *tabs, *tabs)


def _halo_specs(T, S, w, cb):
    r = T // HALO
    last = S // HALO - 1
    return [pl.BlockSpec((HALO, w), lambda i: (jnp.maximum(i * r - 1, 0), cb)),
            pl.BlockSpec((T, w), lambda i: (i, cb)),
            pl.BlockSpec((HALO, w), lambda i: (jnp.minimum((i + 1) * r, last), cb))]


def _fill_ext(ext, prev_ref, cur_ref, next_ref, i, nt, T):
    ext[0:HALO, :] = jnp.where(i > 0, prev_ref[...], 0.0).astype(F32)
    ext[HALO:HALO + T, :] = cur_ref[...].astype(F32)
    ext[HALO + T:2 * HALO + T, :] = jnp.where(i < nt - 1, next_ref[...], 0.0).astype(F32)


def _pool_cnt(t, half, S):
    return (jnp.clip(t + half, 0, S) - jnp.clip(t - half, 0, S)).astype(F32)


def pool_fwd(z, w_pool, scale, T=256):
    S = z.shape[0]
    T = min(T, S)
    nt = S // T

    def body(up_ref, uc_ref, un_ref, w_ref, sc_ref, o_ref, ext):
        i = pl.program_id(0)
        _fill_ext(ext, up_ref, uc_ref, un_ref, i, nt, T)
        t = i * T + lax.broadcasted_iota(jnp.int32, (T, 1), 0)
        for g, win in enumerate(POOL_WINDOWS):
            half = win // 2
            sl = slice(g * LANES, (g + 1) * LANES)
            acc = ext[pl.ds(HALO - half, T), sl]
            for off in range(-half + 1, half):
                acc = acc + ext[pl.ds(HALO + off, T), sl]
            d = acc / _pool_cnt(t, half, S) - ext[pl.ds(HALO, T), sl]
            o_ref[:, sl] = (_dot(_mx(d), w_ref[g]) * sc_ref[:, sl]).astype(o_ref.dtype)

    return _call(body, name="pool_fwd", grid=(nt,),
                 in_specs=_halo_specs(T, S, 512, 3) + [pl.BlockSpec((4, LANES, LANES), lambda i: (0, 0, 0)),
                                                      pl.BlockSpec((1, 512), lambda i: (0, 0))],
                 out_specs=pl.BlockSpec((T, 512), lambda i: (i, 0)),
                 out_shape=jax.ShapeDtypeStruct((S, 512), MXU_DTYPE),
                 scratch=[pltpu.VMEM((T + 2 * HALO, 512), F32)])(z, z, z, w_pool, scale)


def pool_bwd(z, dycat, w_pool, scale, T=256):
    S = z.shape[0]
    T = min(T, S)
    nt = S // T
    TE = T + 2 * HALO

    def body(up_ref, uc_ref, un_ref, yp_ref, yc_ref, yn_ref, w_ref, sc_ref, du_ref, dw_ref, dsc_ref, extu, exty, exte):
        i = pl.program_id(0)

        @pl.when(i == 0)
        def _():
            dw_ref[...] = jnp.zeros_like(dw_ref)
            dsc_ref[...] = jnp.zeros_like(dsc_ref)

        _fill_ext(extu, up_ref, uc_ref, un_ref, i, nt, T)
        _fill_ext(exty, yp_ref, yc_ref, yn_ref, i, nt, T)
        t = i * T + lax.broadcasted_iota(jnp.int32, (T, 1), 0)
        te = i * T - HALO + lax.broadcasted_iota(jnp.int32, (TE, 1), 0)
        for g, win in enumerate(POOL_WINDOWS):
            half = win // 2
            sl = slice(g * LANES, (g + 1) * LANES)
            w = w_ref[g]
            acc = extu[pl.ds(HALO - half, T), sl]
            for off in range(-half + 1, half):
                acc = acc + extu[pl.ds(HALO + off, T), sl]
            d = _mx(acc / _pool_cnt(t, half, S) - extu[pl.ds(HALO, T), sl])
            dy = exty[pl.ds(HALO, T), sl]
            dsc_ref[:, sl] += jnp.sum(dy * _dot(d, w), axis=0, keepdims=True)
            dw_ref[g] += _dot_tn(d, _mx(dy * sc_ref[:, sl]))
            dd = _dot_nt(_mx(exty[:, sl] * sc_ref[:, sl]), w)
            exte[:, sl] = dd / jnp.maximum(_pool_cnt(te, half, S), 1.0)
            acc = exte[pl.ds(HALO - half + 1, T), sl]
            for off in range(-half + 2, half + 1):
                acc = acc + exte[pl.ds(HALO + off, T), sl]
            du_ref[:, sl] = acc - dd[HALO:HALO + T, :]

    return _call(body, name="pool_bwd", grid=(nt,),
                 in_specs=_halo_specs(T, S, 512, 3) + _halo_specs(T, S, 512, 2)
                 + [pl.BlockSpec((4, LANES, LANES), lambda i: (0, 0, 0)), pl.BlockSpec((1, 512), lambda i: (0, 0))],
                 out_specs=[pl.BlockSpec((T, 512), lambda i: (i, 0)), pl.BlockSpec((4, LANES, LANES), lambda i: (0, 0, 0)),
                            pl.BlockSpec((1, 512), lambda i: (0, 0))],
                 out_shape=[jax.ShapeDtypeStruct((S, 512), F32), jax.ShapeDtypeStruct((4, LANES, LANES), F32),
                            jax.ShapeDtypeStruct((1, 512), F32)],
                 scratch=[pltpu.VMEM((TE, 512), F32)] * 3)(z, z, z, dycat, dycat, dycat, w_pool, scale)


MLA_HALF = 16
MLA_V_ONE = 64


def mla_prep(z, g_cq, g_ckv, w_uq, w_k, w_v, tabs_q, tabs_k, T=256):
    S = z.shape[0]
    T = min(T, S)

    def body(z_ref, gq_ref, gkv_ref, wq_ref, wk_ref, wv_ref, cq_ref, saq_ref, sbq_ref, ck_ref, sak_ref, sbk_ref,
             q_ref, k_ref, v_ref, nq_ref, nkv_ref):
        nq = _mx(_rms(z_ref[:, 0:256], gq_ref[...]))
        nkv = _mx(_rms(z_ref[:, 256:384], gkv_ref[...]))
        nq_ref[...] = nq
        nkv_ref[...] = nkv
        q = _dot(nq, wq_ref[...])
        kn = _dot(nkv, wk_ref[...])
        lane = lax.broadcasted_iota(jnp.int32, (T, 1024), 1)
        v_ref[...] = jnp.where(lane % LANES == MLA_V_ONE, 1.0, _dot(nkv, wv_ref[...])).astype(v_ref.dtype)
        kr = pltpu.roll(_rope(z_ref[:, 384:512], ck_ref[...], sak_ref[...], sbk_ref[...], MLA_HALF), 64, 1)
        cq, saq, sbq = cq_ref[...], saq_ref[...], sbq_ref[...]
        for h in range(8):
            hs = slice(h * LANES, (h + 1) * LANES)
            q_ref[:, hs] = _rope(q[:, hs], cq, saq, sbq, MLA_HALF).astype(q_ref.dtype)
            k_ref[:, hs] = (kn[:, hs] + kr).astype(k_ref.dtype)

    tab = pl.BlockSpec((T, LANES), lambda i: (i, 0))
    full = lambda a: pl.BlockSpec(a.shape, lambda i: (0, 0))
    row = lambda w: pl.BlockSpec((T, w), lambda i: (i, 0))
    sd = lambda w: jax.ShapeDtypeStruct((S, w), MXU_DTYPE)
    return _call(body, name="mla_prep", grid=(S // T,),
                 in_specs=[row(512), full(g_cq), full(g_ckv), full(w_uq), full(w_k), full(w_v)] + [tab] * 6,
                 out_specs=[row(1024), row(1024), row(1024), row(256), row(128)],
                 out_shape=[sd(1024), sd(1024), sd(1024), sd(256), sd(128)],
                 )(z, g_cq, g_ckv, w_uq, w_k, w_v, *tabs_q, *tabs_k)


def _col_to_row(c):
    return jnp.transpose(jnp.broadcast_to(c, (c.shape[0], LANES)))[0:1, :]


def mla_fwd(q, k, v, TQ=512, TK=512):
    S = q.shape[0]
    TQ, TK = min(TQ, S), min(TK, S)
    nk = S // TK
    ng = TK // LANES

    def body(q_ref, k_ref, v_ref, o_ref, lse_ref, m_s, acc_s):
        m_s[...] = jnp.full_like(m_s, -jnp.inf)
        acc_s[...] = jnp.zeros_like(acc_s)
        q = q_ref[...]

        def chunk(j, carry):
            off = pl.multiple_of(j * TK, TK)
            s = _dot_nt(q, k_ref[pl.ds(off, TK), :])
            parts = [s[:, g * LANES:(g + 1) * LANES] for g in range(ng)]
            mg = parts[0]
            for g in range(1, ng):
                mg = jnp.maximum(mg, parts[g])
            m_old = m_s[...]
            m_new = jnp.maximum(m_old, jnp.max(mg, axis=-1, keepdims=True))
            alpha = jnp.exp((m_old - m_new) * MLA_SCALE)
            p = jnp.concatenate([jnp.exp((parts[g] - m_new) * MLA_SCALE) for g in range(ng)], axis=1)
            acc_s[...] = alpha * acc_s[...] + _dot(_mx(p), v_ref[pl.ds(off, TK), :])
            m_s[...] = m_new
            return carry

        lax.fori_loop(0, nk, chunk, 0)
        acc = acc_s[...]
        lane = lax.broadcasted_iota(jnp.int32, (TQ, LANES), 1)
        den = jnp.sum(jnp.where(lane == MLA_V_ONE, acc, 0.0), axis=-1, keepdims=True)
        o_ref[...] = jnp.where(lane < MLA_V_ONE, acc / den, 0.0)
        lse_ref[0] = _col_to_row(m_s[:, 0:1] * MLA_SCALE + jnp.log(den))

    qs = pl.BlockSpec((TQ, LANES), lambda h, i: (i, h))
    kv = pl.BlockSpec((S, LANES), lambda h, i: (0, h))
    return _call(body, name="mla_fwd", grid=(8, S // TQ), in_specs=[qs, kv, kv],
                 out_specs=[qs, pl.BlockSpec((1, 1, TQ), lambda h, i: (h, 0, i))],
                 out_shape=[jax.ShapeDtypeStruct((S, 1024), F32), jax.ShapeDtypeStruct((8, 1, S), F32)],
                 scratch=[pltpu.VMEM((TQ, LANES), F32), pltpu.VMEM((TQ, LANES), F32)],
                 )(q, k, v)


def mla_delta(o, dycat, TQ=512):
    S = o.shape[0]
    TQ = min(TQ, S)

    def body(o_ref, do_ref, d_ref, dob_ref):
        do = do_ref[...]
        d_ref[0] = _col_to_row(jnp.sum(o_ref[...] * do, axis=-1, keepdims=True))
        dob_ref[...] = do.astype(dob_ref.dtype)

    qs = pl.BlockSpec((TQ, LANES), lambda h, i: (i, h))
    return _call(body, name="mla_delta", grid=(8, S // TQ), in_specs=[qs, qs],
                 out_specs=[pl.BlockSpec((1, 1, TQ), lambda h, i: (h, 0, i)), qs],
                 out_shape=[jax.ShapeDtypeStruct((8, 1, S), F32), jax.ShapeDtypeStruct((S, 1024), MXU_DTYPE)])(o, dycat)


def mla_bwd(q, k, v, do, lse, delta, TQ=512, TK=512):
    S = q.shape[0]
    TQ, TK = min(TQ, S), min(TK, S)
    nq, nk = S // TQ, S // TK
    lse = lse.reshape(8, nq, 1, TQ)
    delta = delta.reshape(8, nq, 1, TQ)

    def body(q_ref, do_ref, k_ref, v_ref, lse_ref, d_ref, dq_ref, dk_ref, dv_ref, dk_s, dv_s):
        @pl.when(pl.program_id(1) == 0)
        def _():
            dq_ref[...] = jnp.zeros_like(dq_ref)

        dk_s[...] = jnp.zeros_like(dk_s)
        dv_s[...] = jnp.zeros_like(dv_s)
        kk, vv = k_ref[...], v_ref[...]

        def chunk(j, carry):
            off = pl.multiple_of(j * TQ, TQ)
            qc, doc = q_ref[pl.ds(off, TQ), :], do_ref[pl.ds(off, TQ), :]
            pt = jnp.exp(_dot_nt(kk, qc) * MLA_SCALE - lse_ref[0, j])
            dv_s[...] += _dot(_mx(pt), doc)
            ds = _mx(pt * (_dot_nt(vv, doc) - d_ref[0, j]) * MLA_SCALE)
            dk_s[...] += _dot(ds, qc)
            dq_ref[pl.ds(off, TQ), :] += _dot_tn(ds, kk)
            return carry

        lax.fori_loop(0, nq, chunk, 0)
        dk_ref[...] = dk_s[...]
        dv_ref[...] = dv_s[...]

    full = pl.BlockSpec((S, LANES), lambda h, j: (0, h))
    ks = pl.BlockSpec((TK, LANES), lambda h, j: (j, h))
    st = pl.BlockSpec((1, nq, 1, TQ), lambda h, j: (h, 0, 0, 0))
    sd = jax.ShapeDtypeStruct((S, 1024), F32)
    return _call(body, name="mla_bwd", grid=(8, nk), in_specs=[full, full, ks, ks, st, st],
                 out_specs=[full, ks, ks], out_shape=[sd, sd, sd],
                 scratch=[pltpu.VMEM((TK, LANES), F32), pltpu.VMEM((TK, LANES), F32)],
                 )(q, do, k, v, lse, delta)


def mla_prep_bwd(z, g_cq, g_ckv, w_uq, w_k, w_v, dq, dk, dv, tabs_q, tabs_k, T=256):
    S = z.shape[0]
    T = min(T, S)

    def body(z_ref, gq_ref, gkv_ref, wq_ref, wk_ref, wv_ref, dq_ref, dk_ref, dv_ref,
             cq_ref, saq_ref, sbq_ref, ck_ref, sak_ref, sbk_ref, dz_ref, dqp_ref, dgq_ref, dgkv_ref):
        @pl.when(pl.program_id(0) == 0)
        def _():
            dgq_ref[...] = jnp.zeros_like(dgq_ref)
            dgkv_ref[...] = jnp.zeros_like(dgkv_ref)

        cq, saq, sbq = cq_ref[...], saq_ref[...], sbq_ref[...]
        dkr = jnp.zeros((T, LANES), F32)
        for h in range(8):
            hs = slice(h * LANES, (h + 1) * LANES)
            dqp_ref[:, hs] = _rope_t(dq_ref[:, hs], cq, saq, sbq, MLA_HALF).astype(dqp_ref.dtype)
            dkr = dkr + dk_ref[:, hs]
        lane = lax.broadcasted_iota(jnp.int32, (T, LANES), 1)
        dkr = jnp.where(lane < 2 * MLA_HALF, pltpu.roll(dkr, 64, 1), 0.0)
        dz_ref[:, 384:512] = _rope_t(dkr, ck_ref[...], sak_ref[...], sbk_ref[...], MLA_HALF)
        dnq = _dot_nt(dqp_ref[...], wq_ref[...])
        dx, dg = _rms_bwd(z_ref[:, 0:256], gq_ref[...], dnq)
        dz_ref[:, 0:256] = dx
        dgq_ref[...] += dg
        dnkv = _dot_nt(_mx(dk_ref[...]), wk_ref[...]) + _dot_nt(_mx(dv_ref[...]), wv_ref[...])
        dx, dg = _rms_bwd(z_ref[:, 256:384], gkv_ref[...], dnkv)
        dz_ref[:, 256:384] = dx
        dgkv_ref[...] += dg

    tab = pl.BlockSpec((T, LANES), lambda i: (i, 0))
    full = lambda a: pl.BlockSpec(a.shape, lambda i: (0, 0))
    row = lambda w: pl.BlockSpec((T, w), lambda i: (i, 0))
    return _call(body, name="mla_prep_bwd", grid=(S // T,),
                 in_specs=[row(512), full(g_cq), full(g_ckv), full(w_uq), full(w_k), full(w_v),
                           row(1024), row(1024), row(1024)] + [tab] * 6,
                 out_specs=[row(512), row(1024), full(g_cq), full(g_ckv)],
                 out_shape=[jax.ShapeDtypeStruct((S, 512), F32), jax.ShapeDtypeStruct((S, 1024), MXU_DTYPE),
                            jax.ShapeDtypeStruct(g_cq.shape, F32), jax.ShapeDtypeStruct(g_ckv.shape, F32)],
                 )(z, g_cq, g_ckv, w_uq, w_k, w_v, dq, dk, dv, *tabs_q, *tabs_k)


def _lru_gates(xc, w_ref, bias_ref, lam_ref):
    pre = _dot(_mx(xc), w_ref[...]) + bias_ref[...]
    out = []
    for d in range(2):
        r = _sigmoid(pre[:, d * 1024:d * 1024 + 512])
        ig = _sigmoid(pre[:, d * 1024 + 512:(d + 1) * 1024])
        log_a = -LRU_C * r * _softplus(-lam_ref[:, d * 512:(d + 1) * 512])
        out.append((r, ig, jnp.exp(log_a), jnp.sqrt(-_expm1(2.0 * log_a))))
    return out


def lru_pre(z, conv_w, conv_b, w_gate, b_gate, lam, T=256):
    S = z.shape[0]
    T = min(T, S)
    nt = S // T

    def body(xp_ref, xcur_ref, xn_ref, cw_ref, cb_ref, w_ref, bias_ref, lam_ref, xc_ref, a0_ref, b0_ref, a1_ref, b1_ref, ext):
        i = pl.program_id(0)
        _fill_ext(ext, xp_ref, xcur_ref, xn_ref, i, nt, T)
        xc = cb_ref[...] + cw_ref[0:1, :] * ext[pl.ds(HALO - 2, T), :]
        for j in range(1, 4):
            xc = xc + cw_ref[j:j + 1, :] * ext[pl.ds(HALO - 2 + j, T), :]
        xc_ref[...] = xc
        (_, i0, a0, m0), (_, i1, a1, m1) = _lru_gates(xc, w_ref, bias_ref, lam_ref)
        a0_ref[...] = a0
        b0_ref[...] = m0 * (i0 * xc)
        a1_ref[...] = a1
        b1_ref[...] = m1 * (i1 * xc)

    full = lambda a: pl.BlockSpec(a.shape, lambda i: (0, 0))
    row = pl.BlockSpec((T, 512), lambda i: (i, 0))
    sd = jax.ShapeDtypeStruct((S, 512), F32)
    return _call(body, name="lru_pre", grid=(nt,),
                 in_specs=_halo_specs(T, S, 512, 1) + [full(conv_w), full(conv_b), full(w_gate), full(b_gate), full(lam)],
                 out_specs=[row] * 5, out_shape=[sd] * 5,
                 scratch=[pltpu.VMEM((T + 2 * HALO, 512), F32)])(z, z, z, conv_w, conv_b, w_gate, b_gate, lam)


def lru_scan(name, af, bf, ar, br, *, adjoint, T=256):
    S, W = af.shape
    T = min(T, S)
    nt = S // T
    nc = T // 8

    def body(af_ref, bf_ref, ar_ref, br_ref, hf_ref, hr_ref, cf, cr):
        @pl.when(pl.program_id(0) == 0)
        def _():
            cf[...] = jnp.zeros_like(cf)
            cr[...] = jnp.zeros_like(cr)

        row = lax.broadcasted_iota(jnp.int32, (8, W), 0)

        def step(a, b, carry):
            if adjoint:
                val = b + carry
                return val, a * val
            val = a * carry + b
            return val, val

        def chunk(c, carry):
            hf, hr = carry
            of = pl.multiple_of(c * 8, 8)
            orv = pl.multiple_of((nc - 1 - c) * 8, 8)
            a8, b8 = af_ref[pl.ds(of, 8), :], bf_ref[pl.ds(of, 8), :]
            ra8, rb8 = ar_ref[pl.ds(orv, 8), :], br_ref[pl.ds(orv, 8), :]
            outf = jnp.zeros((8, W), F32)
            outr = jnp.zeros((8, W), F32)
            for k in range(8):
                val, hf = step(a8[k:k + 1, :], b8[k:k + 1, :], hf)
                outf = jnp.where(row == k, val, outf)
                kr = 7 - k
                val, hr = step(ra8[kr:kr + 1, :], rb8[kr:kr + 1, :], hr)
                outr = jnp.where(row == kr, val, outr)
            hf_ref[pl.ds(of, 8), :] = outf
            hr_ref[pl.ds(orv, 8), :] = outr
            return hf, hr

        hf, hr = lax.fori_loop(0, nc, chunk, (cf[0:1, :], cr[0:1, :]))
        cf[0:1, :] = hf
        cr[0:1, :] = hr

    fw = pl.BlockSpec((T, W), lambda i: (i, 0))
    rv = pl.BlockSpec((T, W), lambda i: (nt - 1 - i, 0))
    sd = jax.ShapeDtypeStruct((S, W), F32)
    return _call(body, name=name, grid=(nt,), in_specs=[fw, fw, rv, rv], out_specs=[fw, rv], out_shape=[sd, sd],
                 scratch=[pltpu.VMEM((8, W), F32), pltpu.VMEM((8, W), F32)])(af, bf, ar, br)


def lru_gate(h0, h1, z, T=256):
    S = z.shape[0]
    T = min(T, S)

    def body(h0_ref, h1_ref, xg_ref, y_ref):
        y_ref[...] = ((h0_ref[...] + h1_ref[...]) * _gelu(xg_ref[...])).astype(y_ref.dtype)

    row = pl.BlockSpec((T, 512), lambda i: (i, 0))
    return _call(body, name="lru_gate", grid=(S // T,), in_specs=[row, row, pl.BlockSpec((T, 512), lambda i: (i, 2))],
                 out_specs=row, out_shape=jax.ShapeDtypeStruct((S, 512), MXU_DTYPE))(h0, h1, z)


def lru_gate_bwd(h0, h1, z, dycat, T=256):
    S = z.shape[0]
    T = min(T, S)

    def body(h0_ref, h1_ref, xg_ref, dy_ref, dxg_ref, dh_ref):
        xg, dy = xg_ref[...], dy_ref[...]
        dxg_ref[...] = dy * (h0_ref[...] + h1_ref[...]) * _gelu_grad(xg)
        dh_ref[...] = dy * _gelu(xg)

    row = pl.BlockSpec((T, 512), lambda i: (i, 0))
    col2 = pl.BlockSpec((T, 512), lambda i: (i, 2))
    sd = jax.ShapeDtypeStruct((S, 512), F32)
    return _call(body, name="lru_gate_bwd", grid=(S // T,), in_specs=[row, row, col2, col2],
                 out_specs=[row, row], out_shape=[sd, sd])(h0, h1, z, dycat)


def lru_bwd_point(xc, h0, h1, g0, g1, w_gate, b_gate, lam, T=256):
    S = xc.shape[0]
    T = min(T, S)
    nt = S // T

    def body(xc_ref, h0p_ref, h0_ref, h0n_ref, h1p_ref, h1_ref, h1n_ref, g0_ref, g1_ref, w_ref, bias_ref, lam_ref,
             dxc_ref, dpre_ref, dbias_ref, dlam_ref, ext0, ext1):
        i = pl.program_id(0)

        @pl.when(i == 0)
        def _():
            dbias_ref[...] = jnp.zeros_like(dbias_ref)
            dlam_ref[...] = jnp.zeros_like(dlam_ref)

        _fill_ext(ext0, h0p_ref, h0_ref, h0n_ref, i, nt, T)
        _fill_ext(ext1, h1p_ref, h1_ref, h1n_ref, i, nt, T)
        xc = xc_ref[...]
        gates = _lru_gates(xc, w_ref, bias_ref, lam_ref)
        hshift = (ext0[pl.ds(HALO - 1, T), :], ext1[pl.ds(HALO + 1, T), :])
        gs = (g0_ref[...], g1_ref[...])
        dxc = jnp.zeros((T, 512), F32)
        for d in range(2):
            r, ig, a, mult = gates[d]
            db = gs[d]
            da = db * hshift[d]
            dmult = db * (ig * xc)
            di = db * (mult * xc)
            dxc = dxc + db * (mult * ig)
            dloga = da * a - dmult * (a * a / mult)
            lam_d = lam_ref[:, d * 512:(d + 1) * 512]
            dr = dloga * (-LRU_C * _softplus(-lam_d))
            dsp = jnp.sum(dloga * (-LRU_C * r), axis=0, keepdims=True)
            dlam_ref[:, d * 512:(d + 1) * 512] += dsp * (-_sigmoid(-lam_d))
            dpre_ref[:, d * 1024:d * 1024 + 512] = (dr * (r * (1.0 - r))).astype(dpre_ref.dtype)
            dpre_ref[:, d * 1024 + 512:(d + 1) * 1024] = (di * (ig * (1.0 - ig))).astype(dpre_ref.dtype)
            dbias_ref[:, d * 1024:d * 1024 + 512] += jnp.sum(dr * (r * (1.0 - r)), axis=0, keepdims=True)
            dbias_ref[:, d * 1024 + 512:(d + 1) * 1024] += jnp.sum(di * (ig * (1.0 - ig)), axis=0, keepdims=True)
        dxc_ref[...] = dxc + _dot_nt(dpre_ref[...], w_ref[...])

    full = lambda a: pl.BlockSpec(a.shape, lambda i: (0, 0))
    row = pl.BlockSpec((T, 512), lambda i: (i, 0))
    return _call(body, name="lru_bwd_point", grid=(nt,),
                 in_specs=[row] + _halo_specs(T, S, 512, 0) + _halo_specs(T, S, 512, 0) + [row, row, full(w_gate), full(b_gate), full(lam)],
                 out_specs=[row, pl.BlockSpec((T, 2048), lambda i: (i, 0)), full(b_gate), full(lam)],
                 out_shape=[jax.ShapeDtypeStruct((S, 512), F32), jax.ShapeDtypeStruct((S, 2048), MXU_DTYPE),
                            jax.ShapeDtypeStruct(b_gate.shape, F32), jax.ShapeDtypeStruct(lam.shape, F32)],
                 scratch=[pltpu.VMEM((T + 2 * HALO, 512), F32)] * 2,
                 )(xc, h0, h0, h0, h1, h1, h1, g0, g1, w_gate, b_gate, lam)


def conv_bwd(z, dxc, conv_w, T=256):
    S = z.shape[0]
    T = min(T, S)
    nt = S // T

    def body(xp_ref, xcur_ref, xn_ref, dp_ref, dcur_ref, dn_ref, cw_ref, dx_ref, dw_ref, db_ref, extx, extd):
        i = pl.program_id(0)

        @pl.when(i == 0)
        def _():
            dw_ref[...] = jnp.zeros_like(dw_ref)
            db_ref[...] = jnp.zeros_like(db_ref)

        _fill_ext(extx, xp_ref, xcur_ref, xn_ref, i, nt, T)
        _fill_ext(extd, dp_ref, dcur_ref, dn_ref, i, nt, T)
        d = extd[pl.ds(HALO, T), :]
        dx = cw_ref[0:1, :] * extd[pl.ds(HALO + 2, T), :]
        for j in range(1, 4):
            dx = dx + cw_ref[j:j + 1, :] * extd[pl.ds(HALO + 2 - j, T), :]
        dx_ref[...] = dx
        for j in range(4):
            dw_ref[j:j + 1, :] += jnp.sum(d * extx[pl.ds(HALO - 2 + j, T), :], axis=0, keepdims=True)
        db_ref[...] += jnp.sum(d, axis=0, keepdims=True)

    full = lambda a: pl.BlockSpec(a.shape, lambda i: (0, 0))
    row = pl.BlockSpec((T, 512), lambda i: (i, 0))
    return _call(body, name="conv_bwd", grid=(nt,),
                 in_specs=_halo_specs(T, S, 512, 1) + _halo_specs(T, S, 512, 0) + [full(conv_w)],
                 out_specs=[row, full(conv_w), pl.BlockSpec((1, 512), lambda i: (0, 0))],
                 out_shape=[jax.ShapeDtypeStruct((S, 512), F32), jax.ShapeDtypeStruct(conv_w.shape, F32),
                            jax.ShapeDtypeStruct((1, 512), F32)],
                 scratch=[pltpu.VMEM((T + 2 * HALO, 512), F32)] * 2)(z, z, z, dxc, dxc, dxc, conv_w)


def _me_and_peers():
    x, y, c = lax.axis_index("x"), lax.axis_index("y"), lax.axis_index("c")
    peers = []
    for k in range(1, NDEV):
        px, py, pc = x ^ (k >> 2), y ^ ((k >> 1) & 1), c ^ (k & 1)
        peers.append(((px, py, pc), 4 * px + 2 * py + pc))
    return 4 * x + 2 * y + c, peers


def all_gather(arrays):
    n = len(arrays)

    def body(*refs):
        ins, outs = refs[:n], refs[n:2 * n]
        send, recv, loc = refs[2 * n:]
        me, peers = _me_and_peers()
        local = [pltpu.make_async_copy(ins[a], outs[a].at[me], loc.at[a]) for a in range(n)]
        for cp in local:
            cp.start()
        sends = []
        for a in range(n):
            for k, (dev, _) in enumerate(peers):
                cp = pltpu.make_async_remote_copy(src_ref=ins[a], dst_ref=outs[a].at[me], send_sem=send.at[a * (NDEV - 1) + k],
                                                  recv_sem=recv.at[a * (NDEV - 1) + k], device_id=dev,
                                                  device_id_type=pl.DeviceIdType.MESH)
                cp.start()
                sends.append(cp)
        for a in range(n):
            for k, (dev, idx) in enumerate(peers):
                pltpu.make_async_remote_copy(src_ref=ins[a], dst_ref=outs[a].at[idx], send_sem=send.at[a * (NDEV - 1) + k],
                                             recv_sem=recv.at[a * (NDEV - 1) + k], device_id=dev,
                                             device_id_type=pl.DeviceIdType.MESH).wait_recv()
        for cp in sends:
            cp.wait_send()
        for cp in local:
            cp.wait()

    any_spec = pl.BlockSpec(memory_space=pl.ANY)
    return pl.pallas_call(
        body, name="all_gather", in_specs=[any_spec] * n, out_specs=[any_spec] * n,
        out_shape=[jax.ShapeDtypeStruct((NDEV,) + a.shape, a.dtype) for a in arrays],
        scratch_shapes=[pltpu.SemaphoreType.DMA((n * (NDEV - 1),)), pltpu.SemaphoreType.DMA((n * (NDEV - 1),)),
                        pltpu.SemaphoreType.DMA((n,))],
    )(*arrays)


def grad_exchange(gs, gr):
    def body(gs_ref, gr_ref, os_ref, or_ref, send, recv, loc):
        me, peers = _me_and_peers()
        local = [pltpu.make_async_copy(gs_ref.at[me], os_ref.at[me], loc.at[0]),
                 pltpu.make_async_copy(gr_ref, or_ref.at[me], loc.at[1])]
        for cp in local:
            cp.start()
        sends = []
        for k, (dev, idx) in enumerate(peers):
            for a, (src, dst) in enumerate(((gs_ref.at[idx], os_ref.at[me]), (gr_ref, or_ref.at[me]))):
                cp = pltpu.make_async_remote_copy(src_ref=src, dst_ref=dst, send_sem=send.at[a * (NDEV - 1) + k], recv_sem=recv.at[a * (NDEV - 1) + k],
                                                  device_id=dev, device_id_type=pl.DeviceIdType.MESH)
                cp.start()
                sends.append(cp)
        for k, (dev, idx) in enumerate(peers):
            for a, (src, dst) in enumerate(((gs_ref.at[idx], os_ref.at[idx]), (gr_ref, or_ref.at[idx]))):
                pltpu.make_async_remote_copy(src_ref=src, dst_ref=dst, send_sem=send.at[a * (NDEV - 1) + k], recv_sem=recv.at[a * (NDEV - 1) + k],
                                             device_id=dev, device_id_type=pl.DeviceIdType.MESH).wait_recv()
        for cp in sends:
            cp.wait_send()
        for cp in local:
            cp.wait()

    any_spec = pl.BlockSpec(memory_space=pl.ANY)
    return pl.pallas_call(
        body, name="grad_exchange", in_specs=[any_spec] * 2, out_specs=[any_spec] * 2,
        out_shape=[jax.ShapeDtypeStruct(gs.shape, gs.dtype), jax.ShapeDtypeStruct((NDEV,) + gr.shape, gr.dtype)],
        scratch_shapes=[pltpu.SemaphoreType.DMA((2 * (NDEV - 1),)), pltpu.SemaphoreType.DMA((2 * (NDEV - 1),)),
                        pltpu.SemaphoreType.DMA((2,))],
    )(gs, gr)


def adamw(name, gparts, w, m, v, T=128):
    R, C = w.shape
    T = min(T, R)

    def body(g_ref, w_ref, m_ref, v_ref, go_ref, d_ref, mo_ref, vo_ref):
        g = g_ref[0]
        for k in range(1, NDEV):
            g = g + g_ref[k]
        mn = ADAM_B1 * m_ref[...] + (1.0 - ADAM_B1) * g
        vn = ADAM_B2 * v_ref[...] + (1.0 - ADAM_B2) * (g * g)
        m_hat = mn / (1.0 - ADAM_B1 ** ADAM_STEP)
        v_hat = vn / (1.0 - ADAM_B2 ** ADAM_STEP)
        go_ref[...] = g
        d_ref[...] = -ADAM_LR * (m_hat / (jnp.sqrt(v_hat) + ADAM_EPS) + ADAM_WD * w_ref[...])
        mo_ref[...] = mn
        vo_ref[...] = vn

    row = pl.BlockSpec((T, C), lambda i: (i, 0))
    sd = jax.ShapeDtypeStruct((R, C), F32)
    return _call(body, name=name, grid=(R // T,), in_specs=[pl.BlockSpec((NDEV, T, C), lambda i: (0, i, 0)), row, row, row],
                 out_specs=[row] * 4, out_shape=[sd] * 4)(gparts, w, m, v)


PACK_W = 1024
PACK_ROWS = 128


def _pack(arrays, dtype):
    flat = jnp.concatenate([a.astype(dtype).reshape(-1) for a in arrays])
    n = flat.shape[0]
    gran = PACK_W * PACK_ROWS
    total = -(-n // gran) * gran
    return jnp.pad(flat, (0, total - n)).reshape(total // PACK_W, PACK_W)


def _unpack(buf, shapes, lead=()):
    flat = buf.reshape(lead + (-1,))
    out, off = [], 0
    for sh in shapes:
        n = 1
        for s in sh:
            n *= s
        out.append(flat[..., off:off + n].reshape(lead + tuple(sh)))
        off += n
    return out


def _unshard(parts, axis):
    return jnp.concatenate([parts[d] for d in range(NDEV)], axis=axis)


def _split(full, axis):
    return jnp.stack(jnp.split(full, NDEV, axis=axis))


def _slab_cols(w, head, used, n_heads, slab=LANES):
    lead = w.shape[:-1]
    w = w.reshape(lead + (n_heads, head))[..., :used]
    w = jnp.pad(w, [(0, 0)] * len(lead) + [(0, 0), (0, slab - used)])
    return w.reshape(lead + (n_heads * slab,))


def _unslab_cols(g, used, n_heads, slab=LANES):
    lead = g.shape[:-1]
    return g.reshape(lead + (n_heads, slab))[..., :used].reshape(lead + (n_heads * used,))


def _block_diag(w):
    eye = jnp.eye(8, dtype=w.dtype)
    return jnp.einsum("nij,nm->nimj", w, eye).reshape(512, 512)


def _block_diag_t(g):
    g = g.reshape(8, 64, 8, 64)
    return jnp.stack([g[n, :, n, :] for n in range(8)])


BIG = (("e_w_in", 2), ("e_w_out", 1), ("o_w_in", 2), ("o_w_uq", 2), ("o_w_ukv", 2), ("o_w_out", 1),
       ("w_mlp1", 2), ("w_mlp2", 1))
SMALL = (("o_norm_mix", 1), ("o_g_cq", 1), ("o_conv_w", 2), ("o_conv_b", 1), ("o_lru_ba", 2), ("o_lru_bx", 2),
         ("o_lru_lambda", 2))
REPL = ("e_norm_mix", "e_sink", "e_w_pool", "e_pool_scale", "o_g_ckv", "o_lru_wa", "o_lru_wx", "norm_mlp",
        "final_norm")
WEIGHTS = ("e_norm_mix", "e_w_in", "e_sink", "e_w_pool", "e_pool_scale", "e_w_out", "o_norm_mix", "o_w_in", "o_g_cq",
           "o_w_uq", "o_g_ckv", "o_w_ukv", "o_conv_w", "o_conv_b", "o_lru_wa", "o_lru_ba", "o_lru_wx", "o_lru_bx",
           "o_lru_lambda", "o_w_out", "norm_mlp", "w_mlp1", "w_mlp2", "final_norm")


def _mlp_fwd(l, x, g, w1, w2):
    u1, hn = mm_nn(f"mlp1_{l}", [x], [w1], norm_g=g, emit_norm=True, tm=512, tn=1024)
    x2 = mm_nn(f"mlp2_{l}", [u1], [w2], act="relu2", res=x, tm=256, tn=1024)
    return x2, (x, u1, hn)


def _mlp_bwd(l, saved, g, w1, w2, dx2):
    x, u1, hn = saved
    du1 = mm_nt(f"mlp2_dx_{l}", [dx2], [w2], relu2_of=u1, out_dtype=MXU_DTYPE, tm=512, tn=1024)
    dw2 = mm_tn(f"mlp2_dw_{l}", u1, dx2, act="relu2", tn=1024)
    dhn = mm_nt(f"mlp1_dx_{l}", [du1], [w1], tm=512, tn=1024)
    dw1 = mm_tn(f"mlp1_dw_{l}", hn, du1, tn=1024)
    dx, dg = rms_bwd(f"mlp_norm_bwd_{l}", x, g, dhn, dx2)
    return dx, dg, dw1, dw2


def kernel(x, e_norm_mix, e_w_in, e_sink, e_w_pool, e_pool_scale, e_w_out, o_norm_mix, o_w_in, o_g_cq, o_w_uq, o_g_ckv, o_w_ukv, o_conv_w, o_conv_b, o_lru_wa, o_lru_ba, o_lru_wx, o_lru_bx, o_lru_lambda, o_w_out, norm_mlp, w_mlp1, w_mlp2, final_norm, loss_target, m_e_norm_mix, m_e_w_in, m_e_sink, m_e_w_pool, m_e_pool_scale, m_e_w_out, m_o_norm_mix, m_o_w_in, m_o_g_cq, m_o_w_uq, m_o_g_ckv, m_o_w_ukv, m_o_conv_w, m_o_conv_b, m_o_lru_wa, m_o_lru_ba, m_o_lru_wx, m_o_lru_bx, m_o_lru_lambda, m_o_w_out, m_norm_mlp, m_w_mlp1, m_w_mlp2, m_final_norm, v_e_norm_mix, v_e_w_in, v_e_sink, v_e_w_pool, v_e_pool_scale, v_e_w_out, v_o_norm_mix, v_o_w_in, v_o_g_cq, v_o_w_uq, v_o_g_ckv, v_o_w_ukv, v_o_conv_w, v_o_conv_b, v_o_lru_wa, v_o_lru_ba, v_o_lru_wx, v_o_lru_bx, v_o_lru_lambda, v_o_w_out, v_norm_mlp, v_w_mlp1, v_w_mlp2, v_final_norm):
    W = dict(e_norm_mix=e_norm_mix, e_w_in=e_w_in, e_sink=e_sink, e_w_pool=e_w_pool, e_pool_scale=e_pool_scale, e_w_out=e_w_out, o_norm_mix=o_norm_mix, o_w_in=o_w_in, o_g_cq=o_g_cq, o_w_uq=o_w_uq, o_g_ckv=o_g_ckv, o_w_ukv=o_w_ukv, o_conv_w=o_conv_w, o_conv_b=o_conv_b, o_lru_wa=o_lru_wa, o_lru_ba=o_lru_ba, o_lru_wx=o_lru_wx, o_lru_bx=o_lru_bx, o_lru_lambda=o_lru_lambda, o_w_out=o_w_out, norm_mlp=norm_mlp, w_mlp1=w_mlp1, w_mlp2=w_mlp2, final_norm=final_norm)
    Mo = dict(e_norm_mix=m_e_norm_mix, e_w_in=m_e_w_in, e_sink=m_e_sink, e_w_pool=m_e_w_pool, e_pool_scale=m_e_pool_scale, e_w_out=m_e_w_out, o_norm_mix=m_o_norm_mix, o_w_in=m_o_w_in, o_g_cq=m_o_g_cq, o_w_uq=m_o_w_uq, o_g_ckv=m_o_g_ckv, o_w_ukv=m_o_w_ukv, o_conv_w=m_o_conv_w, o_conv_b=m_o_conv_b, o_lru_wa=m_o_lru_wa, o_lru_ba=m_o_lru_ba, o_lru_wx=m_o_lru_wx, o_lru_bx=m_o_lru_bx, o_lru_lambda=m_o_lru_lambda, o_w_out=m_o_w_out, norm_mlp=m_norm_mlp, w_mlp1=m_w_mlp1, w_mlp2=m_w_mlp2, final_norm=m_final_norm)
    Vo = dict(e_norm_mix=v_e_norm_mix, e_w_in=v_e_w_in, e_sink=v_e_sink, e_w_pool=v_e_w_pool, e_pool_scale=v_e_pool_scale, e_w_out=v_e_w_out, o_norm_mix=v_o_norm_mix, o_w_in=v_o_w_in, o_g_cq=v_o_g_cq, o_w_uq=v_o_w_uq, o_g_ckv=v_o_g_ckv, o_w_ukv=v_o_w_ukv, o_conv_w=v_o_conv_w, o_conv_b=v_o_conv_b, o_lru_wa=v_o_lru_wa, o_lru_ba=v_o_lru_ba, o_lru_wx=v_o_lru_wx, o_lru_bx=v_o_lru_bx, o_lru_lambda=v_o_lru_lambda, o_w_out=v_o_w_out, norm_mlp=v_norm_mlp, w_mlp1=v_w_mlp1, w_mlp2=v_w_mlp2, final_norm=v_final_norm)

    S = x.shape[1]
    x0 = x[0]
    target = loss_target[0]

    big_g, small_g = all_gather([_pack([W[n] for n, _ in BIG], MXU_DTYPE), _pack([W[n] for n, _ in SMALL], F32)])
    full = {}
    for (n, ax), parts in zip(BIG, _unpack(big_g, [W[n].shape for n, _ in BIG], (NDEV,))):
        full[n] = _unshard(parts, ax)
    for (n, ax), parts in zip(SMALL, _unpack(small_g, [W[n].shape for n, _ in SMALL], (NDEV,))):
        full[n] = _unshard(parts, ax)

    def even_in(w):
        return jnp.concatenate([_slab_cols(w[:, 0:512], 64, 64, 8), _slab_cols(w[:, 512:640], 64, 64, 2),
                                _slab_cols(w[:, 640:768], 64, 64, 2), w[:, 768:1280]], axis=1)

    def even_in_t(g):
        return jnp.concatenate([_unslab_cols(g[:, 0:1024], 64, 8), _unslab_cols(g[:, 1024:1280], 64, 2),
                                _unslab_cols(g[:, 1280:1536], 64, 2), g[:, 1536:2048]], axis=1)

    def slab_rows(w, n_heads):
        return _slab_cols(w.T, 64, 64, n_heads).T

    def unslab_rows(g, n_heads):
        return _unslab_cols(g.T, 64, n_heads).T

    def odd_in(w):
        return jnp.concatenate([w[:, 0:384], jnp.pad(w[:, 384:416], ((0, 0), (0, 96))), w[:, 416:1440]], axis=1)

    def odd_in_t(g):
        return jnp.concatenate([g[:, 0:416], g[:, 512:1536]], axis=1)

    def uq(w):
        return _slab_cols(w, 96, 96, 8)

    def ukv(w):
        w = w.reshape(128, 8, 128)
        pad = lambda t: jnp.pad(t, ((0, 0), (0, 0), (0, 64))).reshape(128, 1024)
        return pad(w[:, :, :64]), pad(w[:, :, 64:])

    def ukv_t(gk, gv):
        gk = gk.reshape(128, 8, 128)[:, :, :64]
        gv = gv.reshape(128, 8, 128)[:, :, :64]
        return jnp.concatenate([gk, gv], axis=2).reshape(128, 1024)

    tabs_swa = rope_tables(S, 0, SWA_HALF)
    tabs_mq = rope_tables(S, 64, MLA_HALF)
    tabs_mk = rope_tables(S, 0, MLA_HALF)
    row = lambda v: v.reshape(1, -1)

    saved = []
    xcur = x0
    for l in range(4):
        j = l // 2
        if l % 2 == 0:
            w_in = even_in(full["e_w_in"][j])
            w_out = full["e_w_out"][j]
            w_out_a, w_out_b = slab_rows(w_out[0:512], 8), w_out[512:1024]
            w_pool = W["e_w_pool"][j].astype(MXU_DTYPE)
            z, h = mm_nn(f"e_in_{j}", [xcur], [w_in], norm_g=row(W["e_norm_mix"][j]), emit_norm=True, tm=256, tn=1024)
            qkv = swa_prep(z, tabs_swa)
            ya, lse = swa_fwd(qkv, W["e_sink"][j])
            yb = pool_fwd(z, w_pool, row(W["e_pool_scale"][j]))
            x1 = mm_nn(f"e_out_{j}", [ya, yb], [w_out_a, w_out_b], res=xcur)
            mix = (xcur, z, h, qkv, lse, ya, yb, w_in, w_out_a, w_out_b, w_pool)
        else:
            w_in = odd_in(full["o_w_in"][j])
            w_out = full["o_w_out"][j]
            w_out_a, w_out_b = slab_rows(w_out[0:512], 8), w_out[512:1024]
            w_uq = uq(full["o_w_uq"][j])
            w_k, w_v = ukv(full["o_w_ukv"][j])
            g_cq, g_ckv = row(full["o_g_cq"][j]), row(W["o_g_ckv"][j])
            w_gate = jnp.concatenate([_block_diag(W["o_lru_wa"][j, 0]), _block_diag(W["o_lru_wx"][j, 0]),
                                      _block_diag(W["o_lru_wa"][j, 1]), _block_diag(W["o_lru_wx"][j, 1])], axis=1).astype(MXU_DTYPE)
            b_gate = jnp.concatenate([full["o_lru_ba"][j, 0], full["o_lru_bx"][j, 0], full["o_lru_ba"][j, 1],
                                      full["o_lru_bx"][j, 1]]).reshape(1, 2048)
            lam = full["o_lru_lambda"][j].reshape(1, 1024)
            conv_w, conv_b = full["o_conv_w"][j], row(full["o_conv_b"][j])
            z, h = mm_nn(f"o_in_{j}", [xcur], [w_in], norm_g=row(full["o_norm_mix"][j]), emit_norm=True, tm=256, tn=512)
            q, k, v, nq, nkv = mla_prep(z, g_cq, g_ckv, w_uq, w_k, w_v, tabs_mq, tabs_mk)
            yc, lse = mla_fwd(q, k, v)
            xc, a0, b0, a1, b1 = lru_pre(z, conv_w, conv_b, w_gate, b_gate, lam)
            h0, h1 = lru_scan(f"lru_scan_fwd_{j}", a0, b0, a1, b1, adjoint=False)
            yd = lru_gate(h0, h1, z)
            x1 = mm_nn(f"o_out_{j}", [yc, yd], [w_out_a, w_out_b], res=xcur)
            mix = (xcur, z, h, q, k, v, nq, nkv, yc, lse, xc, a0, a1, h0, h1, yd, w_in, w_out_a, w_out_b, w_uq, w_k, w_v,
                   g_cq, g_ckv, w_gate, b_gate, lam, conv_w)
        xcur, mlp = _mlp_fwd(l, x1, row(W["norm_mlp"][l]), full["w_mlp1"][l], full["w_mlp2"][l])
        saved.append((mix, mlp))

    loss_row, dx, dg_final = loss_head(xcur, row(W["final_norm"]), target)

    G = {n: [None] * W[n].shape[0] for n in WEIGHTS if n != "final_norm"}
    G["final_norm"] = dg_final.reshape(-1)
    for l in reversed(range(4)):
        j = l // 2
        mix, mlp = saved[l]
        dx, dg, dw1, dw2 = _mlp_bwd(l, mlp, row(W["norm_mlp"][l]), full["w_mlp1"][l], full["w_mlp2"][l], dx)
        G["norm_mlp"][l], G["w_mlp1"][l], G["w_mlp2"][l] = dg.reshape(-1), dw1, dw2
        if l % 2 == 0:
            xin, z, h, qkv, lse, ya, yb, w_in, w_out_a, w_out_b, w_pool = mix
            w_out_cat = jnp.concatenate([w_out_a, w_out_b], axis=0)
            dycat = mm_nt(f"e_out_dx_{j}", [dx], [w_out_cat])
            G["e_w_out"][j] = jnp.concatenate([unslab_rows(mm_tn(f"e_out_dwa_{j}", ya, dx), 8),
                                               mm_tn(f"e_out_dwb_{j}", yb, dx)], axis=0)
            dq, dk, dv, dsink = swa_bwd(qkv, W["e_sink"][j], lse, dycat, tabs_swa)
            du, dwp, dsc = pool_bwd(z, dycat, w_pool, row(W["e_pool_scale"][j]))
            G["e_sink"][j], G["e_w_pool"][j], G["e_pool_scale"][j] = dsink[0, 0:8], dwp, dsc.reshape(-1)
            dh = mm_nt(f"e_in_dx_{j}", [dq, dk, dv, du], [(w_in, 0, 1024), (w_in, 4, 256), (w_in, 5, 256), (w_in, 3, 512)])
            G["e_w_in"][j] = even_in_t(jnp.concatenate(
                [mm_tn(f"e_in_dwq_{j}", h, dq), mm_tn(f"e_in_dwk_{j}", h, dk), mm_tn(f"e_in_dwv_{j}", h, dv),
                 mm_tn(f"e_in_dwu_{j}", h, du)], axis=1))
            dx, dg = rms_bwd(f"e_norm_bwd_{j}", xin, row(W["e_norm_mix"][j]), dh, dx)
            G["e_norm_mix"][j] = dg.reshape(-1)
        else:
            (xin, z, h, q, k, v, nq, nkv, yc, lse, xc, a0, a1, h0, h1, yd, w_in, w_out_a, w_out_b, w_uq, w_k, w_v,
             g_cq, g_ckv, w_gate, b_gate, lam, conv_w) = mix
            w_out_cat = jnp.concatenate([w_out_a, w_out_b], axis=0)
            dycat = mm_nt(f"o_out_dx_{j}", [dx], [w_out_cat])
            G["o_w_out"][j] = jnp.concatenate([unslab_rows(mm_tn(f"o_out_dwa_{j}", yc, dx), 8),
                                               mm_tn(f"o_out_dwb_{j}", yd, dx)], axis=0)
            delta, dob = mla_delta(yc, dycat)
            dq, dk, dv = mla_bwd(q, k, v, dob, lse, delta)
            dza, dqp, dgq, dgkv = mla_prep_bwd(z, g_cq, g_ckv, w_uq, w_k, w_v, dq, dk, dv, tabs_mq, tabs_mk)
            G["o_g_cq"][j], G["o_g_ckv"][j] = dgq.reshape(-1), dgkv.reshape(-1)
            G["o_w_uq"][j] = _unslab_cols(mm_tn(f"o_uq_dw_{j}", nq, dqp), 96, 8)
            G["o_w_ukv"][j] = ukv_t(mm_tn(f"o_uk_dw_{j}", nkv, dk), mm_tn(f"o_uv_dw_{j}", nkv, dv))
            dxg, dhh = lru_gate_bwd(h0, h1, z, dycat)
            g1, g0 = lru_scan(f"lru_scan_bwd_{j}", a1, dhh, a0, dhh, adjoint=True)
            dxc, dpre, dbias, dlam = lru_bwd_point(xc, h0, h1, g0, g1, w_gate, b_gate, lam)
            dwg = mm_tn(f"o_gate_dw_{j}", xc, dpre)
            G["o_lru_wa"][j] = jnp.stack([_block_diag_t(dwg[:, 0:512]), _block_diag_t(dwg[:, 1024:1536])])
            G["o_lru_wx"][j] = jnp.stack([_block_diag_t(dwg[:, 512:1024]), _block_diag_t(dwg[:, 1536:2048])])
            G["o_lru_ba"][j] = jnp.stack([dbias[0, 0:512], dbias[0, 1024:1536]])
            G["o_lru_bx"][j] = jnp.stack([dbias[0, 512:1024], dbias[0, 1536:2048]])
            G["o_lru_lambda"][j] = dlam.reshape(2, 512)
            dxr, dcw, dcb = conv_bwd(z, dxc, conv_w)
            G["o_conv_w"][j], G["o_conv_b"][j] = dcw, dcb.reshape(-1)
            dh = mm_nt(f"o_in_dx_{j}", [dza, dxr, dxg], [(w_in, 0, 512), (w_in, 1, 512), (w_in, 2, 512)])
            G["o_w_in"][j] = odd_in_t(jnp.concatenate(
                [mm_tn(f"o_in_dwa_{j}", h, dza), mm_tn(f"o_in_dwr_{j}", h, dxr), mm_tn(f"o_in_dwg_{j}", h, dxg)], axis=1))
            dx, dg = rms_bwd(f"o_norm_bwd_{j}", xin, row(full["o_norm_mix"][j]), dh, dx)
            G["o_norm_mix"][j] = dg.reshape(-1)
    grad_x = dx[None]
    G = {n: (g if n == "final_norm" else jnp.stack(g)) for n, g in G.items()}

    sharded = BIG + SMALL
    parts = [_split(G[n], ax) for n, ax in sharded]
    gs = jnp.stack([_pack([p[d] for p in parts], F32) for d in range(NDEV)])
    gr = _pack([G[n] for n in REPL] + [loss_row[0, 0:1]], F32)
    gs_all, gr_all = grad_exchange(gs, gr)
    outs_s = adamw("adamw_sharded", gs_all, *[_pack([T[n] for n, _ in sharded], F32) for T in (W, Mo, Vo)])
    outs_r = adamw("adamw_replicated", gr_all,
                   *[_pack([T[n] for n in REPL] + [jnp.zeros((1,), F32)], F32) for T in (W, Mo, Vo)])
    res = [dict(), dict(), dict(), dict()]
    for kind in range(4):
        for (n, _), a in zip(sharded, _unpack(outs_s[kind], [W[n].shape for n, _ in sharded])):
            res[kind][n] = a
        for n, a in zip(REPL + ("loss",), _unpack(outs_r[kind], [W[n].shape for n in REPL] + [(1,)])):
            res[kind][n] = a
    loss = res[0]["loss"][0]
    return (loss, grad_x, *[res[0][n] for n in WEIGHTS], *[res[1][n] for n in WEIGHTS],
            *[res[2][n] for n in WEIGHTS], *[res[3][n] for n in WEIGHTS])
```

```python
import functools

import jax
import jax.numpy as jnp
from jax import lax
from jax.experimental import pallas as pl
from jax.experimental.pallas import tpu as pltpu

F32 = jnp.float32
MXU_DTYPE = jnp.bfloat16
EPS = 1e-6
ROPE_THETA = 10000.0
NDEV = 8
LANES = 128
VMEM_LIMIT = 48 * 1024 * 1024

D_MODEL = 1024
D_FF = 4096
LRU_C = 8.0
POOL_WINDOWS = (2, 4, 8, 16)
HALO = 8
MLA_SCALE = 96.0 ** -0.5

ADAM_LR, ADAM_B1, ADAM_B2, ADAM_EPS, ADAM_WD, ADAM_STEP = 0.001, 0.9, 0.999, 1e-08, 0.01, 10


def _mx(v):
    return v.astype(MXU_DTYPE)


def _call(body, *, name, grid, in_specs, out_specs, out_shape, scratch=()):
    return pl.pallas_call(
        body, name=name, grid=grid, in_specs=in_specs, out_specs=out_specs, out_shape=out_shape,
        scratch_shapes=list(scratch),
        compiler_params=pltpu.CompilerParams(
            dimension_semantics=("arbitrary",) * len(grid), vmem_limit_bytes=VMEM_LIMIT),
    )


def _dot(a, b):
    return jnp.dot(a, b, preferred_element_type=F32)


def _dot_nt(a, b):
    return lax.dot_general(a, b, (((1,), (1,)), ((), ())), preferred_element_type=F32)


def _dot_tn(a, b):
    return lax.dot_general(a, b, (((0,), (0,)), ((), ())), preferred_element_type=F32)


def _rms(x, g):
    r = lax.rsqrt(jnp.mean(x * x, axis=-1, keepdims=True) + EPS)
    return (x * r) * g


def _rms_bwd(x, g, dy):
    r = lax.rsqrt(jnp.mean(x * x, axis=-1, keepdims=True) + EPS)
    xh = x * r
    dyg = dy * g
    dx = r * (dyg - xh * jnp.mean(dyg * xh, axis=-1, keepdims=True))
    return dx, jnp.sum(dy * xh, axis=0, keepdims=True)


def _sigmoid(x):
    return 1.0 / (1.0 + jnp.exp(-x))


def _log1p(e):
    u = 1.0 + e
    d = u - 1.0
    return jnp.where(d == 0.0, e, jnp.log(u) * (e / jnp.where(d == 0.0, 1.0, d)))


def _softplus(x):
    return jnp.maximum(x, 0.0) + _log1p(jnp.exp(-jnp.abs(x)))


def _expm1(x):
    u = jnp.exp(x)
    lu = jnp.log(u)
    safe = jnp.where((lu == 0.0) | (u == 0.0), 1.0, lu)
    return jnp.where(u == 1.0, x, jnp.where(u == 0.0, -1.0, (u - 1.0) * x / safe))


_GELU_K = 0.7978845608028654


def _gelu(x):
    return 0.5 * x * (1.0 + jnp.tanh(_GELU_K * (x + 0.044715 * x * x * x)))


def _gelu_grad(x):
    t = jnp.tanh(_GELU_K * (x + 0.044715 * x * x * x))
    return 0.5 * (1.0 + t) + 0.5 * x * (1.0 - t * t) * _GELU_K * (1.0 + 3.0 * 0.044715 * x * x)


def _rope(x, c, sa, sb, half):
    return x * c + pltpu.roll(x, LANES - half, 1) * sa + pltpu.roll(x, half, 1) * sb


def _rope_t(d, c, sa, sb, half):
    return d * c - pltpu.roll(d, LANES - half, 1) * sa - pltpu.roll(d, half, 1) * sb


def _as_cols(a):
    return a if isinstance(a, tuple) else (a, 0, a.shape[1])


def mm_nn(name, a_list, b_list, *, res=None, act=None, norm_g=None, emit_norm=False,
          out_dtype=F32, tm=256, tn=512):
    a_list = [_as_cols(a) for a in a_list]
    M, N = a_list[0][0].shape[0], b_list[0].shape[1]
    tm, tn = min(tm, M), min(tn, N)
    na = len(a_list)

    def body(*refs):
        a_refs, b_refs = refs[:na], refs[na:2 * na]
        k = 2 * na
        g_ref = res_ref = hn_ref = None
        if norm_g is not None:
            g_ref = refs[k]
            k += 1
        if res is not None:
            res_ref = refs[k]
            k += 1
        o_ref = refs[k]
        if emit_norm:
            hn_ref = refs[k + 1]
        acc = None
        for a_ref, b_ref in zip(a_refs, b_refs):
            a = a_ref[...]
            if g_ref is not None:
                a = _rms(a.astype(F32), g_ref[...])
                if hn_ref is not None:
                    hn_ref[...] = a.astype(hn_ref.dtype)
            if act == "relu2":
                a = jnp.maximum(a, 0.0)
                a = a * a
            d = _dot(_mx(a), _mx(b_ref[...]))
            acc = d if acc is None else acc + d
        if res_ref is not None:
            acc = acc + res_ref[...]
        o_ref[...] = acc.astype(o_ref.dtype)

    in_specs = [pl.BlockSpec((tm, w), functools.partial(lambda i, j, cb: (i, cb), cb=cb)) for (_, cb, w) in a_list]
    in_specs += [pl.BlockSpec((b.shape[0], tn), lambda i, j: (0, j)) for b in b_list]
    args = [a for (a, _, _) in a_list] + list(b_list)
    if norm_g is not None:
        in_specs.append(pl.BlockSpec((1, norm_g.shape[1]), lambda i, j: (0, 0)))
        args.append(norm_g)
    if res is not None:
        in_specs.append(pl.BlockSpec((tm, tn), lambda i, j: (i, j)))
        args.append(res)
    out_specs = [pl.BlockSpec((tm, tn), lambda i, j: (i, j))]
    out_shape = [jax.ShapeDtypeStruct((M, N), out_dtype)]
    if emit_norm:
        K = a_list[0][2]
        out_specs.append(pl.BlockSpec((tm, K), lambda i, j: (i, 0)))
        out_shape.append(jax.ShapeDtypeStruct((M, K), MXU_DTYPE))
    out = _call(body, name=name, grid=(M // tm, N // tn), in_specs=in_specs, out_specs=out_specs,
                out_shape=out_shape)(*args)
    return out if emit_norm else out[0]


def mm_nt(name, a_list, b_list, *, relu2_of=None, out_dtype=F32, tm=256, tn=512):
    a_list = [_as_cols(a) for a in a_list]
    b_list = [_as_cols(b) for b in b_list]
    M, N = a_list[0][0].shape[0], b_list[0][0].shape[0]
    tm, tn = min(tm, M), min(tn, N)
    na = len(a_list)

    def body(*refs):
        a_refs, b_refs = refs[:na], refs[na:2 * na]
        u_ref = refs[2 * na] if relu2_of is not None else None
        o_ref = refs[-1]
        acc = None
        for a_ref, b_ref in zip(a_refs, b_refs):
            d = _dot_nt(_mx(a_ref[...]), _mx(b_ref[...]))
            acc = d if acc is None else acc + d
        if u_ref is not None:
            acc = acc * (2.0 * jnp.maximum(u_ref[...], 0.0))
        o_ref[...] = acc.astype(o_ref.dtype)

    in_specs = [pl.BlockSpec((tm, w), functools.partial(lambda i, j, cb: (i, cb), cb=cb)) for (_, cb, w) in a_list]
    in_specs += [pl.BlockSpec((tn, w), functools.partial(lambda i, j, cb: (j, cb), cb=cb)) for (_, cb, w) in b_list]
    args = [a for (a, _, _) in a_list] + [b for (b, _, _) in b_list]
    if relu2_of is not None:
        in_specs.append(pl.BlockSpec((tm, tn), lambda i, j: (i, j)))
        args.append(relu2_of)
    return _call(body, name=name, grid=(M // tm, N // tn), in_specs=in_specs,
                 out_specs=pl.BlockSpec((tm, tn), lambda i, j: (i, j)),
                 out_shape=jax.ShapeDtypeStruct((M, N), out_dtype))(*args)


def mm_tn(name, a, b, *, act=None, tm=512, tn=512, tk=512):
    a, acb, Ma = _as_cols(a)
    b, bcb, Nb = _as_cols(b)
    S = a.shape[0]
    tm, tn, tk = min(tm, Ma), min(tn, Nb), min(tk, S)
    a0, b0 = acb * (Ma // tm), bcb * (Nb // tn)

    def body(a_ref, b_ref, o_ref):
        @pl.when(pl.program_id(2) == 0)
        def _():
            o_ref[...] = jnp.zeros_like(o_ref)

        av = a_ref[...]
        if act == "relu2":
            av = jnp.maximum(av, 0.0)
            av = av * av
        o_ref[...] += _dot_tn(_mx(av), _mx(b_ref[...]))

    return _call(body, name=name, grid=(Ma // tm, Nb // tn, S // tk),
                 in_specs=[pl.BlockSpec((tk, tm), lambda i, j, k: (k, a0 + i)),
                           pl.BlockSpec((tk, tn), lambda i, j, k: (k, b0 + j))],
                 out_specs=pl.BlockSpec((tm, tn), lambda i, j, k: (i, j)),
                 out_shape=jax.ShapeDtypeStruct((Ma, Nb), F32))(a, b)


def rms_bwd(name, x, g, dh, dres, T=256):
    S, D = x.shape
    T = min(T, S)

    def body(x_ref, g_ref, dh_ref, dres_ref, dx_ref, dg_ref):
        @pl.when(pl.program_id(0) == 0)
        def _():
            dg_ref[...] = jnp.zeros_like(dg_ref)

        dx, dg = _rms_bwd(x_ref[...], g_ref[...], dh_ref[...])
        dx_ref[...] = dres_ref[...] + dx
        dg_ref[...] += dg

    row = pl.BlockSpec((T, D), lambda i: (i, 0))
    vec = pl.BlockSpec((1, D), lambda i: (0, 0))
    return _call(body, name=name, grid=(S // T,), in_specs=[row, vec, row, row], out_specs=[row, vec],
                 out_shape=[jax.ShapeDtypeStruct((S, D), F32), jax.ShapeDtypeStruct((1, D), F32)])(x, g, dh, dres)


def loss_head(x, g, target, T=256):
    S, D = x.shape
    T = min(T, S)

    def body(x_ref, g_ref, t_ref, loss_ref, dx_ref, dg_ref):
        @pl.when(pl.program_id(0) == 0)
        def _():
            dg_ref[...] = jnp.zeros_like(dg_ref)
            loss_ref[...] = jnp.zeros_like(loss_ref)

        x = x_ref[...]
        err = _rms(x, g_ref[...]) - t_ref[...]
        loss_ref[...] += 0.5 * jnp.sum(jnp.sum(err * err, axis=-1, keepdims=True) / D, axis=0, keepdims=True)
        dx, dg = _rms_bwd(x, g_ref[...], err / D)
        dx_ref[...] = dx
        dg_ref[...] += dg

    row = pl.BlockSpec((T, D), lambda i: (i, 0))
    vec = pl.BlockSpec((1, D), lambda i: (0, 0))
    return _call(body, name="loss_head", grid=(S // T,), in_specs=[row, vec, row],
                 out_specs=[pl.BlockSpec((1, LANES), lambda i: (0, 0)), row, vec],
                 out_shape=[jax.ShapeDtypeStruct((1, LANES), F32), jax.ShapeDtypeStruct((S, D), F32),
                            jax.ShapeDtypeStruct((1, D), F32)])(x, g, target)


def rope_tables(S, lo, half):
    inv = ROPE_THETA ** (-jnp.arange(half, dtype=F32) / half)
    ang = jnp.arange(S, dtype=F32)[:, None] * inv[None, :]
    cos, sin = jnp.cos(ang), jnp.sin(ang)
    one = lambda n: jnp.ones((S, n), F32)
    zero = lambda n: jnp.zeros((S, n), F32)
    hi = LANES - lo - 2 * half
    c = jnp.concatenate([one(lo), cos, cos, one(hi)], axis=1)
    sa = jnp.concatenate([zero(lo), -sin, zero(half), zero(hi)], axis=1)
    sb = jnp.concatenate([zero(lo), zero(half), sin, zero(hi)], axis=1)
    return c, sa, sb


SWA_BLOCK = 128
SWA_HALF = 32


def swa_prep(z, tabs, T=256):
    S = z.shape[0]
    T = min(T, S)

    def body(z_ref, c_ref, sa_ref, sb_ref, o_ref):
        c, sa, sb = c_ref[...], sa_ref[...], sb_ref[...]
        for s in range(10):
            sl = slice(s * LANES, (s + 1) * LANES)
            y = _rope(z_ref[:, sl], c, sa, sb, SWA_HALF)
            if s < 8:
                y = y * 0.125
            o_ref[:, sl] = y.astype(o_ref.dtype)
        o_ref[:, 1280:1536] = z_ref[:, 1280:1536].astype(o_ref.dtype)

    tab = pl.BlockSpec((T, LANES), lambda i: (i, 0))
    return _call(body, name="swa_prep", grid=(S // T,),
                 in_specs=[pl.BlockSpec((T, 1536), lambda i: (i, 0)), tab, tab, tab],
                 out_specs=pl.BlockSpec((T, 1536), lambda i: (i, 0)),
                 out_shape=jax.ShapeDtypeStruct((S, 1536), MXU_DTYPE))(z, *tabs)


def _swa_valid(n, S):
    B = SWA_BLOCK
    i = lax.broadcasted_iota(jnp.int32, (B, 3 * B), 0)
    j = lax.broadcasted_iota(jnp.int32, (B, 3 * B), 1)
    kpos = j + (n - 1) * B
    return (jnp.abs(j - B - i) <= B) & (kpos >= 0) & (kpos < S)


def swa_fwd(qkv, sink):
    S = qkv.shape[0]
    B = SWA_BLOCK
    nb = S // B

    def body(sink_ref, q_ref, kp_ref, kc_ref, kn_ref, vp_ref, vc_ref, vn_ref, o_ref, st_ref):
        n = pl.program_id(0)
        valid = _swa_valid(n, S)
        lane = lax.broadcasted_iota(jnp.int32, (B, LANES), 1)
        st = jnp.zeros((B, LANES), F32)
        for hk in range(2):
            sl = slice(hk * LANES, (hk + 1) * LANES)
            k3 = jnp.concatenate([kp_ref[:, sl], kc_ref[:, sl], kn_ref[:, sl]], axis=0)
            v3 = jnp.concatenate([vp_ref[:, sl], vc_ref[:, sl], vn_ref[:, sl]], axis=0)
            for g in range(4):
                h = hk * 4 + g
                hs = slice(h * LANES, (h + 1) * LANES)
                s = jnp.where(valid, _dot_nt(q_ref[:, hs], k3), -jnp.inf)
                sk = sink_ref[h]
                m = jnp.maximum(jnp.max(s, axis=-1, keepdims=True), sk)
                p = jnp.exp(s - m)
                den = jnp.sum(p, axis=-1, keepdims=True) + jnp.exp(sk - m)
                p = p / den
                o_ref[:, hs] = _dot(_mx(p), v3).astype(o_ref.dtype)
                st = jnp.where(lane == h, m + jnp.log(den), st)
        st_ref[...] = st

    kv = lambda cb, d: pl.BlockSpec((B, 2 * LANES), lambda n: (jnp.clip(n + d, 0, nb - 1), cb))
    return _call(body, name="swa_fwd", grid=(nb,),
                 in_specs=[pl.BlockSpec(memory_space=pltpu.SMEM),
                           pl.BlockSpec((B, 1024), lambda n: (n, 0)),
                           kv(4, -1), kv(4, 0), kv(4, 1), kv(5, -1), kv(5, 0), kv(5, 1)],
                 out_specs=[pl.BlockSpec((B, 1024), lambda n: (n, 0)), pl.BlockSpec((B, LANES), lambda n: (n, 0))],
                 out_shape=[jax.ShapeDtypeStruct((S, 1024), MXU_DTYPE), jax.ShapeDtypeStruct((S, LANES), F32)],
                 )(sink, qkv, qkv, qkv, qkv, qkv, qkv, qkv)


def swa_bwd(qkv, sink, lse, dycat, tabs):
    S = qkv.shape[0]
    B = SWA_BLOCK
    nb = S // B

    def body(sink_ref, q_ref, kp_ref, kc_ref, kn_ref, vp_ref, vc_ref, vn_ref, do_ref, st_ref,
             cq_ref, saq_ref, sbq_ref, ck_ref, sak_ref, sbk_ref,
             dq_ref, dk_ref, dv_ref, dsink_ref, dk_acc, dv_acc):
        n = pl.program_id(0)

        @pl.when(n == 0)
        def _():
            dk_acc[...] = jnp.zeros_like(dk_acc)
            dv_acc[...] = jnp.zeros_like(dv_acc)
            dsink_ref[...] = jnp.zeros_like(dsink_ref)

        @pl.when(n < nb)
        def _():
            valid = _swa_valid(n, S)
            lane = lax.broadcasted_iota(jnp.int32, (B, LANES), 1)
            lane1 = lax.broadcasted_iota(jnp.int32, (1, LANES), 1)
            st = st_ref[...]
            cq, saq, sbq = cq_ref[...], saq_ref[...], sbq_ref[...]
            dsink = jnp.zeros((1, LANES), F32)
            for hk in range(2):
                sl = slice(hk * LANES, (hk + 1) * LANES)
                k3 = jnp.concatenate([kp_ref[:, sl], kc_ref[:, sl], kn_ref[:, sl]], axis=0)
                v3 = jnp.concatenate([vp_ref[:, sl], vc_ref[:, sl], vn_ref[:, sl]], axis=0)
                dk3 = jnp.zeros((3 * B, LANES), F32)
                dv3 = jnp.zeros((3 * B, LANES), F32)
                for g in range(4):
                    h = hk * 4 + g
                    hs = slice(h * LANES, (h + 1) * LANES)
                    q = q_ref[:, hs]
                    do = _mx(do_ref[:, hs])
                    lse_h = jnp.sum(jnp.where(lane == h, st, 0.0), axis=-1, keepdims=True)
                    p = jnp.where(valid, jnp.exp(_dot_nt(q, k3) - lse_h), 0.0)
                    dp = _dot_nt(do, v3)
                    dsum = jnp.sum(p * dp, axis=-1, keepdims=True)
                    ds = _mx(p * (dp - dsum))
                    dq_ref[:, hs] = _rope_t(_dot(ds, k3) * 0.125, cq, saq, sbq, SWA_HALF)
                    dk3 = dk3 + _dot_tn(ds, q)
                    dv3 = dv3 + _dot_tn(_mx(p), do)
                    dsk = -jnp.sum(jnp.exp(sink_ref[h] - lse_h) * dsum, axis=0, keepdims=True)
                    dsink = jnp.where(lane1 == h, dsk, dsink)
                dk_acc[:, sl] += dk3
                dv_acc[:, sl] += dv3
            dsink_ref[...] += dsink

        ck, sak, sbk = ck_ref[...], sak_ref[...], sbk_ref[...]
        for hk in range(2):
            sl = slice(hk * LANES, (hk + 1) * LANES)
            dk_ref[:, sl] = _rope_t(dk_acc[0:B, sl], ck, sak, sbk, SWA_HALF)
        dv_ref[...] = dv_acc[0:B, :]
        for acc in (dk_acc, dv_acc):
            acc[0:B, :] = acc[B:2 * B, :]
            acc[B:2 * B, :] = acc[2 * B:3 * B, :]
            acc[2 * B:3 * B, :] = jnp.zeros((B, 2 * LANES), F32)

    qn = lambda n: jnp.minimum(n, nb - 1)
    kv = lambda cb, d: pl.BlockSpec((B, 2 * LANES), lambda n: (jnp.clip(qn(n) + d, 0, nb - 1), cb))
    qrow = lambda w: pl.BlockSpec((B, w), lambda n: (qn(n), 0))
    krow = lambda w: pl.BlockSpec((B, w), lambda n: (jnp.maximum(n - 1, 0), 0))
    return _call(body, name="swa_bwd", grid=(nb + 1,),
                 in_specs=[pl.BlockSpec(memory_space=pltpu.SMEM), qrow(1024),
                           kv(4, -1), kv(4, 0), kv(4, 1), kv(5, -1), kv(5, 0), kv(5, 1),
                           qrow(1024), qrow(LANES),
                           qrow(LANES), qrow(LANES), qrow(LANES), krow(LANES), krow(LANES), krow(LANES)],
                 out_specs=[qrow(1024), krow(2 * LANES), krow(2 * LANES), pl.BlockSpec((1, LANES), lambda n: (0, 0))],
                 out_shape=[jax.ShapeDtypeStruct((S, 1024), F32), jax.ShapeDtypeStruct((S, 2 * LANES), F32),
                            jax.ShapeDtypeStruct((S, 2 * LANES), F32), jax.ShapeDtypeStruct((1, LANES), F32)],
                 scratch=[pltpu.VMEM((3 * B, 2 * LANES), F32), pltpu.VMEM((3 * B, 2 * LANES), F32)],
                 )(sink, qkv, qkv, qkv, qkv, qkv, qkv, qkv, dycat, lse, *tabs, *tabs)


def _halo_specs(T, S, w, cb):
    r = T // HALO
    last = S // HALO - 1
    return [pl.BlockSpec((HALO, w), lambda i: (jnp.maximum(i * r - 1, 0), cb)),
            pl.BlockSpec((T, w), lambda i: (i, cb)),
            pl.BlockSpec((HALO, w), lambda i: (jnp.minimum((i + 1) * r, last), cb))]


def _fill_ext(ext, prev_ref, cur_ref, next_ref, i, nt, T):
    ext[0:HALO, :] = jnp.where(i > 0, prev_ref[...], 0.0).astype(F32)
    ext[HALO:HALO + T, :] = cur_ref[...].astype(F32)
    ext[HALO + T:2 * HALO + T, :] = jnp.where(i < nt - 1, next_ref[...], 0.0).astype(F32)


def _pool_cnt(t, half, S):
    return (jnp.clip(t + half, 0, S) - jnp.clip(t - half, 0, S)).astype(F32)


def pool_fwd(z, w_pool, scale, T=256):
    S = z.shape[0]
    T = min(T, S)
    nt = S // T

    def body(up_ref, uc_ref, un_ref, w_ref, sc_ref, o_ref, ext):
        i = pl.program_id(0)
        _fill_ext(ext, up_ref, uc_ref, un_ref, i, nt, T)
        t = i * T + lax.broadcasted_iota(jnp.int32, (T, 1), 0)
        for g, win in enumerate(POOL_WINDOWS):
            half = win // 2
            sl = slice(g * LANES, (g + 1) * LANES)
            acc = ext[pl.ds(HALO - half, T), sl]
            for off in range(-half + 1, half):
                acc = acc + ext[pl.ds(HALO + off, T), sl]
            d = acc / _pool_cnt(t, half, S) - ext[pl.ds(HALO, T), sl]
            o_ref[:, sl] = (_dot(_mx(d), w_ref[g]) * sc_ref[:, sl]).astype(o_ref.dtype)

    return _call(body, name="pool_fwd", grid=(nt,),
                 in_specs=_halo_specs(T, S, 512, 3) + [pl.BlockSpec((4, LANES, LANES), lambda i: (0, 0, 0)),
                                                      pl.BlockSpec((1, 512), lambda i: (0, 0))],
                 out_specs=pl.BlockSpec((T, 512), lambda i: (i, 0)),
                 out_shape=jax.ShapeDtypeStruct((S, 512), MXU_DTYPE),
                 scratch=[pltpu.VMEM((T + 2 * HALO, 512), F32)])(z, z, z, w_pool, scale)


def pool_bwd(z, dycat, w_pool, scale, T=256):
    S = z.shape[0]
    T = min(T, S)
    nt = S // T
    TE = T + 2 * HALO

    def body(up_ref, uc_ref, un_ref, yp_ref, yc_ref, yn_ref, w_ref, sc_ref, du_ref, dw_ref, dsc_ref, extu, exty, exte):
        i = pl.program_id(0)

        @pl.when(i == 0)
        def _():
            dw_ref[...] = jnp.zeros_like(dw_ref)
            dsc_ref[...] = jnp.zeros_like(dsc_ref)

        _fill_ext(extu, up_ref, uc_ref, un_ref, i, nt, T)
        _fill_ext(exty, yp_ref, yc_ref, yn_ref, i, nt, T)
        t = i * T + lax.broadcasted_iota(jnp.int32, (T, 1), 0)
        te = i * T - HALO + lax.broadcasted_iota(jnp.int32, (TE, 1), 0)
        for g, win in enumerate(POOL_WINDOWS):
            half = win // 2
            sl = slice(g * LANES, (g + 1) * LANES)
            w = w_ref[g]
            acc = extu[pl.ds(HALO - half, T), sl]
            for off in range(-half + 1, half):
                acc = acc + extu[pl.ds(HALO + off, T), sl]
            d = _mx(acc / _pool_cnt(t, half, S) - extu[pl.ds(HALO, T), sl])
            dy = exty[pl.ds(HALO, T), sl]
            dsc_ref[:, sl] += jnp.sum(dy * _dot(d, w), axis=0, keepdims=True)
            dw_ref[g] += _dot_tn(d, _mx(dy * sc_ref[:, sl]))
            dd = _dot_nt(_mx(exty[:, sl] * sc_ref[:, sl]), w)
            exte[:, sl] = dd / jnp.maximum(_pool_cnt(te, half, S), 1.0)
            acc = exte[pl.ds(HALO - half + 1, T), sl]
            for off in range(-half + 2, half + 1):
                acc = acc + exte[pl.ds(HALO + off, T), sl]
            du_ref[:, sl] = acc - dd[HALO:HALO + T, :]

    return _call(body, name="pool_bwd", grid=(nt,),
                 in_specs=_halo_specs(T, S, 512, 3) + _halo_specs(T, S, 512, 2)
                 + [pl.BlockSpec((4, LANES, LANES), lambda i: (0, 0, 0)), pl.BlockSpec((1, 512), lambda i: (0, 0))],
                 out_specs=[pl.BlockSpec((T, 512), lambda i: (i, 0)), pl.BlockSpec((4, LANES, LANES), lambda i: (0, 0, 0)),
                            pl.BlockSpec((1, 512), lambda i: (0, 0))],
                 out_shape=[jax.ShapeDtypeStruct((S, 512), F32), jax.ShapeDtypeStruct((4, LANES, LANES), F32),
                            jax.ShapeDtypeStruct((1, 512), F32)],
                 scratch=[pltpu.VMEM((TE, 512), F32)] * 3)(z, z, z, dycat, dycat, dycat, w_pool, scale)


MLA_HALF = 16
MLA_V_ONE = 64


def mla_prep(z, g_cq, g_ckv, w_uq, w_k, w_v, tabs_q, tabs_k, T=256):
    S = z.shape[0]
    T = min(T, S)

    def body(z_ref, gq_ref, gkv_ref, wq_ref, wk_ref, wv_ref, cq_ref, saq_ref, sbq_ref, ck_ref, sak_ref, sbk_ref,
             q_ref, k_ref, v_ref, nq_ref, nkv_ref):
        nq = _mx(_rms(z_ref[:, 0:256], gq_ref[...]))
        nkv = _mx(_rms(z_ref[:, 256:384], gkv_ref[...]))
        nq_ref[...] = nq
        nkv_ref[...] = nkv
        q = _dot(nq, wq_ref[...])
        kn = _dot(nkv, wk_ref[...])
        lane = lax.broadcasted_iota(jnp.int32, (T, 1024), 1)
        v_ref[...] = jnp.where(lane % LANES == MLA_V_ONE, 1.0, _dot(nkv, wv_ref[...])).astype(v_ref.dtype)
        kr = pltpu.roll(_rope(z_ref[:, 384:512], ck_ref[...], sak_ref[...], sbk_ref[...], MLA_HALF), 64, 1)
        cq, saq, sbq = cq_ref[...], saq_ref[...], sbq_ref[...]
        for h in range(8):
            hs = slice(h * LANES, (h + 1) * LANES)
            q_ref[:, hs] = _rope(q[:, hs], cq, saq, sbq, MLA_HALF).astype(q_ref.dtype)
            k_ref[:, hs] = (kn[:, hs] + kr).astype(k_ref.dtype)

    tab = pl.BlockSpec((T, LANES), lambda i: (i, 0))
    full = lambda a: pl.BlockSpec(a.shape, lambda i: (0, 0))
    row = lambda w: pl.BlockSpec((T, w), lambda i: (i, 0))
    sd = lambda w: jax.ShapeDtypeStruct((S, w), MXU_DTYPE)
    return _call(body, name="mla_prep", grid=(S // T,),
                 in_specs=[row(512), full(g_cq), full(g_ckv), full(w_uq), full(w_k), full(w_v)] + [tab] * 6,
                 out_specs=[row(1024), row(1024), row(1024), row(256), row(128)],
                 out_shape=[sd(1024), sd(1024), sd(1024), sd(256), sd(128)],
                 )(z, g_cq, g_ckv, w_uq, w_k, w_v, *tabs_q, *tabs_k)


def _col_to_row(c):
    return jnp.transpose(jnp.broadcast_to(c, (c.shape[0], LANES)))[0:1, :]


def mla_fwd(q, k, v, TQ=512, TK=512):
    S = q.shape[0]
    TQ, TK = min(TQ, S), min(TK, S // 2)
    nk = S // TK
    ng = TK // LANES
    assert nk % 2 == 0

    def body(q_ref, k_ref, v_ref, o_ref, lse_ref, m_s, acc_s, s_a, s_b):
        m_s[...] = jnp.full_like(m_s, -jnp.inf)
        acc_s[...] = jnp.zeros_like(acc_s)
        q = q_ref[...]

        def scores(c):
            return _dot_nt(q, k_ref[pl.ds(pl.multiple_of(c * TK, TK), TK), :])

        def softmax_pv(s_ref, c):
            parts = [s_ref[:, g * LANES:(g + 1) * LANES] for g in range(ng)]
            mg = parts[0]
            for g in range(1, ng):
                mg = jnp.maximum(mg, parts[g])
            m_old = m_s[...]
            m_new = jnp.maximum(m_old, jnp.max(mg, axis=-1, keepdims=True))
            alpha = jnp.exp((m_old - m_new) * MLA_SCALE)
            p = jnp.concatenate([jnp.exp((parts[g] - m_new) * MLA_SCALE) for g in range(ng)], axis=1)
            acc_s[...] = alpha * acc_s[...] + _dot(_mx(p), v_ref[pl.ds(pl.multiple_of(c * TK, TK), TK), :])
            m_s[...] = m_new

        s_a[...] = scores(0)

        def pair(jj, carry):
            c = 2 * jj
            s_b[...] = scores(c + 1)
            softmax_pv(s_a, c)
            s_a[...] = scores(jnp.minimum(c + 2, nk - 1))
            softmax_pv(s_b, c + 1)
            return carry

        lax.fori_loop(0, nk // 2, pair, 0)
        acc = acc_s[...]
        lane = lax.broadcasted_iota(jnp.int32, (TQ, LANES), 1)
        den = jnp.sum(jnp.where(lane == MLA_V_ONE, acc, 0.0), axis=-1, keepdims=True)
        o_ref[...] = jnp.where(lane < MLA_V_ONE, acc / den, 0.0)
        lse_ref[0] = _col_to_row(m_s[:, 0:1] * MLA_SCALE + jnp.log(den))

    qs = pl.BlockSpec((TQ, LANES), lambda h, i: (i, h))
    kv = pl.BlockSpec((S, LANES), lambda h, i: (0, h))
    return _call(body, name="mla_fwd", grid=(8, S // TQ), in_specs=[qs, kv, kv],
                 out_specs=[qs, pl.BlockSpec((1, 1, TQ), lambda h, i: (h, 0, i))],
                 out_shape=[jax.ShapeDtypeStruct((S, 1024), F32), jax.ShapeDtypeStruct((8, 1, S), F32)],
                 scratch=[pltpu.VMEM((TQ, LANES), F32), pltpu.VMEM((TQ, LANES), F32),
                          pltpu.VMEM((TQ, TK), F32), pltpu.VMEM((TQ, TK), F32)],
                 )(q, k, v)


def mla_delta(o, dycat, TQ=512):
    S = o.shape[0]
    TQ = min(TQ, S)

    def body(o_ref, do_ref, d_ref, dob_ref):
        do = do_ref[...]
        d_ref[0] = _col_to_row(jnp.sum(o_ref[...] * do, axis=-1, keepdims=True))
        dob_ref[...] = do.astype(dob_ref.dtype)

    qs = pl.BlockSpec((TQ, LANES), lambda h, i: (i, h))
    return _call(body, name="mla_delta", grid=(8, S // TQ), in_specs=[qs, qs],
                 out_specs=[pl.BlockSpec((1, 1, TQ), lambda h, i: (h, 0, i)), qs],
                 out_shape=[jax.ShapeDtypeStruct((8, 1, S), F32), jax.ShapeDtypeStruct((S, 1024), MXU_DTYPE)])(o, dycat)


def mla_bwd(q, k, v, do, lse, delta, TQ=512, TK=512):
    S = q.shape[0]
    TQ, TK = min(TQ, S // 2), min(TK, S)
    nq, nk = S // TQ, S // TK
    assert nq % 2 == 0
    lse = lse.reshape(8, nq, 1, TQ)
    delta = delta.reshape(8, nq, 1, TQ)

    def body(q_ref, do_ref, k_ref, v_ref, lse_ref, d_ref, dq_ref, dk_ref, dv_ref, dk_s, dv_s, s_a, p_a, s_b, p_b):
        @pl.when(pl.program_id(1) == 0)
        def _():
            dq_ref[...] = jnp.zeros_like(dq_ref)

        dk_s[...] = jnp.zeros_like(dk_s)
        dv_s[...] = jnp.zeros_like(dv_s)
        kk, vv = k_ref[...], v_ref[...]

        def rows(c):
            return pl.ds(pl.multiple_of(c * TQ, TQ), TQ)

        def products(c, s_ref, p_ref):
            s_ref[...] = _dot_nt(kk, q_ref[rows(c), :])
            p_ref[...] = _dot_nt(vv, do_ref[rows(c), :])

        def consume(c, s_ref, p_ref):
            qc, doc = q_ref[rows(c), :], do_ref[rows(c), :]
            pt = jnp.exp(s_ref[...] * MLA_SCALE - lse_ref[0, c])
            dv_s[...] += _dot(_mx(pt), doc)
            ds = _mx(pt * (p_ref[...] - d_ref[0, c]) * MLA_SCALE)
            dk_s[...] += _dot(ds, qc)
            dq_ref[rows(c), :] += _dot_tn(ds, kk)

        products(0, s_a, p_a)

        def pair(jj, carry):
            c = 2 * jj
            products(c + 1, s_b, p_b)
            consume(c, s_a, p_a)
            products(jnp.minimum(c + 2, nq - 1), s_a, p_a)
            consume(c + 1, s_b, p_b)
            return carry

        lax.fori_loop(0, nq // 2, pair, 0)
        dk_ref[...] = dk_s[...]
        dv_ref[...] = dv_s[...]

    full = pl.BlockSpec((S, LANES), lambda h, j: (0, h))
    ks = pl.BlockSpec((TK, LANES), lambda h, j: (j, h))
    st = pl.BlockSpec((1, nq, 1, TQ), lambda h, j: (h, 0, 0, 0))
    sd = jax.ShapeDtypeStruct((S, 1024), F32)
    return _call(body, name="mla_bwd", grid=(8, nk), in_specs=[full, full, ks, ks, st, st],
                 out_specs=[full, ks, ks], out_shape=[sd, sd, sd],
                 scratch=[pltpu.VMEM((TK, LANES), F32), pltpu.VMEM((TK, LANES), F32)] + [pltpu.VMEM((TK, TQ), F32)] * 4,
                 )(q, do, k, v, lse, delta)


def mla_prep_bwd(z, g_cq, g_ckv, w_uq, w_k, w_v, dq, dk, dv, tabs_q, tabs_k, T=256):
    S = z.shape[0]
    T = min(T, S)

    def body(z_ref, gq_ref, gkv_ref, wq_ref, wk_ref, wv_ref, dq_ref, dk_ref, dv_ref,
             cq_ref, saq_ref, sbq_ref, ck_ref, sak_ref, sbk_ref, dz_ref, dqp_ref, dgq_ref, dgkv_ref):
        @pl.when(pl.program_id(0) == 0)
        def _():
            dgq_ref[...] = jnp.zeros_like(dgq_ref)
            dgkv_ref[...] = jnp.zeros_like(dgkv_ref)

        cq, saq, sbq = cq_ref[...], saq_ref[...], sbq_ref[...]
        dkr = jnp.zeros((T, LANES), F32)
        for h in range(8):
            hs = slice(h * LANES, (h + 1) * LANES)
            dqp_ref[:, hs] = _rope_t(dq_ref[:, hs], cq, saq, sbq, MLA_HALF).astype(dqp_ref.dtype)
            dkr = dkr + dk_ref[:, hs]
        lane = lax.broadcasted_iota(jnp.int32, (T, LANES), 1)
        dkr = jnp.where(lane < 2 * MLA_HALF, pltpu.roll(dkr, 64, 1), 0.0)
        dz_ref[:, 384:512] = _rope_t(dkr, ck_ref[...], sak_ref[...], sbk_ref[...], MLA_HALF)
        dnq = _dot_nt(dqp_ref[...], wq_ref[...])
        dx, dg = _rms_bwd(z_ref[:, 0:256], gq_ref[...], dnq)
        dz_ref[:, 0:256] = dx
        dgq_ref[...] += dg
        dnkv = _dot_nt(_mx(dk_ref[...]), wk_ref[...]) + _dot_nt(_mx(dv_ref[...]), wv_ref[...])
        dx, dg = _rms_bwd(z_ref[:, 256:384], gkv_ref[...], dnkv)
        dz_ref[:, 256:384] = dx
        dgkv_ref[...] += dg

    tab = pl.BlockSpec((T, LANES), lambda i: (i, 0))
    full = lambda a: pl.BlockSpec(a.shape, lambda i: (0, 0))
    row = lambda w: pl.BlockSpec((T, w), lambda i: (i, 0))
    return _call(body, name="mla_prep_bwd", grid=(S // T,),
                 in_specs=[row(512), full(g_cq), full(g_ckv), full(w_uq), full(w_k), full(w_v),
                           row(1024), row(1024), row(1024)] + [tab] * 6,
                 out_specs=[row(512), row(1024), full(g_cq), full(g_ckv)],
                 out_shape=[jax.ShapeDtypeStruct((S, 512), F32), jax.ShapeDtypeStruct((S, 1024), MXU_DTYPE),
                            jax.ShapeDtypeStruct(g_cq.shape, F32), jax.ShapeDtypeStruct(g_ckv.shape, F32)],
                 )(z, g_cq, g_ckv, w_uq, w_k, w_v, dq, dk, dv, *tabs_q, *tabs_k)


def _lru_gates(xc, w_ref, bias_ref, lam_ref):
    pre = _dot(_mx(xc), w_ref[...]) + bias_ref[...]
    out = []
    for d in range(2):
        r = _sigmoid(pre[:, d * 1024:d * 1024 + 512])
        ig = _sigmoid(pre[:, d * 1024 + 512:(d + 1) * 1024])
        log_a = -LRU_C * r * _softplus(-lam_ref[:, d * 512:(d + 1) * 512])
        out.append((r, ig, jnp.exp(log_a), jnp.sqrt(-_expm1(2.0 * log_a))))
    return out


def lru_pre(z, conv_w, conv_b, w_gate, b_gate, lam, T=256):
    S = z.shape[0]
    T = min(T, S)
    nt = S // T

    def body(xp_ref, xcur_ref, xn_ref, cw_ref, cb_ref, w_ref, bias_ref, lam_ref, xc_ref, a0_ref, b0_ref, a1_ref, b1_ref, ext):
        i = pl.program_id(0)
        _fill_ext(ext, xp_ref, xcur_ref, xn_ref, i, nt, T)
        xc = cb_ref[...] + cw_ref[0:1, :] * ext[pl.ds(HALO - 2, T), :]
        for j in range(1, 4):
            xc = xc + cw_ref[j:j + 1, :] * ext[pl.ds(HALO - 2 + j, T), :]
        xc_ref[...] = xc
        (_, i0, a0, m0), (_, i1, a1, m1) = _lru_gates(xc, w_ref, bias_ref, lam_ref)
        a0_ref[...] = a0
        b0_ref[...] = m0 * (i0 * xc)
        a1_ref[...] = a1
        b1_ref[...] = m1 * (i1 * xc)

    full = lambda a: pl.BlockSpec(a.shape, lambda i: (0, 0))
    row = pl.BlockSpec((T, 512), lambda i: (i, 0))
    sd = jax.ShapeDtypeStruct((S, 512), F32)
    return _call(body, name="lru_pre", grid=(nt,),
                 in_specs=_halo_specs(T, S, 512, 1) + [full(conv_w), full(conv_b), full(w_gate), full(b_gate), full(lam)],
                 out_specs=[row] * 5, out_shape=[sd] * 5,
                 scratch=[pltpu.VMEM((T + 2 * HALO, 512), F32)])(z, z, z, conv_w, conv_b, w_gate, b_gate, lam)


def lru_scan(name, af, bf, ar, br, *, adjoint, T=256):
    S, W = af.shape
    T = min(T, S)
    nt = S // T
    nc = T // 8

    def body(af_ref, bf_ref, ar_ref, br_ref, hf_ref, hr_ref, cf, cr):
        @pl.when(pl.program_id(0) == 0)
        def _():
            cf[...] = jnp.zeros_like(cf)
            cr[...] = jnp.zeros_like(cr)

        row = lax.broadcasted_iota(jnp.int32, (8, W), 0)

        def step(a, b, carry):
            if adjoint:
                val = b + carry
                return val, a * val
            val = a * carry + b
            return val, val

        def chunk(c, carry):
            hf, hr = carry
            of = pl.multiple_of(c * 8, 8)
            orv = pl.multiple_of((nc - 1 - c) * 8, 8)
            a8, b8 = af_ref[pl.ds(of, 8), :], bf_ref[pl.ds(of, 8), :]
            ra8, rb8 = ar_ref[pl.ds(orv, 8), :], br_ref[pl.ds(orv, 8), :]
            outf = jnp.zeros((8, W), F32)
            outr = jnp.zeros((8, W), F32)
            for k in range(8):
                val, hf = step(a8[k:k + 1, :], b8[k:k + 1, :], hf)
                outf = jnp.where(row == k, val, outf)
                kr = 7 - k
                val, hr = step(ra8[kr:kr + 1, :], rb8[kr:kr + 1, :], hr)
                outr = jnp.where(row == kr, val, outr)
            hf_ref[pl.ds(of, 8), :] = outf
            hr_ref[pl.ds(orv, 8), :] = outr
            return hf, hr

        hf, hr = lax.fori_loop(0, nc, chunk, (cf[0:1, :], cr[0:1, :]))
        cf[0:1, :] = hf
        cr[0:1, :] = hr

    fw = pl.BlockSpec((T, W), lambda i: (i, 0))
    rv = pl.BlockSpec((T, W), lambda i: (nt - 1 - i, 0))
    sd = jax.ShapeDtypeStruct((S, W), F32)
    return _call(body, name=name, grid=(nt,), in_specs=[fw, fw, rv, rv], out_specs=[fw, rv], out_shape=[sd, sd],
                 scratch=[pltpu.VMEM((8, W), F32), pltpu.VMEM((8, W), F32)])(af, bf, ar, br)


def lru_gate(h0, h1, z, T=256):
    S = z.shape[0]
    T = min(T, S)

    def body(h0_ref, h1_ref, xg_ref, y_ref):
        y_ref[...] = ((h0_ref[...] + h1_ref[...]) * _gelu(xg_ref[...])).astype(y_ref.dtype)

    row = pl.BlockSpec((T, 512), lambda i: (i, 0))
    return _call(body, name="lru_gate", grid=(S // T,), in_specs=[row, row, pl.BlockSpec((T, 512), lambda i: (i, 2))],
                 out_specs=row, out_shape=jax.ShapeDtypeStruct((S, 512), MXU_DTYPE))(h0, h1, z)


def lru_gate_bwd(h0, h1, z, dycat, T=256):
    S = z.shape[0]
    T = min(T, S)

    def body(h0_ref, h1_ref, xg_ref, dy_ref, dxg_ref, dh_ref):
        xg, dy = xg_ref[...], dy_ref[...]
        dxg_ref[...] = dy * (h0_ref[...] + h1_ref[...]) * _gelu_grad(xg)
        dh_ref[...] = dy * _gelu(xg)

    row = pl.BlockSpec((T, 512), lambda i: (i, 0))
    col2 = pl.BlockSpec((T, 512), lambda i: (i, 2))
    sd = jax.ShapeDtypeStruct((S, 512), F32)
    return _call(body, name="lru_gate_bwd", grid=(S // T,), in_specs=[row, row, col2, col2],
                 out_specs=[row, row], out_shape=[sd, sd])(h0, h1, z, dycat)


def lru_bwd_point(xc, h0, h1, g0, g1, w_gate, b_gate, lam, T=256):
    S = xc.shape[0]
    T = min(T, S)
    nt = S // T

    def body(xc_ref, h0p_ref, h0_ref, h0n_ref, h1p_ref, h1_ref, h1n_ref, g0_ref, g1_ref, w_ref, bias_ref, lam_ref,
             dxc_ref, dpre_ref, dbias_ref, dlam_ref, ext0, ext1):
        i = pl.program_id(0)

        @pl.when(i == 0)
        def _():
            dbias_ref[...] = jnp.zeros_like(dbias_ref)
            dlam_ref[...] = jnp.zeros_like(dlam_ref)

        _fill_ext(ext0, h0p_ref, h0_ref, h0n_ref, i, nt, T)
        _fill_ext(ext1, h1p_ref, h1_ref, h1n_ref, i, nt, T)
        xc = xc_ref[...]
        gates = _lru_gates(xc, w_ref, bias_ref, lam_ref)
        hshift = (ext0[pl.ds(HALO - 1, T), :], ext1[pl.ds(HALO + 1, T), :])
        gs = (g0_ref[...], g1_ref[...])
        dxc = jnp.zeros((T, 512), F32)
        for d in range(2):
            r, ig, a, mult = gates[d]
            db = gs[d]
            da = db * hshift[d]
            dmult = db * (ig * xc)
            di = db * (mult * xc)
            dxc = dxc + db * (mult * ig)
            dloga = da * a - dmult * (a * a / mult)
            lam_d = lam_ref[:, d * 512:(d + 1) * 512]
            dr = dloga * (-LRU_C * _softplus(-lam_d))
            dsp = jnp.sum(dloga * (-LRU_C * r), axis=0, keepdims=True)
            dlam_ref[:, d * 512:(d + 1) * 512] += dsp * (-_sigmoid(-lam_d))
            dpre_ref[:, d * 1024:d * 1024 + 512] = (dr * (r * (1.0 - r))).astype(dpre_ref.dtype)
            dpre_ref[:, d * 1024 + 512:(d + 1) * 1024] = (di * (ig * (1.0 - ig))).astype(dpre_ref.dtype)
            dbias_ref[:, d * 1024:d * 1024 + 512] += jnp.sum(dr * (r * (1.0 - r)), axis=0, keepdims=True)
            dbias_ref[:, d * 1024 + 512:(d + 1) * 1024] += jnp.sum(di * (ig * (1.0 - ig)), axis=0, keepdims=True)
        dxc_ref[...] = dxc + _dot_nt(dpre_ref[...], w_ref[...])

    full = lambda a: pl.BlockSpec(a.shape, lambda i: (0, 0))
    row = pl.BlockSpec((T, 512), lambda i: (i, 0))
    return _call(body, name="lru_bwd_point", grid=(nt,),
                 in_specs=[row] + _halo_specs(T, S, 512, 0) + _halo_specs(T, S, 512, 0) + [row, row, full(w_gate), full(b_gate), full(lam)],
                 out_specs=[row, pl.BlockSpec((T, 2048), lambda i: (i, 0)), full(b_gate), full(lam)],
                 out_shape=[jax.ShapeDtypeStruct((S, 512), F32), jax.ShapeDtypeStruct((S, 2048), MXU_DTYPE),
                            jax.ShapeDtypeStruct(b_gate.shape, F32), jax.ShapeDtypeStruct(lam.shape, F32)],
                 scratch=[pltpu.VMEM((T + 2 * HALO, 512), F32)] * 2,
                 )(xc, h0, h0, h0, h1, h1, h1, g0, g1, w_gate, b_gate, lam)


def conv_bwd(z, dxc, conv_w, T=256):
    S = z.shape[0]
    T = min(T, S)
    nt = S // T

    def body(xp_ref, xcur_ref, xn_ref, dp_ref, dcur_ref, dn_ref, cw_ref, dx_ref, dw_ref, db_ref, extx, extd):
        i = pl.program_id(0)

        @pl.when(i == 0)
        def _():
            dw_ref[...] = jnp.zeros_like(dw_ref)
            db_ref[...] = jnp.zeros_like(db_ref)

        _fill_ext(extx, xp_ref, xcur_ref, xn_ref, i, nt, T)
        _fill_ext(extd, dp_ref, dcur_ref, dn_ref, i, nt, T)
        d = extd[pl.ds(HALO, T), :]
        dx = cw_ref[0:1, :] * extd[pl.ds(HALO + 2, T), :]
        for j in range(1, 4):
            dx = dx + cw_ref[j:j + 1, :] * extd[pl.ds(HALO + 2 - j, T), :]
        dx_ref[...] = dx
        for j in range(4):
            dw_ref[j:j + 1, :] += jnp.sum(d * extx[pl.ds(HALO - 2 + j, T), :], axis=0, keepdims=True)
        db_ref[...] += jnp.sum(d, axis=0, keepdims=True)

    full = lambda a: pl.BlockSpec(a.shape, lambda i: (0, 0))
    row = pl.BlockSpec((T, 512), lambda i: (i, 0))
    return _call(body, name="conv_bwd", grid=(nt,),
                 in_specs=_halo_specs(T, S, 512, 1) + _halo_specs(T, S, 512, 0) + [full(conv_w)],
                 out_specs=[row, full(conv_w), pl.BlockSpec((1, 512), lambda i: (0, 0))],
                 out_shape=[jax.ShapeDtypeStruct((S, 512), F32), jax.ShapeDtypeStruct(conv_w.shape, F32),
                            jax.ShapeDtypeStruct((1, 512), F32)],
                 scratch=[pltpu.VMEM((T + 2 * HALO, 512), F32)] * 2)(z, z, z, dxc, dxc, dxc, conv_w)


MESH_ID = pl.DeviceIdType.MESH
ANY = pl.BlockSpec(memory_space=pl.ANY)


def _place():
    x, y, c = lax.axis_index("x"), lax.axis_index("y"), lax.axis_index("c")
    chips = [(1 - x, y), (x, 1 - y), (1 - x, 1 - y)]
    return x, y, c, chips


def _slot(px, py, pc):
    return 4 * px + 2 * py + pc


def all_gather(arrays):
    n = len(arrays)

    def body(*refs):
        ins, outs = refs[:n], refs[n:2 * n]
        send, recv, loc = refs[2 * n:]
        x, y, c, chips = _place()
        me, sibling = (x, y, c), (x, y, 1 - c)

        def copy(a, k, block, to, src=None):
            slot = outs[a].at[_slot(*block)]
            return pltpu.make_async_remote_copy(src_ref=slot if src is None else src, dst_ref=slot,
                                                send_sem=send.at[a * 7 + k], recv_sem=recv.at[a * 7 + k],
                                                device_id=to, device_id_type=MESH_ID)

        local = [pltpu.make_async_copy(ins[a], outs[a].at[_slot(*me)], loc.at[a]) for a in range(n)]
        for cp in local:
            cp.start()
        first = []
        for a in range(n):
            first.append(copy(a, 0, me, sibling, src=ins[a]))
            first += [copy(a, 1 + j, me, (*chip, c), src=ins[a]) for j, chip in enumerate(chips)]
        for cp in first:
            cp.start()
        passed = []
        for a in range(n):
            for j, chip in enumerate(chips):
                copy(a, 1 + j, (*chip, c), me).wait_recv()
                cp = copy(a, 4 + j, (*chip, c), sibling)
                cp.start()
                passed.append(cp)
        for a in range(n):
            copy(a, 0, sibling, me).wait_recv()
            for j, chip in enumerate(chips):
                copy(a, 4 + j, (*chip, 1 - c), me).wait_recv()
        for cp in first + passed:
            cp.wait_send()
        for cp in local:
            cp.wait()

    return pl.pallas_call(
        body, name="all_gather", in_specs=[ANY] * n, out_specs=[ANY] * n,
        out_shape=[jax.ShapeDtypeStruct((NDEV,) + a.shape, a.dtype) for a in arrays],
        scratch_shapes=[pltpu.SemaphoreType.DMA((n * 7,)), pltpu.SemaphoreType.DMA((n * 7,)),
                        pltpu.SemaphoreType.DMA((n,))],
    )(*arrays)


def grad_pair(gs):
    def body(gs_ref, o_ref, send, recv):
        x, y, c, _ = _place()
        cps = [pltpu.make_async_remote_copy(src_ref=gs_ref.at[2 * q + 1], dst_ref=o_ref.at[q], send_sem=send.at[q],
                                            recv_sem=recv.at[q], device_id=(x, y, 1 - c), device_id_type=MESH_ID)
               for q in range(4)]
        for cp in cps:
            cp.start()
        for cp in cps:
            cp.wait_recv()
        for cp in cps:
            cp.wait_send()

    return pl.pallas_call(
        body, name="grad_pair", in_specs=[ANY], out_specs=ANY,
        out_shape=jax.ShapeDtypeStruct((4,) + gs.shape[1:], gs.dtype),
        scratch_shapes=[pltpu.SemaphoreType.DMA((4,)), pltpu.SemaphoreType.DMA((4,))],
    )(gs)


def pair_add(gs, got, T=128):
    _, R, C = gs.shape
    T = min(T, R)

    def body(g0, g1, g2, g3, got_ref, own_ref, out_ref):
        own_ref[...] = g0[0] + got_ref[0]
        for q, g in enumerate((g1, g2, g3)):
            out_ref[q] = (g[0] + got_ref[q + 1]).astype(out_ref.dtype)

    even = lambda q: pl.BlockSpec((1, T, C), lambda i: (2 * q, i, 0))
    return _call(body, name="pair_add", grid=(R // T,),
                 in_specs=[even(0), even(1), even(2), even(3), pl.BlockSpec((4, T, C), lambda i: (0, i, 0))],
                 out_specs=[pl.BlockSpec((T, C), lambda i: (i, 0)), pl.BlockSpec((3, T, C), lambda i: (0, i, 0))],
                 out_shape=[jax.ShapeDtypeStruct((R, C), F32), jax.ShapeDtypeStruct((3, R, C), MXU_DTYPE)],
                 )(gs, gs, gs, gs, got)


def grad_cross(part, gr):
    def body(p_ref, gr_ref, o_ref, or_ref, send, recv, loc):
        x, y, c, chips = _place()
        peers = [(x, y, 1 - c)] + [(*chip, c) for chip in chips] + [(*chip, 1 - c) for chip in chips]
        local = pltpu.make_async_copy(gr_ref, or_ref.at[_slot(x, y, c)], loc.at[0])
        local.start()
        by_offset = [(x, 1 - y), (1 - x, y), (1 - x, 1 - y)]
        cps = [pltpu.make_async_remote_copy(src_ref=p_ref.at[j], dst_ref=o_ref.at[j], send_sem=send.at[j],
                                            recv_sem=recv.at[j], device_id=(*chip, c), device_id_type=MESH_ID)
               for j, chip in enumerate(by_offset)]
        cps += [pltpu.make_async_remote_copy(src_ref=gr_ref, dst_ref=or_ref.at[_slot(x, y, c)], send_sem=send.at[3 + k],
                                             recv_sem=recv.at[3 + k], device_id=peer, device_id_type=MESH_ID)
                for k, peer in enumerate(peers)]
        for cp in cps:
            cp.start()
        for cp in cps[:3]:
            cp.wait_recv()
        for k, peer in enumerate(peers):
            pltpu.make_async_remote_copy(src_ref=gr_ref, dst_ref=or_ref.at[_slot(*peer)], send_sem=send.at[3 + k],
                                         recv_sem=recv.at[3 + k], device_id=peer, device_id_type=MESH_ID).wait_recv()
        for cp in cps:
            cp.wait_send()
        local.wait()

    return pl.pallas_call(
        body, name="grad_cross", in_specs=[ANY, ANY], out_specs=[ANY, ANY],
        out_shape=[jax.ShapeDtypeStruct(part.shape, part.dtype), jax.ShapeDtypeStruct((NDEV,) + gr.shape, gr.dtype)],
        scratch_shapes=[pltpu.SemaphoreType.DMA((10,)), pltpu.SemaphoreType.DMA((10,)), pltpu.SemaphoreType.DMA((1,))],
    )(part, gr)


def adamw(name, gparts, w, m, v, T=128):
    R, C = w.shape
    T = min(T, R)
    ng = len(gparts)

    def body(*refs):
        g_refs = refs[:ng]
        w_ref, m_ref, v_ref, go_ref, d_ref, mo_ref, vo_ref = refs[ng:]
        g = None
        for g_ref in g_refs:
            for k in range(g_ref.shape[0]):
                t = g_ref[k].astype(F32)
                g = t if g is None else g + t
        mn = ADAM_B1 * m_ref[...] + (1.0 - ADAM_B1) * g
        vn = ADAM_B2 * v_ref[...] + (1.0 - ADAM_B2) * (g * g)
        m_hat = mn / (1.0 - ADAM_B1 ** ADAM_STEP)
        v_hat = vn / (1.0 - ADAM_B2 ** ADAM_STEP)
        go_ref[...] = g
        d_ref[...] = -ADAM_LR * (m_hat / (jnp.sqrt(v_hat) + ADAM_EPS) + ADAM_WD * w_ref[...])
        mo_ref[...] = mn
        vo_ref[...] = vn

    row = pl.BlockSpec((T, C), lambda i: (i, 0))
    sd = jax.ShapeDtypeStruct((R, C), F32)
    return _call(body, name=name, grid=(R // T,),
                 in_specs=[pl.BlockSpec((g.shape[0], T, C), lambda i: (0, i, 0)) for g in gparts] + [row, row, row],
                 out_specs=[row] * 4, out_shape=[sd] * 4)(*gparts, w, m, v)


PACK_W = 1024
PACK_TILE = 16
PACK_ROWS = 128


def _entry_rows(shape):
    n = 1
    for s in shape:
        n *= s
    return n, -(-n // (PACK_TILE * PACK_W)) * PACK_TILE


def _pack(arrays, dtype):
    mats, rows = [], 0
    for a in arrays:
        n, r = _entry_rows(a.shape)
        flat = a.astype(dtype).reshape(-1)
        if r * PACK_W != n:
            flat = jnp.pad(flat, (0, r * PACK_W - n))
        mats.append(flat.reshape(r, PACK_W))
        rows += r
    total = -(-rows // PACK_ROWS) * PACK_ROWS
    if total > rows:
        mats.append(jnp.zeros((total - rows, PACK_W), dtype))
    return jnp.concatenate(mats, axis=0)


def _unpack(buf, shapes, lead=()):
    out, off = [], 0
    for sh in shapes:
        n, r = _entry_rows(sh)
        piece = buf[..., off:off + r, :].reshape(lead + (r * PACK_W,))
        out.append(piece[..., :n].reshape(lead + tuple(sh)))
        off += r
    return out


def _unshard(parts, axis):
    return jnp.concatenate([parts[d] for d in range(NDEV)], axis=axis)


def _slab_cols(w, head, used, n_heads, slab=LANES):
    lead = w.shape[:-1]
    w = w.reshape(lead + (n_heads, head))[..., :used]
    w = jnp.pad(w, [(0, 0)] * len(lead) + [(0, 0), (0, slab - used)])
    return w.reshape(lead + (n_heads * slab,))


def _unslab_cols(g, used, n_heads, slab=LANES):
    lead = g.shape[:-1]
    return g.reshape(lead + (n_heads, slab))[..., :used].reshape(lead + (n_heads * used,))


def _block_diag(w):
    eye = jnp.eye(8, dtype=w.dtype)
    return jnp.einsum("nij,nm->nimj", w, eye).reshape(512, 512)


def _block_diag_t(g):
    g = g.reshape(8, 64, 8, 64)
    return jnp.stack([g[n, :, n, :] for n in range(8)])


BIG = (("e_w_in", 2), ("e_w_out", 1), ("o_w_in", 2), ("o_w_uq", 2), ("o_w_ukv", 2), ("o_w_out", 1),
       ("w_mlp1", 2), ("w_mlp2", 1))
SMALL = (("o_norm_mix", 1), ("o_g_cq", 1), ("o_conv_w", 2), ("o_conv_b", 1), ("o_lru_ba", 2), ("o_lru_bx", 2),
         ("o_lru_lambda", 2))
REPL = ("e_norm_mix", "e_sink", "e_w_pool", "e_pool_scale", "o_g_ckv", "o_lru_wa", "o_lru_wx", "norm_mlp",
        "final_norm")
WEIGHTS = ("e_norm_mix", "e_w_in", "e_sink", "e_w_pool", "e_pool_scale", "e_w_out", "o_norm_mix", "o_w_in", "o_g_cq",
           "o_w_uq", "o_g_ckv", "o_w_ukv", "o_conv_w", "o_conv_b", "o_lru_wa", "o_lru_ba", "o_lru_wx", "o_lru_bx",
           "o_lru_lambda", "o_w_out", "norm_mlp", "w_mlp1", "w_mlp2", "final_norm")


def _mlp_fwd(l, x, g, w1, w2):
    u1, hn = mm_nn(f"mlp1_{l}", [x], [w1], norm_g=g, emit_norm=True, tm=512, tn=1024)
    x2 = mm_nn(f"mlp2_{l}", [u1], [w2], act="relu2", res=x, tm=256, tn=1024)
    return x2, (x, u1, hn)


def _mlp_bwd(l, saved, g, w1, w2, dx2):
    x, u1, hn = saved
    du1 = mm_nt(f"mlp2_dx_{l}", [dx2], [w2], relu2_of=u1, out_dtype=MXU_DTYPE, tm=512, tn=1024)
    dw2 = mm_tn(f"mlp2_dw_{l}", u1, dx2, act="relu2", tn=1024)
    dhn = mm_nt(f"mlp1_dx_{l}", [du1], [w1], tm=512, tn=1024)
    dw1 = mm_tn(f"mlp1_dw_{l}", hn, du1, tn=1024)
    dx, dg = rms_bwd(f"mlp_norm_bwd_{l}", x, g, dhn, dx2)
    return dx, dg, dw1, dw2


def kernel(x, e_norm_mix, e_w_in, e_sink, e_w_pool, e_pool_scale, e_w_out, o_norm_mix, o_w_in, o_g_cq, o_w_uq, o_g_ckv, o_w_ukv, o_conv_w, o_conv_b, o_lru_wa, o_lru_ba, o_lru_wx, o_lru_bx, o_lru_lambda, o_w_out, norm_mlp, w_mlp1, w_mlp2, final_norm, loss_target, m_e_norm_mix, m_e_w_in, m_e_sink, m_e_w_pool, m_e_pool_scale, m_e_w_out, m_o_norm_mix, m_o_w_in, m_o_g_cq, m_o_w_uq, m_o_g_ckv, m_o_w_ukv, m_o_conv_w, m_o_conv_b, m_o_lru_wa, m_o_lru_ba, m_o_lru_wx, m_o_lru_bx, m_o_lru_lambda, m_o_w_out, m_norm_mlp, m_w_mlp1, m_w_mlp2, m_final_norm, v_e_norm_mix, v_e_w_in, v_e_sink, v_e_w_pool, v_e_pool_scale, v_e_w_out, v_o_norm_mix, v_o_w_in, v_o_g_cq, v_o_w_uq, v_o_g_ckv, v_o_w_ukv, v_o_conv_w, v_o_conv_b, v_o_lru_wa, v_o_lru_ba, v_o_lru_wx, v_o_lru_bx, v_o_lru_lambda, v_o_w_out, v_norm_mlp, v_w_mlp1, v_w_mlp2, v_final_norm):
    W = dict(e_norm_mix=e_norm_mix, e_w_in=e_w_in, e_sink=e_sink, e_w_pool=e_w_pool, e_pool_scale=e_pool_scale, e_w_out=e_w_out, o_norm_mix=o_norm_mix, o_w_in=o_w_in, o_g_cq=o_g_cq, o_w_uq=o_w_uq, o_g_ckv=o_g_ckv, o_w_ukv=o_w_ukv, o_conv_w=o_conv_w, o_conv_b=o_conv_b, o_lru_wa=o_lru_wa, o_lru_ba=o_lru_ba, o_lru_wx=o_lru_wx, o_lru_bx=o_lru_bx, o_lru_lambda=o_lru_lambda, o_w_out=o_w_out, norm_mlp=norm_mlp, w_mlp1=w_mlp1, w_mlp2=w_mlp2, final_norm=final_norm)
    Mo = dict(e_norm_mix=m_e_norm_mix, e_w_in=m_e_w_in, e_sink=m_e_sink, e_w_pool=m_e_w_pool, e_pool_scale=m_e_pool_scale, e_w_out=m_e_w_out, o_norm_mix=m_o_norm_mix, o_w_in=m_o_w_in, o_g_cq=m_o_g_cq, o_w_uq=m_o_w_uq, o_g_ckv=m_o_g_ckv, o_w_ukv=m_o_w_ukv, o_conv_w=m_o_conv_w, o_conv_b=m_o_conv_b, o_lru_wa=m_o_lru_wa, o_lru_ba=m_o_lru_ba, o_lru_wx=m_o_lru_wx, o_lru_bx=m_o_lru_bx, o_lru_lambda=m_o_lru_lambda, o_w_out=m_o_w_out, norm_mlp=m_norm_mlp, w_mlp1=m_w_mlp1, w_mlp2=m_w_mlp2, final_norm=m_final_norm)
    Vo = dict(e_norm_mix=v_e_norm_mix, e_w_in=v_e_w_in, e_sink=v_e_sink, e_w_pool=v_e_w_pool, e_pool_scale=v_e_pool_scale, e_w_out=v_e_w_out, o_norm_mix=v_o_norm_mix, o_w_in=v_o_w_in, o_g_cq=v_o_g_cq, o_w_uq=v_o_w_uq, o_g_ckv=v_o_g_ckv, o_w_ukv=v_o_w_ukv, o_conv_w=v_o_conv_w, o_conv_b=v_o_conv_b, o_lru_wa=v_o_lru_wa, o_lru_ba=v_o_lru_ba, o_lru_wx=v_o_lru_wx, o_lru_bx=v_o_lru_bx, o_lru_lambda=v_o_lru_lambda, o_w_out=v_o_w_out, norm_mlp=v_norm_mlp, w_mlp1=v_w_mlp1, w_mlp2=v_w_mlp2, final_norm=v_final_norm)

    S = x.shape[1]
    x0 = x[0]
    target = loss_target[0]

    big_g, small_g = all_gather([_pack([W[n] for n, _ in BIG], MXU_DTYPE), _pack([W[n] for n, _ in SMALL], F32)])
    full = {}
    for (n, ax), parts in zip(BIG, _unpack(big_g, [W[n].shape for n, _ in BIG], (NDEV,))):
        full[n] = _unshard(parts, ax)
    for (n, ax), parts in zip(SMALL, _unpack(small_g, [W[n].shape for n, _ in SMALL], (NDEV,))):
        full[n] = _unshard(parts, ax)

    def even_in(w):
        return jnp.concatenate([_slab_cols(w[:, 0:512], 64, 64, 8), _slab_cols(w[:, 512:640], 64, 64, 2),
                                _slab_cols(w[:, 640:768], 64, 64, 2), w[:, 768:1280]], axis=1)

    def even_in_t(g):
        return jnp.concatenate([_unslab_cols(g[:, 0:1024], 64, 8), _unslab_cols(g[:, 1024:1280], 64, 2),
                                _unslab_cols(g[:, 1280:1536], 64, 2), g[:, 1536:2048]], axis=1)

    def slab_rows(w, n_heads):
        return _slab_cols(w.T, 64, 64, n_heads).T

    def unslab_rows(g, n_heads):
        return _unslab_cols(g.T, 64, n_heads).T

    def odd_in(w):
        return jnp.concatenate([w[:, 0:384], jnp.pad(w[:, 384:416], ((0, 0), (0, 96))), w[:, 416:1440]], axis=1)

    def odd_in_t(g):
        return jnp.concatenate([g[:, 0:416], g[:, 512:1536]], axis=1)

    def uq(w):
        return _slab_cols(w, 96, 96, 8)

    def ukv(w):
        w = w.reshape(128, 8, 128)
        pad = lambda t: jnp.pad(t, ((0, 0), (0, 0), (0, 64))).reshape(128, 1024)
        return pad(w[:, :, :64]), pad(w[:, :, 64:])

    def ukv_t(gk, gv):
        gk = gk.reshape(128, 8, 128)[:, :, :64]
        gv = gv.reshape(128, 8, 128)[:, :, :64]
        return jnp.concatenate([gk, gv], axis=2).reshape(128, 1024)

    tabs_swa = rope_tables(S, 0, SWA_HALF)
    tabs_mq = rope_tables(S, 64, MLA_HALF)
    tabs_mk = rope_tables(S, 0, MLA_HALF)
    row = lambda v: v.reshape(1, -1)

    saved = []
    xcur = x0
    for l in range(4):
        j = l // 2
        if l % 2 == 0:
            w_in = even_in(full["e_w_in"][j])
            w_out = full["e_w_out"][j]
            w_out_a, w_out_b = slab_rows(w_out[0:512], 8), w_out[512:1024]
            w_pool = W["e_w_pool"][j].astype(MXU_DTYPE)
            z, h = mm_nn(f"e_in_{j}", [xcur], [w_in], norm_g=row(W["e_norm_mix"][j]), emit_norm=True, tm=256, tn=1024)
            qkv = swa_prep(z, tabs_swa)
            ya, lse = swa_fwd(qkv, W["e_sink"][j])
            yb = pool_fwd(z, w_pool, row(W["e_pool_scale"][j]))
            x1 = mm_nn(f"e_out_{j}", [ya, yb], [w_out_a, w_out_b], res=xcur)
            mix = (xcur, z, h, qkv, lse, ya, yb, w_in, w_out_a, w_out_b, w_pool)
        else:
            w_in = odd_in(full["o_w_in"][j])
            w_out = full["o_w_out"][j]
            w_out_a, w_out_b = slab_rows(w_out[0:512], 8), w_out[512:1024]
            w_uq = uq(full["o_w_uq"][j])
            w_k, w_v = ukv(full["o_w_ukv"][j])
            g_cq, g_ckv = row(full["o_g_cq"][j]), row(W["o_g_ckv"][j])
            w_gate = jnp.concatenate([_block_diag(W["o_lru_wa"][j, 0]), _block_diag(W["o_lru_wx"][j, 0]),
                                      _block_diag(W["o_lru_wa"][j, 1]), _block_diag(W["o_lru_wx"][j, 1])], axis=1).astype(MXU_DTYPE)
            b_gate = jnp.concatenate([full["o_lru_ba"][j, 0], full["o_lru_bx"][j, 0], full["o_lru_ba"][j, 1],
                                      full["o_lru_bx"][j, 1]]).reshape(1, 2048)
            lam = full["o_lru_lambda"][j].reshape(1, 1024)
            conv_w, conv_b = full["o_conv_w"][j], row(full["o_conv_b"][j])
            z, h = mm_nn(f"o_in_{j}", [xcur], [w_in], norm_g=row(full["o_norm_mix"][j]), emit_norm=True, tm=256, tn=512)
            q, k, v, nq, nkv = mla_prep(z, g_cq, g_ckv, w_uq, w_k, w_v, tabs_mq, tabs_mk)
            yc, lse = mla_fwd(q, k, v)
            xc, a0, b0, a1, b1 = lru_pre(z, conv_w, conv_b, w_gate, b_gate, lam)
            h0, h1 = lru_scan(f"lru_scan_fwd_{j}", a0, b0, a1, b1, adjoint=False)
            yd = lru_gate(h0, h1, z)
            x1 = mm_nn(f"o_out_{j}", [yc, yd], [w_out_a, w_out_b], res=xcur)
            mix = (xcur, z, h, q, k, v, nq, nkv, yc, lse, xc, a0, a1, h0, h1, yd, w_in, w_out_a, w_out_b, w_uq, w_k, w_v,
                   g_cq, g_ckv, w_gate, b_gate, lam, conv_w)
        xcur, mlp = _mlp_fwd(l, x1, row(W["norm_mlp"][l]), full["w_mlp1"][l], full["w_mlp2"][l])
        saved.append((mix, mlp))

    loss_row, dx, dg_final = loss_head(xcur, row(W["final_norm"]), target)

    G = {n: [None] * W[n].shape[0] for n in WEIGHTS if n != "final_norm"}
    G["final_norm"] = dg_final.reshape(-1)
    for l in reversed(range(4)):
        j = l // 2
        mix, mlp = saved[l]
        dx, dg, dw1, dw2 = _mlp_bwd(l, mlp, row(W["norm_mlp"][l]), full["w_mlp1"][l], full["w_mlp2"][l], dx)
        G["norm_mlp"][l], G["w_mlp1"][l], G["w_mlp2"][l] = dg.reshape(-1), dw1, dw2
        if l % 2 == 0:
            xin, z, h, qkv, lse, ya, yb, w_in, w_out_a, w_out_b, w_pool = mix
            w_out_cat = jnp.concatenate([w_out_a, w_out_b], axis=0)
            dycat = mm_nt(f"e_out_dx_{j}", [dx], [w_out_cat])
            G["e_w_out"][j] = jnp.concatenate([unslab_rows(mm_tn(f"e_out_dwa_{j}", ya, dx), 8),
                                               mm_tn(f"e_out_dwb_{j}", yb, dx)], axis=0)
            dq, dk, dv, dsink = swa_bwd(qkv, W["e_sink"][j], lse, dycat, tabs_swa)
            du, dwp, dsc = pool_bwd(z, dycat, w_pool, row(W["e_pool_scale"][j]))
            G["e_sink"][j], G["e_w_pool"][j], G["e_pool_scale"][j] = dsink[0, 0:8], dwp, dsc.reshape(-1)
            dh = mm_nt(f"e_in_dx_{j}", [dq, dk, dv, du], [(w_in, 0, 1024), (w_in, 4, 256), (w_in, 5, 256), (w_in, 3, 512)])
            G["e_w_in"][j] = even_in_t(jnp.concatenate(
                [mm_tn(f"e_in_dwq_{j}", h, dq), mm_tn(f"e_in_dwk_{j}", h, dk), mm_tn(f"e_in_dwv_{j}", h, dv),
                 mm_tn(f"e_in_dwu_{j}", h, du)], axis=1))
            dx, dg = rms_bwd(f"e_norm_bwd_{j}", xin, row(W["e_norm_mix"][j]), dh, dx)
            G["e_norm_mix"][j] = dg.reshape(-1)
        else:
            (xin, z, h, q, k, v, nq, nkv, yc, lse, xc, a0, a1, h0, h1, yd, w_in, w_out_a, w_out_b, w_uq, w_k, w_v,
             g_cq, g_ckv, w_gate, b_gate, lam, conv_w) = mix
            w_out_cat = jnp.concatenate([w_out_a, w_out_b], axis=0)
            dycat = mm_nt(f"o_out_dx_{j}", [dx], [w_out_cat])
            G["o_w_out"][j] = jnp.concatenate([unslab_rows(mm_tn(f"o_out_dwa_{j}", yc, dx), 8),
                                               mm_tn(f"o_out_dwb_{j}", yd, dx)], axis=0)
            delta, dob = mla_delta(yc, dycat)
            dq, dk, dv = mla_bwd(q, k, v, dob, lse, delta)
            dza, dqp, dgq, dgkv = mla_prep_bwd(z, g_cq, g_ckv, w_uq, w_k, w_v, dq, dk, dv, tabs_mq, tabs_mk)
            G["o_g_cq"][j], G["o_g_ckv"][j] = dgq.reshape(-1), dgkv.reshape(-1)
            G["o_w_uq"][j] = _unslab_cols(mm_tn(f"o_uq_dw_{j}", nq, dqp), 96, 8)
            G["o_w_ukv"][j] = ukv_t(mm_tn(f"o_uk_dw_{j}", nkv, dk), mm_tn(f"o_uv_dw_{j}", nkv, dv))
            dxg, dhh = lru_gate_bwd(h0, h1, z, dycat)
            g1, g0 = lru_scan(f"lru_scan_bwd_{j}", a1, dhh, a0, dhh, adjoint=True)
            dxc, dpre, dbias, dlam = lru_bwd_point(xc, h0, h1, g0, g1, w_gate, b_gate, lam)
            dwg = mm_tn(f"o_gate_dw_{j}", xc, dpre)
            G["o_lru_wa"][j] = jnp.stack([_block_diag_t(dwg[:, 0:512]), _block_diag_t(dwg[:, 1024:1536])])
            G["o_lru_wx"][j] = jnp.stack([_block_diag_t(dwg[:, 512:1024]), _block_diag_t(dwg[:, 1536:2048])])
            G["o_lru_ba"][j] = jnp.stack([dbias[0, 0:512], dbias[0, 1024:1536]])
            G["o_lru_bx"][j] = jnp.stack([dbias[0, 512:1024], dbias[0, 1536:2048]])
            G["o_lru_lambda"][j] = dlam.reshape(2, 512)
            dxr, dcw, dcb = conv_bwd(z, dxc, conv_w)
            G["o_conv_w"][j], G["o_conv_b"][j] = dcw, dcb.reshape(-1)
            dh = mm_nt(f"o_in_dx_{j}", [dza, dxr, dxg], [(w_in, 0, 512), (w_in, 1, 512), (w_in, 2, 512)])
            G["o_w_in"][j] = odd_in_t(jnp.concatenate(
                [mm_tn(f"o_in_dwa_{j}", h, dza), mm_tn(f"o_in_dwr_{j}", h, dxr), mm_tn(f"o_in_dwg_{j}", h, dxg)], axis=1))
            dx, dg = rms_bwd(f"o_norm_bwd_{j}", xin, row(full["o_norm_mix"][j]), dh, dx)
            G["o_norm_mix"][j] = dg.reshape(-1)
    grad_x = dx[None]
    G = {n: (g if n == "final_norm" else jnp.stack(g)) for n, g in G.items()}

    sharded = BIG + SMALL
    me = 4 * lax.axis_index("x") + 2 * lax.axis_index("y") + lax.axis_index("c")

    def shard_of(n, ax, dev):
        size = W[n].shape[ax]
        return lax.dynamic_slice_in_dim(G[n], dev * size, size, axis=ax)

    gs = jnp.stack([_pack([shard_of(n, ax, me ^ r) for n, ax in sharded], F32) for r in range(NDEV)])
    gr = _pack([G[n] for n in REPL] + [loss_row[0, 0:1]], F32)
    mine, others = pair_add(gs, grad_pair(gs))
    others, gr_all = grad_cross(others, gr)
    outs_s = adamw("adamw_sharded", [mine[None], others], *[_pack([T[n] for n, _ in sharded], F32) for T in (W, Mo, Vo)])
    outs_r = adamw("adamw_replicated", [gr_all],
                   *[_pack([T[n] for n in REPL] + [jnp.zeros((1,), F32)], F32) for T in (W, Mo, Vo)])
    res = [dict(), dict(), dict(), dict()]
    for kind in range(4):
        for (n, _), a in zip(sharded, _unpack(outs_s[kind], [W[n].shape for n, _ in sharded])):
            res[kind][n] = a
        for n, a in zip(REPL + ("loss",), _unpack(outs_r[kind], [W[n].shape for n in REPL] + [(1,)])):
            res[kind][n] = a
    loss = res[0]["loss"][0]
    return (loss, grad_x, *[res[0][n] for n in WEIGHTS], *[res[1][n] for n in WEIGHTS],
            *[res[2][n] for n in WEIGHTS], *[res[3][n] for n in WEIGHTS])
```

```python
import functools

import jax
import jax.numpy as jnp
from jax import lax
from jax.experimental import pallas as pl
from jax.experimental.pallas import tpu as pltpu

F32 = jnp.float32
MXU_DTYPE = jnp.bfloat16
EPS = 1e-6
ROPE_THETA = 10000.0
NDEV = 8
LANES = 128
VMEM_LIMIT = 48 * 1024 * 1024

D_MODEL = 1024
D_FF = 4096
LRU_C = 8.0
POOL_WINDOWS = (2, 4, 8, 16)
HALO = 8
MLA_SCALE = 96.0 ** -0.5

ADAM_LR, ADAM_B1, ADAM_B2, ADAM_EPS, ADAM_WD, ADAM_STEP = 0.001, 0.9, 0.999, 1e-08, 0.01, 10


def _mx(v):
    return v.astype(MXU_DTYPE)


def _call(body, *, name, grid, in_specs, out_specs, out_shape, scratch=()):
    return pl.pallas_call(
        body, name=name, grid=grid, in_specs=in_specs, out_specs=out_specs, out_shape=out_shape,
        scratch_shapes=list(scratch),
        compiler_params=pltpu.CompilerParams(
            dimension_semantics=("arbitrary",) * len(grid), vmem_limit_bytes=VMEM_LIMIT),
    )


def _dot(a, b):
    return jnp.dot(a, b, preferred_element_type=F32)


def _dot_nt(a, b):
    return lax.dot_general(a, b, (((1,), (1,)), ((), ())), preferred_element_type=F32)


def _dot_tn(a, b):
    return lax.dot_general(a, b, (((0,), (0,)), ((), ())), preferred_element_type=F32)


def _rms(x, g):
    r = lax.rsqrt(jnp.mean(x * x, axis=-1, keepdims=True) + EPS)
    return (x * r) * g


def _rms_bwd(x, g, dy):
    r = lax.rsqrt(jnp.mean(x * x, axis=-1, keepdims=True) + EPS)
    xh = x * r
    dyg = dy * g
    dx = r * (dyg - xh * jnp.mean(dyg * xh, axis=-1, keepdims=True))
    return dx, jnp.sum(dy * xh, axis=0, keepdims=True)


def _sigmoid(x):
    return 1.0 / (1.0 + jnp.exp(-x))


def _log1p(e):
    u = 1.0 + e
    d = u - 1.0
    return jnp.where(d == 0.0, e, jnp.log(u) * (e / jnp.where(d == 0.0, 1.0, d)))


def _softplus(x):
    return jnp.maximum(x, 0.0) + _log1p(jnp.exp(-jnp.abs(x)))


def _expm1(x):
    u = jnp.exp(x)
    lu = jnp.log(u)
    safe = jnp.where((lu == 0.0) | (u == 0.0), 1.0, lu)
    return jnp.where(u == 1.0, x, jnp.where(u == 0.0, -1.0, (u - 1.0) * x / safe))


_GELU_K = 0.7978845608028654


def _gelu(x):
    return 0.5 * x * (1.0 + jnp.tanh(_GELU_K * (x + 0.044715 * x * x * x)))


def _gelu_grad(x):
    t = jnp.tanh(_GELU_K * (x + 0.044715 * x * x * x))
    return 0.5 * (1.0 + t) + 0.5 * x * (1.0 - t * t) * _GELU_K * (1.0 + 3.0 * 0.044715 * x * x)


def _rope(x, c, sa, sb, half):
    return x * c + pltpu.roll(x, LANES - half, 1) * sa + pltpu.roll(x, half, 1) * sb


def _rope_t(d, c, sa, sb, half):
    return d * c - pltpu.roll(d, LANES - half, 1) * sa - pltpu.roll(d, half, 1) * sb


def _as_cols(a):
    return a if isinstance(a, tuple) else (a, 0, a.shape[1])


def mm_nn(name, a_list, b_list, *, res=None, act=None, norm_g=None, emit_norm=False,
          out_dtype=F32, tm=256, tn=512):
    a_list = [_as_cols(a) for a in a_list]
    M, N = a_list[0][0].shape[0], b_list[0].shape[1]
    tm, tn = min(tm, M), min(tn, N)
    na = len(a_list)

    def body(*refs):
        a_refs, b_refs = refs[:na], refs[na:2 * na]
        k = 2 * na
        g_ref = res_ref = hn_ref = None
        if norm_g is not None:
            g_ref = refs[k]
            k += 1
        if res is not None:
            res_ref = refs[k]
            k += 1
        o_ref = refs[k]
        if emit_norm:
            hn_ref = refs[k + 1]
        acc = None
        for a_ref, b_ref in zip(a_refs, b_refs):
            a = a_ref[...]
            if g_ref is not None:
                a = _rms(a.astype(F32), g_ref[...])
                if hn_ref is not None:
                    hn_ref[...] = a.astype(hn_ref.dtype)
            if act == "relu2":
                a = jnp.maximum(a, 0.0)
                a = a * a
            d = _dot(_mx(a), _mx(b_ref[...]))
            acc = d if acc is None else acc + d
        if res_ref is not None:
            acc = acc + res_ref[...]
        o_ref[...] = acc.astype(o_ref.dtype)

    in_specs = [pl.BlockSpec((tm, w), functools.partial(lambda i, j, cb: (i, cb), cb=cb)) for (_, cb, w) in a_list]
    in_specs += [pl.BlockSpec((b.shape[0], tn), lambda i, j: (0, j)) for b in b_list]
    args = [a for (a, _, _) in a_list] + list(b_list)
    if norm_g is not None:
        in_specs.append(pl.BlockSpec((1, norm_g.shape[1]), lambda i, j: (0, 0)))
        args.append(norm_g)
    if res is not None:
        in_specs.append(pl.BlockSpec((tm, tn), lambda i, j: (i, j)))
        args.append(res)
    out_specs = [pl.BlockSpec((tm, tn), lambda i, j: (i, j))]
    out_shape = [jax.ShapeDtypeStruct((M, N), out_dtype)]
    if emit_norm:
        K = a_list[0][2]
        out_specs.append(pl.BlockSpec((tm, K), lambda i, j: (i, 0)))
        out_shape.append(jax.ShapeDtypeStruct((M, K), MXU_DTYPE))
    out = _call(body, name=name, grid=(M // tm, N // tn), in_specs=in_specs, out_specs=out_specs,
                out_shape=out_shape)(*args)
    return out if emit_norm else out[0]


def mm_nt(name, a_list, b_list, *, relu2_of=None, out_dtype=F32, tm=256, tn=512):
    a_list = [_as_cols(a) for a in a_list]
    b_list = [_as_cols(b) for b in b_list]
    M, N = a_list[0][0].shape[0], b_list[0][0].shape[0]
    tm, tn = min(tm, M), min(tn, N)
    na = len(a_list)

    def body(*refs):
        a_refs, b_refs = refs[:na], refs[na:2 * na]
        u_ref = refs[2 * na] if relu2_of is not None else None
        o_ref = refs[-1]
        acc = None
        for a_ref, b_ref in zip(a_refs, b_refs):
            d = _dot_nt(_mx(a_ref[...]), _mx(b_ref[...]))
            acc = d if acc is None else acc + d
        if u_ref is not None:
            acc = acc * (2.0 * jnp.maximum(u_ref[...], 0.0))
        o_ref[...] = acc.astype(o_ref.dtype)

    in_specs = [pl.BlockSpec((tm, w), functools.partial(lambda i, j, cb: (i, cb), cb=cb)) for (_, cb, w) in a_list]
    in_specs += [pl.BlockSpec((tn, w), functools.partial(lambda i, j, cb: (j, cb), cb=cb)) for (_, cb, w) in b_list]
    args = [a for (a, _, _) in a_list] + [b for (b, _, _) in b_list]
    if relu2_of is not None:
        in_specs.append(pl.BlockSpec((tm, tn), lambda i, j: (i, j)))
        args.append(relu2_of)
    return _call(body, name=name, grid=(M // tm, N // tn), in_specs=in_specs,
                 out_specs=pl.BlockSpec((tm, tn), lambda i, j: (i, j)),
                 out_shape=jax.ShapeDtypeStruct((M, N), out_dtype))(*args)


def mm_tn(name, a, b, *, act=None, tm=512, tn=512, tk=512):
    a, acb, Ma = _as_cols(a)
    b, bcb, Nb = _as_cols(b)
    S = a.shape[0]
    tm, tn, tk = min(tm, Ma), min(tn, Nb), min(tk, S)
    a0, b0 = acb * (Ma // tm), bcb * (Nb // tn)

    def body(a_ref, b_ref, o_ref):
        @pl.when(pl.program_id(2) == 0)
        def _():
            o_ref[...] = jnp.zeros_like(o_ref)

        av = a_ref[...]
        if act == "relu2":
            av = jnp.maximum(av, 0.0)
            av = av * av
        o_ref[...] += _dot_tn(_mx(av), _mx(b_ref[...]))

    return _call(body, name=name, grid=(Ma // tm, Nb // tn, S // tk),
                 in_specs=[pl.BlockSpec((tk, tm), lambda i, j, k: (k, a0 + i)),
                           pl.BlockSpec((tk, tn), lambda i, j, k: (k, b0 + j))],
                 out_specs=pl.BlockSpec((tm, tn), lambda i, j, k: (i, j)),
                 out_shape=jax.ShapeDtypeStruct((Ma, Nb), F32))(a, b)


def rms_bwd(name, x, g, dh, dres, T=256):
    S, D = x.shape
    T = min(T, S)

    def body(x_ref, g_ref, dh_ref, dres_ref, dx_ref, dg_ref):
        @pl.when(pl.program_id(0) == 0)
        def _():
            dg_ref[...] = jnp.zeros_like(dg_ref)

        dx, dg = _rms_bwd(x_ref[...], g_ref[...], dh_ref[...])
        dx_ref[...] = dres_ref[...] + dx
        dg_ref[...] += dg

    row = pl.BlockSpec((T, D), lambda i: (i, 0))
    vec = pl.BlockSpec((1, D), lambda i: (0, 0))
    return _call(body, name=name, grid=(S // T,), in_specs=[row, vec, row, row], out_specs=[row, vec],
                 out_shape=[jax.ShapeDtypeStruct((S, D), F32), jax.ShapeDtypeStruct((1, D), F32)])(x, g, dh, dres)


def loss_head(x, g, target, T=256):
    S, D = x.shape
    T = min(T, S)

    def body(x_ref, g_ref, t_ref, loss_ref, dx_ref, dg_ref):
        @pl.when(pl.program_id(0) == 0)
        def _():
            dg_ref[...] = jnp.zeros_like(dg_ref)
            loss_ref[...] = jnp.zeros_like(loss_ref)

        x = x_ref[...]
        err = _rms(x, g_ref[...]) - t_ref[...]
        loss_ref[...] += 0.5 * jnp.sum(jnp.sum(err * err, axis=-1, keepdims=True) / D, axis=0, keepdims=True)
        dx, dg = _rms_bwd(x, g_ref[...], err / D)
        dx_ref[...] = dx
        dg_ref[...] += dg

    row = pl.BlockSpec((T, D), lambda i: (i, 0))
    vec = pl.BlockSpec((1, D), lambda i: (0, 0))
    return _call(body, name="loss_head", grid=(S // T,), in_specs=[row, vec, row],
                 out_specs=[pl.BlockSpec((1, LANES), lambda i: (0, 0)), row, vec],
                 out_shape=[jax.ShapeDtypeStruct((1, LANES), F32), jax.ShapeDtypeStruct((S, D), F32),
                            jax.ShapeDtypeStruct((1, D), F32)])(x, g, target)


def rope_tables(S, lo, half):
    inv = ROPE_THETA ** (-jnp.arange(half, dtype=F32) / half)
    ang = jnp.arange(S, dtype=F32)[:, None] * inv[None, :]
    cos, sin = jnp.cos(ang), jnp.sin(ang)
    one = lambda n: jnp.ones((S, n), F32)
    zero = lambda n: jnp.zeros((S, n), F32)
    hi = LANES - lo - 2 * half
    c = jnp.concatenate([one(lo), cos, cos, one(hi)], axis=1)
    sa = jnp.concatenate([zero(lo), -sin, zero(half), zero(hi)], axis=1)
    sb = jnp.concatenate([zero(lo), zero(half), sin, zero(hi)], axis=1)
    return c, sa, sb


SWA_BLOCK = 128
SWA_HALF = 32


def swa_prep(z, tabs, T=256):
    S = z.shape[0]
    T = min(T, S)

    def body(z_ref, c_ref, sa_ref, sb_ref, o_ref):
        c, sa, sb = c_ref[...], sa_ref[...], sb_ref[...]
        for s in range(10):
            sl = slice(s * LANES, (s + 1) * LANES)
            y = _rope(z_ref[:, sl], c, sa, sb, SWA_HALF)
            if s < 8:
                y = y * 0.125
            o_ref[:, sl] = y.astype(o_ref.dtype)
        o_ref[:, 1280:1536] = z_ref[:, 1280:1536].astype(o_ref.dtype)

    tab = pl.BlockSpec((T, LANES), lambda i: (i, 0))
    return _call(body, name="swa_prep", grid=(S // T,),
                 in_specs=[pl.BlockSpec((T, 1536), lambda i: (i, 0)), tab, tab, tab],
                 out_specs=pl.BlockSpec((T, 1536), lambda i: (i, 0)),
                 out_shape=jax.ShapeDtypeStruct((S, 1536), MXU_DTYPE))(z, *tabs)


def _swa_valid(n, S):
    B = SWA_BLOCK
    i = lax.broadcasted_iota(jnp.int32, (B, 3 * B), 0)
    j = lax.broadcasted_iota(jnp.int32, (B, 3 * B), 1)
    kpos = j + (n - 1) * B
    return (jnp.abs(j - B - i) <= B) & (kpos >= 0) & (kpos < S)


def swa_fwd(qkv, sink):
    S = qkv.shape[0]
    B = SWA_BLOCK
    nb = S // B

    def body(sink_ref, q_ref, kp_ref, kc_ref, kn_ref, vp_ref, vc_ref, vn_ref, o_ref, st_ref):
        n = pl.program_id(0)
        valid = _swa_valid(n, S)
        lane = lax.broadcasted_iota(jnp.int32, (B, LANES), 1)
        st = jnp.zeros((B, LANES), F32)
        for hk in range(2):
            sl = slice(hk * LANES, (hk + 1) * LANES)
            k3 = jnp.concatenate([kp_ref[:, sl], kc_ref[:, sl], kn_ref[:, sl]], axis=0)
            v3 = jnp.concatenate([vp_ref[:, sl], vc_ref[:, sl], vn_ref[:, sl]], axis=0)
            for g in range(4):
                h = hk * 4 + g
                hs = slice(h * LANES, (h + 1) * LANES)
                s = jnp.where(valid, _dot_nt(q_ref[:, hs], k3), -jnp.inf)
                sk = sink_ref[h]
                m = jnp.maximum(jnp.max(s, axis=-1, keepdims=True), sk)
                p = jnp.exp(s - m)
                den = jnp.sum(p, axis=-1, keepdims=True) + jnp.exp(sk - m)
                p = p / den
                o_ref[:, hs] = _dot(_mx(p), v3).astype(o_ref.dtype)
                st = jnp.where(lane == h, m + jnp.log(den), st)
        st_ref[...] = st

    kv = lambda cb, d: pl.BlockSpec((B, 2 * LANES), lambda n: (jnp.clip(n + d, 0, nb - 1), cb))
    return _call(body, name="swa_fwd", grid=(nb,),
                 in_specs=[pl.BlockSpec(memory_space=pltpu.SMEM),
                           pl.BlockSpec((B, 1024), lambda n: (n, 0)),
                           kv(4, -1), kv(4, 0), kv(4, 1), kv(5, -1), kv(5, 0), kv(5, 1)],
                 out_specs=[pl.BlockSpec((B, 1024), lambda n: (n, 0)), pl.BlockSpec((B, LANES), lambda n: (n, 0))],
                 out_shape=[jax.ShapeDtypeStruct((S, 1024), MXU_DTYPE), jax.ShapeDtypeStruct((S, LANES), F32)],
                 )(sink, qkv, qkv, qkv, qkv, qkv, qkv, qkv)


def swa_bwd(qkv, sink, lse, dycat, tabs):
    S = qkv.shape[0]
    B = SWA_BLOCK
    nb = S // B

    def body(sink_ref, q_ref, kp_ref, kc_ref, kn_ref, vp_ref, vc_ref, vn_ref, do_ref, st_ref,
             cq_ref, saq_ref, sbq_ref, ck_ref, sak_ref, sbk_ref,
             dq_ref, dk_ref, dv_ref, dsink_ref, dk_acc, dv_acc):
        n = pl.program_id(0)

        @pl.when(n == 0)
        def _():
            dk_acc[...] = jnp.zeros_like(dk_acc)
            dv_acc[...] = jnp.zeros_like(dv_acc)
            dsink_ref[...] = jnp.zeros_like(dsink_ref)

        @pl.when(n < nb)
        def _():
            valid = _swa_valid(n, S)
            lane = lax.broadcasted_iota(jnp.int32, (B, LANES), 1)
            lane1 = lax.broadcasted_iota(jnp.int32, (1, LANES), 1)
            st = st_ref[...]
            cq, saq, sbq = cq_ref[...], saq_ref[...], sbq_ref[...]
            dsink = jnp.zeros((1, LANES), F32)
            for hk in range(2):
                sl = slice(hk * LANES, (hk + 1) * LANES)
                k3 = jnp.concatenate([kp_ref[:, sl], kc_ref[:, sl], kn_ref[:, sl]], axis=0)
                v3 = jnp.concatenate([vp_ref[:, sl], vc_ref[:, sl], vn_ref[:, sl]], axis=0)
                dk3 = jnp.zeros((3 * B, LANES), F32)
                dv3 = jnp.zeros((3 * B, LANES), F32)
                for g in range(4):
                    h = hk * 4 + g
                    hs = slice(h * LANES, (h + 1) * LANES)
                    q = q_ref[:, hs]
                    do = _mx(do_ref[:, hs])
                    lse_h = jnp.sum(jnp.where(lane == h, st, 0.0), axis=-1, keepdims=True)
                    p = jnp.where(valid, jnp.exp(_dot_nt(q, k3) - lse_h), 0.0)
                    dp = _dot_nt(do, v3)
                    dsum = jnp.sum(p * dp, axis=-1, keepdims=True)
                    ds = _mx(p * (dp - dsum))
                    dq_ref[:, hs] = _rope_t(_dot(ds, k3) * 0.125, cq, saq, sbq, SWA_HALF)
                    dk3 = dk3 + _dot_tn(ds, q)
                    dv3 = dv3 + _dot_tn(_mx(p), do)
                    dsk = -jnp.sum(jnp.exp(sink_ref[h] - lse_h) * dsum, axis=0, keepdims=True)
                    dsink = jnp.where(lane1 == h, dsk, dsink)
                dk_acc[:, sl] += dk3
                dv_acc[:, sl] += dv3
            dsink_ref[...] += dsink

        ck, sak, sbk = ck_ref[...], sak_ref[...], sbk_ref[...]
        for hk in range(2):
            sl = slice(hk * LANES, (hk + 1) * LANES)
            dk_ref[:, sl] = _rope_t(dk_acc[0:B, sl], ck, sak, sbk, SWA_HALF)
        dv_ref[...] = dv_acc[0:B, :]
        for acc in (dk_acc, dv_acc):
            acc[0:B, :] = acc[B:2 * B, :]
            acc[B:2 * B, :] = acc[2 * B:3 * B, :]
            acc[2 * B:3 * B, :] = jnp.zeros((B, 2 * LANES), F32)

    qn = lambda n: jnp.minimum(n, nb - 1)
    kv = lambda cb, d: pl.BlockSpec((B, 2 * LANES), lambda n: (jnp.clip(qn(n) + d, 0, nb - 1), cb))
    qrow = lambda w: pl.BlockSpec((B, w), lambda n: (qn(n), 0))
    krow = lambda w: pl.BlockSpec((B, w), lambda n: (jnp.maximum(n - 1, 0), 0))
    return _call(body, name="swa_bwd", grid=(nb + 1,),
                 in_specs=[pl.BlockSpec(memory_space=pltpu.SMEM), qrow(1024),
                           kv(4, -1), kv(4, 0), kv(4, 1), kv(5, -1), kv(5, 0), kv(5, 1),
                           qrow(1024), qrow(LANES),
                           qrow(LANES), qrow(LANES), qrow(LANES), krow(LANES), krow(LANES), krow(LANES)],
                 out_specs=[qrow(1024), krow(2 * LANES), krow(2 * LANES), pl.BlockSpec((1, LANES), lambda n: (0, 0))],
                 out_shape=[jax.ShapeDtypeStruct((S, 1024), F32), jax.ShapeDtypeStruct((S, 2 * LANES), F32),
                            jax.ShapeDtypeStruct((S, 2 * LANES), F32), jax.ShapeDtypeStruct((1, LANES), F32)],
                 scratch=[pltpu.VMEM((3 * B, 2 * LANES), F32), pltpu.VMEM((3 * B, 2 * LANES), F32)],
                 )(sink, qkv, qkv, qkv, qkv, qkv, qkv, qkv, dycat, lse, *tabs, *tabs)


def _halo_specs(T, S, w, cb):
    r = T // HALO
    last = S // HALO - 1
    return [pl.BlockSpec((HALO, w), lambda i: (jnp.maximum(i * r - 1, 0), cb)),
            pl.BlockSpec((T, w), lambda i: (i, cb)),
            pl.BlockSpec((HALO, w), lambda i: (jnp.minimum((i + 1) * r, last), cb))]


def _fill_ext(ext, prev_ref, cur_ref, next_ref, i, nt, T):
    ext[0:HALO, :] = jnp.where(i > 0, prev_ref[...], 0.0).astype(F32)
    ext[HALO:HALO + T, :] = cur_ref[...].astype(F32)
    ext[HALO + T:2 * HALO + T, :] = jnp.where(i < nt - 1, next_ref[...], 0.0).astype(F32)


def _pool_cnt(t, half, S):
    return (jnp.clip(t + half, 0, S) - jnp.clip(t - half, 0, S)).astype(F32)


def pool_fwd(z, w_pool, scale, T=256):
    S = z.shape[0]
    T = min(T, S)
    nt = S // T

    def body(up_ref, uc_ref, un_ref, w_ref, sc_ref, o_ref, ext):
        i = pl.program_id(0)
        _fill_ext(ext, up_ref, uc_ref, un_ref, i, nt, T)
        t = i * T + lax.broadcasted_iota(jnp.int32, (T, 1), 0)
        for g, win in enumerate(POOL_WINDOWS):
            half = win // 2
            sl = slice(g * LANES, (g + 1) * LANES)
            acc = ext[pl.ds(HALO - half, T), sl]
            for off in range(-half + 1, half):
                acc = acc + ext[pl.ds(HALO + off, T), sl]
            d = acc / _pool_cnt(t, half, S) - ext[pl.ds(HALO, T), sl]
            o_ref[:, sl] = (_dot(_mx(d), w_ref[g]) * sc_ref[:, sl]).astype(o_ref.dtype)

    return _call(body, name="pool_fwd", grid=(nt,),
                 in_specs=_halo_specs(T, S, 512, 3) + [pl.BlockSpec((4, LANES, LANES), lambda i: (0, 0, 0)),
                                                      pl.BlockSpec((1, 512), lambda i: (0, 0))],
                 out_specs=pl.BlockSpec((T, 512), lambda i: (i, 0)),
                 out_shape=jax.ShapeDtypeStruct((S, 512), MXU_DTYPE),
                 scratch=[pltpu.VMEM((T + 2 * HALO, 512), F32)])(z, z, z, w_pool, scale)


def pool_bwd(z, dycat, w_pool, scale, T=256):
    S = z.shape[0]
    T = min(T, S)
    nt = S // T
    TE = T + 2 * HALO

    def body(up_ref, uc_ref, un_ref, yp_ref, yc_ref, yn_ref, w_ref, sc_ref, du_ref, dw_ref, dsc_ref, extu, exty, exte):
        i = pl.program_id(0)

        @pl.when(i == 0)
        def _():
            dw_ref[...] = jnp.zeros_like(dw_ref)
            dsc_ref[...] = jnp.zeros_like(dsc_ref)

        _fill_ext(extu, up_ref, uc_ref, un_ref, i, nt, T)
        _fill_ext(exty, yp_ref, yc_ref, yn_ref, i, nt, T)
        t = i * T + lax.broadcasted_iota(jnp.int32, (T, 1), 0)
        te = i * T - HALO + lax.broadcasted_iota(jnp.int32, (TE, 1), 0)
        for g, win in enumerate(POOL_WINDOWS):
            half = win // 2
            sl = slice(g * LANES, (g + 1) * LANES)
            w = w_ref[g]
            acc = extu[pl.ds(HALO - half, T), sl]
            for off in range(-half + 1, half):
                acc = acc + extu[pl.ds(HALO + off, T), sl]
            d = _mx(acc / _pool_cnt(t, half, S) - extu[pl.ds(HALO, T), sl])
            dy = exty[pl.ds(HALO, T), sl]
            dsc_ref[:, sl] += jnp.sum(dy * _dot(d, w), axis=0, keepdims=True)
            dw_ref[g] += _dot_tn(d, _mx(dy * sc_ref[:, sl]))
            dd = _dot_nt(_mx(exty[:, sl] * sc_ref[:, sl]), w)
            exte[:, sl] = dd / jnp.maximum(_pool_cnt(te, half, S), 1.0)
            acc = exte[pl.ds(HALO - half + 1, T), sl]
            for off in range(-half + 2, half + 1):
                acc = acc + exte[pl.ds(HALO + off, T), sl]
            du_ref[:, sl] = acc - dd[HALO:HALO + T, :]

    return _call(body, name="pool_bwd", grid=(nt,),
                 in_specs=_halo_specs(T, S, 512, 3) + _halo_specs(T, S, 512, 2)
                 + [pl.BlockSpec((4, LANES, LANES), lambda i: (0, 0, 0)), pl.BlockSpec((1, 512), lambda i: (0, 0))],
                 out_specs=[pl.BlockSpec((T, 512), lambda i: (i, 0)), pl.BlockSpec((4, LANES, LANES), lambda i: (0, 0, 0)),
                            pl.BlockSpec((1, 512), lambda i: (0, 0))],
                 out_shape=[jax.ShapeDtypeStruct((S, 512), F32), jax.ShapeDtypeStruct((4, LANES, LANES), F32),
                            jax.ShapeDtypeStruct((1, 512), F32)],
                 scratch=[pltpu.VMEM((TE, 512), F32)] * 3)(z, z, z, dycat, dycat, dycat, w_pool, scale)


MLA_HALF = 16
MLA_V_ONE = 64


def mla_prep(z, g_cq, g_ckv, w_uq, w_k, w_v, tabs_q, tabs_k, T=256):
    S = z.shape[0]
    T = min(T, S)

    def body(z_ref, gq_ref, gkv_ref, wq_ref, wk_ref, wv_ref, cq_ref, saq_ref, sbq_ref, ck_ref, sak_ref, sbk_ref,
             q_ref, k_ref, v_ref, nq_ref, nkv_ref):
        nq = _mx(_rms(z_ref[:, 0:256], gq_ref[...]))
        nkv = _mx(_rms(z_ref[:, 256:384], gkv_ref[...]))
        nq_ref[...] = nq
        nkv_ref[...] = nkv
        q = _dot(nq, wq_ref[...])
        kn = _dot(nkv, wk_ref[...])
        lane = lax.broadcasted_iota(jnp.int32, (T, 1024), 1)
        v_ref[...] = jnp.where(lane % LANES == MLA_V_ONE, 1.0, _dot(nkv, wv_ref[...])).astype(v_ref.dtype)
        kr = pltpu.roll(_rope(z_ref[:, 384:512], ck_ref[...], sak_ref[...], sbk_ref[...], MLA_HALF), 64, 1)
        cq, saq, sbq = cq_ref[...], saq_ref[...], sbq_ref[...]
        for h in range(8):
            hs = slice(h * LANES, (h + 1) * LANES)
            q_ref[:, hs] = _rope(q[:, hs], cq, saq, sbq, MLA_HALF).astype(q_ref.dtype)
            k_ref[:, hs] = (kn[:, hs] + kr).astype(k_ref.dtype)

    tab = pl.BlockSpec((T, LANES), lambda i: (i, 0))
    full = lambda a: pl.BlockSpec(a.shape, lambda i: (0, 0))
    row = lambda w: pl.BlockSpec((T, w), lambda i: (i, 0))
    sd = lambda w: jax.ShapeDtypeStruct((S, w), MXU_DTYPE)
    return _call(body, name="mla_prep", grid=(S // T,),
                 in_specs=[row(512), full(g_cq), full(g_ckv), full(w_uq), full(w_k), full(w_v)] + [tab] * 6,
                 out_specs=[row(1024), row(1024), row(1024), row(256), row(128)],
                 out_shape=[sd(1024), sd(1024), sd(1024), sd(256), sd(128)],
                 )(z, g_cq, g_ckv, w_uq, w_k, w_v, *tabs_q, *tabs_k)


def _col_to_row(c):
    return jnp.transpose(jnp.broadcast_to(c, (c.shape[0], LANES)))[0:1, :]


def mla_vt(v, TK=512):
    S = v.shape[0]
    TK = min(TK, S // 2)
    return v.reshape(S // TK, TK, 8, LANES).transpose(2, 0, 3, 1)


def mla_dq(dqt):
    _, nq, _, TQ = dqt.shape
    return dqt.transpose(1, 3, 0, 2).reshape(nq * TQ, 1024)


def mla_fwd(q, k, vt, TQ=512):
    S = q.shape[0]
    nk, TK = vt.shape[1], vt.shape[3]
    TQ = min(TQ, S)
    assert nk % 2 == 0

    def body(q_ref, k_ref, vt_ref, o_ref, lse_ref, m_s, acc_s, s_a, s_b):
        m_s[...] = jnp.full_like(m_s, -jnp.inf)
        acc_s[...] = jnp.zeros_like(acc_s)
        q = q_ref[...]

        def scores(c):
            return _dot_nt(k_ref[pl.ds(pl.multiple_of(c * TK, TK), TK), :], q)

        def softmax_pv(s_ref, c):
            s = s_ref[...]
            m_old = m_s[...]
            m_new = jnp.maximum(m_old, jnp.max(s, axis=0, keepdims=True))
            alpha = jnp.exp((m_old[0:1, :] - m_new[0:1, :]) * MLA_SCALE)
            p = jnp.exp((s - m_new[0:1, :]) * MLA_SCALE)
            acc_s[...] = alpha * acc_s[...] + _dot(vt_ref[0, c], _mx(p))
            m_s[...] = m_new

        s_a[...] = scores(0)

        def pair(jj, carry):
            c = 2 * jj
            s_b[...] = scores(c + 1)
            softmax_pv(s_a, c)
            s_a[...] = scores(jnp.minimum(c + 2, nk - 1))
            softmax_pv(s_b, c + 1)
            return carry

        lax.fori_loop(0, nk // 2, pair, 0)
        acc = acc_s[...]
        den = acc[MLA_V_ONE:MLA_V_ONE + 1, :]
        sub = lax.broadcasted_iota(jnp.int32, (LANES, TQ), 0)
        o_ref[...] = jnp.transpose(jnp.where(sub < MLA_V_ONE, acc / den, 0.0))
        lse_ref[0] = m_s[0:1, :] * MLA_SCALE + jnp.log(den)

    qs = pl.BlockSpec((TQ, LANES), lambda h, i: (i, h))
    return _call(body, name="mla_fwd", grid=(8, S // TQ),
                 in_specs=[qs, pl.BlockSpec((S, LANES), lambda h, i: (0, h)),
                           pl.BlockSpec((1, nk, LANES, TK), lambda h, i: (h, 0, 0, 0))],
                 out_specs=[qs, pl.BlockSpec((1, 1, TQ), lambda h, i: (h, 0, i))],
                 out_shape=[jax.ShapeDtypeStruct((S, 1024), F32), jax.ShapeDtypeStruct((8, 1, S), F32)],
                 scratch=[pltpu.VMEM((8, TQ), F32), pltpu.VMEM((LANES, TQ), F32),
                          pltpu.VMEM((TK, TQ), F32), pltpu.VMEM((TK, TQ), F32)],
                 )(q, k, vt)


def mla_delta(o, dycat, TQ=512):
    S = o.shape[0]
    TQ = min(TQ, S)

    def body(o_ref, do_ref, d_ref, dob_ref):
        do = do_ref[...]
        d_ref[0] = _col_to_row(jnp.sum(o_ref[...] * do, axis=-1, keepdims=True))
        dob_ref[...] = do.astype(dob_ref.dtype)

    qs = pl.BlockSpec((TQ, LANES), lambda h, i: (i, h))
    return _call(body, name="mla_delta", grid=(8, S // TQ), in_specs=[qs, qs],
                 out_specs=[pl.BlockSpec((1, 1, TQ), lambda h, i: (h, 0, i)), qs],
                 out_shape=[jax.ShapeDtypeStruct((8, 1, S), F32), jax.ShapeDtypeStruct((S, 1024), MXU_DTYPE)])(o, dycat)


def mla_bwd(q, k, v, do, lse, delta, TQ=512, TK=512):
    S = q.shape[0]
    TQ, TK = min(TQ, S // 2), min(TK, S)
    nq, nk = S // TQ, S // TK
    assert nq % 2 == 0
    lse = lse.reshape(8, nq, 1, TQ)
    delta = delta.reshape(8, nq, 1, TQ)

    def body(q_ref, do_ref, k_ref, v_ref, lse_ref, d_ref, dqt_ref, dk_ref, dv_ref, dk_s, dv_s, s_a, p_a, s_b, p_b):
        @pl.when(pl.program_id(1) == 0)
        def _():
            dqt_ref[...] = jnp.zeros_like(dqt_ref)

        dk_s[...] = jnp.zeros_like(dk_s)
        dv_s[...] = jnp.zeros_like(dv_s)
        kk, vv = k_ref[...], v_ref[...]
        kt = _mx(jnp.transpose(kk.astype(F32)))

        def rows(c):
            return pl.ds(pl.multiple_of(c * TQ, TQ), TQ)

        def products(c, s_ref, p_ref):
            s_ref[...] = _dot_nt(kk, q_ref[rows(c), :])
            p_ref[...] = _dot_nt(vv, do_ref[rows(c), :])

        def consume(c, s_ref, p_ref):
            qc, doc = q_ref[rows(c), :], do_ref[rows(c), :]
            pt = jnp.exp(s_ref[...] * MLA_SCALE - lse_ref[0, c])
            dv_s[...] += _dot(_mx(pt), doc)
            ds = _mx(pt * (p_ref[...] - d_ref[0, c]) * MLA_SCALE)
            dk_s[...] += _dot(ds, qc)
            dqt_ref[0, c] += _dot(kt, ds)

        products(0, s_a, p_a)

        def pair(jj, carry):
            c = 2 * jj
            products(c + 1, s_b, p_b)
            consume(c, s_a, p_a)
            products(jnp.minimum(c + 2, nq - 1), s_a, p_a)
            consume(c + 1, s_b, p_b)
            return carry

        lax.fori_loop(0, nq // 2, pair, 0)
        dk_ref[...] = dk_s[...]
        dv_ref[...] = dv_s[...]

    full = pl.BlockSpec((S, LANES), lambda h, j: (0, h))
    ks = pl.BlockSpec((TK, LANES), lambda h, j: (j, h))
    st = pl.BlockSpec((1, nq, 1, TQ), lambda h, j: (h, 0, 0, 0))
    sd = jax.ShapeDtypeStruct((S, 1024), F32)
    return _call(body, name="mla_bwd", grid=(8, nk), in_specs=[full, full, ks, ks, st, st],
                 out_specs=[pl.BlockSpec((1, nq, LANES, TQ), lambda h, j: (h, 0, 0, 0)), ks, ks],
                 out_shape=[jax.ShapeDtypeStruct((8, nq, LANES, TQ), F32), sd, sd],
                 scratch=[pltpu.VMEM((TK, LANES), F32), pltpu.VMEM((TK, LANES), F32)] + [pltpu.VMEM((TK, TQ), F32)] * 4,
                 )(q, do, k, v, lse, delta)


def mla_prep_bwd(z, g_cq, g_ckv, w_uq, w_k, w_v, dq, dk, dv, tabs_q, tabs_k, T=256):
    S = z.shape[0]
    T = min(T, S)

    def body(z_ref, gq_ref, gkv_ref, wq_ref, wk_ref, wv_ref, dq_ref, dk_ref, dv_ref,
             cq_ref, saq_ref, sbq_ref, ck_ref, sak_ref, sbk_ref, dz_ref, dqp_ref, dgq_ref, dgkv_ref):
        @pl.when(pl.program_id(0) == 0)
        def _():
            dgq_ref[...] = jnp.zeros_like(dgq_ref)
            dgkv_ref[...] = jnp.zeros_like(dgkv_ref)

        cq, saq, sbq = cq_ref[...], saq_ref[...], sbq_ref[...]
        dkr = jnp.zeros((T, LANES), F32)
        for h in range(8):
            hs = slice(h * LANES, (h + 1) * LANES)
            dqp_ref[:, hs] = _rope_t(dq_ref[:, hs], cq, saq, sbq, MLA_HALF).astype(dqp_ref.dtype)
            dkr = dkr + dk_ref[:, hs]
        lane = lax.broadcasted_iota(jnp.int32, (T, LANES), 1)
        dkr = jnp.where(lane < 2 * MLA_HALF, pltpu.roll(dkr, 64, 1), 0.0)
        dz_ref[:, 384:512] = _rope_t(dkr, ck_ref[...], sak_ref[...], sbk_ref[...], MLA_HALF)
        dnq = _dot_nt(dqp_ref[...], wq_ref[...])
        dx, dg = _rms_bwd(z_ref[:, 0:256], gq_ref[...], dnq)
        dz_ref[:, 0:256] = dx
        dgq_ref[...] += dg
        dnkv = _dot_nt(_mx(dk_ref[...]), wk_ref[...]) + _dot_nt(_mx(dv_ref[...]), wv_ref[...])
        dx, dg = _rms_bwd(z_ref[:, 256:384], gkv_ref[...], dnkv)
        dz_ref[:, 256:384] = dx
        dgkv_ref[...] += dg

    tab = pl.BlockSpec((T, LANES), lambda i: (i, 0))
    full = lambda a: pl.BlockSpec(a.shape, lambda i: (0, 0))
    row = lambda w: pl.BlockSpec((T, w), lambda i: (i, 0))
    return _call(body, name="mla_prep_bwd", grid=(S // T,),
                 in_specs=[row(512), full(g_cq), full(g_ckv), full(w_uq), full(w_k), full(w_v),
                           row(1024), row(1024), row(1024)] + [tab] * 6,
                 out_specs=[row(512), row(1024), full(g_cq), full(g_ckv)],
                 out_shape=[jax.ShapeDtypeStruct((S, 512), F32), jax.ShapeDtypeStruct((S, 1024), MXU_DTYPE),
                            jax.ShapeDtypeStruct(g_cq.shape, F32), jax.ShapeDtypeStruct(g_ckv.shape, F32)],
                 )(z, g_cq, g_ckv, w_uq, w_k, w_v, dq, dk, dv, *tabs_q, *tabs_k)


def _lru_gates(xc, w_ref, bias_ref, lam_ref):
    pre = _dot(_mx(xc), w_ref[...]) + bias_ref[...]
    out = []
    for d in range(2):
        r = _sigmoid(pre[:, d * 1024:d * 1024 + 512])
        ig = _sigmoid(pre[:, d * 1024 + 512:(d + 1) * 1024])
        log_a = -LRU_C * r * _softplus(-lam_ref[:, d * 512:(d + 1) * 512])
        out.append((r, ig, jnp.exp(log_a), jnp.sqrt(-_expm1(2.0 * log_a))))
    return out


def lru_pre(z, conv_w, conv_b, w_gate, b_gate, lam, T=256):
    S = z.shape[0]
    T = min(T, S)
    nt = S // T

    def body(xp_ref, xcur_ref, xn_ref, cw_ref, cb_ref, w_ref, bias_ref, lam_ref, xc_ref, a0_ref, b0_ref, a1_ref, b1_ref, ext):
        i = pl.program_id(0)
        _fill_ext(ext, xp_ref, xcur_ref, xn_ref, i, nt, T)
        xc = cb_ref[...] + cw_ref[0:1, :] * ext[pl.ds(HALO - 2, T), :]
        for j in range(1, 4):
            xc = xc + cw_ref[j:j + 1, :] * ext[pl.ds(HALO - 2 + j, T), :]
        xc_ref[...] = xc
        (_, i0, a0, m0), (_, i1, a1, m1) = _lru_gates(xc, w_ref, bias_ref, lam_ref)
        a0_ref[...] = a0
        b0_ref[...] = m0 * (i0 * xc)
        a1_ref[...] = a1
        b1_ref[...] = m1 * (i1 * xc)

    full = lambda a: pl.BlockSpec(a.shape, lambda i: (0, 0))
    row = pl.BlockSpec((T, 512), lambda i: (i, 0))
    sd = jax.ShapeDtypeStruct((S, 512), F32)
    return _call(body, name="lru_pre", grid=(nt,),
                 in_specs=_halo_specs(T, S, 512, 1) + [full(conv_w), full(conv_b), full(w_gate), full(b_gate), full(lam)],
                 out_specs=[row] * 5, out_shape=[sd] * 5,
                 scratch=[pltpu.VMEM((T + 2 * HALO, 512), F32)])(z, z, z, conv_w, conv_b, w_gate, b_gate, lam)


def lru_scan(name, af, bf, ar, br, *, adjoint, T=256):
    S, W = af.shape
    T = min(T, S)
    nt = S // T
    nc = T // 8

    def body(af_ref, bf_ref, ar_ref, br_ref, hf_ref, hr_ref, cf, cr):
        @pl.when(pl.program_id(0) == 0)
        def _():
            cf[...] = jnp.zeros_like(cf)
            cr[...] = jnp.zeros_like(cr)

        row = lax.broadcasted_iota(jnp.int32, (8, W), 0)

        def step(a, b, carry):
            if adjoint:
                val = b + carry
                return val, a * val
            val = a * carry + b
            return val, val

        def chunk(c, carry):
            hf, hr = carry
            of = pl.multiple_of(c * 8, 8)
            orv = pl.multiple_of((nc - 1 - c) * 8, 8)
            a8, b8 = af_ref[pl.ds(of, 8), :], bf_ref[pl.ds(of, 8), :]
            ra8, rb8 = ar_ref[pl.ds(orv, 8), :], br_ref[pl.ds(orv, 8), :]
            outf = jnp.zeros((8, W), F32)
            outr = jnp.zeros((8, W), F32)
            for k in range(8):
                val, hf = step(a8[k:k + 1, :], b8[k:k + 1, :], hf)
                outf = jnp.where(row == k, val, outf)
                kr = 7 - k
                val, hr = step(ra8[kr:kr + 1, :], rb8[kr:kr + 1, :], hr)
                outr = jnp.where(row == kr, val, outr)
            hf_ref[pl.ds(of, 8), :] = outf
            hr_ref[pl.ds(orv, 8), :] = outr
            return hf, hr

        hf, hr = lax.fori_loop(0, nc, chunk, (cf[0:1, :], cr[0:1, :]))
        cf[0:1, :] = hf
        cr[0:1, :] = hr

    fw = pl.BlockSpec((T, W), lambda i: (i, 0))
    rv = pl.BlockSpec((T, W), lambda i: (nt - 1 - i, 0))
    sd = jax.ShapeDtypeStruct((S, W), F32)
    return _call(body, name=name, grid=(nt,), in_specs=[fw, fw, rv, rv], out_specs=[fw, rv], out_shape=[sd, sd],
                 scratch=[pltpu.VMEM((8, W), F32), pltpu.VMEM((8, W), F32)])(af, bf, ar, br)


def lru_gate(h0, h1, z, T=256):
    S = z.shape[0]
    T = min(T, S)

    def body(h0_ref, h1_ref, xg_ref, y_ref):
        y_ref[...] = ((h0_ref[...] + h1_ref[...]) * _gelu(xg_ref[...])).astype(y_ref.dtype)

    row = pl.BlockSpec((T, 512), lambda i: (i, 0))
    return _call(body, name="lru_gate", grid=(S // T,), in_specs=[row, row, pl.BlockSpec((T, 512), lambda i: (i, 2))],
                 out_specs=row, out_shape=jax.ShapeDtypeStruct((S, 512), MXU_DTYPE))(h0, h1, z)


def lru_gate_bwd(h0, h1, z, dycat, T=256):
    S = z.shape[0]
    T = min(T, S)

    def body(h0_ref, h1_ref, xg_ref, dy_ref, dxg_ref, dh_ref):
        xg, dy = xg_ref[...], dy_ref[...]
        dxg_ref[...] = dy * (h0_ref[...] + h1_ref[...]) * _gelu_grad(xg)
        dh_ref[...] = dy * _gelu(xg)

    row = pl.BlockSpec((T, 512), lambda i: (i, 0))
    col2 = pl.BlockSpec((T, 512), lambda i: (i, 2))
    sd = jax.ShapeDtypeStruct((S, 512), F32)
    return _call(body, name="lru_gate_bwd", grid=(S // T,), in_specs=[row, row, col2, col2],
                 out_specs=[row, row], out_shape=[sd, sd])(h0, h1, z, dycat)


def lru_bwd_point(xc, h0, h1, g0, g1, w_gate, b_gate, lam, T=256):
    S = xc.shape[0]
    T = min(T, S)
    nt = S // T

    def body(xc_ref, h0p_ref, h0_ref, h0n_ref, h1p_ref, h1_ref, h1n_ref, g0_ref, g1_ref, w_ref, bias_ref, lam_ref,
             dxc_ref, dpre_ref, dbias_ref, dlam_ref, ext0, ext1):
        i = pl.program_id(0)

        @pl.when(i == 0)
        def _():
            dbias_ref[...] = jnp.zeros_like(dbias_ref)
            dlam_ref[...] = jnp.zeros_like(dlam_ref)

        _fill_ext(ext0, h0p_ref, h0_ref, h0n_ref, i, nt, T)
        _fill_ext(ext1, h1p_ref, h1_ref, h1n_ref, i, nt, T)
        xc = xc_ref[...]
        gates = _lru_gates(xc, w_ref, bias_ref, lam_ref)
        hshift = (ext0[pl.ds(HALO - 1, T), :], ext1[pl.ds(HALO + 1, T), :])
        gs = (g0_ref[...], g1_ref[...])
        dxc = jnp.zeros((T, 512), F32)
        for d in range(2):
            r, ig, a, mult = gates[d]
            db = gs[d]
            da = db * hshift[d]
            dmult = db * (ig * xc)
            di = db * (mult * xc)
            dxc = dxc + db * (mult * ig)
            dloga = da * a - dmult * (a * a / mult)
            lam_d = lam_ref[:, d * 512:(d + 1) * 512]
            dr = dloga * (-LRU_C * _softplus(-lam_d))
            dsp = jnp.sum(dloga * (-LRU_C * r), axis=0, keepdims=True)
            dlam_ref[:, d * 512:(d + 1) * 512] += dsp * (-_sigmoid(-lam_d))
            dpre_ref[:, d * 1024:d * 1024 + 512] = (dr * (r * (1.0 - r))).astype(dpre_ref.dtype)
            dpre_ref[:, d * 1024 + 512:(d + 1) * 1024] = (di * (ig * (1.0 - ig))).astype(dpre_ref.dtype)
            dbias_ref[:, d * 1024:d * 1024 + 512] += jnp.sum(dr * (r * (1.0 - r)), axis=0, keepdims=True)
            dbias_ref[:, d * 1024 + 512:(d + 1) * 1024] += jnp.sum(di * (ig * (1.0 - ig)), axis=0, keepdims=True)
        dxc_ref[...] = dxc + _dot_nt(dpre_ref[...], w_ref[...])

    full = lambda a: pl.BlockSpec(a.shape, lambda i: (0, 0))
    row = pl.BlockSpec((T, 512), lambda i: (i, 0))
    return _call(body, name="lru_bwd_point", grid=(nt,),
                 in_specs=[row] + _halo_specs(T, S, 512, 0) + _halo_specs(T, S, 512, 0) + [row, row, full(w_gate), full(b_gate), full(lam)],
                 out_specs=[row, pl.BlockSpec((T, 2048), lambda i: (i, 0)), full(b_gate), full(lam)],
                 out_shape=[jax.ShapeDtypeStruct((S, 512), F32), jax.ShapeDtypeStruct((S, 2048), MXU_DTYPE),
                            jax.ShapeDtypeStruct(b_gate.shape, F32), jax.ShapeDtypeStruct(lam.shape, F32)],
                 scratch=[pltpu.VMEM((T + 2 * HALO, 512), F32)] * 2,
                 )(xc, h0, h0, h0, h1, h1, h1, g0, g1, w_gate, b_gate, lam)


def conv_bwd(z, dxc, conv_w, T=256):
    S = z.shape[0]
    T = min(T, S)
    nt = S // T

    def body(xp_ref, xcur_ref, xn_ref, dp_ref, dcur_ref, dn_ref, cw_ref, dx_ref, dw_ref, db_ref, extx, extd):
        i = pl.program_id(0)

        @pl.when(i == 0)
        def _():
            dw_ref[...] = jnp.zeros_like(dw_ref)
            db_ref[...] = jnp.zeros_like(db_ref)

        _fill_ext(extx, xp_ref, xcur_ref, xn_ref, i, nt, T)
        _fill_ext(extd, dp_ref, dcur_ref, dn_ref, i, nt, T)
        d = extd[pl.ds(HALO, T), :]
        dx = cw_ref[0:1, :] * extd[pl.ds(HALO + 2, T), :]
        for j in range(1, 4):
            dx = dx + cw_ref[j:j + 1, :] * extd[pl.ds(HALO + 2 - j, T), :]
        dx_ref[...] = dx
        for j in range(4):
            dw_ref[j:j + 1, :] += jnp.sum(d * extx[pl.ds(HALO - 2 + j, T), :], axis=0, keepdims=True)
        db_ref[...] += jnp.sum(d, axis=0, keepdims=True)

    full = lambda a: pl.BlockSpec(a.shape, lambda i: (0, 0))
    row = pl.BlockSpec((T, 512), lambda i: (i, 0))
    return _call(body, name="conv_bwd", grid=(nt,),
                 in_specs=_halo_specs(T, S, 512, 1) + _halo_specs(T, S, 512, 0) + [full(conv_w)],
                 out_specs=[row, full(conv_w), pl.BlockSpec((1, 512), lambda i: (0, 0))],
                 out_shape=[jax.ShapeDtypeStruct((S, 512), F32), jax.ShapeDtypeStruct(conv_w.shape, F32),
                            jax.ShapeDtypeStruct((1, 512), F32)],
                 scratch=[pltpu.VMEM((T + 2 * HALO, 512), F32)] * 2)(z, z, z, dxc, dxc, dxc, conv_w)


MESH_ID = pl.DeviceIdType.MESH
ANY = pl.BlockSpec(memory_space=pl.ANY)


def _place():
    x, y, c = lax.axis_index("x"), lax.axis_index("y"), lax.axis_index("c")
    chips = [(1 - x, y), (x, 1 - y), (1 - x, 1 - y)]
    return x, y, c, chips


def _slot(px, py, pc):
    return 4 * px + 2 * py + pc


def all_gather(arrays):
    n = len(arrays)

    def body(*refs):
        ins, outs = refs[:n], refs[n:2 * n]
        send, recv, loc = refs[2 * n:]
        x, y, c, chips = _place()
        me, sibling = (x, y, c), (x, y, 1 - c)

        def copy(a, k, block, to, src=None):
            slot = outs[a].at[_slot(*block)]
            return pltpu.make_async_remote_copy(src_ref=slot if src is None else src, dst_ref=slot,
                                                send_sem=send.at[a * 7 + k], recv_sem=recv.at[a * 7 + k],
                                                device_id=to, device_id_type=MESH_ID)

        local = [pltpu.make_async_copy(ins[a], outs[a].at[_slot(*me)], loc.at[a]) for a in range(n)]
        for cp in local:
            cp.start()
        first = []
        for a in range(n):
            first.append(copy(a, 0, me, sibling, src=ins[a]))
            first += [copy(a, 1 + j, me, (*chip, c), src=ins[a]) for j, chip in enumerate(chips)]
        for cp in first:
            cp.start()
        passed = []
        for a in range(n):
            for j, chip in enumerate(chips):
                copy(a, 1 + j, (*chip, c), me).wait_recv()
                cp = copy(a, 4 + j, (*chip, c), sibling)
                cp.start()
                passed.append(cp)
        for a in range(n):
            copy(a, 0, sibling, me).wait_recv()
            for j, chip in enumerate(chips):
                copy(a, 4 + j, (*chip, 1 - c), me).wait_recv()
        for cp in first + passed:
            cp.wait_send()
        for cp in local:
            cp.wait()

    return pl.pallas_call(
        body, name="all_gather", in_specs=[ANY] * n, out_specs=[ANY] * n,
        out_shape=[jax.ShapeDtypeStruct((NDEV,) + a.shape, a.dtype) for a in arrays],
        scratch_shapes=[pltpu.SemaphoreType.DMA((n * 7,)), pltpu.SemaphoreType.DMA((n * 7,)),
                        pltpu.SemaphoreType.DMA((n,))],
    )(*arrays)


def grad_pair(gs):
    def body(gs_ref, o_ref, send, recv):
        x, y, c, _ = _place()
        cps = [pltpu.make_async_remote_copy(src_ref=gs_ref.at[2 * q + 1], dst_ref=o_ref.at[q], send_sem=send.at[q],
                                            recv_sem=recv.at[q], device_id=(x, y, 1 - c), device_id_type=MESH_ID)
               for q in range(4)]
        for cp in cps:
            cp.start()
        for cp in cps:
            cp.wait_recv()
        for cp in cps:
            cp.wait_send()

    return pl.pallas_call(
        body, name="grad_pair", in_specs=[ANY], out_specs=ANY,
        out_shape=jax.ShapeDtypeStruct((4,) + gs.shape[1:], gs.dtype),
        scratch_shapes=[pltpu.SemaphoreType.DMA((4,)), pltpu.SemaphoreType.DMA((4,))],
    )(gs)


def pair_add(gs, got, T=128):
    _, R, C = gs.shape
    T = min(T, R)

    def body(g0, g1, g2, g3, got_ref, own_ref, out_ref):
        own_ref[...] = g0[0] + got_ref[0]
        for q, g in enumerate((g1, g2, g3)):
            out_ref[q] = (g[0] + got_ref[q + 1]).astype(out_ref.dtype)

    even = lambda q: pl.BlockSpec((1, T, C), lambda i: (2 * q, i, 0))
    return _call(body, name="pair_add", grid=(R // T,),
                 in_specs=[even(0), even(1), even(2), even(3), pl.BlockSpec((4, T, C), lambda i: (0, i, 0))],
                 out_specs=[pl.BlockSpec((T, C), lambda i: (i, 0)), pl.BlockSpec((3, T, C), lambda i: (0, i, 0))],
                 out_shape=[jax.ShapeDtypeStruct((R, C), F32), jax.ShapeDtypeStruct((3, R, C), MXU_DTYPE)],
                 )(gs, gs, gs, gs, got)


def grad_cross(part, gr):
    def body(p_ref, gr_ref, o_ref, or_ref, send, recv, loc):
        x, y, c, chips = _place()
        peers = [(x, y, 1 - c)] + [(*chip, c) for chip in chips] + [(*chip, 1 - c) for chip in chips]
        local = pltpu.make_async_copy(gr_ref, or_ref.at[_slot(x, y, c)], loc.at[0])
        local.start()
        by_offset = [(x, 1 - y), (1 - x, y), (1 - x, 1 - y)]
        cps = [pltpu.make_async_remote_copy(src_ref=p_ref.at[j], dst_ref=o_ref.at[j], send_sem=send.at[j],
                                            recv_sem=recv.at[j], device_id=(*chip, c), device_id_type=MESH_ID)
               for j, chip in enumerate(by_offset)]
        cps += [pltpu.make_async_remote_copy(src_ref=gr_ref, dst_ref=or_ref.at[_slot(x, y, c)], send_sem=send.at[3 + k],
                                             recv_sem=recv.at[3 + k], device_id=peer, device_id_type=MESH_ID)
                for k, peer in enumerate(peers)]
        for cp in cps:
            cp.start()
        for cp in cps[:3]:
            cp.wait_recv()
        for k, peer in enumerate(peers):
            pltpu.make_async_remote_copy(src_ref=gr_ref, dst_ref=or_ref.at[_slot(*peer)], send_sem=send.at[3 + k],
                                         recv_sem=recv.at[3 + k], device_id=peer, device_id_type=MESH_ID).wait_recv()
        for cp in cps:
            cp.wait_send()
        local.wait()

    return pl.pallas_call(
        body, name="grad_cross", in_specs=[ANY, ANY], out_specs=[ANY, ANY],
        out_shape=[jax.ShapeDtypeStruct(part.shape, part.dtype), jax.ShapeDtypeStruct((NDEV,) + gr.shape, gr.dtype)],
        scratch_shapes=[pltpu.SemaphoreType.DMA((10,)), pltpu.SemaphoreType.DMA((10,)), pltpu.SemaphoreType.DMA((1,))],
    )(part, gr)


def adamw(name, gparts, w, m, v, T=128):
    R, C = w.shape
    T = min(T, R)
    ng = len(gparts)

    def body(*refs):
        g_refs = refs[:ng]
        w_ref, m_ref, v_ref, go_ref, d_ref, mo_ref, vo_ref = refs[ng:]
        g = None
        for g_ref in g_refs:
            for k in range(g_ref.shape[0]):
                t = g_ref[k].astype(F32)
                g = t if g is None else g + t
        mn = ADAM_B1 * m_ref[...] + (1.0 - ADAM_B1) * g
        vn = ADAM_B2 * v_ref[...] + (1.0 - ADAM_B2) * (g * g)
        m_hat = mn / (1.0 - ADAM_B1 ** ADAM_STEP)
        v_hat = vn / (1.0 - ADAM_B2 ** ADAM_STEP)
        go_ref[...] = g
        d_ref[...] = -ADAM_LR * (m_hat / (jnp.sqrt(v_hat) + ADAM_EPS) + ADAM_WD * w_ref[...])
        mo_ref[...] = mn
        vo_ref[...] = vn

    row = pl.BlockSpec((T, C), lambda i: (i, 0))
    sd = jax.ShapeDtypeStruct((R, C), F32)
    return _call(body, name=name, grid=(R // T,),
                 in_specs=[pl.BlockSpec((g.shape[0], T, C), lambda i: (0, i, 0)) for g in gparts] + [row, row, row],
                 out_specs=[row] * 4, out_shape=[sd] * 4)(*gparts, w, m, v)


PACK_W = 1024
PACK_TILE = 16
PACK_ROWS = 128


def _entry_rows(shape):
    n = 1
    for s in shape:
        n *= s
    return n, -(-n // (PACK_TILE * PACK_W)) * PACK_TILE


def _pack(arrays, dtype):
    mats, rows = [], 0
    for a in arrays:
        n, r = _entry_rows(a.shape)
        flat = a.astype(dtype).reshape(-1)
        if r * PACK_W != n:
            flat = jnp.pad(flat, (0, r * PACK_W - n))
        mats.append(flat.reshape(r, PACK_W))
        rows += r
    total = -(-rows // PACK_ROWS) * PACK_ROWS
    if total > rows:
        mats.append(jnp.zeros((total - rows, PACK_W), dtype))
    return jnp.concatenate(mats, axis=0)


def _unpack(buf, shapes, lead=()):
    out, off = [], 0
    for sh in shapes:
        n, r = _entry_rows(sh)
        piece = buf[..., off:off + r, :].reshape(lead + (r * PACK_W,))
        out.append(piece[..., :n].reshape(lead + tuple(sh)))
        off += r
    return out


def _unshard(parts, axis):
    return jnp.concatenate([parts[d] for d in range(NDEV)], axis=axis)


def _slab_cols(w, head, used, n_heads, slab=LANES):
    lead = w.shape[:-1]
    w = w.reshape(lead + (n_heads, head))[..., :used]
    w = jnp.pad(w, [(0, 0)] * len(lead) + [(0, 0), (0, slab - used)])
    return w.reshape(lead + (n_heads * slab,))


def _unslab_cols(g, used, n_heads, slab=LANES):
    lead = g.shape[:-1]
    return g.reshape(lead + (n_heads, slab))[..., :used].reshape(lead + (n_heads * used,))


def _block_diag(w):
    eye = jnp.eye(8, dtype=w.dtype)
    return jnp.einsum("nij,nm->nimj", w, eye).reshape(512, 512)


def _block_diag_t(g):
    g = g.reshape(8, 64, 8, 64)
    return jnp.stack([g[n, :, n, :] for n in range(8)])


BIG = (("e_w_in", 2), ("e_w_out", 1), ("o_w_in", 2), ("o_w_uq", 2), ("o_w_ukv", 2), ("o_w_out", 1),
       ("w_mlp1", 2), ("w_mlp2", 1))
SMALL = (("o_norm_mix", 1), ("o_g_cq", 1), ("o_conv_w", 2), ("o_conv_b", 1), ("o_lru_ba", 2), ("o_lru_bx", 2),
         ("o_lru_lambda", 2))
REPL = ("e_norm_mix", "e_sink", "e_w_pool", "e_pool_scale", "o_g_ckv", "o_lru_wa", "o_lru_wx", "norm_mlp",
        "final_norm")
WEIGHTS = ("e_norm_mix", "e_w_in", "e_sink", "e_w_pool", "e_pool_scale", "e_w_out", "o_norm_mix", "o_w_in", "o_g_cq",
           "o_w_uq", "o_g_ckv", "o_w_ukv", "o_conv_w", "o_conv_b", "o_lru_wa", "o_lru_ba", "o_lru_wx", "o_lru_bx",
           "o_lru_lambda", "o_w_out", "norm_mlp", "w_mlp1", "w_mlp2", "final_norm")


def _mlp_fwd(l, x, g, w1, w2):
    u1, hn = mm_nn(f"mlp1_{l}", [x], [w1], norm_g=g, emit_norm=True, out_dtype=MXU_DTYPE, tm=512, tn=1024)
    x2 = mm_nn(f"mlp2_{l}", [u1], [w2], act="relu2", res=x, tm=512, tn=1024)
    return x2, (x, u1, hn)


def _mlp_bwd(l, saved, g, w1, w2, dx2):
    x, u1, hn = saved
    du1 = mm_nt(f"mlp2_dx_{l}", [dx2], [w2], relu2_of=u1, out_dtype=MXU_DTYPE, tm=512, tn=1024)
    dw2 = mm_tn(f"mlp2_dw_{l}", u1, dx2, act="relu2", tm=1024, tn=1024)
    dhn = mm_nt(f"mlp1_dx_{l}", [du1], [w1], tm=512, tn=1024)
    dw1 = mm_tn(f"mlp1_dw_{l}", hn, du1, tm=1024, tn=1024)
    dx, dg = rms_bwd(f"mlp_norm_bwd_{l}", x, g, dhn, dx2)
    return dx, dg, dw1, dw2


def kernel(x, e_norm_mix, e_w_in, e_sink, e_w_pool, e_pool_scale, e_w_out, o_norm_mix, o_w_in, o_g_cq, o_w_uq, o_g_ckv, o_w_ukv, o_conv_w, o_conv_b, o_lru_wa, o_lru_ba, o_lru_wx, o_lru_bx, o_lru_lambda, o_w_out, norm_mlp, w_mlp1, w_mlp2, final_norm, loss_target, m_e_norm_mix, m_e_w_in, m_e_sink, m_e_w_pool, m_e_pool_scale, m_e_w_out, m_o_norm_mix, m_o_w_in, m_o_g_cq, m_o_w_uq, m_o_g_ckv, m_o_w_ukv, m_o_conv_w, m_o_conv_b, m_o_lru_wa, m_o_lru_ba, m_o_lru_wx, m_o_lru_bx, m_o_lru_lambda, m_o_w_out, m_norm_mlp, m_w_mlp1, m_w_mlp2, m_final_norm, v_e_norm_mix, v_e_w_in, v_e_sink, v_e_w_pool, v_e_pool_scale, v_e_w_out, v_o_norm_mix, v_o_w_in, v_o_g_cq, v_o_w_uq, v_o_g_ckv, v_o_w_ukv, v_o_conv_w, v_o_conv_b, v_o_lru_wa, v_o_lru_ba, v_o_lru_wx, v_o_lru_bx, v_o_lru_lambda, v_o_w_out, v_norm_mlp, v_w_mlp1, v_w_mlp2, v_final_norm):
    W = dict(e_norm_mix=e_norm_mix, e_w_in=e_w_in, e_sink=e_sink, e_w_pool=e_w_pool, e_pool_scale=e_pool_scale, e_w_out=e_w_out, o_norm_mix=o_norm_mix, o_w_in=o_w_in, o_g_cq=o_g_cq, o_w_uq=o_w_uq, o_g_ckv=o_g_ckv, o_w_ukv=o_w_ukv, o_conv_w=o_conv_w, o_conv_b=o_conv_b, o_lru_wa=o_lru_wa, o_lru_ba=o_lru_ba, o_lru_wx=o_lru_wx, o_lru_bx=o_lru_bx, o_lru_lambda=o_lru_lambda, o_w_out=o_w_out, norm_mlp=norm_mlp, w_mlp1=w_mlp1, w_mlp2=w_mlp2, final_norm=final_norm)
    Mo = dict(e_norm_mix=m_e_norm_mix, e_w_in=m_e_w_in, e_sink=m_e_sink, e_w_pool=m_e_w_pool, e_pool_scale=m_e_pool_scale, e_w_out=m_e_w_out, o_norm_mix=m_o_norm_mix, o_w_in=m_o_w_in, o_g_cq=m_o_g_cq, o_w_uq=m_o_w_uq, o_g_ckv=m_o_g_ckv, o_w_ukv=m_o_w_ukv, o_conv_w=m_o_conv_w, o_conv_b=m_o_conv_b, o_lru_wa=m_o_lru_wa, o_lru_ba=m_o_lru_ba, o_lru_wx=m_o_lru_wx, o_lru_bx=m_o_lru_bx, o_lru_lambda=m_o_lru_lambda, o_w_out=m_o_w_out, norm_mlp=m_norm_mlp, w_mlp1=m_w_mlp1, w_mlp2=m_w_mlp2, final_norm=m_final_norm)
    Vo = dict(e_norm_mix=v_e_norm_mix, e_w_in=v_e_w_in, e_sink=v_e_sink, e_w_pool=v_e_w_pool, e_pool_scale=v_e_pool_scale, e_w_out=v_e_w_out, o_norm_mix=v_o_norm_mix, o_w_in=v_o_w_in, o_g_cq=v_o_g_cq, o_w_uq=v_o_w_uq, o_g_ckv=v_o_g_ckv, o_w_ukv=v_o_w_ukv, o_conv_w=v_o_conv_w, o_conv_b=v_o_conv_b, o_lru_wa=v_o_lru_wa, o_lru_ba=v_o_lru_ba, o_lru_wx=v_o_lru_wx, o_lru_bx=v_o_lru_bx, o_lru_lambda=v_o_lru_lambda, o_w_out=v_o_w_out, norm_mlp=v_norm_mlp, w_mlp1=v_w_mlp1, w_mlp2=v_w_mlp2, final_norm=v_final_norm)

    S = x.shape[1]
    x0 = x[0]
    target = loss_target[0]

    big_g, small_g = all_gather([_pack([W[n] for n, _ in BIG], MXU_DTYPE), _pack([W[n] for n, _ in SMALL], F32)])
    full = {}
    for (n, ax), parts in zip(BIG, _unpack(big_g, [W[n].shape for n, _ in BIG], (NDEV,))):
        full[n] = _unshard(parts, ax)
    for (n, ax), parts in zip(SMALL, _unpack(small_g, [W[n].shape for n, _ in SMALL], (NDEV,))):
        full[n] = _unshard(parts, ax)

    def even_in(w):
        return jnp.concatenate([_slab_cols(w[:, 0:512], 64, 64, 8), _slab_cols(w[:, 512:640], 64, 64, 2),
                                _slab_cols(w[:, 640:768], 64, 64, 2), w[:, 768:1280]], axis=1)

    def even_in_t(g):
        return jnp.concatenate([_unslab_cols(g[:, 0:1024], 64, 8), _unslab_cols(g[:, 1024:1280], 64, 2),
                                _unslab_cols(g[:, 1280:1536], 64, 2), g[:, 1536:2048]], axis=1)

    def slab_rows(w, n_heads):
        return _slab_cols(w.T, 64, 64, n_heads).T

    def unslab_rows(g, n_heads):
        return _unslab_cols(g.T, 64, n_heads).T

    def odd_in(w):
        return jnp.concatenate([w[:, 0:384], jnp.pad(w[:, 384:416], ((0, 0), (0, 96))), w[:, 416:1440]], axis=1)

    def odd_in_t(g):
        return jnp.concatenate([g[:, 0:416], g[:, 512:1536]], axis=1)

    def uq(w):
        return _slab_cols(w, 96, 96, 8)

    def ukv(w):
        w = w.reshape(128, 8, 128)
        pad = lambda t: jnp.pad(t, ((0, 0), (0, 0), (0, 64))).reshape(128, 1024)
        return pad(w[:, :, :64]), pad(w[:, :, 64:])

    def ukv_t(gk, gv):
        gk = gk.reshape(128, 8, 128)[:, :, :64]
        gv = gv.reshape(128, 8, 128)[:, :, :64]
        return jnp.concatenate([gk, gv], axis=2).reshape(128, 1024)

    tabs_swa = rope_tables(S, 0, SWA_HALF)
    tabs_mq = rope_tables(S, 64, MLA_HALF)
    tabs_mk = rope_tables(S, 0, MLA_HALF)
    row = lambda v: v.reshape(1, -1)

    saved = []
    xcur = x0
    for l in range(4):
        j = l // 2
        if l % 2 == 0:
            w_in = even_in(full["e_w_in"][j])
            w_out = full["e_w_out"][j]
            w_out_a, w_out_b = slab_rows(w_out[0:512], 8), w_out[512:1024]
            w_pool = W["e_w_pool"][j].astype(MXU_DTYPE)
            z, h = mm_nn(f"e_in_{j}", [xcur], [w_in], norm_g=row(W["e_norm_mix"][j]), emit_norm=True, tm=256, tn=1024)
            qkv = swa_prep(z, tabs_swa)
            ya, lse = swa_fwd(qkv, W["e_sink"][j])
            yb = pool_fwd(z, w_pool, row(W["e_pool_scale"][j]))
            x1 = mm_nn(f"e_out_{j}", [ya, yb], [w_out_a, w_out_b], res=xcur)
            mix = (xcur, z, h, qkv, lse, ya, yb, w_in, w_out_a, w_out_b, w_pool)
        else:
            w_in = odd_in(full["o_w_in"][j])
            w_out = full["o_w_out"][j]
            w_out_a, w_out_b = slab_rows(w_out[0:512], 8), w_out[512:1024]
            w_uq = uq(full["o_w_uq"][j])
            w_k, w_v = ukv(full["o_w_ukv"][j])
            g_cq, g_ckv = row(full["o_g_cq"][j]), row(W["o_g_ckv"][j])
            w_gate = jnp.concatenate([_block_diag(W["o_lru_wa"][j, 0]), _block_diag(W["o_lru_wx"][j, 0]),
                                      _block_diag(W["o_lru_wa"][j, 1]), _block_diag(W["o_lru_wx"][j, 1])], axis=1).astype(MXU_DTYPE)
            b_gate = jnp.concatenate([full["o_lru_ba"][j, 0], full["o_lru_bx"][j, 0], full["o_lru_ba"][j, 1],
                                      full["o_lru_bx"][j, 1]]).reshape(1, 2048)
            lam = full["o_lru_lambda"][j].reshape(1, 1024)
            conv_w, conv_b = full["o_conv_w"][j], row(full["o_conv_b"][j])
            z, h = mm_nn(f"o_in_{j}", [xcur], [w_in], norm_g=row(full["o_norm_mix"][j]), emit_norm=True, tm=256, tn=512)
            q, k, v, nq, nkv = mla_prep(z, g_cq, g_ckv, w_uq, w_k, w_v, tabs_mq, tabs_mk)
            yc, lse = mla_fwd(q, k, mla_vt(v))
            xc, a0, b0, a1, b1 = lru_pre(z, conv_w, conv_b, w_gate, b_gate, lam)
            h0, h1 = lru_scan(f"lru_scan_fwd_{j}", a0, b0, a1, b1, adjoint=False)
            yd = lru_gate(h0, h1, z)
            x1 = mm_nn(f"o_out_{j}", [yc, yd], [w_out_a, w_out_b], res=xcur)
            mix = (xcur, z, h, q, k, v, nq, nkv, yc, lse, xc, a0, a1, h0, h1, yd, w_in, w_out_a, w_out_b, w_uq, w_k, w_v,
                   g_cq, g_ckv, w_gate, b_gate, lam, conv_w)
        xcur, mlp = _mlp_fwd(l, x1, row(W["norm_mlp"][l]), full["w_mlp1"][l], full["w_mlp2"][l])
        saved.append((mix, mlp))

    loss_row, dx, dg_final = loss_head(xcur, row(W["final_norm"]), target)

    G = {n: [None] * W[n].shape[0] for n in WEIGHTS if n != "final_norm"}
    G["final_norm"] = dg_final.reshape(-1)
    for l in reversed(range(4)):
        j = l // 2
        mix, mlp = saved[l]
        dx, dg, dw1, dw2 = _mlp_bwd(l, mlp, row(W["norm_mlp"][l]), full["w_mlp1"][l], full["w_mlp2"][l], dx)
        G["norm_mlp"][l], G["w_mlp1"][l], G["w_mlp2"][l] = dg.reshape(-1), dw1, dw2
        if l % 2 == 0:
            xin, z, h, qkv, lse, ya, yb, w_in, w_out_a, w_out_b, w_pool = mix
            w_out_cat = jnp.concatenate([w_out_a, w_out_b], axis=0)
            dycat = mm_nt(f"e_out_dx_{j}", [dx], [w_out_cat])
            G["e_w_out"][j] = jnp.concatenate([unslab_rows(mm_tn(f"e_out_dwa_{j}", ya, dx), 8),
                                               mm_tn(f"e_out_dwb_{j}", yb, dx)], axis=0)
            dq, dk, dv, dsink = swa_bwd(qkv, W["e_sink"][j], lse, dycat, tabs_swa)
            du, dwp, dsc = pool_bwd(z, dycat, w_pool, row(W["e_pool_scale"][j]))
            G["e_sink"][j], G["e_w_pool"][j], G["e_pool_scale"][j] = dsink[0, 0:8], dwp, dsc.reshape(-1)
            dh = mm_nt(f"e_in_dx_{j}", [dq, dk, dv, du], [(w_in, 0, 1024), (w_in, 4, 256), (w_in, 5, 256), (w_in, 3, 512)])
            G["e_w_in"][j] = even_in_t(jnp.concatenate(
                [mm_tn(f"e_in_dwq_{j}", h, dq), mm_tn(f"e_in_dwk_{j}", h, dk), mm_tn(f"e_in_dwv_{j}", h, dv),
                 mm_tn(f"e_in_dwu_{j}", h, du)], axis=1))
            dx, dg = rms_bwd(f"e_norm_bwd_{j}", xin, row(W["e_norm_mix"][j]), dh, dx)
            G["e_norm_mix"][j] = dg.reshape(-1)
        else:
            (xin, z, h, q, k, v, nq, nkv, yc, lse, xc, a0, a1, h0, h1, yd, w_in, w_out_a, w_out_b, w_uq, w_k, w_v,
             g_cq, g_ckv, w_gate, b_gate, lam, conv_w) = mix
            w_out_cat = jnp.concatenate([w_out_a, w_out_b], axis=0)
            dycat = mm_nt(f"o_out_dx_{j}", [dx], [w_out_cat])
            G["o_w_out"][j] = jnp.concatenate([unslab_rows(mm_tn(f"o_out_dwa_{j}", yc, dx), 8),
                                               mm_tn(f"o_out_dwb_{j}", yd, dx)], axis=0)
            delta, dob = mla_delta(yc, dycat)
            dqt, dk, dv = mla_bwd(q, k, v, dob, lse, delta)
            dq = mla_dq(dqt)
            dza, dqp, dgq, dgkv = mla_prep_bwd(z, g_cq, g_ckv, w_uq, w_k, w_v, dq, dk, dv, tabs_mq, tabs_mk)
            G["o_g_cq"][j], G["o_g_ckv"][j] = dgq.reshape(-1), dgkv.reshape(-1)
            G["o_w_uq"][j] = _unslab_cols(mm_tn(f"o_uq_dw_{j}", nq, dqp), 96, 8)
            G["o_w_ukv"][j] = ukv_t(mm_tn(f"o_uk_dw_{j}", nkv, dk), mm_tn(f"o_uv_dw_{j}", nkv, dv))
            dxg, dhh = lru_gate_bwd(h0, h1, z, dycat)
            g1, g0 = lru_scan(f"lru_scan_bwd_{j}", a1, dhh, a0, dhh, adjoint=True)
            dxc, dpre, dbias, dlam = lru_bwd_point(xc, h0, h1, g0, g1, w_gate, b_gate, lam)
            dwg = mm_tn(f"o_gate_dw_{j}", xc, dpre)
            G["o_lru_wa"][j] = jnp.stack([_block_diag_t(dwg[:, 0:512]), _block_diag_t(dwg[:, 1024:1536])])
            G["o_lru_wx"][j] = jnp.stack([_block_diag_t(dwg[:, 512:1024]), _block_diag_t(dwg[:, 1536:2048])])
            G["o_lru_ba"][j] = jnp.stack([dbias[0, 0:512], dbias[0, 1024:1536]])
            G["o_lru_bx"][j] = jnp.stack([dbias[0, 512:1024], dbias[0, 1536:2048]])
            G["o_lru_lambda"][j] = dlam.reshape(2, 512)
            dxr, dcw, dcb = conv_bwd(z, dxc, conv_w)
            G["o_conv_w"][j], G["o_conv_b"][j] = dcw, dcb.reshape(-1)
            dh = mm_nt(f"o_in_dx_{j}", [dza, dxr, dxg], [(w_in, 0, 512), (w_in, 1, 512), (w_in, 2, 512)])
            G["o_w_in"][j] = odd_in_t(jnp.concatenate(
                [mm_tn(f"o_in_dwa_{j}", h, dza), mm_tn(f"o_in_dwr_{j}", h, dxr), mm_tn(f"o_in_dwg_{j}", h, dxg)], axis=1))
            dx, dg = rms_bwd(f"o_norm_bwd_{j}", xin, row(full["o_norm_mix"][j]), dh, dx)
            G["o_norm_mix"][j] = dg.reshape(-1)
    grad_x = dx[None]
    G = {n: (g if n == "final_norm" else jnp.stack(g)) for n, g in G.items()}

    sharded = BIG + SMALL
    me = 4 * lax.axis_index("x") + 2 * lax.axis_index("y") + lax.axis_index("c")

    def shard_of(n, ax, dev):
        size = W[n].shape[ax]
        return lax.dynamic_slice_in_dim(G[n], dev * size, size, axis=ax)

    gs = jnp.stack([_pack([shard_of(n, ax, me ^ r) for n, ax in sharded], F32) for r in range(NDEV)])
    gr = _pack([G[n] for n in REPL] + [loss_row[0, 0:1]], F32)
    mine, others = pair_add(gs, grad_pair(gs))
    others, gr_all = grad_cross(others, gr)
    outs_s = adamw("adamw_sharded", [mine[None], others], *[_pack([T[n] for n, _ in sharded], F32) for T in (W, Mo, Vo)])
    outs_r = adamw("adamw_replicated", [gr_all],
                   *[_pack([T[n] for n in REPL] + [jnp.zeros((1,), F32)], F32) for T in (W, Mo, Vo)])
    res = [dict(), dict(), dict(), dict()]
    for kind in range(4):
        for (n, _), a in zip(sharded, _unpack(outs_s[kind], [W[n].shape for n, _ in sharded])):
            res[kind][n] = a
        for n, a in zip(REPL + ("loss",), _unpack(outs_r[kind], [W[n].shape for n in REPL] + [(1,)])):
            res[kind][n] = a
    loss = res[0]["loss"][0]
    return (loss, grad_x, *[res[0][n] for n in WEIGHTS], *[res[1][n] for n in WEIGHTS],
            *[res[2][n] for n in WEIGHTS], *[res[3][n] for n in WEIGHTS])
```

```python
import functools

import jax
import jax.numpy as jnp
from jax import lax
from jax.experimental import pallas as pl
from jax.experimental.pallas import tpu as pltpu

F32 = jnp.float32
MXU_DTYPE = jnp.bfloat16
EPS = 1e-6
ROPE_THETA = 10000.0
NDEV = 8
LANES = 128
VMEM_LIMIT = 48 * 1024 * 1024

D_MODEL = 1024
D_FF = 4096
LRU_C = 8.0
POOL_WINDOWS = (2, 4, 8, 16)
HALO = 8
MLA_SCALE = 96.0 ** -0.5
LOG2E = 1.4426950408889634
MLA_QSCALE = MLA_SCALE * LOG2E

ADAM_LR, ADAM_B1, ADAM_B2, ADAM_EPS, ADAM_WD, ADAM_STEP = 0.001, 0.9, 0.999, 1e-08, 0.01, 10


def _mx(v):
    return v.astype(MXU_DTYPE)


def _call(body, *, name, grid, in_specs, out_specs, out_shape, scratch=()):
    return pl.pallas_call(
        body, name=name, grid=grid, in_specs=in_specs, out_specs=out_specs, out_shape=out_shape,
        scratch_shapes=list(scratch),
        compiler_params=pltpu.CompilerParams(
            dimension_semantics=("arbitrary",) * len(grid), vmem_limit_bytes=VMEM_LIMIT),
    )


def _dot(a, b):
    return jnp.dot(a, b, preferred_element_type=F32)


def _dot_nt(a, b):
    return lax.dot_general(a, b, (((1,), (1,)), ((), ())), preferred_element_type=F32)


def _dot_tn(a, b):
    return lax.dot_general(a, b, (((0,), (0,)), ((), ())), preferred_element_type=F32)


def _rms(x, g):
    r = lax.rsqrt(jnp.mean(x * x, axis=-1, keepdims=True) + EPS)
    return (x * r) * g


def _rms_bwd(x, g, dy):
    r = lax.rsqrt(jnp.mean(x * x, axis=-1, keepdims=True) + EPS)
    xh = x * r
    dyg = dy * g
    dx = r * (dyg - xh * jnp.mean(dyg * xh, axis=-1, keepdims=True))
    return dx, jnp.sum(dy * xh, axis=0, keepdims=True)


def _sigmoid(x):
    return 1.0 / (1.0 + jnp.exp(-x))


def _log1p(e):
    u = 1.0 + e
    d = u - 1.0
    return jnp.where(d == 0.0, e, jnp.log(u) * (e / jnp.where(d == 0.0, 1.0, d)))


def _softplus(x):
    return jnp.maximum(x, 0.0) + _log1p(jnp.exp(-jnp.abs(x)))


def _expm1(x):
    u = jnp.exp(x)
    lu = jnp.log(u)
    safe = jnp.where((lu == 0.0) | (u == 0.0), 1.0, lu)
    return jnp.where(u == 1.0, x, jnp.where(u == 0.0, -1.0, (u - 1.0) * x / safe))


_GELU_K = 0.7978845608028654


def _gelu(x):
    return 0.5 * x * (1.0 + jnp.tanh(_GELU_K * (x + 0.044715 * x * x * x)))


def _gelu_grad(x):
    t = jnp.tanh(_GELU_K * (x + 0.044715 * x * x * x))
    return 0.5 * (1.0 + t) + 0.5 * x * (1.0 - t * t) * _GELU_K * (1.0 + 3.0 * 0.044715 * x * x)


def _rope(x, c, sa, sb, half):
    return x * c + pltpu.roll(x, LANES - half, 1) * sa + pltpu.roll(x, half, 1) * sb


def _rope_t(d, c, sa, sb, half):
    return d * c - pltpu.roll(d, LANES - half, 1) * sa - pltpu.roll(d, half, 1) * sb


def _as_cols(a):
    return a if isinstance(a, tuple) else (a, 0, a.shape[1])


def mm_nn(name, a_list, b_list, *, res=None, act=None, norm_g=None, emit_norm=False,
          out_dtype=F32, tm=256, tn=512):
    a_list = [_as_cols(a) for a in a_list]
    M, N = a_list[0][0].shape[0], b_list[0].shape[1]
    tm, tn = min(tm, M), min(tn, N)
    na = len(a_list)

    def body(*refs):
        a_refs, b_refs = refs[:na], refs[na:2 * na]
        k = 2 * na
        g_ref = res_ref = hn_ref = None
        if norm_g is not None:
            g_ref = refs[k]
            k += 1
        if res is not None:
            res_ref = refs[k]
            k += 1
        o_ref = refs[k]
        if emit_norm:
            hn_ref = refs[k + 1]
        acc = None
        for a_ref, b_ref in zip(a_refs, b_refs):
            a = a_ref[...]
            if g_ref is not None:
                a = _rms(a.astype(F32), g_ref[...])
                if hn_ref is not None:
                    hn_ref[...] = a.astype(hn_ref.dtype)
            if act == "relu2":
                a = jnp.maximum(a, 0.0)
                a = a * a
            d = _dot(_mx(a), _mx(b_ref[...]))
            acc = d if acc is None else acc + d
        if res_ref is not None:
            acc = acc + res_ref[...]
        o_ref[...] = acc.astype(o_ref.dtype)

    in_specs = [pl.BlockSpec((tm, w), functools.partial(lambda i, j, cb: (i, cb), cb=cb)) for (_, cb, w) in a_list]
    in_specs += [pl.BlockSpec((b.shape[0], tn), lambda i, j: (0, j)) for b in b_list]
    args = [a for (a, _, _) in a_list] + list(b_list)
    if norm_g is not None:
        in_specs.append(pl.BlockSpec((1, norm_g.shape[1]), lambda i, j: (0, 0)))
        args.append(norm_g)
    if res is not None:
        in_specs.append(pl.BlockSpec((tm, tn), lambda i, j: (i, j)))
        args.append(res)
    out_specs = [pl.BlockSpec((tm, tn), lambda i, j: (i, j))]
    out_shape = [jax.ShapeDtypeStruct((M, N), out_dtype)]
    if emit_norm:
        K = a_list[0][2]
        out_specs.append(pl.BlockSpec((tm, K), lambda i, j: (i, 0)))
        out_shape.append(jax.ShapeDtypeStruct((M, K), MXU_DTYPE))
    out = _call(body, name=name, grid=(M // tm, N // tn), in_specs=in_specs, out_specs=out_specs,
                out_shape=out_shape)(*args)
    return out if emit_norm else out[0]


def mm_nt(name, a_list, b_list, *, relu2_of=None, out_dtype=F32, tm=256, tn=512):
    a_list = [_as_cols(a) for a in a_list]
    b_list = [_as_cols(b) for b in b_list]
    M, N = a_list[0][0].shape[0], b_list[0][0].shape[0]
    tm, tn = min(tm, M), min(tn, N)
    na = len(a_list)

    def body(*refs):
        a_refs, b_refs = refs[:na], refs[na:2 * na]
        u_ref = refs[2 * na] if relu2_of is not None else None
        o_ref = refs[-1]
        acc = None
        for a_ref, b_ref in zip(a_refs, b_refs):
            d = _dot_nt(_mx(a_ref[...]), _mx(b_ref[...]))
            acc = d if acc is None else acc + d
        if u_ref is not None:
            acc = acc * (2.0 * jnp.maximum(u_ref[...], 0.0))
        o_ref[...] = acc.astype(o_ref.dtype)

    in_specs = [pl.BlockSpec((tm, w), functools.partial(lambda i, j, cb: (i, cb), cb=cb)) for (_, cb, w) in a_list]
    in_specs += [pl.BlockSpec((tn, w), functools.partial(lambda i, j, cb: (j, cb), cb=cb)) for (_, cb, w) in b_list]
    args = [a for (a, _, _) in a_list] + [b for (b, _, _) in b_list]
    if relu2_of is not None:
        in_specs.append(pl.BlockSpec((tm, tn), lambda i, j: (i, j)))
        args.append(relu2_of)
    return _call(body, name=name, grid=(M // tm, N // tn), in_specs=in_specs,
                 out_specs=pl.BlockSpec((tm, tn), lambda i, j: (i, j)),
                 out_shape=jax.ShapeDtypeStruct((M, N), out_dtype))(*args)


def mm_tn(name, a, b, *, act=None, tm=512, tn=512, tk=512):
    a, acb, Ma = _as_cols(a)
    b, bcb, Nb = _as_cols(b)
    S = a.shape[0]
    tm, tn, tk = min(tm, Ma), min(tn, Nb), min(tk, S)
    a0, b0 = acb * (Ma // tm), bcb * (Nb // tn)

    def body(a_ref, b_ref, o_ref):
        @pl.when(pl.program_id(2) == 0)
        def _():
            o_ref[...] = jnp.zeros_like(o_ref)

        av = a_ref[...]
        if act == "relu2":
            av = jnp.maximum(av, 0.0)
            av = av * av
        o_ref[...] += _dot_tn(_mx(av), _mx(b_ref[...]))

    return _call(body, name=name, grid=(Ma // tm, Nb // tn, S // tk),
                 in_specs=[pl.BlockSpec((tk, tm), lambda i, j, k: (k, a0 + i)),
                           pl.BlockSpec((tk, tn), lambda i, j, k: (k, b0 + j))],
                 out_specs=pl.BlockSpec((tm, tn), lambda i, j, k: (i, j)),
                 out_shape=jax.ShapeDtypeStruct((Ma, Nb), F32))(a, b)


def rms_bwd(name, x, g, dh, dres, T=256):
    S, D = x.shape
    T = min(T, S)

    def body(x_ref, g_ref, dh_ref, dres_ref, dx_ref, dg_ref):
        @pl.when(pl.program_id(0) == 0)
        def _():
            dg_ref[...] = jnp.zeros_like(dg_ref)

        dx, dg = _rms_bwd(x_ref[...], g_ref[...], dh_ref[...])
        dx_ref[...] = dres_ref[...] + dx
        dg_ref[...] += dg

    row = pl.BlockSpec((T, D), lambda i: (i, 0))
    vec = pl.BlockSpec((1, D), lambda i: (0, 0))
    return _call(body, name=name, grid=(S // T,), in_specs=[row, vec, row, row], out_specs=[row, vec],
                 out_shape=[jax.ShapeDtypeStruct((S, D), F32), jax.ShapeDtypeStruct((1, D), F32)])(x, g, dh, dres)


def loss_head(x, g, target, T=256):
    S, D = x.shape
    T = min(T, S)

    def body(x_ref, g_ref, t_ref, loss_ref, dx_ref, dg_ref):
        @pl.when(pl.program_id(0) == 0)
        def _():
            dg_ref[...] = jnp.zeros_like(dg_ref)
            loss_ref[...] = jnp.zeros_like(loss_ref)

        x = x_ref[...]
        err = _rms(x, g_ref[...]) - t_ref[...]
        loss_ref[...] += 0.5 * jnp.sum(jnp.sum(err * err, axis=-1, keepdims=True) / D, axis=0, keepdims=True)
        dx, dg = _rms_bwd(x, g_ref[...], err / D)
        dx_ref[...] = dx
        dg_ref[...] += dg

    row = pl.BlockSpec((T, D), lambda i: (i, 0))
    vec = pl.BlockSpec((1, D), lambda i: (0, 0))
    return _call(body, name="loss_head", grid=(S // T,), in_specs=[row, vec, row],
                 out_specs=[pl.BlockSpec((1, LANES), lambda i: (0, 0)), row, vec],
                 out_shape=[jax.ShapeDtypeStruct((1, LANES), F32), jax.ShapeDtypeStruct((S, D), F32),
                            jax.ShapeDtypeStruct((1, D), F32)])(x, g, target)


def rope_tables(S, lo, half):
    inv = ROPE_THETA ** (-jnp.arange(half, dtype=F32) / half)
    ang = jnp.arange(S, dtype=F32)[:, None] * inv[None, :]
    cos, sin = jnp.cos(ang), jnp.sin(ang)
    one = lambda n: jnp.ones((S, n), F32)
    zero = lambda n: jnp.zeros((S, n), F32)
    hi = LANES - lo - 2 * half
    c = jnp.concatenate([one(lo), cos, cos, one(hi)], axis=1)
    sa = jnp.concatenate([zero(lo), -sin, zero(half), zero(hi)], axis=1)
    sb = jnp.concatenate([zero(lo), zero(half), sin, zero(hi)], axis=1)
    return c, sa, sb


SWA_BLOCK = 128
SWA_HALF = 32


def swa_prep(z, tabs, T=256):
    S = z.shape[0]
    T = min(T, S)

    def body(z_ref, c_ref, sa_ref, sb_ref, o_ref):
        c, sa, sb = c_ref[...], sa_ref[...], sb_ref[...]
        for s in range(10):
            sl = slice(s * LANES, (s + 1) * LANES)
            y = _rope(z_ref[:, sl], c, sa, sb, SWA_HALF)
            if s < 8:
                y = y * 0.125
            o_ref[:, sl] = y.astype(o_ref.dtype)
        o_ref[:, 1280:1536] = z_ref[:, 1280:1536].astype(o_ref.dtype)

    tab = pl.BlockSpec((T, LANES), lambda i: (i, 0))
    return _call(body, name="swa_prep", grid=(S // T,),
                 in_specs=[pl.BlockSpec((T, 1536), lambda i: (i, 0)), tab, tab, tab],
                 out_specs=pl.BlockSpec((T, 1536), lambda i: (i, 0)),
                 out_shape=jax.ShapeDtypeStruct((S, 1536), MXU_DTYPE))(z, *tabs)


SWA_GROUP = 4


def _swa_valid(n, S):
    B = SWA_BLOCK
    i = lax.broadcasted_iota(jnp.int32, (SWA_GROUP * B, 3 * B), 0) & (B - 1)
    j = lax.broadcasted_iota(jnp.int32, (SWA_GROUP * B, 3 * B), 1)
    kpos = j + (n - 1) * B
    return (jnp.abs(j - B - i) <= B) & (kpos >= 0) & (kpos < S)


def _swa_sink(sink_ref, hk):
    B = SWA_BLOCK
    row = lax.broadcasted_iota(jnp.int32, (SWA_GROUP * B, 1), 0)
    sk = jnp.full((SWA_GROUP * B, 1), sink_ref[hk * SWA_GROUP], F32)
    for g in range(1, SWA_GROUP):
        sk = jnp.where(row >= g * B, sink_ref[hk * SWA_GROUP + g], sk)
    return sk


def swa_fwd(qkv, sink):
    S = qkv.shape[0]
    B = SWA_BLOCK
    nb = S // B

    def body(sink_ref, q_ref, kp_ref, kc_ref, kn_ref, vp_ref, vc_ref, vn_ref, o_ref, st_ref):
        n = pl.program_id(0)
        valid = _swa_valid(n, S)
        lane = lax.broadcasted_iota(jnp.int32, (B, LANES), 1)
        st = jnp.zeros((B, LANES), F32)
        for hk in range(2):
            sl = slice(hk * LANES, (hk + 1) * LANES)
            k3 = jnp.concatenate([kp_ref[:, sl], kc_ref[:, sl], kn_ref[:, sl]], axis=0)
            v3 = jnp.concatenate([vp_ref[:, sl], vc_ref[:, sl], vn_ref[:, sl]], axis=0)
            heads = [hk * SWA_GROUP + g for g in range(SWA_GROUP)]
            q4 = jnp.concatenate([q_ref[:, h * LANES:(h + 1) * LANES] for h in heads], axis=0)
            s = jnp.where(valid, _dot_nt(q4, k3), -jnp.inf)
            sk = _swa_sink(sink_ref, hk)
            m = jnp.maximum(jnp.max(s, axis=-1, keepdims=True), sk)
            p = jnp.exp(s - m)
            den = jnp.sum(p, axis=-1, keepdims=True) + jnp.exp(sk - m)
            p = p / den
            o4 = _dot(_mx(p), v3)
            lse4 = m + jnp.log(den)
            for g, h in enumerate(heads):
                rows = slice(g * B, (g + 1) * B)
                o_ref[:, h * LANES:(h + 1) * LANES] = o4[rows].astype(o_ref.dtype)
                st = jnp.where(lane == h, lse4[rows], st)
        st_ref[...] = st

    kv = lambda cb, d: pl.BlockSpec((B, 2 * LANES), lambda n: (jnp.clip(n + d, 0, nb - 1), cb))
    return _call(body, name="swa_fwd", grid=(nb,),
                 in_specs=[pl.BlockSpec(memory_space=pltpu.SMEM),
                           pl.BlockSpec((B, 1024), lambda n: (n, 0)),
                           kv(4, -1), kv(4, 0), kv(4, 1), kv(5, -1), kv(5, 0), kv(5, 1)],
                 out_specs=[pl.BlockSpec((B, 1024), lambda n: (n, 0)), pl.BlockSpec((B, LANES), lambda n: (n, 0))],
                 out_shape=[jax.ShapeDtypeStruct((S, 1024), MXU_DTYPE), jax.ShapeDtypeStruct((S, LANES), F32)],
                 )(sink, qkv, qkv, qkv, qkv, qkv, qkv, qkv)


def swa_bwd(qkv, sink, lse, dycat, tabs):
    S = qkv.shape[0]
    B = SWA_BLOCK
    nb = S // B

    def body(sink_ref, q_ref, kp_ref, kc_ref, kn_ref, vp_ref, vc_ref, vn_ref, do_ref, st_ref,
             cq_ref, saq_ref, sbq_ref, ck_ref, sak_ref, sbk_ref,
             dq_ref, dk_ref, dv_ref, dsink_ref, dk_acc, dv_acc):
        n = pl.program_id(0)

        @pl.when(n == 0)
        def _():
            dk_acc[...] = jnp.zeros_like(dk_acc)
            dv_acc[...] = jnp.zeros_like(dv_acc)
            dsink_ref[...] = jnp.zeros_like(dsink_ref)

        @pl.when(n < nb)
        def _():
            valid = _swa_valid(n, S)
            lane = lax.broadcasted_iota(jnp.int32, (B, LANES), 1)
            lane1 = lax.broadcasted_iota(jnp.int32, (1, LANES), 1)
            st = st_ref[...]
            cq, saq, sbq = cq_ref[...], saq_ref[...], sbq_ref[...]
            dsink = jnp.zeros((1, LANES), F32)
            for hk in range(2):
                sl = slice(hk * LANES, (hk + 1) * LANES)
                k3 = jnp.concatenate([kp_ref[:, sl], kc_ref[:, sl], kn_ref[:, sl]], axis=0)
                v3 = jnp.concatenate([vp_ref[:, sl], vc_ref[:, sl], vn_ref[:, sl]], axis=0)
                heads = [hk * SWA_GROUP + g for g in range(SWA_GROUP)]
                q4 = jnp.concatenate([q_ref[:, h * LANES:(h + 1) * LANES] for h in heads], axis=0)
                do4 = jnp.concatenate([_mx(do_ref[:, h * LANES:(h + 1) * LANES]) for h in heads], axis=0)
                lse4 = jnp.concatenate([jnp.sum(jnp.where(lane == h, st, 0.0), axis=-1, keepdims=True) for h in heads],
                                       axis=0)
                p = jnp.where(valid, jnp.exp(_dot_nt(q4, k3) - lse4), 0.0)
                dp = _dot_nt(do4, v3)
                dsum = jnp.sum(p * dp, axis=-1, keepdims=True)
                ds = _mx(p * (dp - dsum))
                dq4 = _dot(ds, k3) * 0.125
                dk_acc[:, sl] += _dot_tn(ds, q4)
                dv_acc[:, sl] += _dot_tn(_mx(p), do4)
                dsk4 = jnp.exp(_swa_sink(sink_ref, hk) - lse4) * dsum
                for g, h in enumerate(heads):
                    rows = slice(g * B, (g + 1) * B)
                    dq_ref[:, h * LANES:(h + 1) * LANES] = _rope_t(dq4[rows], cq, saq, sbq, SWA_HALF)
                    dsink = jnp.where(lane1 == h, -jnp.sum(dsk4[rows], axis=0, keepdims=True), dsink)
            dsink_ref[...] += dsink

        ck, sak, sbk = ck_ref[...], sak_ref[...], sbk_ref[...]
        for hk in range(2):
            sl = slice(hk * LANES, (hk + 1) * LANES)
            dk_ref[:, sl] = _rope_t(dk_acc[0:B, sl], ck, sak, sbk, SWA_HALF)
        dv_ref[...] = dv_acc[0:B, :]
        for acc in (dk_acc, dv_acc):
            acc[0:B, :] = acc[B:2 * B, :]
            acc[B:2 * B, :] = acc[2 * B:3 * B, :]
            acc[2 * B:3 * B, :] = jnp.zeros((B, 2 * LANES), F32)

    qn = lambda n: jnp.minimum(n, nb - 1)
    kv = lambda cb, d: pl.BlockSpec((B, 2 * LANES), lambda n: (jnp.clip(qn(n) + d, 0, nb - 1), cb))
    qrow = lambda w: pl.BlockSpec((B, w), lambda n: (qn(n), 0))
    krow = lambda w: pl.BlockSpec((B, w), lambda n: (jnp.maximum(n - 1, 0), 0))
    return _call(body, name="swa_bwd", grid=(nb + 1,),
                 in_specs=[pl.BlockSpec(memory_space=pltpu.SMEM), qrow(1024),
                           kv(4, -1), kv(4, 0), kv(4, 1), kv(5, -1), kv(5, 0), kv(5, 1),
                           qrow(1024), qrow(LANES),
                           qrow(LANES), qrow(LANES), qrow(LANES), krow(LANES), krow(LANES), krow(LANES)],
                 out_specs=[qrow(1024), krow(2 * LANES), krow(2 * LANES), pl.BlockSpec((1, LANES), lambda n: (0, 0))],
                 out_shape=[jax.ShapeDtypeStruct((S, 1024), F32), jax.ShapeDtypeStruct((S, 2 * LANES), F32),
                            jax.ShapeDtypeStruct((S, 2 * LANES), F32), jax.ShapeDtypeStruct((1, LANES), F32)],
                 scratch=[pltpu.VMEM((3 * B, 2 * LANES), F32), pltpu.VMEM((3 * B, 2 * LANES), F32)],
                 )(sink, qkv, qkv, qkv, qkv, qkv, qkv, qkv, dycat, lse, *tabs, *tabs)


def _halo_specs(T, S, w, cb):
    r = T // HALO
    last = S // HALO - 1
    return [pl.BlockSpec((HALO, w), lambda i: (jnp.maximum(i * r - 1, 0), cb)),
            pl.BlockSpec((T, w), lambda i: (i, cb)),
            pl.BlockSpec((HALO, w), lambda i: (jnp.minimum((i + 1) * r, last), cb))]


def _fill_ext(ext, prev_ref, cur_ref, next_ref, i, nt, T):
    ext[0:HALO, :] = jnp.where(i > 0, prev_ref[...], 0.0).astype(F32)
    ext[HALO:HALO + T, :] = cur_ref[...].astype(F32)
    ext[HALO + T:2 * HALO + T, :] = jnp.where(i < nt - 1, next_ref[...], 0.0).astype(F32)


def _pool_cnt(t, half, S):
    return (jnp.clip(t + half, 0, S) - jnp.clip(t - half, 0, S)).astype(F32)


def pool_fwd(z, w_pool, scale, T=256):
    S = z.shape[0]
    T = min(T, S)
    nt = S // T

    def body(up_ref, uc_ref, un_ref, w_ref, sc_ref, o_ref, ext):
        i = pl.program_id(0)
        _fill_ext(ext, up_ref, uc_ref, un_ref, i, nt, T)
        t = i * T + lax.broadcasted_iota(jnp.int32, (T, 1), 0)
        for g, win in enumerate(POOL_WINDOWS):
            half = win // 2
            sl = slice(g * LANES, (g + 1) * LANES)
            acc = ext[pl.ds(HALO - half, T), sl]
            for off in range(-half + 1, half):
                acc = acc + ext[pl.ds(HALO + off, T), sl]
            d = acc / _pool_cnt(t, half, S) - ext[pl.ds(HALO, T), sl]
            o_ref[:, sl] = (_dot(_mx(d), w_ref[g]) * sc_ref[:, sl]).astype(o_ref.dtype)

    return _call(body, name="pool_fwd", grid=(nt,),
                 in_specs=_halo_specs(T, S, 512, 3) + [pl.BlockSpec((4, LANES, LANES), lambda i: (0, 0, 0)),
                                                      pl.BlockSpec((1, 512), lambda i: (0, 0))],
                 out_specs=pl.BlockSpec((T, 512), lambda i: (i, 0)),
                 out_shape=jax.ShapeDtypeStruct((S, 512), MXU_DTYPE),
                 scratch=[pltpu.VMEM((T + 2 * HALO, 512), F32)])(z, z, z, w_pool, scale)


def pool_bwd(z, dycat, w_pool, scale, T=256):
    S = z.shape[0]
    T = min(T, S)
    nt = S // T
    TE = T + 2 * HALO

    def body(up_ref, uc_ref, un_ref, yp_ref, yc_ref, yn_ref, w_ref, sc_ref, du_ref, dw_ref, dsc_ref, extu, exty, exte):
        i = pl.program_id(0)

        @pl.when(i == 0)
        def _():
            dw_ref[...] = jnp.zeros_like(dw_ref)
            dsc_ref[...] = jnp.zeros_like(dsc_ref)

        _fill_ext(extu, up_ref, uc_ref, un_ref, i, nt, T)
        _fill_ext(exty, yp_ref, yc_ref, yn_ref, i, nt, T)
        t = i * T + lax.broadcasted_iota(jnp.int32, (T, 1), 0)
        te = i * T - HALO + lax.broadcasted_iota(jnp.int32, (TE, 1), 0)
        for g, win in enumerate(POOL_WINDOWS):
            half = win // 2
            sl = slice(g * LANES, (g + 1) * LANES)
            w = w_ref[g]
            acc = extu[pl.ds(HALO - half, T), sl]
            for off in range(-half + 1, half):
                acc = acc + extu[pl.ds(HALO + off, T), sl]
            d = _mx(acc / _pool_cnt(t, half, S) - extu[pl.ds(HALO, T), sl])
            dy = exty[pl.ds(HALO, T), sl]
            dsc_ref[:, sl] += jnp.sum(dy * _dot(d, w), axis=0, keepdims=True)
            dw_ref[g] += _dot_tn(d, _mx(dy * sc_ref[:, sl]))
            dd = _dot_nt(_mx(exty[:, sl] * sc_ref[:, sl]), w)
            exte[:, sl] = dd / jnp.maximum(_pool_cnt(te, half, S), 1.0)
            acc = exte[pl.ds(HALO - half + 1, T), sl]
            for off in range(-half + 2, half + 1):
                acc = acc + exte[pl.ds(HALO + off, T), sl]
            du_ref[:, sl] = acc - dd[HALO:HALO + T, :]

    return _call(body, name="pool_bwd", grid=(nt,),
                 in_specs=_halo_specs(T, S, 512, 3) + _halo_specs(T, S, 512, 2)
                 + [pl.BlockSpec((4, LANES, LANES), lambda i: (0, 0, 0)), pl.BlockSpec((1, 512), lambda i: (0, 0))],
                 out_specs=[pl.BlockSpec((T, 512), lambda i: (i, 0)), pl.BlockSpec((4, LANES, LANES), lambda i: (0, 0, 0)),
                            pl.BlockSpec((1, 512), lambda i: (0, 0))],
                 out_shape=[jax.ShapeDtypeStruct((S, 512), F32), jax.ShapeDtypeStruct((4, LANES, LANES), F32),
                            jax.ShapeDtypeStruct((1, 512), F32)],
                 scratch=[pltpu.VMEM((TE, 512), F32)] * 3)(z, z, z, dycat, dycat, dycat, w_pool, scale)


MLA_HALF = 16
MLA_V_ONE = 64


def mla_prep(z, g_cq, g_ckv, w_uq, w_k, w_v, tabs_q, tabs_k, T=256):
    S = z.shape[0]
    T = min(T, S)

    def body(z_ref, gq_ref, gkv_ref, wq_ref, wk_ref, wv_ref, cq_ref, saq_ref, sbq_ref, ck_ref, sak_ref, sbk_ref,
             q_ref, k_ref, v_ref, nq_ref, nkv_ref):
        nq = _mx(_rms(z_ref[:, 0:256], gq_ref[...]))
        nkv = _mx(_rms(z_ref[:, 256:384], gkv_ref[...]))
        nq_ref[...] = nq
        nkv_ref[...] = nkv
        q = _dot(nq, wq_ref[...])
        kn = _dot(nkv, wk_ref[...])
        lane = lax.broadcasted_iota(jnp.int32, (T, 1024), 1)
        v_ref[...] = jnp.where(lane % LANES == MLA_V_ONE, 1.0, _dot(nkv, wv_ref[...])).astype(v_ref.dtype)
        kr = pltpu.roll(_rope(z_ref[:, 384:512], ck_ref[...], sak_ref[...], sbk_ref[...], MLA_HALF), 64, 1)
        cq, saq, sbq = cq_ref[...], saq_ref[...], sbq_ref[...]
        for h in range(8):
            hs = slice(h * LANES, (h + 1) * LANES)
            q_ref[:, hs] = (_rope(q[:, hs], cq, saq, sbq, MLA_HALF) * MLA_QSCALE).astype(q_ref.dtype)
            k_ref[:, hs] = (kn[:, hs] + kr).astype(k_ref.dtype)

    tab = pl.BlockSpec((T, LANES), lambda i: (i, 0))
    full = lambda a: pl.BlockSpec(a.shape, lambda i: (0, 0))
    row = lambda w: pl.BlockSpec((T, w), lambda i: (i, 0))
    sd = lambda w: jax.ShapeDtypeStruct((S, w), MXU_DTYPE)
    return _call(body, name="mla_prep", grid=(S // T,),
                 in_specs=[row(512), full(g_cq), full(g_ckv), full(w_uq), full(w_k), full(w_v)] + [tab] * 6,
                 out_specs=[row(1024), row(1024), row(1024), row(256), row(128)],
                 out_shape=[sd(1024), sd(1024), sd(1024), sd(256), sd(128)],
                 )(z, g_cq, g_ckv, w_uq, w_k, w_v, *tabs_q, *tabs_k)


def _col_to_row(c):
    return jnp.transpose(jnp.broadcast_to(c, (c.shape[0], LANES)))[0:1, :]


def mla_vt(v, TK=512):
    S = v.shape[0]
    TK = min(TK, S // 2)
    return v.reshape(S // TK, TK, 8, LANES).transpose(2, 0, 3, 1)


def mla_dq(dqt):
    _, nq, _, TQ = dqt.shape
    return dqt.transpose(1, 3, 0, 2).reshape(nq * TQ, 1024)


def mla_fwd(q, k, vt, TQ=512):
    S = q.shape[0]
    nk, TK = vt.shape[1], vt.shape[3]
    TQ = min(TQ, S)
    assert nk % 2 == 0

    def body(q_ref, k_ref, vt_ref, o_ref, lse_ref, m_s, acc_s, s_a, s_b):
        m_s[...] = jnp.full_like(m_s, -jnp.inf)
        acc_s[...] = jnp.zeros_like(acc_s)
        q = q_ref[...]

        def scores(c):
            return _dot_nt(k_ref[pl.ds(pl.multiple_of(c * TK, TK), TK), :], q)

        def softmax_pv(s_ref, c):
            s = s_ref[...]
            m_old = m_s[...]
            m_new = jnp.maximum(m_old, jnp.max(s, axis=0, keepdims=True))
            alpha = jnp.exp2(m_old[0:1, :] - m_new[0:1, :])
            p = jnp.exp2(s - m_new[0:1, :])
            acc_s[...] = alpha * acc_s[...] + _dot(vt_ref[0, c], _mx(p))
            m_s[...] = m_new

        s_a[...] = scores(0)

        def pair(jj, carry):
            c = 2 * jj
            s_b[...] = scores(c + 1)
            softmax_pv(s_a, c)
            s_a[...] = scores(jnp.minimum(c + 2, nk - 1))
            softmax_pv(s_b, c + 1)
            return carry

        lax.fori_loop(0, nk // 2, pair, 0)
        acc = acc_s[...]
        den = acc[MLA_V_ONE:MLA_V_ONE + 1, :]
        sub = lax.broadcasted_iota(jnp.int32, (LANES, TQ), 0)
        o_ref[...] = jnp.transpose(jnp.where(sub < MLA_V_ONE, acc / den, 0.0))
        lse_ref[0] = m_s[0:1, :] + jnp.log(den) * LOG2E

    qs = pl.BlockSpec((TQ, LANES), lambda h, i: (i, h))
    return _call(body, name="mla_fwd", grid=(8, S // TQ),
                 in_specs=[qs, pl.BlockSpec((S, LANES), lambda h, i: (0, h)),
                           pl.BlockSpec((1, nk, LANES, TK), lambda h, i: (h, 0, 0, 0))],
                 out_specs=[qs, pl.BlockSpec((1, 1, TQ), lambda h, i: (h, 0, i))],
                 out_shape=[jax.ShapeDtypeStruct((S, 1024), F32), jax.ShapeDtypeStruct((8, 1, S), F32)],
                 scratch=[pltpu.VMEM((8, TQ), F32), pltpu.VMEM((LANES, TQ), F32),
                          pltpu.VMEM((TK, TQ), F32), pltpu.VMEM((TK, TQ), F32)],
                 )(q, k, vt)


def mla_delta(o, dycat, TQ=512):
    S = o.shape[0]
    TQ = min(TQ, S)

    def body(o_ref, do_ref, d_ref, dob_ref):
        for h in range(8):
            hs = slice(h * LANES, (h + 1) * LANES)
            do = do_ref[:, hs]
            d_ref[h] = _col_to_row(jnp.sum(o_ref[:, hs] * do, axis=-1, keepdims=True))
            dob_ref[:, hs] = do.astype(dob_ref.dtype)

    qs = pl.BlockSpec((TQ, 1024), lambda i: (i, 0))
    return _call(body, name="mla_delta", grid=(S // TQ,), in_specs=[qs, qs],
                 out_specs=[pl.BlockSpec((8, 1, TQ), lambda i: (0, 0, i)), qs],
                 out_shape=[jax.ShapeDtypeStruct((8, 1, S), F32), jax.ShapeDtypeStruct((S, 1024), MXU_DTYPE)])(o, dycat)


def mla_bwd(q, k, v, do, lse, delta, TQ=512, TK=512):
    S = q.shape[0]
    TQ, TK = min(TQ, S // 2), min(TK, S)
    nq, nk = S // TQ, S // TK
    assert nq % 2 == 0
    lse = lse.reshape(8, nq, 1, TQ)
    delta = delta.reshape(8, nq, 1, TQ)

    def body(q_ref, do_ref, k_ref, v_ref, lse_ref, d_ref, dqt_ref, dk_ref, dv_ref, dk_s, dv_s, s_a, p_a, s_b, p_b):
        @pl.when(pl.program_id(1) == 0)
        def _():
            dqt_ref[...] = jnp.zeros_like(dqt_ref)

        dk_s[...] = jnp.zeros_like(dk_s)
        dv_s[...] = jnp.zeros_like(dv_s)
        kk, vv = k_ref[...], v_ref[...]
        kt = _mx(jnp.transpose(kk.astype(F32)))

        def rows(c):
            return pl.ds(pl.multiple_of(c * TQ, TQ), TQ)

        def products(c, s_ref, p_ref):
            s_ref[...] = _dot_nt(kk, q_ref[rows(c), :])
            p_ref[...] = _dot_nt(vv, do_ref[rows(c), :])

        def consume(c, s_ref, p_ref):
            qc, doc = q_ref[rows(c), :], do_ref[rows(c), :]
            pt = jnp.exp2(s_ref[...] - lse_ref[0, c])
            dv_s[...] += _dot(_mx(pt), doc)
            ds = _mx(pt * (p_ref[...] - d_ref[0, c]))
            dk_s[...] += _dot(ds, qc)
            dqt_ref[0, c] += _dot(kt, ds)

        products(0, s_a, p_a)

        def pair(jj, carry):
            c = 2 * jj
            products(c + 1, s_b, p_b)
            consume(c, s_a, p_a)
            products(jnp.minimum(c + 2, nq - 1), s_a, p_a)
            consume(c + 1, s_b, p_b)
            return carry

        lax.fori_loop(0, nq // 2, pair, 0)
        dk_ref[...] = dk_s[...] * (1.0 / LOG2E)
        dv_ref[...] = dv_s[...]

    full = pl.BlockSpec((S, LANES), lambda h, j: (0, h))
    ks = pl.BlockSpec((TK, LANES), lambda h, j: (j, h))
    st = pl.BlockSpec((1, nq, 1, TQ), lambda h, j: (h, 0, 0, 0))
    sd = jax.ShapeDtypeStruct((S, 1024), F32)
    return _call(body, name="mla_bwd", grid=(8, nk), in_specs=[full, full, ks, ks, st, st],
                 out_specs=[pl.BlockSpec((1, nq, LANES, TQ), lambda h, j: (h, 0, 0, 0)), ks, ks],
                 out_shape=[jax.ShapeDtypeStruct((8, nq, LANES, TQ), F32), sd, sd],
                 scratch=[pltpu.VMEM((TK, LANES), F32), pltpu.VMEM((TK, LANES), F32)] + [pltpu.VMEM((TK, TQ), F32)] * 4,
                 )(q, do, k, v, lse, delta)


def mla_prep_bwd(z, g_cq, g_ckv, w_uq, w_k, w_v, dq, dk, dv, tabs_q, tabs_k, T=256):
    S = z.shape[0]
    T = min(T, S)

    def body(z_ref, gq_ref, gkv_ref, wq_ref, wk_ref, wv_ref, dq_ref, dk_ref, dv_ref,
             cq_ref, saq_ref, sbq_ref, ck_ref, sak_ref, sbk_ref, dz_ref, dqp_ref, dgq_ref, dgkv_ref):
        @pl.when(pl.program_id(0) == 0)
        def _():
            dgq_ref[...] = jnp.zeros_like(dgq_ref)
            dgkv_ref[...] = jnp.zeros_like(dgkv_ref)

        cq, saq, sbq = cq_ref[...], saq_ref[...], sbq_ref[...]
        dkr = jnp.zeros((T, LANES), F32)
        for h in range(8):
            hs = slice(h * LANES, (h + 1) * LANES)
            dqp_ref[:, hs] = _rope_t(dq_ref[:, hs] * MLA_SCALE, cq, saq, sbq, MLA_HALF).astype(dqp_ref.dtype)
            dkr = dkr + dk_ref[:, hs]
        lane = lax.broadcasted_iota(jnp.int32, (T, LANES), 1)
        dkr = jnp.where(lane < 2 * MLA_HALF, pltpu.roll(dkr, 64, 1), 0.0)
        dz_ref[:, 384:512] = _rope_t(dkr, ck_ref[...], sak_ref[...], sbk_ref[...], MLA_HALF)
        dnq = _dot_nt(dqp_ref[...], wq_ref[...])
        dx, dg = _rms_bwd(z_ref[:, 0:256], gq_ref[...], dnq)
        dz_ref[:, 0:256] = dx
        dgq_ref[...] += dg
        dnkv = _dot_nt(_mx(dk_ref[...]), wk_ref[...]) + _dot_nt(_mx(dv_ref[...]), wv_ref[...])
        dx, dg = _rms_bwd(z_ref[:, 256:384], gkv_ref[...], dnkv)
        dz_ref[:, 256:384] = dx
        dgkv_ref[...] += dg

    tab = pl.BlockSpec((T, LANES), lambda i: (i, 0))
    full = lambda a: pl.BlockSpec(a.shape, lambda i: (0, 0))
    row = lambda w: pl.BlockSpec((T, w), lambda i: (i, 0))
    return _call(body, name="mla_prep_bwd", grid=(S // T,),
                 in_specs=[row(512), full(g_cq), full(g_ckv), full(w_uq), full(w_k), full(w_v),
                           row(1024), row(1024), row(1024)] + [tab] * 6,
                 out_specs=[row(512), row(1024), full(g_cq), full(g_ckv)],
                 out_shape=[jax.ShapeDtypeStruct((S, 512), F32), jax.ShapeDtypeStruct((S, 1024), MXU_DTYPE),
                            jax.ShapeDtypeStruct(g_cq.shape, F32), jax.ShapeDtypeStruct(g_ckv.shape, F32)],
                 )(z, g_cq, g_ckv, w_uq, w_k, w_v, dq, dk, dv, *tabs_q, *tabs_k)


def _lru_gates(xc, w_ref, bias_ref, lam_ref):
    pre = _dot(_mx(xc), w_ref[...]) + bias_ref[...]
    out = []
    for d in range(2):
        r = _sigmoid(pre[:, d * 1024:d * 1024 + 512])
        ig = _sigmoid(pre[:, d * 1024 + 512:(d + 1) * 1024])
        log_a = -LRU_C * r * _softplus(-lam_ref[:, d * 512:(d + 1) * 512])
        out.append((r, ig, jnp.exp(log_a), jnp.sqrt(-_expm1(2.0 * log_a))))
    return out


def lru_pre(z, conv_w, conv_b, w_gate, b_gate, lam, T=256):
    S = z.shape[0]
    T = min(T, S)
    nt = S // T

    def body(xp_ref, xcur_ref, xn_ref, cw_ref, cb_ref, w_ref, bias_ref, lam_ref, xc_ref, a0_ref, b0_ref, a1_ref, b1_ref, ext):
        i = pl.program_id(0)
        _fill_ext(ext, xp_ref, xcur_ref, xn_ref, i, nt, T)
        xc = cb_ref[...] + cw_ref[0:1, :] * ext[pl.ds(HALO - 2, T), :]
        for j in range(1, 4):
            xc = xc + cw_ref[j:j + 1, :] * ext[pl.ds(HALO - 2 + j, T), :]
        xc_ref[...] = xc
        (_, i0, a0, m0), (_, i1, a1, m1) = _lru_gates(xc, w_ref, bias_ref, lam_ref)
        a0_ref[...] = a0
        b0_ref[...] = m0 * (i0 * xc)
        a1_ref[...] = a1
        b1_ref[...] = m1 * (i1 * xc)

    full = lambda a: pl.BlockSpec(a.shape, lambda i: (0, 0))
    row = pl.BlockSpec((T, 512), lambda i: (i, 0))
    sd = jax.ShapeDtypeStruct((S, 512), F32)
    return _call(body, name="lru_pre", grid=(nt,),
                 in_specs=_halo_specs(T, S, 512, 1) + [full(conv_w), full(conv_b), full(w_gate), full(b_gate), full(lam)],
                 out_specs=[row] * 5, out_shape=[sd] * 5,
                 scratch=[pltpu.VMEM((T + 2 * HALO, 512), F32)])(z, z, z, conv_w, conv_b, w_gate, b_gate, lam)


def lru_scan(name, af, bf, ar, br, *, adjoint, T=256):
    S, W = af.shape
    T = min(T, S)
    nt = S // T
    nc = T // 8

    def body(af_ref, bf_ref, ar_ref, br_ref, hf_ref, hr_ref, cf, cr):
        @pl.when(pl.program_id(0) == 0)
        def _():
            cf[...] = jnp.zeros_like(cf)
            cr[...] = jnp.zeros_like(cr)

        row = lax.broadcasted_iota(jnp.int32, (8, W), 0)

        def step(a, b, carry):
            if adjoint:
                val = b + carry
                return val, a * val
            val = a * carry + b
            return val, val

        def chunk(c, carry):
            hf, hr = carry
            of = pl.multiple_of(c * 8, 8)
            orv = pl.multiple_of((nc - 1 - c) * 8, 8)
            a8, b8 = af_ref[pl.ds(of, 8), :], bf_ref[pl.ds(of, 8), :]
            ra8, rb8 = ar_ref[pl.ds(orv, 8), :], br_ref[pl.ds(orv, 8), :]
            outf = jnp.zeros((8, W), F32)
            outr = jnp.zeros((8, W), F32)
            for k in range(8):
                val, hf = step(a8[k:k + 1, :], b8[k:k + 1, :], hf)
                outf = jnp.where(row == k, val, outf)
                kr = 7 - k
                val, hr = step(ra8[kr:kr + 1, :], rb8[kr:kr + 1, :], hr)
                outr = jnp.where(row == kr, val, outr)
            hf_ref[pl.ds(of, 8), :] = outf
            hr_ref[pl.ds(orv, 8), :] = outr
            return hf, hr

        hf, hr = lax.fori_loop(0, nc, chunk, (cf[0:1, :], cr[0:1, :]))
        cf[0:1, :] = hf
        cr[0:1, :] = hr

    fw = pl.BlockSpec((T, W), lambda i: (i, 0))
    rv = pl.BlockSpec((T, W), lambda i: (nt - 1 - i, 0))
    sd = jax.ShapeDtypeStruct((S, W), F32)
    return _call(body, name=name, grid=(nt,), in_specs=[fw, fw, rv, rv], out_specs=[fw, rv], out_shape=[sd, sd],
                 scratch=[pltpu.VMEM((8, W), F32), pltpu.VMEM((8, W), F32)])(af, bf, ar, br)


def lru_gate(h0, h1, z, T=256):
    S = z.shape[0]
    T = min(T, S)

    def body(h0_ref, h1_ref, xg_ref, y_ref):
        y_ref[...] = ((h0_ref[...] + h1_ref[...]) * _gelu(xg_ref[...])).astype(y_ref.dtype)

    row = pl.BlockSpec((T, 512), lambda i: (i, 0))
    return _call(body, name="lru_gate", grid=(S // T,), in_specs=[row, row, pl.BlockSpec((T, 512), lambda i: (i, 2))],
                 out_specs=row, out_shape=jax.ShapeDtypeStruct((S, 512), MXU_DTYPE))(h0, h1, z)


def lru_gate_bwd(h0, h1, z, dycat, T=256):
    S = z.shape[0]
    T = min(T, S)

    def body(h0_ref, h1_ref, xg_ref, dy_ref, dxg_ref, dh_ref):
        xg, dy = xg_ref[...], dy_ref[...]
        dxg_ref[...] = dy * (h0_ref[...] + h1_ref[...]) * _gelu_grad(xg)
        dh_ref[...] = dy * _gelu(xg)

    row = pl.BlockSpec((T, 512), lambda i: (i, 0))
    col2 = pl.BlockSpec((T, 512), lambda i: (i, 2))
    sd = jax.ShapeDtypeStruct((S, 512), F32)
    return _call(body, name="lru_gate_bwd", grid=(S // T,), in_specs=[row, row, col2, col2],
                 out_specs=[row, row], out_shape=[sd, sd])(h0, h1, z, dycat)


def lru_bwd_point(xc, h0, h1, g0, g1, w_gate, b_gate, lam, T=256):
    S = xc.shape[0]
    T = min(T, S)
    nt = S // T

    def body(xc_ref, h0p_ref, h0_ref, h0n_ref, h1p_ref, h1_ref, h1n_ref, g0_ref, g1_ref, w_ref, bias_ref, lam_ref,
             dxc_ref, dpre_ref, dbias_ref, dlam_ref, ext0, ext1):
        i = pl.program_id(0)

        @pl.when(i == 0)
        def _():
            dbias_ref[...] = jnp.zeros_like(dbias_ref)
            dlam_ref[...] = jnp.zeros_like(dlam_ref)

        _fill_ext(ext0, h0p_ref, h0_ref, h0n_ref, i, nt, T)
        _fill_ext(ext1, h1p_ref, h1_ref, h1n_ref, i, nt, T)
        xc = xc_ref[...]
        gates = _lru_gates(xc, w_ref, bias_ref, lam_ref)
        hshift = (ext0[pl.ds(HALO - 1, T), :], ext1[pl.ds(HALO + 1, T), :])
        gs = (g0_ref[...], g1_ref[...])
        dxc = jnp.zeros((T, 512), F32)
        for d in range(2):
            r, ig, a, mult = gates[d]
            db = gs[d]
            da = db * hshift[d]
            dmult = db * (ig * xc)
            di = db * (mult * xc)
            dxc = dxc + db * (mult * ig)
            dloga = da * a - dmult * (a * a / mult)
            lam_d = lam_ref[:, d * 512:(d + 1) * 512]
            dr = dloga * (-LRU_C * _softplus(-lam_d))
            dsp = jnp.sum(dloga * (-LRU_C * r), axis=0, keepdims=True)
            dlam_ref[:, d * 512:(d + 1) * 512] += dsp * (-_sigmoid(-lam_d))
            dpre_ref[:, d * 1024:d * 1024 + 512] = (dr * (r * (1.0 - r))).astype(dpre_ref.dtype)
            dpre_ref[:, d * 1024 + 512:(d + 1) * 1024] = (di * (ig * (1.0 - ig))).astype(dpre_ref.dtype)
            dbias_ref[:, d * 1024:d * 1024 + 512] += jnp.sum(dr * (r * (1.0 - r)), axis=0, keepdims=True)
            dbias_ref[:, d * 1024 + 512:(d + 1) * 1024] += jnp.sum(di * (ig * (1.0 - ig)), axis=0, keepdims=True)
        dxc_ref[...] = dxc + _dot_nt(dpre_ref[...], w_ref[...])

    full = lambda a: pl.BlockSpec(a.shape, lambda i: (0, 0))
    row = pl.BlockSpec((T, 512), lambda i: (i, 0))
    return _call(body, name="lru_bwd_point", grid=(nt,),
                 in_specs=[row] + _halo_specs(T, S, 512, 0) + _halo_specs(T, S, 512, 0) + [row, row, full(w_gate), full(b_gate), full(lam)],
                 out_specs=[row, pl.BlockSpec((T, 2048), lambda i: (i, 0)), full(b_gate), full(lam)],
                 out_shape=[jax.ShapeDtypeStruct((S, 512), F32), jax.ShapeDtypeStruct((S, 2048), MXU_DTYPE),
                            jax.ShapeDtypeStruct(b_gate.shape, F32), jax.ShapeDtypeStruct(lam.shape, F32)],
                 scratch=[pltpu.VMEM((T + 2 * HALO, 512), F32)] * 2,
                 )(xc, h0, h0, h0, h1, h1, h1, g0, g1, w_gate, b_gate, lam)


def conv_bwd(z, dxc, conv_w, T=256):
    S = z.shape[0]
    T = min(T, S)
    nt = S // T

    def body(xp_ref, xcur_ref, xn_ref, dp_ref, dcur_ref, dn_ref, cw_ref, dx_ref, dw_ref, db_ref, extx, extd):
        i = pl.program_id(0)

        @pl.when(i == 0)
        def _():
            dw_ref[...] = jnp.zeros_like(dw_ref)
            db_ref[...] = jnp.zeros_like(db_ref)

        _fill_ext(extx, xp_ref, xcur_ref, xn_ref, i, nt, T)
        _fill_ext(extd, dp_ref, dcur_ref, dn_ref, i, nt, T)
        d = extd[pl.ds(HALO, T), :]
        dx = cw_ref[0:1, :] * extd[pl.ds(HALO + 2, T), :]
        for j in range(1, 4):
            dx = dx + cw_ref[j:j + 1, :] * extd[pl.ds(HALO + 2 - j, T), :]
        dx_ref[...] = dx
        for j in range(4):
            dw_ref[j:j + 1, :] += jnp.sum(d * extx[pl.ds(HALO - 2 + j, T), :], axis=0, keepdims=True)
        db_ref[...] += jnp.sum(d, axis=0, keepdims=True)

    full = lambda a: pl.BlockSpec(a.shape, lambda i: (0, 0))
    row = pl.BlockSpec((T, 512), lambda i: (i, 0))
    return _call(body, name="conv_bwd", grid=(nt,),
                 in_specs=_halo_specs(T, S, 512, 1) + _halo_specs(T, S, 512, 0) + [full(conv_w)],
                 out_specs=[row, full(conv_w), pl.BlockSpec((1, 512), lambda i: (0, 0))],
                 out_shape=[jax.ShapeDtypeStruct((S, 512), F32), jax.ShapeDtypeStruct(conv_w.shape, F32),
                            jax.ShapeDtypeStruct((1, 512), F32)],
                 scratch=[pltpu.VMEM((T + 2 * HALO, 512), F32)] * 2)(z, z, z, dxc, dxc, dxc, conv_w)


MESH_ID = pl.DeviceIdType.MESH
ANY = pl.BlockSpec(memory_space=pl.ANY)


def _place():
    x, y, c = lax.axis_index("x"), lax.axis_index("y"), lax.axis_index("c")
    chips = [(1 - x, y), (x, 1 - y), (1 - x, 1 - y)]
    return x, y, c, chips


def _slot(px, py, pc):
    return 4 * px + 2 * py + pc


def all_gather(arrays):
    n = len(arrays)

    def body(*refs):
        ins, outs = refs[:n], refs[n:2 * n]
        send, recv, loc = refs[2 * n:]
        x, y, c, chips = _place()
        me, sibling = (x, y, c), (x, y, 1 - c)

        def copy(a, k, block, to, src=None):
            slot = outs[a].at[_slot(*block)]
            return pltpu.make_async_remote_copy(src_ref=slot if src is None else src, dst_ref=slot,
                                                send_sem=send.at[a * 7 + k], recv_sem=recv.at[a * 7 + k],
                                                device_id=to, device_id_type=MESH_ID)

        local = [pltpu.make_async_copy(ins[a], outs[a].at[_slot(*me)], loc.at[a]) for a in range(n)]
        for cp in local:
            cp.start()
        first = []
        for a in range(n):
            first.append(copy(a, 0, me, sibling, src=ins[a]))
            first += [copy(a, 1 + j, me, (*chip, c), src=ins[a]) for j, chip in enumerate(chips)]
        for cp in first:
            cp.start()
        passed = []
        for a in range(n):
            for j, chip in enumerate(chips):
                copy(a, 1 + j, (*chip, c), me).wait_recv()
                cp = copy(a, 4 + j, (*chip, c), sibling)
                cp.start()
                passed.append(cp)
        for a in range(n):
            copy(a, 0, sibling, me).wait_recv()
            for j, chip in enumerate(chips):
                copy(a, 4 + j, (*chip, 1 - c), me).wait_recv()
        for cp in first + passed:
            cp.wait_send()
        for cp in local:
            cp.wait()

    return pl.pallas_call(
        body, name="all_gather", in_specs=[ANY] * n, out_specs=[ANY] * n,
        out_shape=[jax.ShapeDtypeStruct((NDEV,) + a.shape, a.dtype) for a in arrays],
        scratch_shapes=[pltpu.SemaphoreType.DMA((n * 7,)), pltpu.SemaphoreType.DMA((n * 7,)),
                        pltpu.SemaphoreType.DMA((n,))],
    )(*arrays)


def grad_pair(gs):
    def body(gs_ref, o_ref, send, recv):
        x, y, c, _ = _place()
        cps = [pltpu.make_async_remote_copy(src_ref=gs_ref.at[2 * q + 1], dst_ref=o_ref.at[q], send_sem=send.at[q],
                                            recv_sem=recv.at[q], device_id=(x, y, 1 - c), device_id_type=MESH_ID)
               for q in range(4)]
        for cp in cps:
            cp.start()
        for cp in cps:
            cp.wait_recv()
        for cp in cps:
            cp.wait_send()

    return pl.pallas_call(
        body, name="grad_pair", in_specs=[ANY], out_specs=ANY,
        out_shape=jax.ShapeDtypeStruct((4,) + gs.shape[1:], gs.dtype),
        scratch_shapes=[pltpu.SemaphoreType.DMA((4,)), pltpu.SemaphoreType.DMA((4,))],
    )(gs)


def pair_add(gs, got, T=128):
    _, R, C = gs.shape
    T = min(T, R)

    def body(g0, g1, g2, g3, got_ref, own_ref, out_ref):
        own_ref[...] = g0[0] + got_ref[0]
        for q, g in enumerate((g1, g2, g3)):
            out_ref[q] = (g[0] + got_ref[q + 1]).astype(out_ref.dtype)

    even = lambda q: pl.BlockSpec((1, T, C), lambda i: (2 * q, i, 0))
    return _call(body, name="pair_add", grid=(R // T,),
                 in_specs=[even(0), even(1), even(2), even(3), pl.BlockSpec((4, T, C), lambda i: (0, i, 0))],
                 out_specs=[pl.BlockSpec((T, C), lambda i: (i, 0)), pl.BlockSpec((3, T, C), lambda i: (0, i, 0))],
                 out_shape=[jax.ShapeDtypeStruct((R, C), F32), jax.ShapeDtypeStruct((3, R, C), MXU_DTYPE)],
                 )(gs, gs, gs, gs, got)


def grad_cross(part, gr):
    def body(p_ref, gr_ref, o_ref, or_ref, send, recv, loc):
        x, y, c, chips = _place()
        peers = [(x, y, 1 - c)] + [(*chip, c) for chip in chips] + [(*chip, 1 - c) for chip in chips]
        local = pltpu.make_async_copy(gr_ref, or_ref.at[_slot(x, y, c)], loc.at[0])
        local.start()
        by_offset = [(x, 1 - y), (1 - x, y), (1 - x, 1 - y)]
        cps = [pltpu.make_async_remote_copy(src_ref=p_ref.at[j], dst_ref=o_ref.at[j], send_sem=send.at[j],
                                            recv_sem=recv.at[j], device_id=(*chip, c), device_id_type=MESH_ID)
               for j, chip in enumerate(by_offset)]
        cps += [pltpu.make_async_remote_copy(src_ref=gr_ref, dst_ref=or_ref.at[_slot(x, y, c)], send_sem=send.at[3 + k],
                                             recv_sem=recv.at[3 + k], device_id=peer, device_id_type=MESH_ID)
                for k, peer in enumerate(peers)]
        for cp in cps:
            cp.start()
        for cp in cps[:3]:
            cp.wait_recv()
        for k, peer in enumerate(peers):
            pltpu.make_async_remote_copy(src_ref=gr_ref, dst_ref=or_ref.at[_slot(*peer)], send_sem=send.at[3 + k],
                                         recv_sem=recv.at[3 + k], device_id=peer, device_id_type=MESH_ID).wait_recv()
        for cp in cps:
            cp.wait_send()
        local.wait()

    return pl.pallas_call(
        body, name="grad_cross", in_specs=[ANY, ANY], out_specs=[ANY, ANY],
        out_shape=[jax.ShapeDtypeStruct(part.shape, part.dtype), jax.ShapeDtypeStruct((NDEV,) + gr.shape, gr.dtype)],
        scratch_shapes=[pltpu.SemaphoreType.DMA((10,)), pltpu.SemaphoreType.DMA((10,)), pltpu.SemaphoreType.DMA((1,))],
    )(part, gr)


def adamw(name, gparts, w, m, v, T=128):
    R, C = w.shape
    T = min(T, R)
    ng = len(gparts)

    def body(*refs):
        g_refs = refs[:ng]
        w_ref, m_ref, v_ref, go_ref, d_ref, mo_ref, vo_ref = refs[ng:]
        g = None
        for g_ref in g_refs:
            for k in range(g_ref.shape[0]):
                t = g_ref[k].astype(F32)
                g = t if g is None else g + t
        mn = ADAM_B1 * m_ref[...] + (1.0 - ADAM_B1) * g
        vn = ADAM_B2 * v_ref[...] + (1.0 - ADAM_B2) * (g * g)
        m_hat = mn / (1.0 - ADAM_B1 ** ADAM_STEP)
        v_hat = vn / (1.0 - ADAM_B2 ** ADAM_STEP)
        go_ref[...] = g
        d_ref[...] = -ADAM_LR * (m_hat / (jnp.sqrt(v_hat) + ADAM_EPS) + ADAM_WD * w_ref[...])
        mo_ref[...] = mn
        vo_ref[...] = vn

    row = pl.BlockSpec((T, C), lambda i: (i, 0))
    sd = jax.ShapeDtypeStruct((R, C), F32)
    return _call(body, name=name, grid=(R // T,),
                 in_specs=[pl.BlockSpec((g.shape[0], T, C), lambda i: (0, i, 0)) for g in gparts] + [row, row, row],
                 out_specs=[row] * 4, out_shape=[sd] * 4)(*gparts, w, m, v)


PACK_W = 1024
PACK_TILE = 16
PACK_ROWS = 128


def _entry_rows(shape):
    n = 1
    for s in shape:
        n *= s
    return n, -(-n // (PACK_TILE * PACK_W)) * PACK_TILE


def _pack(arrays, dtype):
    mats, rows = [], 0
    for a in arrays:
        n, r = _entry_rows(a.shape)
        flat = a.astype(dtype).reshape(-1)
        if r * PACK_W != n:
            flat = jnp.pad(flat, (0, r * PACK_W - n))
        mats.append(flat.reshape(r, PACK_W))
        rows += r
    total = -(-rows // PACK_ROWS) * PACK_ROWS
    if total > rows:
        mats.append(jnp.zeros((total - rows, PACK_W), dtype))
    return jnp.concatenate(mats, axis=0)


def _unpack(buf, shapes, lead=()):
    out, off = [], 0
    for sh in shapes:
        n, r = _entry_rows(sh)
        piece = buf[..., off:off + r, :].reshape(lead + (r * PACK_W,))
        out.append(piece[..., :n].reshape(lead + tuple(sh)))
        off += r
    return out


def _unshard(parts, axis):
    return jnp.concatenate([parts[d] for d in range(NDEV)], axis=axis)


def _slab_cols(w, head, used, n_heads, slab=LANES):
    lead = w.shape[:-1]
    w = w.reshape(lead + (n_heads, head))[..., :used]
    w = jnp.pad(w, [(0, 0)] * len(lead) + [(0, 0), (0, slab - used)])
    return w.reshape(lead + (n_heads * slab,))


def _unslab_cols(g, used, n_heads, slab=LANES):
    lead = g.shape[:-1]
    return g.reshape(lead + (n_heads, slab))[..., :used].reshape(lead + (n_heads * used,))


def _block_diag(w):
    eye = jnp.eye(8, dtype=w.dtype)
    return jnp.einsum("nij,nm->nimj", w, eye).reshape(512, 512)


def _block_diag_t(g):
    g = g.reshape(8, 64, 8, 64)
    return jnp.stack([g[n, :, n, :] for n in range(8)])


BIG = (("e_w_in", 2), ("e_w_out", 1), ("o_w_in", 2), ("o_w_uq", 2), ("o_w_ukv", 2), ("o_w_out", 1),
       ("w_mlp1", 2), ("w_mlp2", 1))
SMALL = (("o_norm_mix", 1), ("o_g_cq", 1), ("o_conv_w", 2), ("o_conv_b", 1), ("o_lru_ba", 2), ("o_lru_bx", 2),
         ("o_lru_lambda", 2))
REPL = ("e_norm_mix", "e_sink", "e_w_pool", "e_pool_scale", "o_g_ckv", "o_lru_wa", "o_lru_wx", "norm_mlp",
        "final_norm")
WEIGHTS = ("e_norm_mix", "e_w_in", "e_sink", "e_w_pool", "e_pool_scale", "e_w_out", "o_norm_mix", "o_w_in", "o_g_cq",
           "o_w_uq", "o_g_ckv", "o_w_ukv", "o_conv_w", "o_conv_b", "o_lru_wa", "o_lru_ba", "o_lru_wx", "o_lru_bx",
           "o_lru_lambda", "o_w_out", "norm_mlp", "w_mlp1", "w_mlp2", "final_norm")


def _mlp_fwd(l, x, g, w1, w2):
    u1, hn = mm_nn(f"mlp1_{l}", [x], [w1], norm_g=g, emit_norm=True, out_dtype=MXU_DTYPE, tm=512, tn=1024)
    x2 = mm_nn(f"mlp2_{l}", [u1], [w2], act="relu2", res=x, tm=512, tn=1024)
    return x2, (x, u1, hn)


def _mlp_bwd(l, saved, g, w1, w2, dx2):
    x, u1, hn = saved
    du1 = mm_nt(f"mlp2_dx_{l}", [dx2], [w2], relu2_of=u1, out_dtype=MXU_DTYPE, tm=512, tn=1024)
    dw2 = mm_tn(f"mlp2_dw_{l}", u1, dx2, act="relu2", tm=1024, tn=1024)
    dhn = mm_nt(f"mlp1_dx_{l}", [du1], [w1], tm=512, tn=1024)
    dw1 = mm_tn(f"mlp1_dw_{l}", hn, du1, tm=1024, tn=1024)
    dx, dg = rms_bwd(f"mlp_norm_bwd_{l}", x, g, dhn, dx2)
    return dx, dg, dw1, dw2


def kernel(x, e_norm_mix, e_w_in, e_sink, e_w_pool, e_pool_scale, e_w_out, o_norm_mix, o_w_in, o_g_cq, o_w_uq, o_g_ckv, o_w_ukv, o_conv_w, o_conv_b, o_lru_wa, o_lru_ba, o_lru_wx, o_lru_bx, o_lru_lambda, o_w_out, norm_mlp, w_mlp1, w_mlp2, final_norm, loss_target, m_e_norm_mix, m_e_w_in, m_e_sink, m_e_w_pool, m_e_pool_scale, m_e_w_out, m_o_norm_mix, m_o_w_in, m_o_g_cq, m_o_w_uq, m_o_g_ckv, m_o_w_ukv, m_o_conv_w, m_o_conv_b, m_o_lru_wa, m_o_lru_ba, m_o_lru_wx, m_o_lru_bx, m_o_lru_lambda, m_o_w_out, m_norm_mlp, m_w_mlp1, m_w_mlp2, m_final_norm, v_e_norm_mix, v_e_w_in, v_e_sink, v_e_w_pool, v_e_pool_scale, v_e_w_out, v_o_norm_mix, v_o_w_in, v_o_g_cq, v_o_w_uq, v_o_g_ckv, v_o_w_ukv, v_o_conv_w, v_o_conv_b, v_o_lru_wa, v_o_lru_ba, v_o_lru_wx, v_o_lru_bx, v_o_lru_lambda, v_o_w_out, v_norm_mlp, v_w_mlp1, v_w_mlp2, v_final_norm):
    W = dict(e_norm_mix=e_norm_mix, e_w_in=e_w_in, e_sink=e_sink, e_w_pool=e_w_pool, e_pool_scale=e_pool_scale, e_w_out=e_w_out, o_norm_mix=o_norm_mix, o_w_in=o_w_in, o_g_cq=o_g_cq, o_w_uq=o_w_uq, o_g_ckv=o_g_ckv, o_w_ukv=o_w_ukv, o_conv_w=o_conv_w, o_conv_b=o_conv_b, o_lru_wa=o_lru_wa, o_lru_ba=o_lru_ba, o_lru_wx=o_lru_wx, o_lru_bx=o_lru_bx, o_lru_lambda=o_lru_lambda, o_w_out=o_w_out, norm_mlp=norm_mlp, w_mlp1=w_mlp1, w_mlp2=w_mlp2, final_norm=final_norm)
    Mo = dict(e_norm_mix=m_e_norm_mix, e_w_in=m_e_w_in, e_sink=m_e_sink, e_w_pool=m_e_w_pool, e_pool_scale=m_e_pool_scale, e_w_out=m_e_w_out, o_norm_mix=m_o_norm_mix, o_w_in=m_o_w_in, o_g_cq=m_o_g_cq, o_w_uq=m_o_w_uq, o_g_ckv=m_o_g_ckv, o_w_ukv=m_o_w_ukv, o_conv_w=m_o_conv_w, o_conv_b=m_o_conv_b, o_lru_wa=m_o_lru_wa, o_lru_ba=m_o_lru_ba, o_lru_wx=m_o_lru_wx, o_lru_bx=m_o_lru_bx, o_lru_lambda=m_o_lru_lambda, o_w_out=m_o_w_out, norm_mlp=m_norm_mlp, w_mlp1=m_w_mlp1, w_mlp2=m_w_mlp2, final_norm=m_final_norm)
    Vo = dict(e_norm_mix=v_e_norm_mix, e_w_in=v_e_w_in, e_sink=v_e_sink, e_w_pool=v_e_w_pool, e_pool_scale=v_e_pool_scale, e_w_out=v_e_w_out, o_norm_mix=v_o_norm_mix, o_w_in=v_o_w_in, o_g_cq=v_o_g_cq, o_w_uq=v_o_w_uq, o_g_ckv=v_o_g_ckv, o_w_ukv=v_o_w_ukv, o_conv_w=v_o_conv_w, o_conv_b=v_o_conv_b, o_lru_wa=v_o_lru_wa, o_lru_ba=v_o_lru_ba, o_lru_wx=v_o_lru_wx, o_lru_bx=v_o_lru_bx, o_lru_lambda=v_o_lru_lambda, o_w_out=v_o_w_out, norm_mlp=v_norm_mlp, w_mlp1=v_w_mlp1, w_mlp2=v_w_mlp2, final_norm=v_final_norm)

    S = x.shape[1]
    x0 = x[0]
    target = loss_target[0]

    big_g, small_g = all_gather([_pack([W[n] for n, _ in BIG], MXU_DTYPE), _pack([W[n] for n, _ in SMALL], F32)])
    full = {}
    for (n, ax), parts in zip(BIG, _unpack(big_g, [W[n].shape for n, _ in BIG], (NDEV,))):
        full[n] = _unshard(parts, ax)
    for (n, ax), parts in zip(SMALL, _unpack(small_g, [W[n].shape for n, _ in SMALL], (NDEV,))):
        full[n] = _unshard(parts, ax)

    def even_in(w):
        return jnp.concatenate([_slab_cols(w[:, 0:512], 64, 64, 8), _slab_cols(w[:, 512:640], 64, 64, 2),
                                _slab_cols(w[:, 640:768], 64, 64, 2), w[:, 768:1280]], axis=1)

    def even_in_t(g):
        return jnp.concatenate([_unslab_cols(g[:, 0:1024], 64, 8), _unslab_cols(g[:, 1024:1280], 64, 2),
                                _unslab_cols(g[:, 1280:1536], 64, 2), g[:, 1536:2048]], axis=1)

    def slab_rows(w, n_heads):
        return _slab_cols(w.T, 64, 64, n_heads).T

    def unslab_rows(g, n_heads):
        return _unslab_cols(g.T, 64, n_heads).T

    def odd_in(w):
        return jnp.concatenate([w[:, 0:384], jnp.pad(w[:, 384:416], ((0, 0), (0, 96))), w[:, 416:1440]], axis=1)

    def odd_in_t(g):
        return jnp.concatenate([g[:, 0:416], g[:, 512:1536]], axis=1)

    def uq(w):
        return _slab_cols(w, 96, 96, 8)

    def ukv(w):
        w = w.reshape(128, 8, 128)
        pad = lambda t: jnp.pad(t, ((0, 0), (0, 0), (0, 64))).reshape(128, 1024)
        return pad(w[:, :, :64]), pad(w[:, :, 64:])

    def ukv_t(gk, gv):
        gk = gk.reshape(128, 8, 128)[:, :, :64]
        gv = gv.reshape(128, 8, 128)[:, :, :64]
        return jnp.concatenate([gk, gv], axis=2).reshape(128, 1024)

    tabs_swa = rope_tables(S, 0, SWA_HALF)
    tabs_mq = rope_tables(S, 64, MLA_HALF)
    tabs_mk = rope_tables(S, 0, MLA_HALF)
    row = lambda v: v.reshape(1, -1)

    saved = []
    xcur = x0
    for l in range(4):
        j = l // 2
        if l % 2 == 0:
            w_in = even_in(full["e_w_in"][j])
            w_out = full["e_w_out"][j]
            w_out_a, w_out_b = slab_rows(w_out[0:512], 8), w_out[512:1024]
            w_pool = W["e_w_pool"][j].astype(MXU_DTYPE)
            z, h = mm_nn(f"e_in_{j}", [xcur], [w_in], norm_g=row(W["e_norm_mix"][j]), emit_norm=True, tm=256, tn=1024)
            qkv = swa_prep(z, tabs_swa)
            ya, lse = swa_fwd(qkv, W["e_sink"][j])
            yb = pool_fwd(z, w_pool, row(W["e_pool_scale"][j]))
            x1 = mm_nn(f"e_out_{j}", [ya, yb], [w_out_a, w_out_b], res=xcur)
            mix = (xcur, z, h, qkv, lse, ya, yb, w_in, w_out_a, w_out_b, w_pool)
        else:
            w_in = odd_in(full["o_w_in"][j])
            w_out = full["o_w_out"][j]
            w_out_a, w_out_b = slab_rows(w_out[0:512], 8), w_out[512:1024]
            w_uq = uq(full["o_w_uq"][j])
            w_k, w_v = ukv(full["o_w_ukv"][j])
            g_cq, g_ckv = row(full["o_g_cq"][j]), row(W["o_g_ckv"][j])
            w_gate = jnp.concatenate([_block_diag(W["o_lru_wa"][j, 0]), _block_diag(W["o_lru_wx"][j, 0]),
                                      _block_diag(W["o_lru_wa"][j, 1]), _block_diag(W["o_lru_wx"][j, 1])], axis=1).astype(MXU_DTYPE)
            b_gate = jnp.concatenate([full["o_lru_ba"][j, 0], full["o_lru_bx"][j, 0], full["o_lru_ba"][j, 1],
                                      full["o_lru_bx"][j, 1]]).reshape(1, 2048)
            lam = full["o_lru_lambda"][j].reshape(1, 1024)
            conv_w, conv_b = full["o_conv_w"][j], row(full["o_conv_b"][j])
            z, h = mm_nn(f"o_in_{j}", [xcur], [w_in], norm_g=row(full["o_norm_mix"][j]), emit_norm=True, tm=256, tn=512)
            q, k, v, nq, nkv = mla_prep(z, g_cq, g_ckv, w_uq, w_k, w_v, tabs_mq, tabs_mk)
            yc, lse = mla_fwd(q, k, mla_vt(v))
            xc, a0, b0, a1, b1 = lru_pre(z, conv_w, conv_b, w_gate, b_gate, lam)
            h0, h1 = lru_scan(f"lru_scan_fwd_{j}", a0, b0, a1, b1, adjoint=False)
            yd = lru_gate(h0, h1, z)
            x1 = mm_nn(f"o_out_{j}", [yc, yd], [w_out_a, w_out_b], res=xcur)
            mix = (xcur, z, h, q, k, v, nq, nkv, yc, lse, xc, a0, a1, h0, h1, yd, w_in, w_out_a, w_out_b, w_uq, w_k, w_v,
                   g_cq, g_ckv, w_gate, b_gate, lam, conv_w)
        xcur, mlp = _mlp_fwd(l, x1, row(W["norm_mlp"][l]), full["w_mlp1"][l], full["w_mlp2"][l])
        saved.append((mix, mlp))

    loss_row, dx, dg_final = loss_head(xcur, row(W["final_norm"]), target)

    G = {n: [None] * W[n].shape[0] for n in WEIGHTS if n != "final_norm"}
    G["final_norm"] = dg_final.reshape(-1)
    for l in reversed(range(4)):
        j = l // 2
        mix, mlp = saved[l]
        dx, dg, dw1, dw2 = _mlp_bwd(l, mlp, row(W["norm_mlp"][l]), full["w_mlp1"][l], full["w_mlp2"][l], dx)
        G["norm_mlp"][l], G["w_mlp1"][l], G["w_mlp2"][l] = dg.reshape(-1), dw1, dw2
        if l % 2 == 0:
            xin, z, h, qkv, lse, ya, yb, w_in, w_out_a, w_out_b, w_pool = mix
            w_out_cat = jnp.concatenate([w_out_a, w_out_b], axis=0)
            dycat = mm_nt(f"e_out_dx_{j}", [dx], [w_out_cat])
            G["e_w_out"][j] = jnp.concatenate([unslab_rows(mm_tn(f"e_out_dwa_{j}", ya, dx), 8),
                                               mm_tn(f"e_out_dwb_{j}", yb, dx)], axis=0)
            dq, dk, dv, dsink = swa_bwd(qkv, W["e_sink"][j], lse, dycat, tabs_swa)
            du, dwp, dsc = pool_bwd(z, dycat, w_pool, row(W["e_pool_scale"][j]))
            G["e_sink"][j], G["e_w_pool"][j], G["e_pool_scale"][j] = dsink[0, 0:8], dwp, dsc.reshape(-1)
            dh = mm_nt(f"e_in_dx_{j}", [dq, dk, dv, du], [(w_in, 0, 1024), (w_in, 4, 256), (w_in, 5, 256), (w_in, 3, 512)])
            G["e_w_in"][j] = even_in_t(jnp.concatenate(
                [mm_tn(f"e_in_dwq_{j}", h, dq), mm_tn(f"e_in_dwk_{j}", h, dk), mm_tn(f"e_in_dwv_{j}", h, dv),
                 mm_tn(f"e_in_dwu_{j}", h, du)], axis=1))
            dx, dg = rms_bwd(f"e_norm_bwd_{j}", xin, row(W["e_norm_mix"][j]), dh, dx)
            G["e_norm_mix"][j] = dg.reshape(-1)
        else:
            (xin, z, h, q, k, v, nq, nkv, yc, lse, xc, a0, a1, h0, h1, yd, w_in, w_out_a, w_out_b, w_uq, w_k, w_v,
             g_cq, g_ckv, w_gate, b_gate, lam, conv_w) = mix
            w_out_cat = jnp.concatenate([w_out_a, w_out_b], axis=0)
            dycat = mm_nt(f"o_out_dx_{j}", [dx], [w_out_cat])
            G["o_w_out"][j] = jnp.concatenate([unslab_rows(mm_tn(f"o_out_dwa_{j}", yc, dx), 8),
                                               mm_tn(f"o_out_dwb_{j}", yd, dx)], axis=0)
            delta, dob = mla_delta(yc, dycat)
            dqt, dk, dv = mla_bwd(q, k, v, dob, lse, delta)
            dq = mla_dq(dqt)
            dza, dqp, dgq, dgkv = mla_prep_bwd(z, g_cq, g_ckv, w_uq, w_k, w_v, dq, dk, dv, tabs_mq, tabs_mk)
            G["o_g_cq"][j], G["o_g_ckv"][j] = dgq.reshape(-1), dgkv.reshape(-1)
            G["o_w_uq"][j] = _unslab_cols(mm_tn(f"o_uq_dw_{j}", nq, dqp), 96, 8)
            G["o_w_ukv"][j] = ukv_t(mm_tn(f"o_uk_dw_{j}", nkv, dk), mm_tn(f"o_uv_dw_{j}", nkv, dv))
            dxg, dhh = lru_gate_bwd(h0, h1, z, dycat)
            g1, g0 = lru_scan(f"lru_scan_bwd_{j}", a1, dhh, a0, dhh, adjoint=True)
            dxc, dpre, dbias, dlam = lru_bwd_point(xc, h0, h1, g0, g1, w_gate, b_gate, lam)
            dwg = mm_tn(f"o_gate_dw_{j}", xc, dpre)
            G["o_lru_wa"][j] = jnp.stack([_block_diag_t(dwg[:, 0:512]), _block_diag_t(dwg[:, 1024:1536])])
            G["o_lru_wx"][j] = jnp.stack([_block_diag_t(dwg[:, 512:1024]), _block_diag_t(dwg[:, 1536:2048])])
            G["o_lru_ba"][j] = jnp.stack([dbias[0, 0:512], dbias[0, 1024:1536]])
            G["o_lru_bx"][j] = jnp.stack([dbias[0, 512:1024], dbias[0, 1536:2048]])
            G["o_lru_lambda"][j] = dlam.reshape(2, 512)
            dxr, dcw, dcb = conv_bwd(z, dxc, conv_w)
            G["o_conv_w"][j], G["o_conv_b"][j] = dcw, dcb.reshape(-1)
            dh = mm_nt(f"o_in_dx_{j}", [dza, dxr, dxg], [(w_in, 0, 512), (w_in, 1, 512), (w_in, 2, 512)])
            G["o_w_in"][j] = odd_in_t(jnp.concatenate(
                [mm_tn(f"o_in_dwa_{j}", h, dza), mm_tn(f"o_in_dwr_{j}", h, dxr), mm_tn(f"o_in_dwg_{j}", h, dxg)], axis=1))
            dx, dg = rms_bwd(f"o_norm_bwd_{j}", xin, row(full["o_norm_mix"][j]), dh, dx)
            G["o_norm_mix"][j] = dg.reshape(-1)
    grad_x = dx[None]
    G = {n: (g if n == "final_norm" else jnp.stack(g)) for n, g in G.items()}

    sharded = BIG + SMALL
    me = 4 * lax.axis_index("x") + 2 * lax.axis_index("y") + lax.axis_index("c")

    def shard_of(n, ax, dev):
        size = W[n].shape[ax]
        return lax.dynamic_slice_in_dim(G[n], dev * size, size, axis=ax)

    gs = jnp.stack([_pack([shard_of(n, ax, me ^ r) for n, ax in sharded], F32) for r in range(NDEV)])
    gr = _pack([G[n] for n in REPL] + [loss_row[0, 0:1]], F32)
    mine, others = pair_add(gs, grad_pair(gs))
    others, gr_all = grad_cross(others, gr)
    outs_s = adamw("adamw_sharded", [mine[None], others], *[_pack([T[n] for n, _ in sharded], F32) for T in (W, Mo, Vo)])
    outs_r = adamw("adamw_replicated", [gr_all],
                   *[_pack([T[n] for n in REPL] + [jnp.zeros((1,), F32)], F32) for T in (W, Mo, Vo)])
    res = [dict(), dict(), dict(), dict()]
    for kind in range(4):
        for (n, _), a in zip(sharded, _unpack(outs_s[kind], [W[n].shape for n, _ in sharded])):
            res[kind][n] = a
        for n, a in zip(REPL + ("loss",), _unpack(outs_r[kind], [W[n].shape for n in REPL] + [(1,)])):
            res[kind][n] = a
    loss = res[0]["loss"][0]
    return (loss, grad_x, *[res[0][n] for n in WEIGHTS], *[res[1][n] for n in WEIGHTS],
            *[res[2][n] for n in WEIGHTS], *[res[3][n] for n in WEIGHTS])
```

```python
import functools

import jax
import jax.numpy as jnp
from jax import lax
from jax.experimental import pallas as pl
from jax.experimental.pallas import tpu as pltpu

F32 = jnp.float32
MXU_DTYPE = jnp.bfloat16
EPS = 1e-6
ROPE_THETA = 10000.0
NDEV = 8
LANES = 128
VMEM_LIMIT = 48 * 1024 * 1024

D_MODEL = 1024
D_FF = 4096
LRU_C = 8.0
POOL_WINDOWS = (2, 4, 8, 16)
HALO = 8
MLA_SCALE = 96.0 ** -0.5
LOG2E = 1.4426950408889634
MLA_QSCALE = MLA_SCALE * LOG2E

ADAM_LR, ADAM_B1, ADAM_B2, ADAM_EPS, ADAM_WD, ADAM_STEP = 0.001, 0.9, 0.999, 1e-08, 0.01, 10


def _mx(v):
    return v.astype(MXU_DTYPE)


def _call(body, *, name, grid, in_specs, out_specs, out_shape, scratch=()):
    return pl.pallas_call(
        body, name=name, grid=grid, in_specs=in_specs, out_specs=out_specs, out_shape=out_shape,
        scratch_shapes=list(scratch),
        compiler_params=pltpu.CompilerParams(
            dimension_semantics=("arbitrary",) * len(grid), vmem_limit_bytes=VMEM_LIMIT),
    )


def _dot(a, b):
    return jnp.dot(a, b, preferred_element_type=F32)


def _dot_nt(a, b):
    return lax.dot_general(a, b, (((1,), (1,)), ((), ())), preferred_element_type=F32)


def _dot_tn(a, b):
    return lax.dot_general(a, b, (((0,), (0,)), ((), ())), preferred_element_type=F32)


def _rms(x, g):
    r = lax.rsqrt(jnp.mean(x * x, axis=-1, keepdims=True) + EPS)
    return (x * r) * g


def _rms_bwd(x, g, dy):
    r = lax.rsqrt(jnp.mean(x * x, axis=-1, keepdims=True) + EPS)
    xh = x * r
    dyg = dy * g
    dx = r * (dyg - xh * jnp.mean(dyg * xh, axis=-1, keepdims=True))
    return dx, jnp.sum(dy * xh, axis=0, keepdims=True)


def _sigmoid(x):
    return 1.0 / (1.0 + jnp.exp(-x))


def _log1p(e):
    u = 1.0 + e
    d = u - 1.0
    return jnp.where(d == 0.0, e, jnp.log(u) * (e / jnp.where(d == 0.0, 1.0, d)))


def _softplus(x):
    return jnp.maximum(x, 0.0) + _log1p(jnp.exp(-jnp.abs(x)))


def _expm1(x):
    u = jnp.exp(x)
    lu = jnp.log(u)
    safe = jnp.where((lu == 0.0) | (u == 0.0), 1.0, lu)
    return jnp.where(u == 1.0, x, jnp.where(u == 0.0, -1.0, (u - 1.0) * x / safe))


_GELU_K = 0.7978845608028654


def _gelu(x):
    return 0.5 * x * (1.0 + jnp.tanh(_GELU_K * (x + 0.044715 * x * x * x)))


def _gelu_grad(x):
    t = jnp.tanh(_GELU_K * (x + 0.044715 * x * x * x))
    return 0.5 * (1.0 + t) + 0.5 * x * (1.0 - t * t) * _GELU_K * (1.0 + 3.0 * 0.044715 * x * x)


def _rope(x, c, sa, sb, half):
    return x * c + pltpu.roll(x, LANES - half, 1) * sa + pltpu.roll(x, half, 1) * sb


def _rope_t(d, c, sa, sb, half):
    return d * c - pltpu.roll(d, LANES - half, 1) * sa - pltpu.roll(d, half, 1) * sb


def _as_cols(a):
    return a if isinstance(a, tuple) else (a, 0, a.shape[1])


def mm_nn(name, a_list, b_list, *, res=None, act=None, norm_g=None, emit_norm=False,
          out_dtype=F32, tm=256, tn=512):
    a_list = [_as_cols(a) for a in a_list]
    M, N = a_list[0][0].shape[0], b_list[0].shape[1]
    tm, tn = min(tm, M), min(tn, N)
    na = len(a_list)

    def body(*refs):
        a_refs, b_refs = refs[:na], refs[na:2 * na]
        k = 2 * na
        g_ref = res_ref = hn_ref = None
        if norm_g is not None:
            g_ref = refs[k]
            k += 1
        if res is not None:
            res_ref = refs[k]
            k += 1
        o_ref = refs[k]
        if emit_norm:
            hn_ref = refs[k + 1]
        acc = None
        for a_ref, b_ref in zip(a_refs, b_refs):
            a = a_ref[...]
            if g_ref is not None:
                a = _rms(a.astype(F32), g_ref[...])
                if hn_ref is not None:
                    hn_ref[...] = a.astype(hn_ref.dtype)
            if act == "relu2":
                a = jnp.maximum(a, 0.0)
                a = a * a
            d = _dot(_mx(a), _mx(b_ref[...]))
            acc = d if acc is None else acc + d
        if res_ref is not None:
            acc = acc + res_ref[...]
        o_ref[...] = acc.astype(o_ref.dtype)

    in_specs = [pl.BlockSpec((tm, w), functools.partial(lambda i, j, cb: (i, cb), cb=cb)) for (_, cb, w) in a_list]
    in_specs += [pl.BlockSpec((b.shape[0], tn), lambda i, j: (0, j)) for b in b_list]
    args = [a for (a, _, _) in a_list] + list(b_list)
    if norm_g is not None:
        in_specs.append(pl.BlockSpec((1, norm_g.shape[1]), lambda i, j: (0, 0)))
        args.append(norm_g)
    if res is not None:
        in_specs.append(pl.BlockSpec((tm, tn), lambda i, j: (i, j)))
        args.append(res)
    out_specs = [pl.BlockSpec((tm, tn), lambda i, j: (i, j))]
    out_shape = [jax.ShapeDtypeStruct((M, N), out_dtype)]
    if emit_norm:
        K = a_list[0][2]
        out_specs.append(pl.BlockSpec((tm, K), lambda i, j: (i, 0)))
        out_shape.append(jax.ShapeDtypeStruct((M, K), MXU_DTYPE))
    out = _call(body, name=name, grid=(M // tm, N // tn), in_specs=in_specs, out_specs=out_specs,
                out_shape=out_shape)(*args)
    return out if emit_norm else out[0]


def mm_nt(name, a_list, b_list, *, relu2_of=None, out_dtype=F32, tm=256, tn=512):
    a_list = [_as_cols(a) for a in a_list]
    b_list = [_as_cols(b) for b in b_list]
    M, N = a_list[0][0].shape[0], b_list[0][0].shape[0]
    tm, tn = min(tm, M), min(tn, N)
    na = len(a_list)

    def body(*refs):
        a_refs, b_refs = refs[:na], refs[na:2 * na]
        u_ref = refs[2 * na] if relu2_of is not None else None
        o_ref = refs[-1]
        acc = None
        for a_ref, b_ref in zip(a_refs, b_refs):
            d = _dot_nt(_mx(a_ref[...]), _mx(b_ref[...]))
            acc = d if acc is None else acc + d
        if u_ref is not None:
            acc = acc * (2.0 * jnp.maximum(u_ref[...], 0.0))
        o_ref[...] = acc.astype(o_ref.dtype)

    in_specs = [pl.BlockSpec((tm, w), functools.partial(lambda i, j, cb: (i, cb), cb=cb)) for (_, cb, w) in a_list]
    in_specs += [pl.BlockSpec((tn, w), functools.partial(lambda i, j, cb: (j, cb), cb=cb)) for (_, cb, w) in b_list]
    args = [a for (a, _, _) in a_list] + [b for (b, _, _) in b_list]
    if relu2_of is not None:
        in_specs.append(pl.BlockSpec((tm, tn), lambda i, j: (i, j)))
        args.append(relu2_of)
    return _call(body, name=name, grid=(M // tm, N // tn), in_specs=in_specs,
                 out_specs=pl.BlockSpec((tm, tn), lambda i, j: (i, j)),
                 out_shape=jax.ShapeDtypeStruct((M, N), out_dtype))(*args)


def mm_tn(name, a, b, *, act=None, tm=512, tn=512, tk=512):
    a, acb, Ma = _as_cols(a)
    b, bcb, Nb = _as_cols(b)
    S = a.shape[0]
    tm, tn, tk = min(tm, Ma), min(tn, Nb), min(tk, S)
    a0, b0 = acb * (Ma // tm), bcb * (Nb // tn)

    def body(a_ref, b_ref, o_ref):
        @pl.when(pl.program_id(2) == 0)
        def _():
            o_ref[...] = jnp.zeros_like(o_ref)

        av = a_ref[...]
        if act == "relu2":
            av = jnp.maximum(av, 0.0)
            av = av * av
        o_ref[...] += _dot_tn(_mx(av), _mx(b_ref[...]))

    return _call(body, name=name, grid=(Ma // tm, Nb // tn, S // tk),
                 in_specs=[pl.BlockSpec((tk, tm), lambda i, j, k: (k, a0 + i)),
                           pl.BlockSpec((tk, tn), lambda i, j, k: (k, b0 + j))],
                 out_specs=pl.BlockSpec((tm, tn), lambda i, j, k: (i, j)),
                 out_shape=jax.ShapeDtypeStruct((Ma, Nb), F32))(a, b)


def rms_bwd(name, x, g, dh, dres, T=256):
    S, D = x.shape
    T = min(T, S)

    def body(x_ref, g_ref, dh_ref, dres_ref, dx_ref, dg_ref):
        @pl.when(pl.program_id(0) == 0)
        def _():
            dg_ref[...] = jnp.zeros_like(dg_ref)

        dx, dg = _rms_bwd(x_ref[...], g_ref[...], dh_ref[...])
        dx_ref[...] = dres_ref[...] + dx
        dg_ref[...] += dg

    row = pl.BlockSpec((T, D), lambda i: (i, 0))
    vec = pl.BlockSpec((1, D), lambda i: (0, 0))
    return _call(body, name=name, grid=(S // T,), in_specs=[row, vec, row, row], out_specs=[row, vec],
                 out_shape=[jax.ShapeDtypeStruct((S, D), F32), jax.ShapeDtypeStruct((1, D), F32)])(x, g, dh, dres)


def loss_head(x, g, target, T=256):
    S, D = x.shape
    T = min(T, S)

    def body(x_ref, g_ref, t_ref, loss_ref, dx_ref, dg_ref):
        @pl.when(pl.program_id(0) == 0)
        def _():
            dg_ref[...] = jnp.zeros_like(dg_ref)
            loss_ref[...] = jnp.zeros_like(loss_ref)

        x = x_ref[...]
        err = _rms(x, g_ref[...]) - t_ref[...]
        loss_ref[...] += 0.5 * jnp.sum(jnp.sum(err * err, axis=-1, keepdims=True) / D, axis=0, keepdims=True)
        dx, dg = _rms_bwd(x, g_ref[...], err / D)
        dx_ref[...] = dx
        dg_ref[...] += dg

    row = pl.BlockSpec((T, D), lambda i: (i, 0))
    vec = pl.BlockSpec((1, D), lambda i: (0, 0))
    return _call(body, name="loss_head", grid=(S // T,), in_specs=[row, vec, row],
                 out_specs=[pl.BlockSpec((1, LANES), lambda i: (0, 0)), row, vec],
                 out_shape=[jax.ShapeDtypeStruct((1, LANES), F32), jax.ShapeDtypeStruct((S, D), F32),
                            jax.ShapeDtypeStruct((1, D), F32)])(x, g, target)


def rope_tables(S, lo, half):
    inv = ROPE_THETA ** (-jnp.arange(half, dtype=F32) / half)
    ang = jnp.arange(S, dtype=F32)[:, None] * inv[None, :]
    cos, sin = jnp.cos(ang), jnp.sin(ang)
    one = lambda n: jnp.ones((S, n), F32)
    zero = lambda n: jnp.zeros((S, n), F32)
    hi = LANES - lo - 2 * half
    c = jnp.concatenate([one(lo), cos, cos, one(hi)], axis=1)
    sa = jnp.concatenate([zero(lo), -sin, zero(half), zero(hi)], axis=1)
    sb = jnp.concatenate([zero(lo), zero(half), sin, zero(hi)], axis=1)
    return c, sa, sb


SWA_BLOCK = 128
SWA_HALF = 32


def swa_prep(z, tabs, T=256):
    S = z.shape[0]
    T = min(T, S)

    def body(z_ref, c_ref, sa_ref, sb_ref, o_ref):
        c, sa, sb = c_ref[...], sa_ref[...], sb_ref[...]
        for s in range(10):
            sl = slice(s * LANES, (s + 1) * LANES)
            y = _rope(z_ref[:, sl], c, sa, sb, SWA_HALF)
            if s < 8:
                y = y * 0.125
            o_ref[:, sl] = y.astype(o_ref.dtype)
        o_ref[:, 1280:1536] = z_ref[:, 1280:1536].astype(o_ref.dtype)

    tab = pl.BlockSpec((T, LANES), lambda i: (i, 0))
    return _call(body, name="swa_prep", grid=(S // T,),
                 in_specs=[pl.BlockSpec((T, 1536), lambda i: (i, 0)), tab, tab, tab],
                 out_specs=pl.BlockSpec((T, 1536), lambda i: (i, 0)),
                 out_shape=jax.ShapeDtypeStruct((S, 1536), MXU_DTYPE))(z, *tabs)


SWA_GROUP = 4


def _swa_valid(n, S):
    B = SWA_BLOCK
    i = lax.broadcasted_iota(jnp.int32, (SWA_GROUP * B, 3 * B), 0) & (B - 1)
    j = lax.broadcasted_iota(jnp.int32, (SWA_GROUP * B, 3 * B), 1)
    kpos = j + (n - 1) * B
    return (jnp.abs(j - B - i) <= B) & (kpos >= 0) & (kpos < S)


def _swa_sink(sink_ref, hk):
    B = SWA_BLOCK
    row = lax.broadcasted_iota(jnp.int32, (SWA_GROUP * B, 1), 0)
    sk = jnp.full((SWA_GROUP * B, 1), sink_ref[hk * SWA_GROUP], F32)
    for g in range(1, SWA_GROUP):
        sk = jnp.where(row >= g * B, sink_ref[hk * SWA_GROUP + g], sk)
    return sk


def swa_fwd(qkv, sink):
    S = qkv.shape[0]
    B = SWA_BLOCK
    nb = S // B

    def body(sink_ref, q_ref, kp_ref, kc_ref, kn_ref, vp_ref, vc_ref, vn_ref, o_ref, st_ref):
        n = pl.program_id(0)
        valid = _swa_valid(n, S)
        lane = lax.broadcasted_iota(jnp.int32, (B, LANES), 1)
        st = jnp.zeros((B, LANES), F32)
        for hk in range(2):
            sl = slice(hk * LANES, (hk + 1) * LANES)
            k3 = jnp.concatenate([kp_ref[:, sl], kc_ref[:, sl], kn_ref[:, sl]], axis=0)
            v3 = jnp.concatenate([vp_ref[:, sl], vc_ref[:, sl], vn_ref[:, sl]], axis=0)
            heads = [hk * SWA_GROUP + g for g in range(SWA_GROUP)]
            q4 = jnp.concatenate([q_ref[:, h * LANES:(h + 1) * LANES] for h in heads], axis=0)
            s = jnp.where(valid, _dot_nt(q4, k3), -jnp.inf)
            sk = _swa_sink(sink_ref, hk)
            m = jnp.maximum(jnp.max(s, axis=-1, keepdims=True), sk)
            p = jnp.exp(s - m)
            den = jnp.sum(p, axis=-1, keepdims=True) + jnp.exp(sk - m)
            p = p / den
            o4 = _dot(_mx(p), v3)
            lse4 = m + jnp.log(den)
            for g, h in enumerate(heads):
                rows = slice(g * B, (g + 1) * B)
                o_ref[:, h * LANES:(h + 1) * LANES] = o4[rows].astype(o_ref.dtype)
                st = jnp.where(lane == h, lse4[rows], st)
        st_ref[...] = st

    kv = lambda cb, d: pl.BlockSpec((B, 2 * LANES), lambda n: (jnp.clip(n + d, 0, nb - 1), cb))
    return _call(body, name="swa_fwd", grid=(nb,),
                 in_specs=[pl.BlockSpec(memory_space=pltpu.SMEM),
                           pl.BlockSpec((B, 1024), lambda n: (n, 0)),
                           kv(4, -1), kv(4, 0), kv(4, 1), kv(5, -1), kv(5, 0), kv(5, 1)],
                 out_specs=[pl.BlockSpec((B, 1024), lambda n: (n, 0)), pl.BlockSpec((B, LANES), lambda n: (n, 0))],
                 out_shape=[jax.ShapeDtypeStruct((S, 1024), MXU_DTYPE), jax.ShapeDtypeStruct((S, LANES), F32)],
                 )(sink, qkv, qkv, qkv, qkv, qkv, qkv, qkv)


def swa_bwd(qkv, sink, lse, dycat, tabs):
    S = qkv.shape[0]
    B = SWA_BLOCK
    nb = S // B

    def body(sink_ref, q_ref, kp_ref, kc_ref, kn_ref, vp_ref, vc_ref, vn_ref, do_ref, st_ref,
             cq_ref, saq_ref, sbq_ref, ck_ref, sak_ref, sbk_ref,
             dq_ref, dk_ref, dv_ref, dsink_ref, dk_acc, dv_acc):
        n = pl.program_id(0)

        @pl.when(n == 0)
        def _():
            dk_acc[...] = jnp.zeros_like(dk_acc)
            dv_acc[...] = jnp.zeros_like(dv_acc)
            dsink_ref[...] = jnp.zeros_like(dsink_ref)

        @pl.when(n < nb)
        def _():
            valid = _swa_valid(n, S)
            lane = lax.broadcasted_iota(jnp.int32, (B, LANES), 1)
            lane1 = lax.broadcasted_iota(jnp.int32, (1, LANES), 1)
            st = st_ref[...]
            cq, saq, sbq = cq_ref[...], saq_ref[...], sbq_ref[...]
            dsink = jnp.zeros((1, LANES), F32)
            for hk in range(2):
                sl = slice(hk * LANES, (hk + 1) * LANES)
                k3 = jnp.concatenate([kp_ref[:, sl], kc_ref[:, sl], kn_ref[:, sl]], axis=0)
                v3 = jnp.concatenate([vp_ref[:, sl], vc_ref[:, sl], vn_ref[:, sl]], axis=0)
                heads = [hk * SWA_GROUP + g for g in range(SWA_GROUP)]
                q4 = jnp.concatenate([q_ref[:, h * LANES:(h + 1) * LANES] for h in heads], axis=0)
                do4 = jnp.concatenate([_mx(do_ref[:, h * LANES:(h + 1) * LANES]) for h in heads], axis=0)
                lse4 = jnp.concatenate([jnp.sum(jnp.where(lane == h, st, 0.0), axis=-1, keepdims=True) for h in heads],
                                       axis=0)
                p = jnp.where(valid, jnp.exp(_dot_nt(q4, k3) - lse4), 0.0)
                dp = _dot_nt(do4, v3)
                dsum = jnp.sum(p * dp, axis=-1, keepdims=True)
                ds = _mx(p * (dp - dsum))
                dq4 = _dot(ds, k3) * 0.125
                dk_acc[:, sl] += _dot_tn(ds, q4)
                dv_acc[:, sl] += _dot_tn(_mx(p), do4)
                dsk4 = jnp.exp(_swa_sink(sink_ref, hk) - lse4) * dsum
                for g, h in enumerate(heads):
                    rows = slice(g * B, (g + 1) * B)
                    dq_ref[:, h * LANES:(h + 1) * LANES] = _rope_t(dq4[rows], cq, saq, sbq, SWA_HALF)
                    dsink = jnp.where(lane1 == h, -jnp.sum(dsk4[rows], axis=0, keepdims=True), dsink)
            dsink_ref[...] += dsink

        ck, sak, sbk = ck_ref[...], sak_ref[...], sbk_ref[...]
        for hk in range(2):
            sl = slice(hk * LANES, (hk + 1) * LANES)
            dk_ref[:, sl] = _rope_t(dk_acc[0:B, sl], ck, sak, sbk, SWA_HALF)
        dv_ref[...] = dv_acc[0:B, :]
        for acc in (dk_acc, dv_acc):
            acc[0:B, :] = acc[B:2 * B, :]
            acc[B:2 * B, :] = acc[2 * B:3 * B, :]
            acc[2 * B:3 * B, :] = jnp.zeros((B, 2 * LANES), F32)

    qn = lambda n: jnp.minimum(n, nb - 1)
    kv = lambda cb, d: pl.BlockSpec((B, 2 * LANES), lambda n: (jnp.clip(qn(n) + d, 0, nb - 1), cb))
    qrow = lambda w: pl.BlockSpec((B, w), lambda n: (qn(n), 0))
    krow = lambda w: pl.BlockSpec((B, w), lambda n: (jnp.maximum(n - 1, 0), 0))
    return _call(body, name="swa_bwd", grid=(nb + 1,),
                 in_specs=[pl.BlockSpec(memory_space=pltpu.SMEM), qrow(1024),
                           kv(4, -1), kv(4, 0), kv(4, 1), kv(5, -1), kv(5, 0), kv(5, 1),
                           qrow(1024), qrow(LANES),
                           qrow(LANES), qrow(LANES), qrow(LANES), krow(LANES), krow(LANES), krow(LANES)],
                 out_specs=[qrow(1024), krow(2 * LANES), krow(2 * LANES), pl.BlockSpec((1, LANES), lambda n: (0, 0))],
                 out_shape=[jax.ShapeDtypeStruct((S, 1024), F32), jax.ShapeDtypeStruct((S, 2 * LANES), F32),
                            jax.ShapeDtypeStruct((S, 2 * LANES), F32), jax.ShapeDtypeStruct((1, LANES), F32)],
                 scratch=[pltpu.VMEM((3 * B, 2 * LANES), F32), pltpu.VMEM((3 * B, 2 * LANES), F32)],
                 )(sink, qkv, qkv, qkv, qkv, qkv, qkv, qkv, dycat, lse, *tabs, *tabs)


def _halo_specs(T, S, w, cb):
    r = T // HALO
    last = S // HALO - 1
    return [pl.BlockSpec((HALO, w), lambda i: (jnp.maximum(i * r - 1, 0), cb)),
            pl.BlockSpec((T, w), lambda i: (i, cb)),
            pl.BlockSpec((HALO, w), lambda i: (jnp.minimum((i + 1) * r, last), cb))]


def _fill_ext(ext, prev_ref, cur_ref, next_ref, i, nt, T):
    ext[0:HALO, :] = jnp.where(i > 0, prev_ref[...], 0.0).astype(F32)
    ext[HALO:HALO + T, :] = cur_ref[...].astype(F32)
    ext[HALO + T:2 * HALO + T, :] = jnp.where(i < nt - 1, next_ref[...], 0.0).astype(F32)


def _pool_cnt(t, half, S):
    return (jnp.clip(t + half, 0, S) - jnp.clip(t - half, 0, S)).astype(F32)


def pool_fwd(z, w_pool, scale, T=256):
    S = z.shape[0]
    T = min(T, S)
    nt = S // T

    def body(up_ref, uc_ref, un_ref, w_ref, sc_ref, o_ref, ext):
        i = pl.program_id(0)
        _fill_ext(ext, up_ref, uc_ref, un_ref, i, nt, T)
        t = i * T + lax.broadcasted_iota(jnp.int32, (T, 1), 0)
        for g, win in enumerate(POOL_WINDOWS):
            half = win // 2
            sl = slice(g * LANES, (g + 1) * LANES)
            acc = ext[pl.ds(HALO - half, T), sl]
            for off in range(-half + 1, half):
                acc = acc + ext[pl.ds(HALO + off, T), sl]
            d = acc / _pool_cnt(t, half, S) - ext[pl.ds(HALO, T), sl]
            o_ref[:, sl] = (_dot(_mx(d), w_ref[g]) * sc_ref[:, sl]).astype(o_ref.dtype)

    return _call(body, name="pool_fwd", grid=(nt,),
                 in_specs=_halo_specs(T, S, 512, 3) + [pl.BlockSpec((4, LANES, LANES), lambda i: (0, 0, 0)),
                                                      pl.BlockSpec((1, 512), lambda i: (0, 0))],
                 out_specs=pl.BlockSpec((T, 512), lambda i: (i, 0)),
                 out_shape=jax.ShapeDtypeStruct((S, 512), MXU_DTYPE),
                 scratch=[pltpu.VMEM((T + 2 * HALO, 512), F32)])(z, z, z, w_pool, scale)


def pool_bwd(z, dycat, w_pool, scale, T=256):
    S = z.shape[0]
    T = min(T, S)
    nt = S // T
    TE = T + 2 * HALO

    def body(up_ref, uc_ref, un_ref, yp_ref, yc_ref, yn_ref, w_ref, sc_ref, du_ref, dw_ref, dsc_ref, extu, exty, exte):
        i = pl.program_id(0)

        @pl.when(i == 0)
        def _():
            dw_ref[...] = jnp.zeros_like(dw_ref)
            dsc_ref[...] = jnp.zeros_like(dsc_ref)

        _fill_ext(extu, up_ref, uc_ref, un_ref, i, nt, T)
        _fill_ext(exty, yp_ref, yc_ref, yn_ref, i, nt, T)
        t = i * T + lax.broadcasted_iota(jnp.int32, (T, 1), 0)
        te = i * T - HALO + lax.broadcasted_iota(jnp.int32, (TE, 1), 0)
        for g, win in enumerate(POOL_WINDOWS):
            half = win // 2
            sl = slice(g * LANES, (g + 1) * LANES)
            w = w_ref[g]
            acc = extu[pl.ds(HALO - half, T), sl]
            for off in range(-half + 1, half):
                acc = acc + extu[pl.ds(HALO + off, T), sl]
            d = _mx(acc / _pool_cnt(t, half, S) - extu[pl.ds(HALO, T), sl])
            dy = exty[pl.ds(HALO, T), sl]
            dsc_ref[:, sl] += jnp.sum(dy * _dot(d, w), axis=0, keepdims=True)
            dw_ref[g] += _dot_tn(d, _mx(dy * sc_ref[:, sl]))
            dd = _dot_nt(_mx(exty[:, sl] * sc_ref[:, sl]), w)
            exte[:, sl] = dd / jnp.maximum(_pool_cnt(te, half, S), 1.0)
            acc = exte[pl.ds(HALO - half + 1, T), sl]
            for off in range(-half + 2, half + 1):
                acc = acc + exte[pl.ds(HALO + off, T), sl]
            du_ref[:, sl] = acc - dd[HALO:HALO + T, :]

    return _call(body, name="pool_bwd", grid=(nt,),
                 in_specs=_halo_specs(T, S, 512, 3) + _halo_specs(T, S, 512, 2)
                 + [pl.BlockSpec((4, LANES, LANES), lambda i: (0, 0, 0)), pl.BlockSpec((1, 512), lambda i: (0, 0))],
                 out_specs=[pl.BlockSpec((T, 512), lambda i: (i, 0)), pl.BlockSpec((4, LANES, LANES), lambda i: (0, 0, 0)),
                            pl.BlockSpec((1, 512), lambda i: (0, 0))],
                 out_shape=[jax.ShapeDtypeStruct((S, 512), F32), jax.ShapeDtypeStruct((4, LANES, LANES), F32),
                            jax.ShapeDtypeStruct((1, 512), F32)],
                 scratch=[pltpu.VMEM((TE, 512), F32)] * 3)(z, z, z, dycat, dycat, dycat, w_pool, scale)


MLA_HALF = 16
MLA_V_ONE = 64


def mla_prep(z, g_cq, g_ckv, w_uq, w_k, w_v, tabs_q, tabs_k, T=256):
    S = z.shape[0]
    T = min(T, S)

    def body(z_ref, gq_ref, gkv_ref, wq_ref, wk_ref, wv_ref, cq_ref, saq_ref, sbq_ref, ck_ref, sak_ref, sbk_ref,
             q_ref, k_ref, v_ref, nq_ref, nkv_ref):
        nq = _mx(_rms(z_ref[:, 0:256], gq_ref[...]))
        nkv = _mx(_rms(z_ref[:, 256:384], gkv_ref[...]))
        nq_ref[...] = nq
        nkv_ref[...] = nkv
        q = _dot(nq, wq_ref[...])
        kn = _dot(nkv, wk_ref[...])
        lane = lax.broadcasted_iota(jnp.int32, (T, 1024), 1)
        v_ref[...] = jnp.where(lane % LANES == MLA_V_ONE, 1.0, _dot(nkv, wv_ref[...])).astype(v_ref.dtype)
        kr = pltpu.roll(_rope(z_ref[:, 384:512], ck_ref[...], sak_ref[...], sbk_ref[...], MLA_HALF), 64, 1)
        cq, saq, sbq = cq_ref[...], saq_ref[...], sbq_ref[...]
        for h in range(8):
            hs = slice(h * LANES, (h + 1) * LANES)
            q_ref[:, hs] = (_rope(q[:, hs], cq, saq, sbq, MLA_HALF) * MLA_QSCALE).astype(q_ref.dtype)
            k_ref[:, hs] = (kn[:, hs] + kr).astype(k_ref.dtype)

    tab = pl.BlockSpec((T, LANES), lambda i: (i, 0))
    full = lambda a: pl.BlockSpec(a.shape, lambda i: (0, 0))
    row = lambda w: pl.BlockSpec((T, w), lambda i: (i, 0))
    sd = lambda w: jax.ShapeDtypeStruct((S, w), MXU_DTYPE)
    return _call(body, name="mla_prep", grid=(S // T,),
                 in_specs=[row(512), full(g_cq), full(g_ckv), full(w_uq), full(w_k), full(w_v)] + [tab] * 6,
                 out_specs=[row(1024), row(1024), row(1024), row(256), row(128)],
                 out_shape=[sd(1024), sd(1024), sd(1024), sd(256), sd(128)],
                 )(z, g_cq, g_ckv, w_uq, w_k, w_v, *tabs_q, *tabs_k)


def _col_to_row(c):
    return jnp.transpose(jnp.broadcast_to(c, (c.shape[0], LANES)))[0:1, :]


def mla_vt(v, TK=512):
    S = v.shape[0]
    TK = min(TK, S // 2)
    return v.reshape(S // TK, TK, 8, LANES).transpose(2, 0, 3, 1)


def mla_dq(dqt):
    _, nq, _, TQ = dqt.shape
    return dqt.transpose(1, 3, 0, 2).reshape(nq * TQ, 1024)


def mla_fwd(q, k, vt, TQ=512):
    S = q.shape[0]
    nk, TK = vt.shape[1], vt.shape[3]
    TQ = min(TQ, S)
    QB = min(256, TQ)
    assert nk % 2 == 0
    unroll = 4 if nk % 4 == 0 else 2

    def body(q_ref, k_ref, vt_ref, o_ref, lse_ref, m_s, acc_s, s_a, s_b):
        m_s[...] = jnp.full_like(m_s, -jnp.inf)
        acc_s[...] = jnp.zeros_like(acc_s)
        q = q_ref[...]

        def scores(c):
            return _dot_nt(k_ref[pl.ds(pl.multiple_of(c * TK, TK), TK), :], q)

        def softmax_pv(s_ref, c):
            vt_c = vt_ref[0, c]
            for b in range(TQ // QB):
                cols = slice(b * QB, (b + 1) * QB)
                s = s_ref[:, cols]
                m_old = m_s[:, cols]
                m_new = jnp.maximum(m_old, jnp.max(s, axis=0, keepdims=True))
                alpha = jnp.exp2(m_old[0:1, :] - m_new[0:1, :])
                p = jnp.exp2(s - m_new[0:1, :])
                acc_s[:, cols] = alpha * acc_s[:, cols] + _dot(vt_c, _mx(p))
                m_s[:, cols] = m_new

        s_a[...] = scores(0)
        bufs = (s_a, s_b)

        def group(jj, carry):
            c = unroll * jj
            for u in range(unroll):
                nxt = c + u + 1 if u < unroll - 1 else jnp.minimum(c + unroll, nk - 1)
                bufs[(u + 1) % 2][...] = scores(nxt)
                softmax_pv(bufs[u % 2], c + u)
            return carry

        lax.fori_loop(0, nk // unroll, group, 0)
        acc = acc_s[...]
        den = acc[MLA_V_ONE:MLA_V_ONE + 1, :]
        sub = lax.broadcasted_iota(jnp.int32, (LANES, TQ), 0)
        o_ref[...] = jnp.transpose(jnp.where(sub < MLA_V_ONE, acc / den, 0.0))
        lse_ref[0] = m_s[0:1, :] + jnp.log(den) * LOG2E

    qs = pl.BlockSpec((TQ, LANES), lambda h, i: (i, h))
    return _call(body, name="mla_fwd", grid=(8, S // TQ),
                 in_specs=[qs, pl.BlockSpec((S, LANES), lambda h, i: (0, h)),
                           pl.BlockSpec((1, nk, LANES, TK), lambda h, i: (h, 0, 0, 0))],
                 out_specs=[qs, pl.BlockSpec((1, 1, TQ), lambda h, i: (h, 0, i))],
                 out_shape=[jax.ShapeDtypeStruct((S, 1024), F32), jax.ShapeDtypeStruct((8, 1, S), F32)],
                 scratch=[pltpu.VMEM((8, TQ), F32), pltpu.VMEM((LANES, TQ), F32),
                          pltpu.VMEM((TK, TQ), F32), pltpu.VMEM((TK, TQ), F32)],
                 )(q, k, vt)


def mla_delta(o, dycat, TQ=512):
    S = o.shape[0]
    TQ = min(TQ, S)

    def body(o_ref, do_ref, d_ref, dob_ref):
        for h in range(8):
            hs = slice(h * LANES, (h + 1) * LANES)
            do = do_ref[:, hs]
            d_ref[h] = _col_to_row(jnp.sum(o_ref[:, hs] * do, axis=-1, keepdims=True))
            dob_ref[:, hs] = do.astype(dob_ref.dtype)

    qs = pl.BlockSpec((TQ, 1024), lambda i: (i, 0))
    return _call(body, name="mla_delta", grid=(S // TQ,), in_specs=[qs, qs],
                 out_specs=[pl.BlockSpec((8, 1, TQ), lambda i: (0, 0, i)), qs],
                 out_shape=[jax.ShapeDtypeStruct((8, 1, S), F32), jax.ShapeDtypeStruct((S, 1024), MXU_DTYPE)])(o, dycat)


def mla_bwd(q, k, v, do, lse, delta, TQ=512, TK=512):
    S = q.shape[0]
    TQ, TK = min(TQ, S // 2), min(TK, S)
    nq, nk = S // TQ, S // TK
    assert nq % 2 == 0
    unroll = 4 if nq % 4 == 0 else 2
    lse = lse.reshape(8, nq, 1, TQ)
    delta = delta.reshape(8, nq, 1, TQ)

    def body(q_ref, do_ref, k_ref, v_ref, lse_ref, d_ref, dqt_ref, dk_ref, dv_ref, dk_s, dv_s, s_a, p_a, s_b, p_b):
        @pl.when(pl.program_id(1) == 0)
        def _():
            dqt_ref[...] = jnp.zeros_like(dqt_ref)

        dk_s[...] = jnp.zeros_like(dk_s)
        dv_s[...] = jnp.zeros_like(dv_s)
        kk, vv = k_ref[...], v_ref[...]
        kt = _mx(jnp.transpose(kk.astype(F32)))

        def rows(c):
            return pl.ds(pl.multiple_of(c * TQ, TQ), TQ)

        def products(c, s_ref, p_ref):
            s_ref[...] = _dot_nt(kk, q_ref[rows(c), :])
            p_ref[...] = _dot_nt(vv, do_ref[rows(c), :])

        def consume(c, s_ref, p_ref):
            qc, doc = q_ref[rows(c), :], do_ref[rows(c), :]
            pt = jnp.exp2(s_ref[...] - lse_ref[0, c])
            dv_s[...] += _dot(_mx(pt), doc)
            ds = _mx(pt * (p_ref[...] - d_ref[0, c]))
            dk_s[...] += _dot(ds, qc)
            dqt_ref[0, c] += _dot(kt, ds)

        products(0, s_a, p_a)
        bufs = ((s_a, p_a), (s_b, p_b))

        def group(jj, carry):
            c = unroll * jj
            for u in range(unroll):
                nxt = c + u + 1 if u < unroll - 1 else jnp.minimum(c + unroll, nq - 1)
                products(nxt, *bufs[(u + 1) % 2])
                consume(c + u, *bufs[u % 2])
            return carry

        lax.fori_loop(0, nq // unroll, group, 0)
        dk_ref[...] = dk_s[...] * (1.0 / LOG2E)
        dv_ref[...] = dv_s[...]

    full = pl.BlockSpec((S, LANES), lambda h, j: (0, h))
    ks = pl.BlockSpec((TK, LANES), lambda h, j: (j, h))
    st = pl.BlockSpec((1, nq, 1, TQ), lambda h, j: (h, 0, 0, 0))
    sd = jax.ShapeDtypeStruct((S, 1024), F32)
    return _call(body, name="mla_bwd", grid=(8, nk), in_specs=[full, full, ks, ks, st, st],
                 out_specs=[pl.BlockSpec((1, nq, LANES, TQ), lambda h, j: (h, 0, 0, 0)), ks, ks],
                 out_shape=[jax.ShapeDtypeStruct((8, nq, LANES, TQ), F32), sd, sd],
                 scratch=[pltpu.VMEM((TK, LANES), F32), pltpu.VMEM((TK, LANES), F32)] + [pltpu.VMEM((TK, TQ), F32)] * 4,
                 )(q, do, k, v, lse, delta)


def mla_prep_bwd(z, g_cq, g_ckv, w_uq, w_k, w_v, dq, dk, dv, tabs_q, tabs_k, T=256):
    S = z.shape[0]
    T = min(T, S)

    def body(z_ref, gq_ref, gkv_ref, wq_ref, wk_ref, wv_ref, dq_ref, dk_ref, dv_ref,
             cq_ref, saq_ref, sbq_ref, ck_ref, sak_ref, sbk_ref, dz_ref, dqp_ref, dgq_ref, dgkv_ref):
        @pl.when(pl.program_id(0) == 0)
        def _():
            dgq_ref[...] = jnp.zeros_like(dgq_ref)
            dgkv_ref[...] = jnp.zeros_like(dgkv_ref)

        cq, saq, sbq = cq_ref[...], saq_ref[...], sbq_ref[...]
        dkr = jnp.zeros((T, LANES), F32)
        for h in range(8):
            hs = slice(h * LANES, (h + 1) * LANES)
            dqp_ref[:, hs] = _rope_t(dq_ref[:, hs] * MLA_SCALE, cq, saq, sbq, MLA_HALF).astype(dqp_ref.dtype)
            dkr = dkr + dk_ref[:, hs]
        lane = lax.broadcasted_iota(jnp.int32, (T, LANES), 1)
        dkr = jnp.where(lane < 2 * MLA_HALF, pltpu.roll(dkr, 64, 1), 0.0)
        dz_ref[:, 384:512] = _rope_t(dkr, ck_ref[...], sak_ref[...], sbk_ref[...], MLA_HALF)
        dnq = _dot_nt(dqp_ref[...], wq_ref[...])
        dx, dg = _rms_bwd(z_ref[:, 0:256], gq_ref[...], dnq)
        dz_ref[:, 0:256] = dx
        dgq_ref[...] += dg
        dnkv = _dot_nt(_mx(dk_ref[...]), wk_ref[...]) + _dot_nt(_mx(dv_ref[...]), wv_ref[...])
        dx, dg = _rms_bwd(z_ref[:, 256:384], gkv_ref[...], dnkv)
        dz_ref[:, 256:384] = dx
        dgkv_ref[...] += dg

    tab = pl.BlockSpec((T, LANES), lambda i: (i, 0))
    full = lambda a: pl.BlockSpec(a.shape, lambda i: (0, 0))
    row = lambda w: pl.BlockSpec((T, w), lambda i: (i, 0))
    return _call(body, name="mla_prep_bwd", grid=(S // T,),
                 in_specs=[row(512), full(g_cq), full(g_ckv), full(w_uq), full(w_k), full(w_v),
                           row(1024), row(1024), row(1024)] + [tab] * 6,
                 out_specs=[row(512), row(1024), full(g_cq), full(g_ckv)],
                 out_shape=[jax.ShapeDtypeStruct((S, 512), F32), jax.ShapeDtypeStruct((S, 1024), MXU_DTYPE),
                            jax.ShapeDtypeStruct(g_cq.shape, F32), jax.ShapeDtypeStruct(g_ckv.shape, F32)],
                 )(z, g_cq, g_ckv, w_uq, w_k, w_v, dq, dk, dv, *tabs_q, *tabs_k)


def _lru_gates(xc, w_ref, bias_ref, lam_ref):
    pre = _dot(_mx(xc), w_ref[...]) + bias_ref[...]
    out = []
    for d in range(2):
        r = _sigmoid(pre[:, d * 1024:d * 1024 + 512])
        ig = _sigmoid(pre[:, d * 1024 + 512:(d + 1) * 1024])
        log_a = -LRU_C * r * _softplus(-lam_ref[:, d * 512:(d + 1) * 512])
        out.append((r, ig, jnp.exp(log_a), jnp.sqrt(-_expm1(2.0 * log_a))))
    return out


def lru_pre(z, conv_w, conv_b, w_gate, b_gate, lam, T=256):
    S = z.shape[0]
    T = min(T, S)
    nt = S // T

    def body(xp_ref, xcur_ref, xn_ref, cw_ref, cb_ref, w_ref, bias_ref, lam_ref, xc_ref, a0_ref, b0_ref, a1_ref, b1_ref, ext):
        i = pl.program_id(0)
        _fill_ext(ext, xp_ref, xcur_ref, xn_ref, i, nt, T)
        xc = cb_ref[...] + cw_ref[0:1, :] * ext[pl.ds(HALO - 2, T), :]
        for j in range(1, 4):
            xc = xc + cw_ref[j:j + 1, :] * ext[pl.ds(HALO - 2 + j, T), :]
        xc_ref[...] = xc
        (_, i0, a0, m0), (_, i1, a1, m1) = _lru_gates(xc, w_ref, bias_ref, lam_ref)
        a0_ref[...] = a0
        b0_ref[...] = m0 * (i0 * xc)
        a1_ref[...] = a1
        b1_ref[...] = m1 * (i1 * xc)

    full = lambda a: pl.BlockSpec(a.shape, lambda i: (0, 0))
    row = pl.BlockSpec((T, 512), lambda i: (i, 0))
    sd = jax.ShapeDtypeStruct((S, 512), F32)
    return _call(body, name="lru_pre", grid=(nt,),
                 in_specs=_halo_specs(T, S, 512, 1) + [full(conv_w), full(conv_b), full(w_gate), full(b_gate), full(lam)],
                 out_specs=[row] * 5, out_shape=[sd] * 5,
                 scratch=[pltpu.VMEM((T + 2 * HALO, 512), F32)])(z, z, z, conv_w, conv_b, w_gate, b_gate, lam)


def lru_scan(name, af, bf, ar, br, *, adjoint, T=256):
    S, W = af.shape
    T = min(T, S)
    nt = S // T
    nc = T // 8

    def body(af_ref, bf_ref, ar_ref, br_ref, hf_ref, hr_ref, cf, cr):
        @pl.when(pl.program_id(0) == 0)
        def _():
            cf[...] = jnp.zeros_like(cf)
            cr[...] = jnp.zeros_like(cr)

        row = lax.broadcasted_iota(jnp.int32, (8, W), 0)

        def step(a, b, carry):
            if adjoint:
                val = b + carry
                return val, a * val
            val = a * carry + b
            return val, val

        def chunk(c, carry):
            hf, hr = carry
            of = pl.multiple_of(c * 8, 8)
            orv = pl.multiple_of((nc - 1 - c) * 8, 8)
            a8, b8 = af_ref[pl.ds(of, 8), :], bf_ref[pl.ds(of, 8), :]
            ra8, rb8 = ar_ref[pl.ds(orv, 8), :], br_ref[pl.ds(orv, 8), :]
            outf = jnp.zeros((8, W), F32)
            outr = jnp.zeros((8, W), F32)
            for k in range(8):
                val, hf = step(a8[k:k + 1, :], b8[k:k + 1, :], hf)
                outf = jnp.where(row == k, val, outf)
                kr = 7 - k
                val, hr = step(ra8[kr:kr + 1, :], rb8[kr:kr + 1, :], hr)
                outr = jnp.where(row == kr, val, outr)
            hf_ref[pl.ds(of, 8), :] = outf
            hr_ref[pl.ds(orv, 8), :] = outr
            return hf, hr

        hf, hr = lax.fori_loop(0, nc, chunk, (cf[0:1, :], cr[0:1, :]))
        cf[0:1, :] = hf
        cr[0:1, :] = hr

    fw = pl.BlockSpec((T, W), lambda i: (i, 0))
    rv = pl.BlockSpec((T, W), lambda i: (nt - 1 - i, 0))
    sd = jax.ShapeDtypeStruct((S, W), F32)
    return _call(body, name=name, grid=(nt,), in_specs=[fw, fw, rv, rv], out_specs=[fw, rv], out_shape=[sd, sd],
                 scratch=[pltpu.VMEM((8, W), F32), pltpu.VMEM((8, W), F32)])(af, bf, ar, br)


def lru_gate(h0, h1, z, T=256):
    S = z.shape[0]
    T = min(T, S)

    def body(h0_ref, h1_ref, xg_ref, y_ref):
        y_ref[...] = ((h0_ref[...] + h1_ref[...]) * _gelu(xg_ref[...])).astype(y_ref.dtype)

    row = pl.BlockSpec((T, 512), lambda i: (i, 0))
    return _call(body, name="lru_gate", grid=(S // T,), in_specs=[row, row, pl.BlockSpec((T, 512), lambda i: (i, 2))],
                 out_specs=row, out_shape=jax.ShapeDtypeStruct((S, 512), MXU_DTYPE))(h0, h1, z)


def lru_gate_bwd(h0, h1, z, dycat, T=256):
    S = z.shape[0]
    T = min(T, S)

    def body(h0_ref, h1_ref, xg_ref, dy_ref, dxg_ref, dh_ref):
        xg, dy = xg_ref[...], dy_ref[...]
        dxg_ref[...] = dy * (h0_ref[...] + h1_ref[...]) * _gelu_grad(xg)
        dh_ref[...] = dy * _gelu(xg)

    row = pl.BlockSpec((T, 512), lambda i: (i, 0))
    col2 = pl.BlockSpec((T, 512), lambda i: (i, 2))
    sd = jax.ShapeDtypeStruct((S, 512), F32)
    return _call(body, name="lru_gate_bwd", grid=(S // T,), in_specs=[row, row, col2, col2],
                 out_specs=[row, row], out_shape=[sd, sd])(h0, h1, z, dycat)


def lru_bwd_point(xc, h0, h1, g0, g1, w_gate, b_gate, lam, T=256):
    S = xc.shape[0]
    T = min(T, S)
    nt = S // T

    def body(xc_ref, h0p_ref, h0_ref, h0n_ref, h1p_ref, h1_ref, h1n_ref, g0_ref, g1_ref, w_ref, bias_ref, lam_ref,
             dxc_ref, dpre_ref, dbias_ref, dlam_ref, ext0, ext1):
        i = pl.program_id(0)

        @pl.when(i == 0)
        def _():
            dbias_ref[...] = jnp.zeros_like(dbias_ref)
            dlam_ref[...] = jnp.zeros_like(dlam_ref)

        _fill_ext(ext0, h0p_ref, h0_ref, h0n_ref, i, nt, T)
        _fill_ext(ext1, h1p_ref, h1_ref, h1n_ref, i, nt, T)
        xc = xc_ref[...]
        gates = _lru_gates(xc, w_ref, bias_ref, lam_ref)
        hshift = (ext0[pl.ds(HALO - 1, T), :], ext1[pl.ds(HALO + 1, T), :])
        gs = (g0_ref[...], g1_ref[...])
        dxc = jnp.zeros((T, 512), F32)
        for d in range(2):
            r, ig, a, mult = gates[d]
            db = gs[d]
            da = db * hshift[d]
            dmult = db * (ig * xc)
            di = db * (mult * xc)
            dxc = dxc + db * (mult * ig)
            dloga = da * a - dmult * (a * a / mult)
            lam_d = lam_ref[:, d * 512:(d + 1) * 512]
            dr = dloga * (-LRU_C * _softplus(-lam_d))
            dsp = jnp.sum(dloga * (-LRU_C * r), axis=0, keepdims=True)
            dlam_ref[:, d * 512:(d + 1) * 512] += dsp * (-_sigmoid(-lam_d))
            dpre_ref[:, d * 1024:d * 1024 + 512] = (dr * (r * (1.0 - r))).astype(dpre_ref.dtype)
            dpre_ref[:, d * 1024 + 512:(d + 1) * 1024] = (di * (ig * (1.0 - ig))).astype(dpre_ref.dtype)
            dbias_ref[:, d * 1024:d * 1024 + 512] += jnp.sum(dr * (r * (1.0 - r)), axis=0, keepdims=True)
            dbias_ref[:, d * 1024 + 512:(d + 1) * 1024] += jnp.sum(di * (ig * (1.0 - ig)), axis=0, keepdims=True)
        dxc_ref[...] = dxc + _dot_nt(dpre_ref[...], w_ref[...])

    full = lambda a: pl.BlockSpec(a.shape, lambda i: (0, 0))
    row = pl.BlockSpec((T, 512), lambda i: (i, 0))
    return _call(body, name="lru_bwd_point", grid=(nt,),
                 in_specs=[row] + _halo_specs(T, S, 512, 0) + _halo_specs(T, S, 512, 0) + [row, row, full(w_gate), full(b_gate), full(lam)],
                 out_specs=[row, pl.BlockSpec((T, 2048), lambda i: (i, 0)), full(b_gate), full(lam)],
                 out_shape=[jax.ShapeDtypeStruct((S, 512), F32), jax.ShapeDtypeStruct((S, 2048), MXU_DTYPE),
                            jax.ShapeDtypeStruct(b_gate.shape, F32), jax.ShapeDtypeStruct(lam.shape, F32)],
                 scratch=[pltpu.VMEM((T + 2 * HALO, 512), F32)] * 2,
                 )(xc, h0, h0, h0, h1, h1, h1, g0, g1, w_gate, b_gate, lam)


def conv_bwd(z, dxc, conv_w, T=256):
    S = z.shape[0]
    T = min(T, S)
    nt = S // T

    def body(xp_ref, xcur_ref, xn_ref, dp_ref, dcur_ref, dn_ref, cw_ref, dx_ref, dw_ref, db_ref, extx, extd):
        i = pl.program_id(0)

        @pl.when(i == 0)
        def _():
            dw_ref[...] = jnp.zeros_like(dw_ref)
            db_ref[...] = jnp.zeros_like(db_ref)

        _fill_ext(extx, xp_ref, xcur_ref, xn_ref, i, nt, T)
        _fill_ext(extd, dp_ref, dcur_ref, dn_ref, i, nt, T)
        d = extd[pl.ds(HALO, T), :]
        dx = cw_ref[0:1, :] * extd[pl.ds(HALO + 2, T), :]
        for j in range(1, 4):
            dx = dx + cw_ref[j:j + 1, :] * extd[pl.ds(HALO + 2 - j, T), :]
        dx_ref[...] = dx
        for j in range(4):
            dw_ref[j:j + 1, :] += jnp.sum(d * extx[pl.ds(HALO - 2 + j, T), :], axis=0, keepdims=True)
        db_ref[...] += jnp.sum(d, axis=0, keepdims=True)

    full = lambda a: pl.BlockSpec(a.shape, lambda i: (0, 0))
    row = pl.BlockSpec((T, 512), lambda i: (i, 0))
    return _call(body, name="conv_bwd", grid=(nt,),
                 in_specs=_halo_specs(T, S, 512, 1) + _halo_specs(T, S, 512, 0) + [full(conv_w)],
                 out_specs=[row, full(conv_w), pl.BlockSpec((1, 512), lambda i: (0, 0))],
                 out_shape=[jax.ShapeDtypeStruct((S, 512), F32), jax.ShapeDtypeStruct(conv_w.shape, F32),
                            jax.ShapeDtypeStruct((1, 512), F32)],
                 scratch=[pltpu.VMEM((T + 2 * HALO, 512), F32)] * 2)(z, z, z, dxc, dxc, dxc, conv_w)


MESH_ID = pl.DeviceIdType.MESH
ANY = pl.BlockSpec(memory_space=pl.ANY)


def _place():
    x, y, c = lax.axis_index("x"), lax.axis_index("y"), lax.axis_index("c")
    chips = [(1 - x, y), (x, 1 - y), (1 - x, 1 - y)]
    return x, y, c, chips


def _slot(px, py, pc):
    return 4 * px + 2 * py + pc


def all_gather(arrays):
    n = len(arrays)

    def body(*refs):
        ins, outs = refs[:n], refs[n:2 * n]
        send, recv, loc = refs[2 * n:]
        x, y, c, chips = _place()
        me, sibling = (x, y, c), (x, y, 1 - c)

        def copy(a, k, block, to, src=None):
            slot = outs[a].at[_slot(*block)]
            return pltpu.make_async_remote_copy(src_ref=slot if src is None else src, dst_ref=slot,
                                                send_sem=send.at[a * 7 + k], recv_sem=recv.at[a * 7 + k],
                                                device_id=to, device_id_type=MESH_ID)

        local = [pltpu.make_async_copy(ins[a], outs[a].at[_slot(*me)], loc.at[a]) for a in range(n)]
        for cp in local:
            cp.start()
        first = []
        for a in range(n):
            first.append(copy(a, 0, me, sibling, src=ins[a]))
            first += [copy(a, 1 + j, me, (*chip, c), src=ins[a]) for j, chip in enumerate(chips)]
        for cp in first:
            cp.start()
        passed = []
        for a in range(n):
            for j, chip in enumerate(chips):
                copy(a, 1 + j, (*chip, c), me).wait_recv()
                cp = copy(a, 4 + j, (*chip, c), sibling)
                cp.start()
                passed.append(cp)
        for a in range(n):
            copy(a, 0, sibling, me).wait_recv()
            for j, chip in enumerate(chips):
                copy(a, 4 + j, (*chip, 1 - c), me).wait_recv()
        for cp in first + passed:
            cp.wait_send()
        for cp in local:
            cp.wait()

    return pl.pallas_call(
        body, name="all_gather", in_specs=[ANY] * n, out_specs=[ANY] * n,
        out_shape=[jax.ShapeDtypeStruct((NDEV,) + a.shape, a.dtype) for a in arrays],
        scratch_shapes=[pltpu.SemaphoreType.DMA((n * 7,)), pltpu.SemaphoreType.DMA((n * 7,)),
                        pltpu.SemaphoreType.DMA((n,))],
    )(*arrays)


def grad_pair(gs):
    def body(gs_ref, o_ref, send, recv):
        x, y, c, _ = _place()
        cps = [pltpu.make_async_remote_copy(src_ref=gs_ref.at[2 * q + 1], dst_ref=o_ref.at[q], send_sem=send.at[q],
                                            recv_sem=recv.at[q], device_id=(x, y, 1 - c), device_id_type=MESH_ID)
               for q in range(4)]
        for cp in cps:
            cp.start()
        for cp in cps:
            cp.wait_recv()
        for cp in cps:
            cp.wait_send()

    return pl.pallas_call(
        body, name="grad_pair", in_specs=[ANY], out_specs=ANY,
        out_shape=jax.ShapeDtypeStruct((4,) + gs.shape[1:], gs.dtype),
        scratch_shapes=[pltpu.SemaphoreType.DMA((4,)), pltpu.SemaphoreType.DMA((4,))],
    )(gs)


def pair_add(gs, got, T=128):
    _, R, C = gs.shape
    T = min(T, R)

    def body(g0, g1, g2, g3, got_ref, own_ref, out_ref):
        own_ref[...] = g0[0] + got_ref[0]
        for q, g in enumerate((g1, g2, g3)):
            out_ref[q] = (g[0] + got_ref[q + 1]).astype(out_ref.dtype)

    even = lambda q: pl.BlockSpec((1, T, C), lambda i: (2 * q, i, 0))
    return _call(body, name="pair_add", grid=(R // T,),
                 in_specs=[even(0), even(1), even(2), even(3), pl.BlockSpec((4, T, C), lambda i: (0, i, 0))],
                 out_specs=[pl.BlockSpec((T, C), lambda i: (i, 0)), pl.BlockSpec((3, T, C), lambda i: (0, i, 0))],
                 out_shape=[jax.ShapeDtypeStruct((R, C), F32), jax.ShapeDtypeStruct((3, R, C), MXU_DTYPE)],
                 )(gs, gs, gs, gs, got)


def grad_cross(part, gr):
    def body(p_ref, gr_ref, o_ref, or_ref, send, recv, loc):
        x, y, c, chips = _place()
        peers = [(x, y, 1 - c)] + [(*chip, c) for chip in chips] + [(*chip, 1 - c) for chip in chips]
        local = pltpu.make_async_copy(gr_ref, or_ref.at[_slot(x, y, c)], loc.at[0])
        local.start()
        by_offset = [(x, 1 - y), (1 - x, y), (1 - x, 1 - y)]
        cps = [pltpu.make_async_remote_copy(src_ref=p_ref.at[j], dst_ref=o_ref.at[j], send_sem=send.at[j],
                                            recv_sem=recv.at[j], device_id=(*chip, c), device_id_type=MESH_ID)
               for j, chip in enumerate(by_offset)]
        cps += [pltpu.make_async_remote_copy(src_ref=gr_ref, dst_ref=or_ref.at[_slot(x, y, c)], send_sem=send.at[3 + k],
                                             recv_sem=recv.at[3 + k], device_id=peer, device_id_type=MESH_ID)
                for k, peer in enumerate(peers)]
        for cp in cps:
            cp.start()
        for cp in cps[:3]:
            cp.wait_recv()
        for k, peer in enumerate(peers):
            pltpu.make_async_remote_copy(src_ref=gr_ref, dst_ref=or_ref.at[_slot(*peer)], send_sem=send.at[3 + k],
                                         recv_sem=recv.at[3 + k], device_id=peer, device_id_type=MESH_ID).wait_recv()
        for cp in cps:
            cp.wait_send()
        local.wait()

    return pl.pallas_call(
        body, name="grad_cross", in_specs=[ANY, ANY], out_specs=[ANY, ANY],
        out_shape=[jax.ShapeDtypeStruct(part.shape, part.dtype), jax.ShapeDtypeStruct((NDEV,) + gr.shape, gr.dtype)],
        scratch_shapes=[pltpu.SemaphoreType.DMA((10,)), pltpu.SemaphoreType.DMA((10,)), pltpu.SemaphoreType.DMA((1,))],
    )(part, gr)


def adamw(name, gparts, w, m, v, T=128):
    R, C = w.shape
    T = min(T, R)
    ng = len(gparts)

    def body(*refs):
        g_refs = refs[:ng]
        w_ref, m_ref, v_ref, go_ref, d_ref, mo_ref, vo_ref = refs[ng:]
        g = None
        for g_ref in g_refs:
            for k in range(g_ref.shape[0]):
                t = g_ref[k].astype(F32)
                g = t if g is None else g + t
        mn = ADAM_B1 * m_ref[...] + (1.0 - ADAM_B1) * g
        vn = ADAM_B2 * v_ref[...] + (1.0 - ADAM_B2) * (g * g)
        m_hat = mn / (1.0 - ADAM_B1 ** ADAM_STEP)
        v_hat = vn / (1.0 - ADAM_B2 ** ADAM_STEP)
        go_ref[...] = g
        d_ref[...] = -ADAM_LR * (m_hat / (jnp.sqrt(v_hat) + ADAM_EPS) + ADAM_WD * w_ref[...])
        mo_ref[...] = mn
        vo_ref[...] = vn

    row = pl.BlockSpec((T, C), lambda i: (i, 0))
    sd = jax.ShapeDtypeStruct((R, C), F32)
    return _call(body, name=name, grid=(R // T,),
                 in_specs=[pl.BlockSpec((g.shape[0], T, C), lambda i: (0, i, 0)) for g in gparts] + [row, row, row],
                 out_specs=[row] * 4, out_shape=[sd] * 4)(*gparts, w, m, v)


PACK_W = 1024
PACK_TILE = 16
PACK_ROWS = 128


def _entry_rows(shape):
    n = 1
    for s in shape:
        n *= s
    return n, -(-n // (PACK_TILE * PACK_W)) * PACK_TILE


def _pack(arrays, dtype):
    mats, rows = [], 0
    for a in arrays:
        n, r = _entry_rows(a.shape)
        flat = a.astype(dtype).reshape(-1)
        if r * PACK_W != n:
            flat = jnp.pad(flat, (0, r * PACK_W - n))
        mats.append(flat.reshape(r, PACK_W))
        rows += r
    total = -(-rows // PACK_ROWS) * PACK_ROWS
    if total > rows:
        mats.append(jnp.zeros((total - rows, PACK_W), dtype))
    return jnp.concatenate(mats, axis=0)


def _unpack(buf, shapes, lead=()):
    out, off = [], 0
    for sh in shapes:
        n, r = _entry_rows(sh)
        piece = buf[..., off:off + r, :].reshape(lead + (r * PACK_W,))
        out.append(piece[..., :n].reshape(lead + tuple(sh)))
        off += r
    return out


def _unshard(parts, axis):
    return jnp.concatenate([parts[d] for d in range(NDEV)], axis=axis)


def _slab_cols(w, head, used, n_heads, slab=LANES):
    lead = w.shape[:-1]
    w = w.reshape(lead + (n_heads, head))[..., :used]
    w = jnp.pad(w, [(0, 0)] * len(lead) + [(0, 0), (0, slab - used)])
    return w.reshape(lead + (n_heads * slab,))


def _unslab_cols(g, used, n_heads, slab=LANES):
    lead = g.shape[:-1]
    return g.reshape(lead + (n_heads, slab))[..., :used].reshape(lead + (n_heads * used,))


def _block_diag(w):
    eye = jnp.eye(8, dtype=w.dtype)
    return jnp.einsum("nij,nm->nimj", w, eye).reshape(512, 512)


def _block_diag_t(g):
    g = g.reshape(8, 64, 8, 64)
    return jnp.stack([g[n, :, n, :] for n in range(8)])


BIG = (("e_w_in", 2), ("e_w_out", 1), ("o_w_in", 2), ("o_w_uq", 2), ("o_w_ukv", 2), ("o_w_out", 1),
       ("w_mlp1", 2), ("w_mlp2", 1))
SMALL = (("o_norm_mix", 1), ("o_g_cq", 1), ("o_conv_w", 2), ("o_conv_b", 1), ("o_lru_ba", 2), ("o_lru_bx", 2),
         ("o_lru_lambda", 2))
REPL = ("e_norm_mix", "e_sink", "e_w_pool", "e_pool_scale", "o_g_ckv", "o_lru_wa", "o_lru_wx", "norm_mlp",
        "final_norm")
WEIGHTS = ("e_norm_mix", "e_w_in", "e_sink", "e_w_pool", "e_pool_scale", "e_w_out", "o_norm_mix", "o_w_in", "o_g_cq",
           "o_w_uq", "o_g_ckv", "o_w_ukv", "o_conv_w", "o_conv_b", "o_lru_wa", "o_lru_ba", "o_lru_wx", "o_lru_bx",
           "o_lru_lambda", "o_w_out", "norm_mlp", "w_mlp1", "w_mlp2", "final_norm")


def _mlp_fwd(l, x, g, w1, w2):
    u1, hn = mm_nn(f"mlp1_{l}", [x], [w1], norm_g=g, emit_norm=True, out_dtype=MXU_DTYPE, tm=1024, tn=1024)
    x2 = mm_nn(f"mlp2_{l}", [u1], [w2], act="relu2", res=x, tm=512, tn=1024)
    return x2, (x, u1, hn)


def _mlp_bwd(l, saved, g, w1, w2, dx2):
    x, u1, hn = saved
    du1 = mm_nt(f"mlp2_dx_{l}", [dx2], [w2], relu2_of=u1, out_dtype=MXU_DTYPE, tm=1024, tn=1024)
    dw2 = mm_tn(f"mlp2_dw_{l}", u1, dx2, act="relu2", tm=1024, tn=1024)
    dhn = mm_nt(f"mlp1_dx_{l}", [du1], [w1], tm=512, tn=1024)
    dw1 = mm_tn(f"mlp1_dw_{l}", hn, du1, tm=1024, tn=1024)
    dx, dg = rms_bwd(f"mlp_norm_bwd_{l}", x, g, dhn, dx2)
    return dx, dg, dw1, dw2


def kernel(x, e_norm_mix, e_w_in, e_sink, e_w_pool, e_pool_scale, e_w_out, o_norm_mix, o_w_in, o_g_cq, o_w_uq, o_g_ckv, o_w_ukv, o_conv_w, o_conv_b, o_lru_wa, o_lru_ba, o_lru_wx, o_lru_bx, o_lru_lambda, o_w_out, norm_mlp, w_mlp1, w_mlp2, final_norm, loss_target, m_e_norm_mix, m_e_w_in, m_e_sink, m_e_w_pool, m_e_pool_scale, m_e_w_out, m_o_norm_mix, m_o_w_in, m_o_g_cq, m_o_w_uq, m_o_g_ckv, m_o_w_ukv, m_o_conv_w, m_o_conv_b, m_o_lru_wa, m_o_lru_ba, m_o_lru_wx, m_o_lru_bx, m_o_lru_lambda, m_o_w_out, m_norm_mlp, m_w_mlp1, m_w_mlp2, m_final_norm, v_e_norm_mix, v_e_w_in, v_e_sink, v_e_w_pool, v_e_pool_scale, v_e_w_out, v_o_norm_mix, v_o_w_in, v_o_g_cq, v_o_w_uq, v_o_g_ckv, v_o_w_ukv, v_o_conv_w, v_o_conv_b, v_o_lru_wa, v_o_lru_ba, v_o_lru_wx, v_o_lru_bx, v_o_lru_lambda, v_o_w_out, v_norm_mlp, v_w_mlp1, v_w_mlp2, v_final_norm):
    W = dict(e_norm_mix=e_norm_mix, e_w_in=e_w_in, e_sink=e_sink, e_w_pool=e_w_pool, e_pool_scale=e_pool_scale, e_w_out=e_w_out, o_norm_mix=o_norm_mix, o_w_in=o_w_in, o_g_cq=o_g_cq, o_w_uq=o_w_uq, o_g_ckv=o_g_ckv, o_w_ukv=o_w_ukv, o_conv_w=o_conv_w, o_conv_b=o_conv_b, o_lru_wa=o_lru_wa, o_lru_ba=o_lru_ba, o_lru_wx=o_lru_wx, o_lru_bx=o_lru_bx, o_lru_lambda=o_lru_lambda, o_w_out=o_w_out, norm_mlp=norm_mlp, w_mlp1=w_mlp1, w_mlp2=w_mlp2, final_norm=final_norm)
    Mo = dict(e_norm_mix=m_e_norm_mix, e_w_in=m_e_w_in, e_sink=m_e_sink, e_w_pool=m_e_w_pool, e_pool_scale=m_e_pool_scale, e_w_out=m_e_w_out, o_norm_mix=m_o_norm_mix, o_w_in=m_o_w_in, o_g_cq=m_o_g_cq, o_w_uq=m_o_w_uq, o_g_ckv=m_o_g_ckv, o_w_ukv=m_o_w_ukv, o_conv_w=m_o_conv_w, o_conv_b=m_o_conv_b, o_lru_wa=m_o_lru_wa, o_lru_ba=m_o_lru_ba, o_lru_wx=m_o_lru_wx, o_lru_bx=m_o_lru_bx, o_lru_lambda=m_o_lru_lambda, o_w_out=m_o_w_out, norm_mlp=m_norm_mlp, w_mlp1=m_w_mlp1, w_mlp2=m_w_mlp2, final_norm=m_final_norm)
    Vo = dict(e_norm_mix=v_e_norm_mix, e_w_in=v_e_w_in, e_sink=v_e_sink, e_w_pool=v_e_w_pool, e_pool_scale=v_e_pool_scale, e_w_out=v_e_w_out, o_norm_mix=v_o_norm_mix, o_w_in=v_o_w_in, o_g_cq=v_o_g_cq, o_w_uq=v_o_w_uq, o_g_ckv=v_o_g_ckv, o_w_ukv=v_o_w_ukv, o_conv_w=v_o_conv_w, o_conv_b=v_o_conv_b, o_lru_wa=v_o_lru_wa, o_lru_ba=v_o_lru_ba, o_lru_wx=v_o_lru_wx, o_lru_bx=v_o_lru_bx, o_lru_lambda=v_o_lru_lambda, o_w_out=v_o_w_out, norm_mlp=v_norm_mlp, w_mlp1=v_w_mlp1, w_mlp2=v_w_mlp2, final_norm=v_final_norm)

    S = x.shape[1]
    x0 = x[0]
    target = loss_target[0]

    big_g, small_g = all_gather([_pack([W[n] for n, _ in BIG], MXU_DTYPE), _pack([W[n] for n, _ in SMALL], F32)])
    full = {}
    for (n, ax), parts in zip(BIG, _unpack(big_g, [W[n].shape for n, _ in BIG], (NDEV,))):
        full[n] = _unshard(parts, ax)
    for (n, ax), parts in zip(SMALL, _unpack(small_g, [W[n].shape for n, _ in SMALL], (NDEV,))):
        full[n] = _unshard(parts, ax)

    def even_in(w):
        return jnp.concatenate([_slab_cols(w[:, 0:512], 64, 64, 8), _slab_cols(w[:, 512:640], 64, 64, 2),
                                _slab_cols(w[:, 640:768], 64, 64, 2), w[:, 768:1280]], axis=1)

    def even_in_t(g):
        return jnp.concatenate([_unslab_cols(g[:, 0:1024], 64, 8), _unslab_cols(g[:, 1024:1280], 64, 2),
                                _unslab_cols(g[:, 1280:1536], 64, 2), g[:, 1536:2048]], axis=1)

    def slab_rows(w, n_heads):
        return _slab_cols(w.T, 64, 64, n_heads).T

    def unslab_rows(g, n_heads):
        return _unslab_cols(g.T, 64, n_heads).T

    def odd_in(w):
        return jnp.concatenate([w[:, 0:384], jnp.pad(w[:, 384:416], ((0, 0), (0, 96))), w[:, 416:1440]], axis=1)

    def odd_in_t(g):
        return jnp.concatenate([g[:, 0:416], g[:, 512:1536]], axis=1)

    def uq(w):
        return _slab_cols(w, 96, 96, 8)

    def ukv(w):
        w = w.reshape(128, 8, 128)
        pad = lambda t: jnp.pad(t, ((0, 0), (0, 0), (0, 64))).reshape(128, 1024)
        return pad(w[:, :, :64]), pad(w[:, :, 64:])

    def ukv_t(gk, gv):
        gk = gk.reshape(128, 8, 128)[:, :, :64]
        gv = gv.reshape(128, 8, 128)[:, :, :64]
        return jnp.concatenate([gk, gv], axis=2).reshape(128, 1024)

    tabs_swa = rope_tables(S, 0, SWA_HALF)
    tabs_mq = rope_tables(S, 64, MLA_HALF)
    tabs_mk = rope_tables(S, 0, MLA_HALF)
    row = lambda v: v.reshape(1, -1)

    saved = []
    xcur = x0
    for l in range(4):
        j = l // 2
        if l % 2 == 0:
            w_in = even_in(full["e_w_in"][j])
            w_out = full["e_w_out"][j]
            w_out_a, w_out_b = slab_rows(w_out[0:512], 8), w_out[512:1024]
            w_pool = W["e_w_pool"][j].astype(MXU_DTYPE)
            z, h = mm_nn(f"e_in_{j}", [xcur], [w_in], norm_g=row(W["e_norm_mix"][j]), emit_norm=True, tm=256, tn=1024)
            qkv = swa_prep(z, tabs_swa)
            ya, lse = swa_fwd(qkv, W["e_sink"][j])
            yb = pool_fwd(z, w_pool, row(W["e_pool_scale"][j]))
            x1 = mm_nn(f"e_out_{j}", [ya, yb], [w_out_a, w_out_b], res=xcur)
            mix = (xcur, z, h, qkv, lse, ya, yb, w_in, w_out_a, w_out_b, w_pool)
        else:
            w_in = odd_in(full["o_w_in"][j])
            w_out = full["o_w_out"][j]
            w_out_a, w_out_b = slab_rows(w_out[0:512], 8), w_out[512:1024]
            w_uq = uq(full["o_w_uq"][j])
            w_k, w_v = ukv(full["o_w_ukv"][j])
            g_cq, g_ckv = row(full["o_g_cq"][j]), row(W["o_g_ckv"][j])
            w_gate = jnp.concatenate([_block_diag(W["o_lru_wa"][j, 0]), _block_diag(W["o_lru_wx"][j, 0]),
                                      _block_diag(W["o_lru_wa"][j, 1]), _block_diag(W["o_lru_wx"][j, 1])], axis=1).astype(MXU_DTYPE)
            b_gate = jnp.concatenate([full["o_lru_ba"][j, 0], full["o_lru_bx"][j, 0], full["o_lru_ba"][j, 1],
                                      full["o_lru_bx"][j, 1]]).reshape(1, 2048)
            lam = full["o_lru_lambda"][j].reshape(1, 1024)
            conv_w, conv_b = full["o_conv_w"][j], row(full["o_conv_b"][j])
            z, h = mm_nn(f"o_in_{j}", [xcur], [w_in], norm_g=row(full["o_norm_mix"][j]), emit_norm=True, tm=256, tn=512)
            q, k, v, nq, nkv = mla_prep(z, g_cq, g_ckv, w_uq, w_k, w_v, tabs_mq, tabs_mk)
            yc, lse = mla_fwd(q, k, mla_vt(v), TQ=1024)
            xc, a0, b0, a1, b1 = lru_pre(z, conv_w, conv_b, w_gate, b_gate, lam)
            h0, h1 = lru_scan(f"lru_scan_fwd_{j}", a0, b0, a1, b1, adjoint=False)
            yd = lru_gate(h0, h1, z)
            x1 = mm_nn(f"o_out_{j}", [yc, yd], [w_out_a, w_out_b], res=xcur)
            mix = (xcur, z, h, q, k, v, nq, nkv, yc, lse, xc, a0, a1, h0, h1, yd, w_in, w_out_a, w_out_b, w_uq, w_k, w_v,
                   g_cq, g_ckv, w_gate, b_gate, lam, conv_w)
        xcur, mlp = _mlp_fwd(l, x1, row(W["norm_mlp"][l]), full["w_mlp1"][l], full["w_mlp2"][l])
        saved.append((mix, mlp))

    loss_row, dx, dg_final = loss_head(xcur, row(W["final_norm"]), target)

    G = {n: [None] * W[n].shape[0] for n in WEIGHTS if n != "final_norm"}
    G["final_norm"] = dg_final.reshape(-1)
    for l in reversed(range(4)):
        j = l // 2
        mix, mlp = saved[l]
        dx, dg, dw1, dw2 = _mlp_bwd(l, mlp, row(W["norm_mlp"][l]), full["w_mlp1"][l], full["w_mlp2"][l], dx)
        G["norm_mlp"][l], G["w_mlp1"][l], G["w_mlp2"][l] = dg.reshape(-1), dw1, dw2
        if l % 2 == 0:
            xin, z, h, qkv, lse, ya, yb, w_in, w_out_a, w_out_b, w_pool = mix
            w_out_cat = jnp.concatenate([w_out_a, w_out_b], axis=0)
            dycat = mm_nt(f"e_out_dx_{j}", [dx], [w_out_cat])
            G["e_w_out"][j] = jnp.concatenate([unslab_rows(mm_tn(f"e_out_dwa_{j}", ya, dx), 8),
                                               mm_tn(f"e_out_dwb_{j}", yb, dx)], axis=0)
            dq, dk, dv, dsink = swa_bwd(qkv, W["e_sink"][j], lse, dycat, tabs_swa)
            du, dwp, dsc = pool_bwd(z, dycat, w_pool, row(W["e_pool_scale"][j]))
            G["e_sink"][j], G["e_w_pool"][j], G["e_pool_scale"][j] = dsink[0, 0:8], dwp, dsc.reshape(-1)
            dh = mm_nt(f"e_in_dx_{j}", [dq, dk, dv, du], [(w_in, 0, 1024), (w_in, 4, 256), (w_in, 5, 256), (w_in, 3, 512)])
            G["e_w_in"][j] = even_in_t(jnp.concatenate(
                [mm_tn(f"e_in_dwq_{j}", h, dq), mm_tn(f"e_in_dwk_{j}", h, dk), mm_tn(f"e_in_dwv_{j}", h, dv),
                 mm_tn(f"e_in_dwu_{j}", h, du)], axis=1))
            dx, dg = rms_bwd(f"e_norm_bwd_{j}", xin, row(W["e_norm_mix"][j]), dh, dx)
            G["e_norm_mix"][j] = dg.reshape(-1)
        else:
            (xin, z, h, q, k, v, nq, nkv, yc, lse, xc, a0, a1, h0, h1, yd, w_in, w_out_a, w_out_b, w_uq, w_k, w_v,
             g_cq, g_ckv, w_gate, b_gate, lam, conv_w) = mix
            w_out_cat = jnp.concatenate([w_out_a, w_out_b], axis=0)
            dycat = mm_nt(f"o_out_dx_{j}", [dx], [w_out_cat])
            G["o_w_out"][j] = jnp.concatenate([unslab_rows(mm_tn(f"o_out_dwa_{j}", yc, dx), 8),
                                               mm_tn(f"o_out_dwb_{j}", yd, dx)], axis=0)
            delta, dob = mla_delta(yc, dycat)
            dqt, dk, dv = mla_bwd(q, k, v, dob, lse, delta)
            dq = mla_dq(dqt)
            dza, dqp, dgq, dgkv = mla_prep_bwd(z, g_cq, g_ckv, w_uq, w_k, w_v, dq, dk, dv, tabs_mq, tabs_mk)
            G["o_g_cq"][j], G["o_g_ckv"][j] = dgq.reshape(-1), dgkv.reshape(-1)
            G["o_w_uq"][j] = _unslab_cols(mm_tn(f"o_uq_dw_{j}", nq, dqp), 96, 8)
            G["o_w_ukv"][j] = ukv_t(mm_tn(f"o_uk_dw_{j}", nkv, dk), mm_tn(f"o_uv_dw_{j}", nkv, dv))
            dxg, dhh = lru_gate_bwd(h0, h1, z, dycat)
            g1, g0 = lru_scan(f"lru_scan_bwd_{j}", a1, dhh, a0, dhh, adjoint=True)
            dxc, dpre, dbias, dlam = lru_bwd_point(xc, h0, h1, g0, g1, w_gate, b_gate, lam)
            dwg = mm_tn(f"o_gate_dw_{j}", xc, dpre)
            G["o_lru_wa"][j] = jnp.stack([_block_diag_t(dwg[:, 0:512]), _block_diag_t(dwg[:, 1024:1536])])
            G["o_lru_wx"][j] = jnp.stack([_block_diag_t(dwg[:, 512:1024]), _block_diag_t(dwg[:, 1536:2048])])
            G["o_lru_ba"][j] = jnp.stack([dbias[0, 0:512], dbias[0, 1024:1536]])
            G["o_lru_bx"][j] = jnp.stack([dbias[0, 512:1024], dbias[0, 1536:2048]])
            G["o_lru_lambda"][j] = dlam.reshape(2, 512)
            dxr, dcw, dcb = conv_bwd(z, dxc, conv_w)
            G["o_conv_w"][j], G["o_conv_b"][j] = dcw, dcb.reshape(-1)
            dh = mm_nt(f"o_in_dx_{j}", [dza, dxr, dxg], [(w_in, 0, 512), (w_in, 1, 512), (w_in, 2, 512)])
            G["o_w_in"][j] = odd_in_t(jnp.concatenate(
                [mm_tn(f"o_in_dwa_{j}", h, dza), mm_tn(f"o_in_dwr_{j}", h, dxr), mm_tn(f"o_in_dwg_{j}", h, dxg)], axis=1))
            dx, dg = rms_bwd(f"o_norm_bwd_{j}", xin, row(full["o_norm_mix"][j]), dh, dx)
            G["o_norm_mix"][j] = dg.reshape(-1)
    grad_x = dx[None]
    G["final_norm"] = [G["final_norm"]]

    sharded = BIG + SMALL
    me = 4 * lax.axis_index("x") + 2 * lax.axis_index("y") + lax.axis_index("c")

    def layers(T, n):
        return [T[n]] if n == "final_norm" else [T[n][l] for l in range(T[n].shape[0])]

    def layer_shapes(names):
        return [a.shape for n in names for a in layers(W, n)]

    def shards_of(n, ax, dev):
        size = W[n].shape[ax]
        return [lax.dynamic_slice_in_dim(g, dev * size, size, axis=ax - 1) for g in G[n]]

    gs = jnp.stack([_pack([p for n, ax in sharded for p in shards_of(n, ax, me ^ r)], F32) for r in range(NDEV)])
    gr = _pack([g for n in REPL for g in G[n]] + [loss_row[0, 0:1]], F32)
    mine, others = pair_add(gs, grad_pair(gs))
    others, gr_all = grad_cross(others, gr)
    names_s = [n for n, _ in sharded]
    outs_s = adamw("adamw_sharded", [mine[None], others],
                   *[_pack([a for n in names_s for a in layers(T, n)], F32) for T in (W, Mo, Vo)])
    outs_r = adamw("adamw_replicated", [gr_all],
                   *[_pack([a for n in REPL for a in layers(T, n)] + [jnp.zeros((1,), F32)], F32) for T in (W, Mo, Vo)])
    res = [dict(), dict(), dict(), dict()]
    for kind in range(4):
        pieces = (_unpack(outs_s[kind], layer_shapes(names_s))
                  + _unpack(outs_r[kind], layer_shapes(REPL) + [(1,)]))
        for n in names_s + list(REPL):
            count = len(layers(W, n))
            got, pieces = pieces[:count], pieces[count:]
            res[kind][n] = got[0] if n == "final_norm" else jnp.stack(got)
        res[kind]["loss"] = pieces[0]
    loss = res[0]["loss"][0]
    return (loss, grad_x, *[res[0][n] for n in WEIGHTS], *[res[1][n] for n in WEIGHTS],
            *[res[2][n] for n in WEIGHTS], *[res[3][n] for n in WEIGHTS])
```

```python
import functools

import jax
import jax.numpy as jnp
from jax import lax
from jax.experimental import pallas as pl
from jax.experimental.pallas import tpu as pltpu

F32 = jnp.float32
MXU_DTYPE = jnp.bfloat16
EPS = 1e-6
ROPE_THETA = 10000.0
NDEV = 8
LANES = 128
VMEM_LIMIT = 48 * 1024 * 1024

D_MODEL = 1024
D_FF = 4096
LRU_C = 8.0
POOL_WINDOWS = (2, 4, 8, 16)
HALO = 8
MLA_SCALE = 96.0 ** -0.5
LOG2E = 1.4426950408889634
MLA_QSCALE = MLA_SCALE * LOG2E

ADAM_LR, ADAM_B1, ADAM_B2, ADAM_EPS, ADAM_WD, ADAM_STEP = 0.001, 0.9, 0.999, 1e-08, 0.01, 10


def _mx(v):
    return v.astype(MXU_DTYPE)


def _call(body, *, name, grid, in_specs, out_specs, out_shape, scratch=()):
    return pl.pallas_call(
        body, name=name, grid=grid, in_specs=in_specs, out_specs=out_specs, out_shape=out_shape,
        scratch_shapes=list(scratch),
        compiler_params=pltpu.CompilerParams(
            dimension_semantics=("arbitrary",) * len(grid), vmem_limit_bytes=VMEM_LIMIT),
    )


def _dot(a, b):
    return jnp.dot(a, b, preferred_element_type=F32)


def _dot_nt(a, b):
    return lax.dot_general(a, b, (((1,), (1,)), ((), ())), preferred_element_type=F32)


def _dot_tn(a, b):
    return lax.dot_general(a, b, (((0,), (0,)), ((), ())), preferred_element_type=F32)


def _rms(x, g):
    r = lax.rsqrt(jnp.mean(x * x, axis=-1, keepdims=True) + EPS)
    return (x * r) * g


def _rms_bwd(x, g, dy):
    r = lax.rsqrt(jnp.mean(x * x, axis=-1, keepdims=True) + EPS)
    xh = x * r
    dyg = dy * g
    dx = r * (dyg - xh * jnp.mean(dyg * xh, axis=-1, keepdims=True))
    return dx, jnp.sum(dy * xh, axis=0, keepdims=True)


def _sigmoid(x):
    return 1.0 / (1.0 + jnp.exp(-x))


def _log1p(e):
    u = 1.0 + e
    d = u - 1.0
    return jnp.where(d == 0.0, e, jnp.log(u) * (e / jnp.where(d == 0.0, 1.0, d)))


def _softplus(x):
    return jnp.maximum(x, 0.0) + _log1p(jnp.exp(-jnp.abs(x)))


def _expm1(x):
    u = jnp.exp(x)
    lu = jnp.log(u)
    safe = jnp.where((lu == 0.0) | (u == 0.0), 1.0, lu)
    return jnp.where(u == 1.0, x, jnp.where(u == 0.0, -1.0, (u - 1.0) * x / safe))


_GELU_K = 0.7978845608028654


def _gelu(x):
    return 0.5 * x * (1.0 + jnp.tanh(_GELU_K * (x + 0.044715 * x * x * x)))


def _gelu_grad(x):
    t = jnp.tanh(_GELU_K * (x + 0.044715 * x * x * x))
    return 0.5 * (1.0 + t) + 0.5 * x * (1.0 - t * t) * _GELU_K * (1.0 + 3.0 * 0.044715 * x * x)


def _rope(x, c, sa, sb, half):
    return x * c + pltpu.roll(x, LANES - half, 1) * sa + pltpu.roll(x, half, 1) * sb


def _rope_t(d, c, sa, sb, half):
    return d * c - pltpu.roll(d, LANES - half, 1) * sa - pltpu.roll(d, half, 1) * sb


def _as_cols(a):
    return a if isinstance(a, tuple) else (a, 0, a.shape[1])


def mm_nn(name, a_list, b_list, *, res=None, act=None, norm_g=None, emit_norm=False,
          out_dtype=F32, tm=512, tn=1024):
    a_list = [_as_cols(a) for a in a_list]
    M, N = a_list[0][0].shape[0], b_list[0].shape[1]
    tm, tn = min(tm, M), min(tn, N)
    na = len(a_list)

    def body(*refs):
        a_refs, b_refs = refs[:na], refs[na:2 * na]
        k = 2 * na
        g_ref = res_ref = hn_ref = None
        if norm_g is not None:
            g_ref = refs[k]
            k += 1
        if res is not None:
            res_ref = refs[k]
            k += 1
        o_ref = refs[k]
        if emit_norm:
            hn_ref = refs[k + 1]
        acc = None
        for a_ref, b_ref in zip(a_refs, b_refs):
            a = a_ref[...]
            if g_ref is not None:
                a = _rms(a.astype(F32), g_ref[...])
                if hn_ref is not None:
                    hn_ref[...] = a.astype(hn_ref.dtype)
            if act == "relu2":
                a = jnp.maximum(a, 0.0)
                a = a * a
            d = _dot(_mx(a), _mx(b_ref[...]))
            acc = d if acc is None else acc + d
        if res_ref is not None:
            acc = acc + res_ref[...]
        o_ref[...] = acc.astype(o_ref.dtype)

    in_specs = [pl.BlockSpec((tm, w), functools.partial(lambda i, j, cb: (i, cb), cb=cb)) for (_, cb, w) in a_list]
    in_specs += [pl.BlockSpec((b.shape[0], tn), lambda i, j: (0, j)) for b in b_list]
    args = [a for (a, _, _) in a_list] + list(b_list)
    if norm_g is not None:
        in_specs.append(pl.BlockSpec((1, norm_g.shape[1]), lambda i, j: (0, 0)))
        args.append(norm_g)
    if res is not None:
        in_specs.append(pl.BlockSpec((tm, tn), lambda i, j: (i, j)))
        args.append(res)
    out_specs = [pl.BlockSpec((tm, tn), lambda i, j: (i, j))]
    out_shape = [jax.ShapeDtypeStruct((M, N), out_dtype)]
    if emit_norm:
        K = a_list[0][2]
        out_specs.append(pl.BlockSpec((tm, K), lambda i, j: (i, 0)))
        out_shape.append(jax.ShapeDtypeStruct((M, K), MXU_DTYPE))
    out = _call(body, name=name, grid=(M // tm, N // tn), in_specs=in_specs, out_specs=out_specs,
                out_shape=out_shape)(*args)
    return out if emit_norm else out[0]


def mm_nt(name, a_list, b_list, *, relu2_of=None, out_dtype=F32, tm=512, tn=1536):
    a_list = [_as_cols(a) for a in a_list]
    b_list = [_as_cols(b) for b in b_list]
    M, N = a_list[0][0].shape[0], b_list[0][0].shape[0]
    tm, tn = min(tm, M), min(tn, N)
    na = len(a_list)

    def body(*refs):
        a_refs, b_refs = refs[:na], refs[na:2 * na]
        u_ref = refs[2 * na] if relu2_of is not None else None
        o_ref = refs[-1]
        acc = None
        for a_ref, b_ref in zip(a_refs, b_refs):
            d = _dot_nt(_mx(a_ref[...]), _mx(b_ref[...]))
            acc = d if acc is None else acc + d
        if u_ref is not None:
            acc = acc * (2.0 * jnp.maximum(u_ref[...], 0.0))
        o_ref[...] = acc.astype(o_ref.dtype)

    in_specs = [pl.BlockSpec((tm, w), functools.partial(lambda i, j, cb: (i, cb), cb=cb)) for (_, cb, w) in a_list]
    in_specs += [pl.BlockSpec((tn, w), functools.partial(lambda i, j, cb: (j, cb), cb=cb)) for (_, cb, w) in b_list]
    args = [a for (a, _, _) in a_list] + [b for (b, _, _) in b_list]
    if relu2_of is not None:
        in_specs.append(pl.BlockSpec((tm, tn), lambda i, j: (i, j)))
        args.append(relu2_of)
    return _call(body, name=name, grid=(M // tm, N // tn), in_specs=in_specs,
                 out_specs=pl.BlockSpec((tm, tn), lambda i, j: (i, j)),
                 out_shape=jax.ShapeDtypeStruct((M, N), out_dtype))(*args)


def mm_tn(name, a, b, *, act=None, tm=1024, tn=1024, tk=512):
    a, acb, Ma = _as_cols(a)
    b, bcb, Nb = _as_cols(b)
    S = a.shape[0]
    tm, tn, tk = min(tm, Ma), min(tn, Nb), min(tk, S)
    a0, b0 = acb * (Ma // tm), bcb * (Nb // tn)

    def body(a_ref, b_ref, o_ref):
        @pl.when(pl.program_id(2) == 0)
        def _():
            o_ref[...] = jnp.zeros_like(o_ref)

        av = a_ref[...]
        if act == "relu2":
            av = jnp.maximum(av, 0.0)
            av = av * av
        o_ref[...] += _dot_tn(_mx(av), _mx(b_ref[...]))

    return _call(body, name=name, grid=(Ma // tm, Nb // tn, S // tk),
                 in_specs=[pl.BlockSpec((tk, tm), lambda i, j, k: (k, a0 + i)),
                           pl.BlockSpec((tk, tn), lambda i, j, k: (k, b0 + j))],
                 out_specs=pl.BlockSpec((tm, tn), lambda i, j, k: (i, j)),
                 out_shape=jax.ShapeDtypeStruct((Ma, Nb), F32))(a, b)


def rms_bwd(name, x, g, dh, dres, T=256):
    S, D = x.shape
    T = min(T, S)

    def body(x_ref, g_ref, dh_ref, dres_ref, dx_ref, dg_ref):
        @pl.when(pl.program_id(0) == 0)
        def _():
            dg_ref[...] = jnp.zeros_like(dg_ref)

        dx, dg = _rms_bwd(x_ref[...], g_ref[...], dh_ref[...])
        dx_ref[...] = dres_ref[...] + dx
        dg_ref[...] += dg

    row = pl.BlockSpec((T, D), lambda i: (i, 0))
    vec = pl.BlockSpec((1, D), lambda i: (0, 0))
    return _call(body, name=name, grid=(S // T,), in_specs=[row, vec, row, row], out_specs=[row, vec],
                 out_shape=[jax.ShapeDtypeStruct((S, D), F32), jax.ShapeDtypeStruct((1, D), F32)])(x, g, dh, dres)


def loss_head(x, g, target, T=256):
    S, D = x.shape
    T = min(T, S)

    def body(x_ref, g_ref, t_ref, loss_ref, dx_ref, dg_ref):
        @pl.when(pl.program_id(0) == 0)
        def _():
            dg_ref[...] = jnp.zeros_like(dg_ref)
            loss_ref[...] = jnp.zeros_like(loss_ref)

        x = x_ref[...]
        err = _rms(x, g_ref[...]) - t_ref[...]
        loss_ref[...] += 0.5 * jnp.sum(jnp.sum(err * err, axis=-1, keepdims=True) / D, axis=0, keepdims=True)
        dx, dg = _rms_bwd(x, g_ref[...], err / D)
        dx_ref[...] = dx
        dg_ref[...] += dg

    row = pl.BlockSpec((T, D), lambda i: (i, 0))
    vec = pl.BlockSpec((1, D), lambda i: (0, 0))
    return _call(body, name="loss_head", grid=(S // T,), in_specs=[row, vec, row],
                 out_specs=[pl.BlockSpec((1, LANES), lambda i: (0, 0)), row, vec],
                 out_shape=[jax.ShapeDtypeStruct((1, LANES), F32), jax.ShapeDtypeStruct((S, D), F32),
                            jax.ShapeDtypeStruct((1, D), F32)])(x, g, target)


def rope_tables(S, lo, half):
    inv = ROPE_THETA ** (-jnp.arange(half, dtype=F32) / half)
    ang = jnp.arange(S, dtype=F32)[:, None] * inv[None, :]
    cos, sin = jnp.cos(ang), jnp.sin(ang)
    one = lambda n: jnp.ones((S, n), F32)
    zero = lambda n: jnp.zeros((S, n), F32)
    hi = LANES - lo - 2 * half
    c = jnp.concatenate([one(lo), cos, cos, one(hi)], axis=1)
    sa = jnp.concatenate([zero(lo), -sin, zero(half), zero(hi)], axis=1)
    sb = jnp.concatenate([zero(lo), zero(half), sin, zero(hi)], axis=1)
    return c, sa, sb


SWA_BLOCK = 128
SWA_HALF = 32


def swa_prep(z, tabs, T=256):
    S = z.shape[0]
    T = min(T, S)

    def body(z_ref, c_ref, sa_ref, sb_ref, o_ref):
        c, sa, sb = c_ref[...], sa_ref[...], sb_ref[...]
        for s in range(10):
            sl = slice(s * LANES, (s + 1) * LANES)
            y = _rope(z_ref[:, sl], c, sa, sb, SWA_HALF)
            if s < 8:
                y = y * 0.125
            o_ref[:, sl] = y.astype(o_ref.dtype)
        o_ref[:, 1280:1536] = z_ref[:, 1280:1536].astype(o_ref.dtype)

    tab = pl.BlockSpec((T, LANES), lambda i: (i, 0))
    return _call(body, name="swa_prep", grid=(S // T,),
                 in_specs=[pl.BlockSpec((T, 1536), lambda i: (i, 0)), tab, tab, tab],
                 out_specs=pl.BlockSpec((T, 1536), lambda i: (i, 0)),
                 out_shape=jax.ShapeDtypeStruct((S, 1536), MXU_DTYPE))(z, *tabs)


SWA_GROUP = 4


def _swa_valid(n, S):
    B = SWA_BLOCK
    i = lax.broadcasted_iota(jnp.int32, (SWA_GROUP * B, 3 * B), 0) & (B - 1)
    j = lax.broadcasted_iota(jnp.int32, (SWA_GROUP * B, 3 * B), 1)
    kpos = j + (n - 1) * B
    return (jnp.abs(j - B - i) <= B) & (kpos >= 0) & (kpos < S)


def _swa_sink(sink_ref, hk):
    B = SWA_BLOCK
    row = lax.broadcasted_iota(jnp.int32, (SWA_GROUP * B, 1), 0)
    sk = jnp.full((SWA_GROUP * B, 1), sink_ref[hk * SWA_GROUP], F32)
    for g in range(1, SWA_GROUP):
        sk = jnp.where(row >= g * B, sink_ref[hk * SWA_GROUP + g], sk)
    return sk


def swa_fwd(qkv, sink):
    S = qkv.shape[0]
    B = SWA_BLOCK
    nb = S // B

    def body(sink_ref, q_ref, kp_ref, kc_ref, kn_ref, vp_ref, vc_ref, vn_ref, o_ref, st_ref):
        n = pl.program_id(0)
        valid = _swa_valid(n, S)
        lane = lax.broadcasted_iota(jnp.int32, (B, LANES), 1)
        st = jnp.zeros((B, LANES), F32)
        for hk in range(2):
            sl = slice(hk * LANES, (hk + 1) * LANES)
            k3 = jnp.concatenate([kp_ref[:, sl], kc_ref[:, sl], kn_ref[:, sl]], axis=0)
            v3 = jnp.concatenate([vp_ref[:, sl], vc_ref[:, sl], vn_ref[:, sl]], axis=0)
            heads = [hk * SWA_GROUP + g for g in range(SWA_GROUP)]
            q4 = jnp.concatenate([q_ref[:, h * LANES:(h + 1) * LANES] for h in heads], axis=0)
            s = jnp.where(valid, _dot_nt(q4, k3), -jnp.inf)
            sk = _swa_sink(sink_ref, hk)
            m = jnp.maximum(jnp.max(s, axis=-1, keepdims=True), sk)
            p = jnp.exp(s - m)
            den = jnp.sum(p, axis=-1, keepdims=True) + jnp.exp(sk - m)
            p = p / den
            o4 = _dot(_mx(p), v3)
            lse4 = m + jnp.log(den)
            for g, h in enumerate(heads):
                rows = slice(g * B, (g + 1) * B)
                o_ref[:, h * LANES:(h + 1) * LANES] = o4[rows].astype(o_ref.dtype)
                st = jnp.where(lane == h, lse4[rows], st)
        st_ref[...] = st

    kv = lambda cb, d: pl.BlockSpec((B, 2 * LANES), lambda n: (jnp.clip(n + d, 0, nb - 1), cb))
    return _call(body, name="swa_fwd", grid=(nb,),
                 in_specs=[pl.BlockSpec(memory_space=pltpu.SMEM),
                           pl.BlockSpec((B, 1024), lambda n: (n, 0)),
                           kv(4, -1), kv(4, 0), kv(4, 1), kv(5, -1), kv(5, 0), kv(5, 1)],
                 out_specs=[pl.BlockSpec((B, 1024), lambda n: (n, 0)), pl.BlockSpec((B, LANES), lambda n: (n, 0))],
                 out_shape=[jax.ShapeDtypeStruct((S, 1024), MXU_DTYPE), jax.ShapeDtypeStruct((S, LANES), F32)],
                 )(sink, qkv, qkv, qkv, qkv, qkv, qkv, qkv)


def swa_bwd(qkv, sink, lse, dycat, tabs):
    S = qkv.shape[0]
    B = SWA_BLOCK
    nb = S // B

    def body(sink_ref, q_ref, kp_ref, kc_ref, kn_ref, vp_ref, vc_ref, vn_ref, do_ref, st_ref,
             cq_ref, saq_ref, sbq_ref, ck_ref, sak_ref, sbk_ref,
             dq_ref, dk_ref, dv_ref, dsink_ref, dk_acc, dv_acc):
        n = pl.program_id(0)

        @pl.when(n == 0)
        def _():
            dk_acc[...] = jnp.zeros_like(dk_acc)
            dv_acc[...] = jnp.zeros_like(dv_acc)
            dsink_ref[...] = jnp.zeros_like(dsink_ref)

        @pl.when(n < nb)
        def _():
            valid = _swa_valid(n, S)
            lane = lax.broadcasted_iota(jnp.int32, (B, LANES), 1)
            lane1 = lax.broadcasted_iota(jnp.int32, (1, LANES), 1)
            st = st_ref[...]
            cq, saq, sbq = cq_ref[...], saq_ref[...], sbq_ref[...]
            dsink = jnp.zeros((1, LANES), F32)
            for hk in range(2):
                sl = slice(hk * LANES, (hk + 1) * LANES)
                k3 = jnp.concatenate([kp_ref[:, sl], kc_ref[:, sl], kn_ref[:, sl]], axis=0)
                v3 = jnp.concatenate([vp_ref[:, sl], vc_ref[:, sl], vn_ref[:, sl]], axis=0)
                heads = [hk * SWA_GROUP + g for g in range(SWA_GROUP)]
                q4 = jnp.concatenate([q_ref[:, h * LANES:(h + 1) * LANES] for h in heads], axis=0)
                do4 = jnp.concatenate([_mx(do_ref[:, h * LANES:(h + 1) * LANES]) for h in heads], axis=0)
                lse4 = jnp.concatenate([jnp.sum(jnp.where(lane == h, st, 0.0), axis=-1, keepdims=True) for h in heads],
                                       axis=0)
                p = jnp.where(valid, jnp.exp(_dot_nt(q4, k3) - lse4), 0.0)
                dp = _dot_nt(do4, v3)
                dsum = jnp.sum(p * dp, axis=-1, keepdims=True)
                ds = _mx(p * (dp - dsum))
                dq4 = _dot(ds, k3) * 0.125
                dk_acc[:, sl] += _dot_tn(ds, q4)
                dv_acc[:, sl] += _dot_tn(_mx(p), do4)
                dsk4 = jnp.exp(_swa_sink(sink_ref, hk) - lse4) * dsum
                for g, h in enumerate(heads):
                    rows = slice(g * B, (g + 1) * B)
                    dq_ref[:, h * LANES:(h + 1) * LANES] = _rope_t(dq4[rows], cq, saq, sbq, SWA_HALF)
                    dsink = jnp.where(lane1 == h, -jnp.sum(dsk4[rows], axis=0, keepdims=True), dsink)
            dsink_ref[...] += dsink

        ck, sak, sbk = ck_ref[...], sak_ref[...], sbk_ref[...]
        for hk in range(2):
            sl = slice(hk * LANES, (hk + 1) * LANES)
            dk_ref[:, sl] = _rope_t(dk_acc[0:B, sl], ck, sak, sbk, SWA_HALF)
        dv_ref[...] = dv_acc[0:B, :]
        for acc in (dk_acc, dv_acc):
            acc[0:B, :] = acc[B:2 * B, :]
            acc[B:2 * B, :] = acc[2 * B:3 * B, :]
            acc[2 * B:3 * B, :] = jnp.zeros((B, 2 * LANES), F32)

    qn = lambda n: jnp.minimum(n, nb - 1)
    kv = lambda cb, d: pl.BlockSpec((B, 2 * LANES), lambda n: (jnp.clip(qn(n) + d, 0, nb - 1), cb))
    qrow = lambda w: pl.BlockSpec((B, w), lambda n: (qn(n), 0))
    krow = lambda w: pl.BlockSpec((B, w), lambda n: (jnp.maximum(n - 1, 0), 0))
    return _call(body, name="swa_bwd", grid=(nb + 1,),
                 in_specs=[pl.BlockSpec(memory_space=pltpu.SMEM), qrow(1024),
                           kv(4, -1), kv(4, 0), kv(4, 1), kv(5, -1), kv(5, 0), kv(5, 1),
                           qrow(1024), qrow(LANES),
                           qrow(LANES), qrow(LANES), qrow(LANES), krow(LANES), krow(LANES), krow(LANES)],
                 out_specs=[qrow(1024), krow(2 * LANES), krow(2 * LANES), pl.BlockSpec((1, LANES), lambda n: (0, 0))],
                 out_shape=[jax.ShapeDtypeStruct((S, 1024), F32), jax.ShapeDtypeStruct((S, 2 * LANES), F32),
                            jax.ShapeDtypeStruct((S, 2 * LANES), F32), jax.ShapeDtypeStruct((1, LANES), F32)],
                 scratch=[pltpu.VMEM((3 * B, 2 * LANES), F32), pltpu.VMEM((3 * B, 2 * LANES), F32)],
                 )(sink, qkv, qkv, qkv, qkv, qkv, qkv, qkv, dycat, lse, *tabs, *tabs)


def _halo_specs(T, S, w, cb):
    r = T // HALO
    last = S // HALO - 1
    return [pl.BlockSpec((HALO, w), lambda i: (jnp.maximum(i * r - 1, 0), cb)),
            pl.BlockSpec((T, w), lambda i: (i, cb)),
            pl.BlockSpec((HALO, w), lambda i: (jnp.minimum((i + 1) * r, last), cb))]


def _fill_ext(ext, prev_ref, cur_ref, next_ref, i, nt, T):
    ext[0:HALO, :] = jnp.where(i > 0, prev_ref[...], 0.0).astype(F32)
    ext[HALO:HALO + T, :] = cur_ref[...].astype(F32)
    ext[HALO + T:2 * HALO + T, :] = jnp.where(i < nt - 1, next_ref[...], 0.0).astype(F32)


def _pool_cnt(t, half, S):
    return (jnp.clip(t + half, 0, S) - jnp.clip(t - half, 0, S)).astype(F32)


def pool_fwd(z, w_pool, scale, T=256):
    S = z.shape[0]
    T = min(T, S)
    nt = S // T

    def body(up_ref, uc_ref, un_ref, w_ref, sc_ref, o_ref, ext):
        i = pl.program_id(0)
        _fill_ext(ext, up_ref, uc_ref, un_ref, i, nt, T)
        t = i * T + lax.broadcasted_iota(jnp.int32, (T, 1), 0)
        for g, win in enumerate(POOL_WINDOWS):
            half = win // 2
            sl = slice(g * LANES, (g + 1) * LANES)
            acc = ext[pl.ds(HALO - half, T), sl]
            for off in range(-half + 1, half):
                acc = acc + ext[pl.ds(HALO + off, T), sl]
            d = acc / _pool_cnt(t, half, S) - ext[pl.ds(HALO, T), sl]
            o_ref[:, sl] = (_dot(_mx(d), w_ref[g]) * sc_ref[:, sl]).astype(o_ref.dtype)

    return _call(body, name="pool_fwd", grid=(nt,),
                 in_specs=_halo_specs(T, S, 512, 3) + [pl.BlockSpec((4, LANES, LANES), lambda i: (0, 0, 0)),
                                                      pl.BlockSpec((1, 512), lambda i: (0, 0))],
                 out_specs=pl.BlockSpec((T, 512), lambda i: (i, 0)),
                 out_shape=jax.ShapeDtypeStruct((S, 512), MXU_DTYPE),
                 scratch=[pltpu.VMEM((T + 2 * HALO, 512), F32)])(z, z, z, w_pool, scale)


def pool_bwd(z, dycat, w_pool, scale, T=256):
    S = z.shape[0]
    T = min(T, S)
    nt = S // T
    TE = T + 2 * HALO

    def body(up_ref, uc_ref, un_ref, yp_ref, yc_ref, yn_ref, w_ref, sc_ref, du_ref, dw_ref, dsc_ref, extu, exty, exte):
        i = pl.program_id(0)

        @pl.when(i == 0)
        def _():
            dw_ref[...] = jnp.zeros_like(dw_ref)
            dsc_ref[...] = jnp.zeros_like(dsc_ref)

        _fill_ext(extu, up_ref, uc_ref, un_ref, i, nt, T)
        _fill_ext(exty, yp_ref, yc_ref, yn_ref, i, nt, T)
        t = i * T + lax.broadcasted_iota(jnp.int32, (T, 1), 0)
        te = i * T - HALO + lax.broadcasted_iota(jnp.int32, (TE, 1), 0)
        for g, win in enumerate(POOL_WINDOWS):
            half = win // 2
            sl = slice(g * LANES, (g + 1) * LANES)
            w = w_ref[g]
            acc = extu[pl.ds(HALO - half, T), sl]
            for off in range(-half + 1, half):
                acc = acc + extu[pl.ds(HALO + off, T), sl]
            d = _mx(acc / _pool_cnt(t, half, S) - extu[pl.ds(HALO, T), sl])
            dy = exty[pl.ds(HALO, T), sl]
            dsc_ref[:, sl] += jnp.sum(dy * _dot(d, w), axis=0, keepdims=True)
            dw_ref[g] += _dot_tn(d, _mx(dy * sc_ref[:, sl]))
            dd = _dot_nt(_mx(exty[:, sl] * sc_ref[:, sl]), w)
            exte[:, sl] = dd / jnp.maximum(_pool_cnt(te, half, S), 1.0)
            acc = exte[pl.ds(HALO - half + 1, T), sl]
            for off in range(-half + 2, half + 1):
                acc = acc + exte[pl.ds(HALO + off, T), sl]
            du_ref[:, sl] = acc - dd[HALO:HALO + T, :]

    return _call(body, name="pool_bwd", grid=(nt,),
                 in_specs=_halo_specs(T, S, 512, 3) + _halo_specs(T, S, 512, 2)
                 + [pl.BlockSpec((4, LANES, LANES), lambda i: (0, 0, 0)), pl.BlockSpec((1, 512), lambda i: (0, 0))],
                 out_specs=[pl.BlockSpec((T, 512), lambda i: (i, 0)), pl.BlockSpec((4, LANES, LANES), lambda i: (0, 0, 0)),
                            pl.BlockSpec((1, 512), lambda i: (0, 0))],
                 out_shape=[jax.ShapeDtypeStruct((S, 512), F32), jax.ShapeDtypeStruct((4, LANES, LANES), F32),
                            jax.ShapeDtypeStruct((1, 512), F32)],
                 scratch=[pltpu.VMEM((TE, 512), F32)] * 3)(z, z, z, dycat, dycat, dycat, w_pool, scale)


MLA_HALF = 16
MLA_V_ONE = 64


def mla_prep(z, g_cq, g_ckv, w_uq, w_k, w_v, tabs_q, tabs_k, T=256):
    S = z.shape[0]
    T = min(T, S)

    def body(z_ref, gq_ref, gkv_ref, wq_ref, wk_ref, wv_ref, cq_ref, saq_ref, sbq_ref, ck_ref, sak_ref, sbk_ref,
             q_ref, k_ref, v_ref, nq_ref, nkv_ref):
        nq = _mx(_rms(z_ref[:, 0:256], gq_ref[...]))
        nkv = _mx(_rms(z_ref[:, 256:384], gkv_ref[...]))
        nq_ref[...] = nq
        nkv_ref[...] = nkv
        q = _dot(nq, wq_ref[...])
        kn = _dot(nkv, wk_ref[...])
        lane = lax.broadcasted_iota(jnp.int32, (T, 1024), 1)
        v_ref[...] = jnp.where(lane % LANES == MLA_V_ONE, 1.0, _dot(nkv, wv_ref[...])).astype(v_ref.dtype)
        kr = pltpu.roll(_rope(z_ref[:, 384:512], ck_ref[...], sak_ref[...], sbk_ref[...], MLA_HALF), 64, 1)
        cq, saq, sbq = cq_ref[...], saq_ref[...], sbq_ref[...]
        for h in range(8):
            hs = slice(h * LANES, (h + 1) * LANES)
            q_ref[:, hs] = (_rope(q[:, hs], cq, saq, sbq, MLA_HALF) * MLA_QSCALE).astype(q_ref.dtype)
            k_ref[:, hs] = (kn[:, hs] + kr).astype(k_ref.dtype)

    tab = pl.BlockSpec((T, LANES), lambda i: (i, 0))
    full = lambda a: pl.BlockSpec(a.shape, lambda i: (0, 0))
    row = lambda w: pl.BlockSpec((T, w), lambda i: (i, 0))
    sd = lambda w: jax.ShapeDtypeStruct((S, w), MXU_DTYPE)
    return _call(body, name="mla_prep", grid=(S // T,),
                 in_specs=[row(512), full(g_cq), full(g_ckv), full(w_uq), full(w_k), full(w_v)] + [tab] * 6,
                 out_specs=[row(1024), row(1024), row(1024), row(256), row(128)],
                 out_shape=[sd(1024), sd(1024), sd(1024), sd(256), sd(128)],
                 )(z, g_cq, g_ckv, w_uq, w_k, w_v, *tabs_q, *tabs_k)


def _col_to_row(c):
    return jnp.transpose(jnp.broadcast_to(c, (c.shape[0], LANES)))[0:1, :]


def mla_vt(v, TK=512):
    S = v.shape[0]
    TK = min(TK, S // 2)
    return v.reshape(S // TK, TK, 8, LANES).transpose(2, 0, 3, 1)


def mla_dq(dqt):
    _, nq, _, TQ = dqt.shape
    return dqt.transpose(1, 3, 0, 2).reshape(nq * TQ, 1024)


def mla_fwd(q, k, vt, TQ=512):
    S = q.shape[0]
    nk, TK = vt.shape[1], vt.shape[3]
    TQ = min(TQ, S)
    QB = min(256, TQ)
    assert nk % 2 == 0
    unroll = 8 if nk % 8 == 0 else 4 if nk % 4 == 0 else 2

    def body(q_ref, k_ref, vt_ref, o_ref, lse_ref, m_s, acc_s, s_a, s_b):
        m_s[...] = jnp.full_like(m_s, -jnp.inf)
        acc_s[...] = jnp.zeros_like(acc_s)
        q = q_ref[...]

        def scores(c):
            return _dot_nt(k_ref[pl.ds(pl.multiple_of(c * TK, TK), TK), :], q)

        def softmax_pv(s_ref, c):
            vt_c = vt_ref[0, c]
            for b in range(TQ // QB):
                cols = slice(b * QB, (b + 1) * QB)
                s = s_ref[:, cols]
                m_old = m_s[:, cols]
                m_new = jnp.maximum(m_old, jnp.max(s, axis=0, keepdims=True))
                alpha = jnp.exp2(m_old[0:1, :] - m_new[0:1, :])
                p = jnp.exp2(s - m_new[0:1, :])
                acc_s[:, cols] = alpha * acc_s[:, cols] + _dot(vt_c, _mx(p))
                m_s[:, cols] = m_new

        s_a[...] = scores(0)
        bufs = (s_a, s_b)

        def group(jj, carry):
            c = unroll * jj
            for u in range(unroll):
                nxt = c + u + 1 if u < unroll - 1 else jnp.minimum(c + unroll, nk - 1)
                bufs[(u + 1) % 2][...] = scores(nxt)
                softmax_pv(bufs[u % 2], c + u)
            return carry

        lax.fori_loop(0, nk // unroll, group, 0)
        acc = acc_s[...]
        den = acc[MLA_V_ONE:MLA_V_ONE + 1, :]
        sub = lax.broadcasted_iota(jnp.int32, (LANES, TQ), 0)
        o_ref[...] = jnp.transpose(jnp.where(sub < MLA_V_ONE, acc / den, 0.0))
        lse_ref[0] = m_s[0:1, :] + jnp.log(den) * LOG2E

    qs = pl.BlockSpec((TQ, LANES), lambda h, i: (i, h))
    return _call(body, name="mla_fwd", grid=(8, S // TQ),
                 in_specs=[qs, pl.BlockSpec((S, LANES), lambda h, i: (0, h)),
                           pl.BlockSpec((1, nk, LANES, TK), lambda h, i: (h, 0, 0, 0))],
                 out_specs=[qs, pl.BlockSpec((1, 1, TQ), lambda h, i: (h, 0, i))],
                 out_shape=[jax.ShapeDtypeStruct((S, 1024), F32), jax.ShapeDtypeStruct((8, 1, S), F32)],
                 scratch=[pltpu.VMEM((8, TQ), F32), pltpu.VMEM((LANES, TQ), F32),
                          pltpu.VMEM((TK, TQ), F32), pltpu.VMEM((TK, TQ), F32)],
                 )(q, k, vt)


def mla_delta(o, dycat, TQ=512):
    S = o.shape[0]
    TQ = min(TQ, S)

    def body(o_ref, do_ref, d_ref, dob_ref):
        for h in range(8):
            hs = slice(h * LANES, (h + 1) * LANES)
            do = do_ref[:, hs]
            d_ref[h] = _col_to_row(jnp.sum(o_ref[:, hs] * do, axis=-1, keepdims=True))
            dob_ref[:, hs] = do.astype(dob_ref.dtype)

    qs = pl.BlockSpec((TQ, 1024), lambda i: (i, 0))
    return _call(body, name="mla_delta", grid=(S // TQ,), in_specs=[qs, qs],
                 out_specs=[pl.BlockSpec((8, 1, TQ), lambda i: (0, 0, i)), qs],
                 out_shape=[jax.ShapeDtypeStruct((8, 1, S), F32), jax.ShapeDtypeStruct((S, 1024), MXU_DTYPE)])(o, dycat)


def mla_bwd(q, k, v, do, lse, delta, TQ=512, TK=512):
    S = q.shape[0]
    TQ, TK = min(TQ, S // 2), min(TK, S)
    nq, nk = S // TQ, S // TK
    assert nq % 2 == 0
    unroll = 8 if nq % 8 == 0 else 4 if nq % 4 == 0 else 2
    lse = lse.reshape(8, nq, 1, TQ)
    delta = delta.reshape(8, nq, 1, TQ)

    def body(q_ref, do_ref, k_ref, v_ref, lse_ref, d_ref, dqt_ref, dk_ref, dv_ref, dk_s, dv_s, s_a, p_a, s_b, p_b):
        @pl.when(pl.program_id(1) == 0)
        def _():
            dqt_ref[...] = jnp.zeros_like(dqt_ref)

        dk_s[...] = jnp.zeros_like(dk_s)
        dv_s[...] = jnp.zeros_like(dv_s)
        kk, vv = k_ref[...], v_ref[...]
        kt = _mx(jnp.transpose(kk.astype(F32)))

        def rows(c):
            return pl.ds(pl.multiple_of(c * TQ, TQ), TQ)

        def products(c, s_ref, p_ref):
            s_ref[...] = _dot_nt(kk, q_ref[rows(c), :])
            p_ref[...] = _dot_nt(vv, do_ref[rows(c), :])

        def consume(c, s_ref, p_ref):
            qc, doc = q_ref[rows(c), :], do_ref[rows(c), :]
            pt = jnp.exp2(s_ref[...] - lse_ref[0, c])
            dv_s[...] += _dot(_mx(pt), doc)
            ds = _mx(pt * (p_ref[...] - d_ref[0, c]))
            dk_s[...] += _dot(ds, qc)
            dqt_ref[0, c] += _dot(kt, ds)

        products(0, s_a, p_a)
        bufs = ((s_a, p_a), (s_b, p_b))

        def group(jj, carry):
            c = unroll * jj
            for u in range(unroll):
                nxt = c + u + 1 if u < unroll - 1 else jnp.minimum(c + unroll, nq - 1)
                products(nxt, *bufs[(u + 1) % 2])
                consume(c + u, *bufs[u % 2])
            return carry

        lax.fori_loop(0, nq // unroll, group, 0)
        dk_ref[...] = dk_s[...] * (1.0 / LOG2E)
        dv_ref[...] = dv_s[...]

    full = pl.BlockSpec((S, LANES), lambda h, j: (0, h))
    ks = pl.BlockSpec((TK, LANES), lambda h, j: (j, h))
    st = pl.BlockSpec((1, nq, 1, TQ), lambda h, j: (h, 0, 0, 0))
    sd = jax.ShapeDtypeStruct((S, 1024), F32)
    return _call(body, name="mla_bwd", grid=(8, nk), in_specs=[full, full, ks, ks, st, st],
                 out_specs=[pl.BlockSpec((1, nq, LANES, TQ), lambda h, j: (h, 0, 0, 0)), ks, ks],
                 out_shape=[jax.ShapeDtypeStruct((8, nq, LANES, TQ), F32), sd, sd],
                 scratch=[pltpu.VMEM((TK, LANES), F32), pltpu.VMEM((TK, LANES), F32)] + [pltpu.VMEM((TK, TQ), F32)] * 4,
                 )(q, do, k, v, lse, delta)


def mla_prep_bwd(z, g_cq, g_ckv, w_uq, w_k, w_v, dq, dk, dv, tabs_q, tabs_k, T=256):
    S = z.shape[0]
    T = min(T, S)

    def body(z_ref, gq_ref, gkv_ref, wq_ref, wk_ref, wv_ref, dq_ref, dk_ref, dv_ref,
             cq_ref, saq_ref, sbq_ref, ck_ref, sak_ref, sbk_ref, dz_ref, dqp_ref, dgq_ref, dgkv_ref):
        @pl.when(pl.program_id(0) == 0)
        def _():
            dgq_ref[...] = jnp.zeros_like(dgq_ref)
            dgkv_ref[...] = jnp.zeros_like(dgkv_ref)

        cq, saq, sbq = cq_ref[...], saq_ref[...], sbq_ref[...]
        dkr = jnp.zeros((T, LANES), F32)
        for h in range(8):
            hs = slice(h * LANES, (h + 1) * LANES)
            dqp_ref[:, hs] = _rope_t(dq_ref[:, hs] * MLA_SCALE, cq, saq, sbq, MLA_HALF).astype(dqp_ref.dtype)
            dkr = dkr + dk_ref[:, hs]
        lane = lax.broadcasted_iota(jnp.int32, (T, LANES), 1)
        dkr = jnp.where(lane < 2 * MLA_HALF, pltpu.roll(dkr, 64, 1), 0.0)
        dz_ref[:, 384:512] = _rope_t(dkr, ck_ref[...], sak_ref[...], sbk_ref[...], MLA_HALF)
        dnq = _dot_nt(dqp_ref[...], wq_ref[...])
        dx, dg = _rms_bwd(z_ref[:, 0:256], gq_ref[...], dnq)
        dz_ref[:, 0:256] = dx
        dgq_ref[...] += dg
        dnkv = _dot_nt(_mx(dk_ref[...]), wk_ref[...]) + _dot_nt(_mx(dv_ref[...]), wv_ref[...])
        dx, dg = _rms_bwd(z_ref[:, 256:384], gkv_ref[...], dnkv)
        dz_ref[:, 256:384] = dx
        dgkv_ref[...] += dg

    tab = pl.BlockSpec((T, LANES), lambda i: (i, 0))
    full = lambda a: pl.BlockSpec(a.shape, lambda i: (0, 0))
    row = lambda w: pl.BlockSpec((T, w), lambda i: (i, 0))
    return _call(body, name="mla_prep_bwd", grid=(S // T,),
                 in_specs=[row(512), full(g_cq), full(g_ckv), full(w_uq), full(w_k), full(w_v),
                           row(1024), row(1024), row(1024)] + [tab] * 6,
                 out_specs=[row(512), row(1024), full(g_cq), full(g_ckv)],
                 out_shape=[jax.ShapeDtypeStruct((S, 512), F32), jax.ShapeDtypeStruct((S, 1024), MXU_DTYPE),
                            jax.ShapeDtypeStruct(g_cq.shape, F32), jax.ShapeDtypeStruct(g_ckv.shape, F32)],
                 )(z, g_cq, g_ckv, w_uq, w_k, w_v, dq, dk, dv, *tabs_q, *tabs_k)


def _lru_gates(xc, w_ref, bias_ref, lam_ref):
    pre = _dot(_mx(xc), w_ref[...]) + bias_ref[...]
    out = []
    for d in range(2):
        r = _sigmoid(pre[:, d * 1024:d * 1024 + 512])
        ig = _sigmoid(pre[:, d * 1024 + 512:(d + 1) * 1024])
        log_a = -LRU_C * r * _softplus(-lam_ref[:, d * 512:(d + 1) * 512])
        out.append((r, ig, jnp.exp(log_a), jnp.sqrt(-_expm1(2.0 * log_a))))
    return out


def lru_pre(z, conv_w, conv_b, w_gate, b_gate, lam, T=256):
    S = z.shape[0]
    T = min(T, S)
    nt = S // T

    def body(xp_ref, xcur_ref, xn_ref, cw_ref, cb_ref, w_ref, bias_ref, lam_ref, xc_ref, a0_ref, b0_ref, a1_ref, b1_ref, ext):
        i = pl.program_id(0)
        _fill_ext(ext, xp_ref, xcur_ref, xn_ref, i, nt, T)
        xc = cb_ref[...] + cw_ref[0:1, :] * ext[pl.ds(HALO - 2, T), :]
        for j in range(1, 4):
            xc = xc + cw_ref[j:j + 1, :] * ext[pl.ds(HALO - 2 + j, T), :]
        xc_ref[...] = xc
        (_, i0, a0, m0), (_, i1, a1, m1) = _lru_gates(xc, w_ref, bias_ref, lam_ref)
        a0_ref[...] = a0
        b0_ref[...] = m0 * (i0 * xc)
        a1_ref[...] = a1
        b1_ref[...] = m1 * (i1 * xc)

    full = lambda a: pl.BlockSpec(a.shape, lambda i: (0, 0))
    row = pl.BlockSpec((T, 512), lambda i: (i, 0))
    sd = jax.ShapeDtypeStruct((S, 512), F32)
    return _call(body, name="lru_pre", grid=(nt,),
                 in_specs=_halo_specs(T, S, 512, 1) + [full(conv_w), full(conv_b), full(w_gate), full(b_gate), full(lam)],
                 out_specs=[row] * 5, out_shape=[sd] * 5,
                 scratch=[pltpu.VMEM((T + 2 * HALO, 512), F32)])(z, z, z, conv_w, conv_b, w_gate, b_gate, lam)


def lru_scan(name, af, bf, ar, br, *, adjoint, T=256):
    S, W = af.shape
    T = min(T, S)
    nt = S // T
    nc = T // 8

    def body(af_ref, bf_ref, ar_ref, br_ref, hf_ref, hr_ref, cf, cr):
        @pl.when(pl.program_id(0) == 0)
        def _():
            cf[...] = jnp.zeros_like(cf)
            cr[...] = jnp.zeros_like(cr)

        row = lax.broadcasted_iota(jnp.int32, (8, W), 0)

        def step(a, b, carry):
            if adjoint:
                val = b + carry
                return val, a * val
            val = a * carry + b
            return val, val

        def chunk(c, carry):
            hf, hr = carry
            of = pl.multiple_of(c * 8, 8)
            orv = pl.multiple_of((nc - 1 - c) * 8, 8)
            a8, b8 = af_ref[pl.ds(of, 8), :], bf_ref[pl.ds(of, 8), :]
            ra8, rb8 = ar_ref[pl.ds(orv, 8), :], br_ref[pl.ds(orv, 8), :]
            outf = jnp.zeros((8, W), F32)
            outr = jnp.zeros((8, W), F32)
            for k in range(8):
                val, hf = step(a8[k:k + 1, :], b8[k:k + 1, :], hf)
                outf = jnp.where(row == k, val, outf)
                kr = 7 - k
                val, hr = step(ra8[kr:kr + 1, :], rb8[kr:kr + 1, :], hr)
                outr = jnp.where(row == kr, val, outr)
            hf_ref[pl.ds(of, 8), :] = outf
            hr_ref[pl.ds(orv, 8), :] = outr
            return hf, hr

        hf, hr = lax.fori_loop(0, nc, chunk, (cf[0:1, :], cr[0:1, :]))
        cf[0:1, :] = hf
        cr[0:1, :] = hr

    fw = pl.BlockSpec((T, W), lambda i: (i, 0))
    rv = pl.BlockSpec((T, W), lambda i: (nt - 1 - i, 0))
    sd = jax.ShapeDtypeStruct((S, W), F32)
    return _call(body, name=name, grid=(nt,), in_specs=[fw, fw, rv, rv], out_specs=[fw, rv], out_shape=[sd, sd],
                 scratch=[pltpu.VMEM((8, W), F32), pltpu.VMEM((8, W), F32)])(af, bf, ar, br)


def lru_gate(h0, h1, z, T=256):
    S = z.shape[0]
    T = min(T, S)

    def body(h0_ref, h1_ref, xg_ref, y_ref):
        y_ref[...] = ((h0_ref[...] + h1_ref[...]) * _gelu(xg_ref[...])).astype(y_ref.dtype)

    row = pl.BlockSpec((T, 512), lambda i: (i, 0))
    return _call(body, name="lru_gate", grid=(S // T,), in_specs=[row, row, pl.BlockSpec((T, 512), lambda i: (i, 2))],
                 out_specs=row, out_shape=jax.ShapeDtypeStruct((S, 512), MXU_DTYPE))(h0, h1, z)


def lru_gate_bwd(h0, h1, z, dycat, T=256):
    S = z.shape[0]
    T = min(T, S)

    def body(h0_ref, h1_ref, xg_ref, dy_ref, dxg_ref, dh_ref):
        xg, dy = xg_ref[...], dy_ref[...]
        dxg_ref[...] = dy * (h0_ref[...] + h1_ref[...]) * _gelu_grad(xg)
        dh_ref[...] = dy * _gelu(xg)

    row = pl.BlockSpec((T, 512), lambda i: (i, 0))
    col2 = pl.BlockSpec((T, 512), lambda i: (i, 2))
    sd = jax.ShapeDtypeStruct((S, 512), F32)
    return _call(body, name="lru_gate_bwd", grid=(S // T,), in_specs=[row, row, col2, col2],
                 out_specs=[row, row], out_shape=[sd, sd])(h0, h1, z, dycat)


def lru_bwd_point(xc, h0, h1, g0, g1, w_gate, b_gate, lam, T=256):
    S = xc.shape[0]
    T = min(T, S)
    nt = S // T

    def body(xc_ref, h0p_ref, h0_ref, h0n_ref, h1p_ref, h1_ref, h1n_ref, g0_ref, g1_ref, w_ref, bias_ref, lam_ref,
             dxc_ref, dpre_ref, dbias_ref, dlam_ref, ext0, ext1):
        i = pl.program_id(0)

        @pl.when(i == 0)
        def _():
            dbias_ref[...] = jnp.zeros_like(dbias_ref)
            dlam_ref[...] = jnp.zeros_like(dlam_ref)

        _fill_ext(ext0, h0p_ref, h0_ref, h0n_ref, i, nt, T)
        _fill_ext(ext1, h1p_ref, h1_ref, h1n_ref, i, nt, T)
        xc = xc_ref[...]
        gates = _lru_gates(xc, w_ref, bias_ref, lam_ref)
        hshift = (ext0[pl.ds(HALO - 1, T), :], ext1[pl.ds(HALO + 1, T), :])
        gs = (g0_ref[...], g1_ref[...])
        dxc = jnp.zeros((T, 512), F32)
        for d in range(2):
            r, ig, a, mult = gates[d]
            db = gs[d]
            da = db * hshift[d]
            dmult = db * (ig * xc)
            di = db * (mult * xc)
            dxc = dxc + db * (mult * ig)
            dloga = da * a - dmult * (a * a / mult)
            lam_d = lam_ref[:, d * 512:(d + 1) * 512]
            dr = dloga * (-LRU_C * _softplus(-lam_d))
            dsp = jnp.sum(dloga * (-LRU_C * r), axis=0, keepdims=True)
            dlam_ref[:, d * 512:(d + 1) * 512] += dsp * (-_sigmoid(-lam_d))
            dpre_ref[:, d * 1024:d * 1024 + 512] = (dr * (r * (1.0 - r))).astype(dpre_ref.dtype)
            dpre_ref[:, d * 1024 + 512:(d + 1) * 1024] = (di * (ig * (1.0 - ig))).astype(dpre_ref.dtype)
            dbias_ref[:, d * 1024:d * 1024 + 512] += jnp.sum(dr * (r * (1.0 - r)), axis=0, keepdims=True)
            dbias_ref[:, d * 1024 + 512:(d + 1) * 1024] += jnp.sum(di * (ig * (1.0 - ig)), axis=0, keepdims=True)
        dxc_ref[...] = dxc + _dot_nt(dpre_ref[...], w_ref[...])

    full = lambda a: pl.BlockSpec(a.shape, lambda i: (0, 0))
    row = pl.BlockSpec((T, 512), lambda i: (i, 0))
    return _call(body, name="lru_bwd_point", grid=(nt,),
                 in_specs=[row] + _halo_specs(T, S, 512, 0) + _halo_specs(T, S, 512, 0) + [row, row, full(w_gate), full(b_gate), full(lam)],
                 out_specs=[row, pl.BlockSpec((T, 2048), lambda i: (i, 0)), full(b_gate), full(lam)],
                 out_shape=[jax.ShapeDtypeStruct((S, 512), F32), jax.ShapeDtypeStruct((S, 2048), MXU_DTYPE),
                            jax.ShapeDtypeStruct(b_gate.shape, F32), jax.ShapeDtypeStruct(lam.shape, F32)],
                 scratch=[pltpu.VMEM((T + 2 * HALO, 512), F32)] * 2,
                 )(xc, h0, h0, h0, h1, h1, h1, g0, g1, w_gate, b_gate, lam)


def conv_bwd(z, dxc, conv_w, T=256):
    S = z.shape[0]
    T = min(T, S)
    nt = S // T

    def body(xp_ref, xcur_ref, xn_ref, dp_ref, dcur_ref, dn_ref, cw_ref, dx_ref, dw_ref, db_ref, extx, extd):
        i = pl.program_id(0)

        @pl.when(i == 0)
        def _():
            dw_ref[...] = jnp.zeros_like(dw_ref)
            db_ref[...] = jnp.zeros_like(db_ref)

        _fill_ext(extx, xp_ref, xcur_ref, xn_ref, i, nt, T)
        _fill_ext(extd, dp_ref, dcur_ref, dn_ref, i, nt, T)
        d = extd[pl.ds(HALO, T), :]
        dx = cw_ref[0:1, :] * extd[pl.ds(HALO + 2, T), :]
        for j in range(1, 4):
            dx = dx + cw_ref[j:j + 1, :] * extd[pl.ds(HALO + 2 - j, T), :]
        dx_ref[...] = dx
        for j in range(4):
            dw_ref[j:j + 1, :] += jnp.sum(d * extx[pl.ds(HALO - 2 + j, T), :], axis=0, keepdims=True)
        db_ref[...] += jnp.sum(d, axis=0, keepdims=True)

    full = lambda a: pl.BlockSpec(a.shape, lambda i: (0, 0))
    row = pl.BlockSpec((T, 512), lambda i: (i, 0))
    return _call(body, name="conv_bwd", grid=(nt,),
                 in_specs=_halo_specs(T, S, 512, 1) + _halo_specs(T, S, 512, 0) + [full(conv_w)],
                 out_specs=[row, full(conv_w), pl.BlockSpec((1, 512), lambda i: (0, 0))],
                 out_shape=[jax.ShapeDtypeStruct((S, 512), F32), jax.ShapeDtypeStruct(conv_w.shape, F32),
                            jax.ShapeDtypeStruct((1, 512), F32)],
                 scratch=[pltpu.VMEM((T + 2 * HALO, 512), F32)] * 2)(z, z, z, dxc, dxc, dxc, conv_w)


MESH_ID = pl.DeviceIdType.MESH
ANY = pl.BlockSpec(memory_space=pl.ANY)


def _place():
    x, y, c = lax.axis_index("x"), lax.axis_index("y"), lax.axis_index("c")
    chips = [(1 - x, y), (x, 1 - y), (1 - x, 1 - y)]
    return x, y, c, chips


def _slot(px, py, pc):
    return 4 * px + 2 * py + pc


def all_gather(arrays):
    n = len(arrays)

    def body(*refs):
        ins, outs = refs[:n], refs[n:2 * n]
        send, recv, loc = refs[2 * n:]
        x, y, c, chips = _place()
        me, sibling = (x, y, c), (x, y, 1 - c)

        def copy(a, k, block, to, src=None):
            slot = outs[a].at[_slot(*block)]
            return pltpu.make_async_remote_copy(src_ref=slot if src is None else src, dst_ref=slot,
                                                send_sem=send.at[a * 7 + k], recv_sem=recv.at[a * 7 + k],
                                                device_id=to, device_id_type=MESH_ID)

        local = [pltpu.make_async_copy(ins[a], outs[a].at[_slot(*me)], loc.at[a]) for a in range(n)]
        for cp in local:
            cp.start()
        first = []
        for a in range(n):
            first.append(copy(a, 0, me, sibling, src=ins[a]))
            first += [copy(a, 1 + j, me, (*chip, c), src=ins[a]) for j, chip in enumerate(chips)]
        for cp in first:
            cp.start()
        passed = []
        for a in range(n):
            for j, chip in enumerate(chips):
                copy(a, 1 + j, (*chip, c), me).wait_recv()
                cp = copy(a, 4 + j, (*chip, c), sibling)
                cp.start()
                passed.append(cp)
        for a in range(n):
            copy(a, 0, sibling, me).wait_recv()
            for j, chip in enumerate(chips):
                copy(a, 4 + j, (*chip, 1 - c), me).wait_recv()
        for cp in first + passed:
            cp.wait_send()
        for cp in local:
            cp.wait()

    return pl.pallas_call(
        body, name="all_gather", in_specs=[ANY] * n, out_specs=[ANY] * n,
        out_shape=[jax.ShapeDtypeStruct((NDEV,) + a.shape, a.dtype) for a in arrays],
        scratch_shapes=[pltpu.SemaphoreType.DMA((n * 7,)), pltpu.SemaphoreType.DMA((n * 7,)),
                        pltpu.SemaphoreType.DMA((n,))],
    )(*arrays)


def grad_pair(gs):
    def body(gs_ref, o_ref, send, recv):
        x, y, c, _ = _place()
        cps = [pltpu.make_async_remote_copy(src_ref=gs_ref.at[2 * q + 1], dst_ref=o_ref.at[q], send_sem=send.at[q],
                                            recv_sem=recv.at[q], device_id=(x, y, 1 - c), device_id_type=MESH_ID)
               for q in range(4)]
        for cp in cps:
            cp.start()
        for cp in cps:
            cp.wait_recv()
        for cp in cps:
            cp.wait_send()

    return pl.pallas_call(
        body, name="grad_pair", in_specs=[ANY], out_specs=ANY,
        out_shape=jax.ShapeDtypeStruct((4,) + gs.shape[1:], gs.dtype),
        scratch_shapes=[pltpu.SemaphoreType.DMA((4,)), pltpu.SemaphoreType.DMA((4,))],
    )(gs)


def pair_add(gs, got, T=128):
    _, R, C = gs.shape
    T = min(T, R)

    def body(g0, g1, g2, g3, got_ref, own_ref, out_ref):
        own_ref[...] = g0[0] + got_ref[0]
        for q, g in enumerate((g1, g2, g3)):
            out_ref[q] = (g[0] + got_ref[q + 1]).astype(out_ref.dtype)

    even = lambda q: pl.BlockSpec((1, T, C), lambda i: (2 * q, i, 0))
    return _call(body, name="pair_add", grid=(R // T,),
                 in_specs=[even(0), even(1), even(2), even(3), pl.BlockSpec((4, T, C), lambda i: (0, i, 0))],
                 out_specs=[pl.BlockSpec((T, C), lambda i: (i, 0)), pl.BlockSpec((3, T, C), lambda i: (0, i, 0))],
                 out_shape=[jax.ShapeDtypeStruct((R, C), F32), jax.ShapeDtypeStruct((3, R, C), MXU_DTYPE)],
                 )(gs, gs, gs, gs, got)


def grad_cross(part, gr):
    def body(p_ref, gr_ref, o_ref, or_ref, send, recv, loc):
        x, y, c, chips = _place()
        peers = [(x, y, 1 - c)] + [(*chip, c) for chip in chips] + [(*chip, 1 - c) for chip in chips]
        local = pltpu.make_async_copy(gr_ref, or_ref.at[_slot(x, y, c)], loc.at[0])
        local.start()
        by_offset = [(x, 1 - y), (1 - x, y), (1 - x, 1 - y)]
        cps = [pltpu.make_async_remote_copy(src_ref=p_ref.at[j], dst_ref=o_ref.at[j], send_sem=send.at[j],
                                            recv_sem=recv.at[j], device_id=(*chip, c), device_id_type=MESH_ID)
               for j, chip in enumerate(by_offset)]
        cps += [pltpu.make_async_remote_copy(src_ref=gr_ref, dst_ref=or_ref.at[_slot(x, y, c)], send_sem=send.at[3 + k],
                                             recv_sem=recv.at[3 + k], device_id=peer, device_id_type=MESH_ID)
                for k, peer in enumerate(peers)]
        for cp in cps:
            cp.start()
        for cp in cps[:3]:
            cp.wait_recv()
        for k, peer in enumerate(peers):
            pltpu.make_async_remote_copy(src_ref=gr_ref, dst_ref=or_ref.at[_slot(*peer)], send_sem=send.at[3 + k],
                                         recv_sem=recv.at[3 + k], device_id=peer, device_id_type=MESH_ID).wait_recv()
        for cp in cps:
            cp.wait_send()
        local.wait()

    return pl.pallas_call(
        body, name="grad_cross", in_specs=[ANY, ANY], out_specs=[ANY, ANY],
        out_shape=[jax.ShapeDtypeStruct(part.shape, part.dtype), jax.ShapeDtypeStruct((NDEV,) + gr.shape, gr.dtype)],
        scratch_shapes=[pltpu.SemaphoreType.DMA((10,)), pltpu.SemaphoreType.DMA((10,)), pltpu.SemaphoreType.DMA((1,))],
    )(part, gr)


def adamw(name, gparts, w, m, v, T=128):
    R, C = w.shape
    T = min(T, R)
    ng = len(gparts)

    def body(*refs):
        g_refs = refs[:ng]
        w_ref, m_ref, v_ref, go_ref, d_ref, mo_ref, vo_ref = refs[ng:]
        g = None
        for g_ref in g_refs:
            for k in range(g_ref.shape[0]):
                t = g_ref[k].astype(F32)
                g = t if g is None else g + t
        mn = ADAM_B1 * m_ref[...] + (1.0 - ADAM_B1) * g
        vn = ADAM_B2 * v_ref[...] + (1.0 - ADAM_B2) * (g * g)
        m_hat = mn / (1.0 - ADAM_B1 ** ADAM_STEP)
        v_hat = vn / (1.0 - ADAM_B2 ** ADAM_STEP)
        go_ref[...] = g
        d_ref[...] = -ADAM_LR * (m_hat / (jnp.sqrt(v_hat) + ADAM_EPS) + ADAM_WD * w_ref[...])
        mo_ref[...] = mn
        vo_ref[...] = vn

    row = pl.BlockSpec((T, C), lambda i: (i, 0))
    sd = jax.ShapeDtypeStruct((R, C), F32)
    return _call(body, name=name, grid=(R // T,),
                 in_specs=[pl.BlockSpec((g.shape[0], T, C), lambda i: (0, i, 0)) for g in gparts] + [row, row, row],
                 out_specs=[row] * 4, out_shape=[sd] * 4)(*gparts, w, m, v)


PACK_W = 1024
PACK_TILE = 16
PACK_ROWS = 128


def _entry_rows(shape):
    n = 1
    for s in shape:
        n *= s
    return n, -(-n // (PACK_TILE * PACK_W)) * PACK_TILE


def _pack(arrays, dtype):
    mats, rows = [], 0
    for a in arrays:
        n, r = _entry_rows(a.shape)
        flat = a.astype(dtype).reshape(-1)
        if r * PACK_W != n:
            flat = jnp.pad(flat, (0, r * PACK_W - n))
        mats.append(flat.reshape(r, PACK_W))
        rows += r
    total = -(-rows // PACK_ROWS) * PACK_ROWS
    if total > rows:
        mats.append(jnp.zeros((total - rows, PACK_W), dtype))
    return jnp.concatenate(mats, axis=0)


def _unpack(buf, shapes, lead=()):
    out, off = [], 0
    for sh in shapes:
        n, r = _entry_rows(sh)
        piece = buf[..., off:off + r, :].reshape(lead + (r * PACK_W,))
        out.append(piece[..., :n].reshape(lead + tuple(sh)))
        off += r
    return out


def _unshard(parts, axis):
    return jnp.concatenate([parts[d] for d in range(NDEV)], axis=axis)


def _slab_cols(w, head, used, n_heads, slab=LANES):
    lead = w.shape[:-1]
    w = w.reshape(lead + (n_heads, head))[..., :used]
    w = jnp.pad(w, [(0, 0)] * len(lead) + [(0, 0), (0, slab - used)])
    return w.reshape(lead + (n_heads * slab,))


def _unslab_cols(g, used, n_heads, slab=LANES):
    lead = g.shape[:-1]
    return g.reshape(lead + (n_heads, slab))[..., :used].reshape(lead + (n_heads * used,))


def _block_diag(w):
    eye = jnp.eye(8, dtype=w.dtype)
    return jnp.einsum("nij,nm->nimj", w, eye).reshape(512, 512)


def _block_diag_t(g):
    g = g.reshape(8, 64, 8, 64)
    return jnp.stack([g[n, :, n, :] for n in range(8)])


BIG = (("e_w_in", 2), ("e_w_out", 1), ("o_w_in", 2), ("o_w_uq", 2), ("o_w_ukv", 2), ("o_w_out", 1),
       ("w_mlp1", 2), ("w_mlp2", 1))
SMALL = (("o_norm_mix", 1), ("o_g_cq", 1), ("o_conv_w", 2), ("o_conv_b", 1), ("o_lru_ba", 2), ("o_lru_bx", 2),
         ("o_lru_lambda", 2))
REPL = ("e_norm_mix", "e_sink", "e_w_pool", "e_pool_scale", "o_g_ckv", "o_lru_wa", "o_lru_wx", "norm_mlp",
        "final_norm")
WEIGHTS = ("e_norm_mix", "e_w_in", "e_sink", "e_w_pool", "e_pool_scale", "e_w_out", "o_norm_mix", "o_w_in", "o_g_cq",
           "o_w_uq", "o_g_ckv", "o_w_ukv", "o_conv_w", "o_conv_b", "o_lru_wa", "o_lru_ba", "o_lru_wx", "o_lru_bx",
           "o_lru_lambda", "o_w_out", "norm_mlp", "w_mlp1", "w_mlp2", "final_norm")


def _mlp_fwd(l, x, g, w1, w2):
    u1, hn = mm_nn(f"mlp1_{l}", [x], [w1], norm_g=g, emit_norm=True, out_dtype=MXU_DTYPE, tm=1024, tn=1024)
    x2 = mm_nn(f"mlp2_{l}", [u1], [w2], act="relu2", res=x, tm=512, tn=1024)
    return x2, (x, u1, hn)


def _mlp_bwd(l, saved, g, w1, w2, dx2):
    x, u1, hn = saved
    du1 = mm_nt(f"mlp2_dx_{l}", [dx2], [w2], relu2_of=u1, out_dtype=MXU_DTYPE, tm=1024, tn=1024)
    dw2 = mm_tn(f"mlp2_dw_{l}", u1, dx2, act="relu2", tm=1024, tn=1024)
    dhn = mm_nt(f"mlp1_dx_{l}", [du1], [w1], tm=512, tn=1024)
    dw1 = mm_tn(f"mlp1_dw_{l}", hn, du1, tm=1024, tn=1024)
    dx, dg = rms_bwd(f"mlp_norm_bwd_{l}", x, g, dhn, dx2)
    return dx, dg, dw1, dw2


def kernel(x, e_norm_mix, e_w_in, e_sink, e_w_pool, e_pool_scale, e_w_out, o_norm_mix, o_w_in, o_g_cq, o_w_uq, o_g_ckv, o_w_ukv, o_conv_w, o_conv_b, o_lru_wa, o_lru_ba, o_lru_wx, o_lru_bx, o_lru_lambda, o_w_out, norm_mlp, w_mlp1, w_mlp2, final_norm, loss_target, m_e_norm_mix, m_e_w_in, m_e_sink, m_e_w_pool, m_e_pool_scale, m_e_w_out, m_o_norm_mix, m_o_w_in, m_o_g_cq, m_o_w_uq, m_o_g_ckv, m_o_w_ukv, m_o_conv_w, m_o_conv_b, m_o_lru_wa, m_o_lru_ba, m_o_lru_wx, m_o_lru_bx, m_o_lru_lambda, m_o_w_out, m_norm_mlp, m_w_mlp1, m_w_mlp2, m_final_norm, v_e_norm_mix, v_e_w_in, v_e_sink, v_e_w_pool, v_e_pool_scale, v_e_w_out, v_o_norm_mix, v_o_w_in, v_o_g_cq, v_o_w_uq, v_o_g_ckv, v_o_w_ukv, v_o_conv_w, v_o_conv_b, v_o_lru_wa, v_o_lru_ba, v_o_lru_wx, v_o_lru_bx, v_o_lru_lambda, v_o_w_out, v_norm_mlp, v_w_mlp1, v_w_mlp2, v_final_norm):
    W = dict(e_norm_mix=e_norm_mix, e_w_in=e_w_in, e_sink=e_sink, e_w_pool=e_w_pool, e_pool_scale=e_pool_scale, e_w_out=e_w_out, o_norm_mix=o_norm_mix, o_w_in=o_w_in, o_g_cq=o_g_cq, o_w_uq=o_w_uq, o_g_ckv=o_g_ckv, o_w_ukv=o_w_ukv, o_conv_w=o_conv_w, o_conv_b=o_conv_b, o_lru_wa=o_lru_wa, o_lru_ba=o_lru_ba, o_lru_wx=o_lru_wx, o_lru_bx=o_lru_bx, o_lru_lambda=o_lru_lambda, o_w_out=o_w_out, norm_mlp=norm_mlp, w_mlp1=w_mlp1, w_mlp2=w_mlp2, final_norm=final_norm)
    Mo = dict(e_norm_mix=m_e_norm_mix, e_w_in=m_e_w_in, e_sink=m_e_sink, e_w_pool=m_e_w_pool, e_pool_scale=m_e_pool_scale, e_w_out=m_e_w_out, o_norm_mix=m_o_norm_mix, o_w_in=m_o_w_in, o_g_cq=m_o_g_cq, o_w_uq=m_o_w_uq, o_g_ckv=m_o_g_ckv, o_w_ukv=m_o_w_ukv, o_conv_w=m_o_conv_w, o_conv_b=m_o_conv_b, o_lru_wa=m_o_lru_wa, o_lru_ba=m_o_lru_ba, o_lru_wx=m_o_lru_wx, o_lru_bx=m_o_lru_bx, o_lru_lambda=m_o_lru_lambda, o_w_out=m_o_w_out, norm_mlp=m_norm_mlp, w_mlp1=m_w_mlp1, w_mlp2=m_w_mlp2, final_norm=m_final_norm)
    Vo = dict(e_norm_mix=v_e_norm_mix, e_w_in=v_e_w_in, e_sink=v_e_sink, e_w_pool=v_e_w_pool, e_pool_scale=v_e_pool_scale, e_w_out=v_e_w_out, o_norm_mix=v_o_norm_mix, o_w_in=v_o_w_in, o_g_cq=v_o_g_cq, o_w_uq=v_o_w_uq, o_g_ckv=v_o_g_ckv, o_w_ukv=v_o_w_ukv, o_conv_w=v_o_conv_w, o_conv_b=v_o_conv_b, o_lru_wa=v_o_lru_wa, o_lru_ba=v_o_lru_ba, o_lru_wx=v_o_lru_wx, o_lru_bx=v_o_lru_bx, o_lru_lambda=v_o_lru_lambda, o_w_out=v_o_w_out, norm_mlp=v_norm_mlp, w_mlp1=v_w_mlp1, w_mlp2=v_w_mlp2, final_norm=v_final_norm)

    S = x.shape[1]
    x0 = x[0]
    target = loss_target[0]

    big_g, small_g = all_gather([_pack([W[n] for n, _ in BIG], MXU_DTYPE), _pack([W[n] for n, _ in SMALL], F32)])
    full = {}
    for (n, ax), parts in zip(BIG, _unpack(big_g, [W[n].shape for n, _ in BIG], (NDEV,))):
        full[n] = _unshard(parts, ax)
    for (n, ax), parts in zip(SMALL, _unpack(small_g, [W[n].shape for n, _ in SMALL], (NDEV,))):
        full[n] = _unshard(parts, ax)

    def even_in(w):
        return jnp.concatenate([_slab_cols(w[:, 0:512], 64, 64, 8), _slab_cols(w[:, 512:640], 64, 64, 2),
                                _slab_cols(w[:, 640:768], 64, 64, 2), w[:, 768:1280]], axis=1)

    def even_in_t(g):
        return jnp.concatenate([_unslab_cols(g[:, 0:1024], 64, 8), _unslab_cols(g[:, 1024:1280], 64, 2),
                                _unslab_cols(g[:, 1280:1536], 64, 2), g[:, 1536:2048]], axis=1)

    def slab_rows(w, n_heads):
        return _slab_cols(w.T, 64, 64, n_heads).T

    def unslab_rows(g, n_heads):
        return _unslab_cols(g.T, 64, n_heads).T

    def odd_in(w):
        return jnp.concatenate([w[:, 0:384], jnp.pad(w[:, 384:416], ((0, 0), (0, 96))), w[:, 416:1440]], axis=1)

    def odd_in_t(g):
        return jnp.concatenate([g[:, 0:416], g[:, 512:1536]], axis=1)

    def uq(w):
        return _slab_cols(w, 96, 96, 8)

    def ukv(w):
        w = w.reshape(128, 8, 128)
        pad = lambda t: jnp.pad(t, ((0, 0), (0, 0), (0, 64))).reshape(128, 1024)
        return pad(w[:, :, :64]), pad(w[:, :, 64:])

    def ukv_t(gk, gv):
        gk = gk.reshape(128, 8, 128)[:, :, :64]
        gv = gv.reshape(128, 8, 128)[:, :, :64]
        return jnp.concatenate([gk, gv], axis=2).reshape(128, 1024)

    tabs_swa = rope_tables(S, 0, SWA_HALF)
    tabs_mq = rope_tables(S, 64, MLA_HALF)
    tabs_mk = rope_tables(S, 0, MLA_HALF)
    row = lambda v: v.reshape(1, -1)

    saved = []
    xcur = x0
    for l in range(4):
        j = l // 2
        if l % 2 == 0:
            w_in = even_in(full["e_w_in"][j])
            w_out = full["e_w_out"][j]
            w_out_a, w_out_b = slab_rows(w_out[0:512], 8), w_out[512:1024]
            w_pool = W["e_w_pool"][j].astype(MXU_DTYPE)
            z, h = mm_nn(f"e_in_{j}", [xcur], [w_in], norm_g=row(W["e_norm_mix"][j]), emit_norm=True, tm=512, tn=2048)
            qkv = swa_prep(z, tabs_swa)
            ya, lse = swa_fwd(qkv, W["e_sink"][j])
            yb = pool_fwd(z, w_pool, row(W["e_pool_scale"][j]))
            x1 = mm_nn(f"e_out_{j}", [ya, yb], [w_out_a, w_out_b], res=xcur)
            mix = (xcur, z, h, qkv, lse, ya, yb, w_in, w_out_a, w_out_b, w_pool)
        else:
            w_in = odd_in(full["o_w_in"][j])
            w_out = full["o_w_out"][j]
            w_out_a, w_out_b = slab_rows(w_out[0:512], 8), w_out[512:1024]
            w_uq = uq(full["o_w_uq"][j])
            w_k, w_v = ukv(full["o_w_ukv"][j])
            g_cq, g_ckv = row(full["o_g_cq"][j]), row(W["o_g_ckv"][j])
            w_gate = jnp.concatenate([_block_diag(W["o_lru_wa"][j, 0]), _block_diag(W["o_lru_wx"][j, 0]),
                                      _block_diag(W["o_lru_wa"][j, 1]), _block_diag(W["o_lru_wx"][j, 1])], axis=1).astype(MXU_DTYPE)
            b_gate = jnp.concatenate([full["o_lru_ba"][j, 0], full["o_lru_bx"][j, 0], full["o_lru_ba"][j, 1],
                                      full["o_lru_bx"][j, 1]]).reshape(1, 2048)
            lam = full["o_lru_lambda"][j].reshape(1, 1024)
            conv_w, conv_b = full["o_conv_w"][j], row(full["o_conv_b"][j])
            z, h = mm_nn(f"o_in_{j}", [xcur], [w_in], norm_g=row(full["o_norm_mix"][j]), emit_norm=True, tm=512, tn=1536)
            q, k, v, nq, nkv = mla_prep(z, g_cq, g_ckv, w_uq, w_k, w_v, tabs_mq, tabs_mk)
            yc, lse = mla_fwd(q, k, mla_vt(v), TQ=1024)
            xc, a0, b0, a1, b1 = lru_pre(z, conv_w, conv_b, w_gate, b_gate, lam)
            h0, h1 = lru_scan(f"lru_scan_fwd_{j}", a0, b0, a1, b1, adjoint=False)
            yd = lru_gate(h0, h1, z)
            x1 = mm_nn(f"o_out_{j}", [yc, yd], [w_out_a, w_out_b], res=xcur)
            mix = (xcur, z, h, q, k, v, nq, nkv, yc, lse, xc, a0, a1, h0, h1, yd, w_in, w_out_a, w_out_b, w_uq, w_k, w_v,
                   g_cq, g_ckv, w_gate, b_gate, lam, conv_w)
        xcur, mlp = _mlp_fwd(l, x1, row(W["norm_mlp"][l]), full["w_mlp1"][l], full["w_mlp2"][l])
        saved.append((mix, mlp))

    loss_row, dx, dg_final = loss_head(xcur, row(W["final_norm"]), target)

    G = {n: [None] * W[n].shape[0] for n in WEIGHTS if n != "final_norm"}
    G["final_norm"] = dg_final.reshape(-1)
    for l in reversed(range(4)):
        j = l // 2
        mix, mlp = saved[l]
        dx, dg, dw1, dw2 = _mlp_bwd(l, mlp, row(W["norm_mlp"][l]), full["w_mlp1"][l], full["w_mlp2"][l], dx)
        G["norm_mlp"][l], G["w_mlp1"][l], G["w_mlp2"][l] = dg.reshape(-1), dw1, dw2
        if l % 2 == 0:
            xin, z, h, qkv, lse, ya, yb, w_in, w_out_a, w_out_b, w_pool = mix
            w_out_cat = jnp.concatenate([w_out_a, w_out_b], axis=0)
            dycat = mm_nt(f"e_out_dx_{j}", [dx], [w_out_cat])
            G["e_w_out"][j] = jnp.concatenate([unslab_rows(mm_tn(f"e_out_dwa_{j}", ya, dx), 8),
                                               mm_tn(f"e_out_dwb_{j}", yb, dx)], axis=0)
            dq, dk, dv, dsink = swa_bwd(qkv, W["e_sink"][j], lse, dycat, tabs_swa)
            du, dwp, dsc = pool_bwd(z, dycat, w_pool, row(W["e_pool_scale"][j]))
            G["e_sink"][j], G["e_w_pool"][j], G["e_pool_scale"][j] = dsink[0, 0:8], dwp, dsc.reshape(-1)
            dh = mm_nt(f"e_in_dx_{j}", [dq, dk, dv, du], [(w_in, 0, 1024), (w_in, 4, 256), (w_in, 5, 256), (w_in, 3, 512)])
            G["e_w_in"][j] = even_in_t(jnp.concatenate(
                [mm_tn(f"e_in_dwq_{j}", h, dq), mm_tn(f"e_in_dwk_{j}", h, dk), mm_tn(f"e_in_dwv_{j}", h, dv),
                 mm_tn(f"e_in_dwu_{j}", h, du)], axis=1))
            dx, dg = rms_bwd(f"e_norm_bwd_{j}", xin, row(W["e_norm_mix"][j]), dh, dx)
            G["e_norm_mix"][j] = dg.reshape(-1)
        else:
            (xin, z, h, q, k, v, nq, nkv, yc, lse, xc, a0, a1, h0, h1, yd, w_in, w_out_a, w_out_b, w_uq, w_k, w_v,
             g_cq, g_ckv, w_gate, b_gate, lam, conv_w) = mix
            w_out_cat = jnp.concatenate([w_out_a, w_out_b], axis=0)
            dycat = mm_nt(f"o_out_dx_{j}", [dx], [w_out_cat])
            G["o_w_out"][j] = jnp.concatenate([unslab_rows(mm_tn(f"o_out_dwa_{j}", yc, dx), 8),
                                               mm_tn(f"o_out_dwb_{j}", yd, dx)], axis=0)
            delta, dob = mla_delta(yc, dycat)
            dqt, dk, dv = mla_bwd(q, k, v, dob, lse, delta)
            dq = mla_dq(dqt)
            dza, dqp, dgq, dgkv = mla_prep_bwd(z, g_cq, g_ckv, w_uq, w_k, w_v, dq, dk, dv, tabs_mq, tabs_mk)
            G["o_g_cq"][j], G["o_g_ckv"][j] = dgq.reshape(-1), dgkv.reshape(-1)
            G["o_w_uq"][j] = _unslab_cols(mm_tn(f"o_uq_dw_{j}", nq, dqp), 96, 8)
            G["o_w_ukv"][j] = ukv_t(mm_tn(f"o_uk_dw_{j}", nkv, dk), mm_tn(f"o_uv_dw_{j}", nkv, dv))
            dxg, dhh = lru_gate_bwd(h0, h1, z, dycat)
            g1, g0 = lru_scan(f"lru_scan_bwd_{j}", a1, dhh, a0, dhh, adjoint=True)
            dxc, dpre, dbias, dlam = lru_bwd_point(xc, h0, h1, g0, g1, w_gate, b_gate, lam)
            dwg = mm_tn(f"o_gate_dw_{j}", xc, dpre)
            G["o_lru_wa"][j] = jnp.stack([_block_diag_t(dwg[:, 0:512]), _block_diag_t(dwg[:, 1024:1536])])
            G["o_lru_wx"][j] = jnp.stack([_block_diag_t(dwg[:, 512:1024]), _block_diag_t(dwg[:, 1536:2048])])
            G["o_lru_ba"][j] = jnp.stack([dbias[0, 0:512], dbias[0, 1024:1536]])
            G["o_lru_bx"][j] = jnp.stack([dbias[0, 512:1024], dbias[0, 1536:2048]])
            G["o_lru_lambda"][j] = dlam.reshape(2, 512)
            dxr, dcw, dcb = conv_bwd(z, dxc, conv_w)
            G["o_conv_w"][j], G["o_conv_b"][j] = dcw, dcb.reshape(-1)
            dh = mm_nt(f"o_in_dx_{j}", [dza, dxr, dxg], [(w_in, 0, 512), (w_in, 1, 512), (w_in, 2, 512)])
            G["o_w_in"][j] = odd_in_t(jnp.concatenate(
                [mm_tn(f"o_in_dwa_{j}", h, dza), mm_tn(f"o_in_dwr_{j}", h, dxr), mm_tn(f"o_in_dwg_{j}", h, dxg)], axis=1))
            dx, dg = rms_bwd(f"o_norm_bwd_{j}", xin, row(full["o_norm_mix"][j]), dh, dx)
            G["o_norm_mix"][j] = dg.reshape(-1)
    grad_x = dx[None]
    G["final_norm"] = [G["final_norm"]]

    sharded = BIG + SMALL
    me = 4 * lax.axis_index("x") + 2 * lax.axis_index("y") + lax.axis_index("c")

    def layers(T, n):
        return [T[n]] if n == "final_norm" else [T[n][l] for l in range(T[n].shape[0])]

    def layer_shapes(names):
        return [a.shape for n in names for a in layers(W, n)]

    def shards_of(n, ax, dev):
        size = W[n].shape[ax]
        return [lax.dynamic_slice_in_dim(g, dev * size, size, axis=ax - 1) for g in G[n]]

    gs = jnp.stack([_pack([p for n, ax in sharded for p in shards_of(n, ax, me ^ r)], F32) for r in range(NDEV)])
    gr = _pack([g for n in REPL for g in G[n]] + [loss_row[0, 0:1]], F32)
    mine, others = pair_add(gs, grad_pair(gs))
    others, gr_all = grad_cross(others, gr)
    names_s = [n for n, _ in sharded]
    outs_s = adamw("adamw_sharded", [mine[None], others],
                   *[_pack([a for n in names_s for a in layers(T, n)], F32) for T in (W, Mo, Vo)])
    outs_r = adamw("adamw_replicated", [gr_all],
                   *[_pack([a for n in REPL for a in layers(T, n)] + [jnp.zeros((1,), F32)], F32) for T in (W, Mo, Vo)])
    res = [dict(), dict(), dict(), dict()]
    for kind in range(4):
        pieces = (_unpack(outs_s[kind], layer_shapes(names_s))
                  + _unpack(outs_r[kind], layer_shapes(REPL) + [(1,)]))
        for n in names_s + list(REPL):
            count = len(layers(W, n))
            got, pieces = pieces[:count], pieces[count:]
            res[kind][n] = got[0] if n == "final_norm" else jnp.stack(got)
        res[kind]["loss"] = pieces[0]
    loss = res[0]["loss"][0]
    return (loss, grad_x, *[res[0][n] for n in WEIGHTS], *[res[1][n] for n in WEIGHTS],
            *[res[2][n] for n in WEIGHTS], *[res[3][n] for n in WEIGHTS])
```

```python
import functools

import jax
import jax.numpy as jnp
from jax import lax
from jax.experimental import pallas as pl
from jax.experimental.pallas import tpu as pltpu

F32 = jnp.float32
MXU_DTYPE = jnp.bfloat16
EPS = 1e-6
ROPE_THETA = 10000.0
NDEV = 8
LANES = 128
VMEM_LIMIT = 48 * 1024 * 1024

D_MODEL = 1024
D_FF = 4096
LRU_C = 8.0
POOL_WINDOWS = (2, 4, 8, 16)
HALO = 8
MLA_SCALE = 96.0 ** -0.5
LOG2E = 1.4426950408889634
MLA_QSCALE = MLA_SCALE * LOG2E

ADAM_LR, ADAM_B1, ADAM_B2, ADAM_EPS, ADAM_WD, ADAM_STEP = 0.001, 0.9, 0.999, 1e-08, 0.01, 10


def _mx(v):
    return v.astype(MXU_DTYPE)


def _call(body, *, name, grid, in_specs, out_specs, out_shape, scratch=()):
    return pl.pallas_call(
        body, name=name, grid=grid, in_specs=in_specs, out_specs=out_specs, out_shape=out_shape,
        scratch_shapes=list(scratch),
        compiler_params=pltpu.CompilerParams(
            dimension_semantics=("arbitrary",) * len(grid), vmem_limit_bytes=VMEM_LIMIT),
    )


def _dot(a, b):
    return jnp.dot(a, b, preferred_element_type=F32)


def _dot_nt(a, b):
    return lax.dot_general(a, b, (((1,), (1,)), ((), ())), preferred_element_type=F32)


def _dot_tn(a, b):
    return lax.dot_general(a, b, (((0,), (0,)), ((), ())), preferred_element_type=F32)


def _rms(x, g):
    r = lax.rsqrt(jnp.mean(x * x, axis=-1, keepdims=True) + EPS)
    return (x * r) * g


def _rms_bwd(x, g, dy):
    r = lax.rsqrt(jnp.mean(x * x, axis=-1, keepdims=True) + EPS)
    xh = x * r
    dyg = dy * g
    dx = r * (dyg - xh * jnp.mean(dyg * xh, axis=-1, keepdims=True))
    return dx, jnp.sum(dy * xh, axis=0, keepdims=True)


def _sigmoid(x):
    return 1.0 / (1.0 + jnp.exp(-x))


def _log1p(e):
    u = 1.0 + e
    d = u - 1.0
    return jnp.where(d == 0.0, e, jnp.log(u) * (e / jnp.where(d == 0.0, 1.0, d)))


def _softplus(x):
    return jnp.maximum(x, 0.0) + _log1p(jnp.exp(-jnp.abs(x)))


def _expm1(x):
    u = jnp.exp(x)
    lu = jnp.log(u)
    safe = jnp.where((lu == 0.0) | (u == 0.0), 1.0, lu)
    return jnp.where(u == 1.0, x, jnp.where(u == 0.0, -1.0, (u - 1.0) * x / safe))


_GELU_K = 0.7978845608028654


def _gelu(x):
    return 0.5 * x * (1.0 + jnp.tanh(_GELU_K * (x + 0.044715 * x * x * x)))


def _gelu_grad(x):
    t = jnp.tanh(_GELU_K * (x + 0.044715 * x * x * x))
    return 0.5 * (1.0 + t) + 0.5 * x * (1.0 - t * t) * _GELU_K * (1.0 + 3.0 * 0.044715 * x * x)


def _rope(x, c, sa, sb, half):
    return x * c + pltpu.roll(x, LANES - half, 1) * sa + pltpu.roll(x, half, 1) * sb


def _rope_t(d, c, sa, sb, half):
    return d * c - pltpu.roll(d, LANES - half, 1) * sa - pltpu.roll(d, half, 1) * sb


def _as_cols(a):
    return a if isinstance(a, tuple) else (a, 0, a.shape[1])


def mm_nn(name, a_list, b_list, *, res=None, act=None, norm_g=None, emit_norm=False,
          out_dtype=F32, tm=512, tn=1024):
    a_list = [_as_cols(a) for a in a_list]
    M, N = a_list[0][0].shape[0], b_list[0].shape[1]
    tm, tn = min(tm, M), min(tn, N)
    na = len(a_list)

    def body(*refs):
        a_refs, b_refs = refs[:na], refs[na:2 * na]
        k = 2 * na
        g_ref = res_ref = hn_ref = None
        if norm_g is not None:
            g_ref = refs[k]
            k += 1
        if res is not None:
            res_ref = refs[k]
            k += 1
        o_ref = refs[k]
        if emit_norm:
            hn_ref = refs[k + 1]
        acc = None
        for a_ref, b_ref in zip(a_refs, b_refs):
            a = a_ref[...]
            if g_ref is not None:
                a = _rms(a.astype(F32), g_ref[...])
                if hn_ref is not None:
                    hn_ref[...] = a.astype(hn_ref.dtype)
            if act == "relu2":
                a = jnp.maximum(a, 0.0)
                a = a * a
            d = _dot(_mx(a), _mx(b_ref[...]))
            acc = d if acc is None else acc + d
        if res_ref is not None:
            acc = acc + res_ref[...]
        o_ref[...] = acc.astype(o_ref.dtype)

    in_specs = [pl.BlockSpec((tm, w), functools.partial(lambda i, j, cb: (i, cb), cb=cb)) for (_, cb, w) in a_list]
    in_specs += [pl.BlockSpec((b.shape[0], tn), lambda i, j: (0, j)) for b in b_list]
    args = [a for (a, _, _) in a_list] + list(b_list)
    if norm_g is not None:
        in_specs.append(pl.BlockSpec((1, norm_g.shape[1]), lambda i, j: (0, 0)))
        args.append(norm_g)
    if res is not None:
        in_specs.append(pl.BlockSpec((tm, tn), lambda i, j: (i, j)))
        args.append(res)
    out_specs = [pl.BlockSpec((tm, tn), lambda i, j: (i, j))]
    out_shape = [jax.ShapeDtypeStruct((M, N), out_dtype)]
    if emit_norm:
        K = a_list[0][2]
        out_specs.append(pl.BlockSpec((tm, K), lambda i, j: (i, 0)))
        out_shape.append(jax.ShapeDtypeStruct((M, K), MXU_DTYPE))
    out = _call(body, name=name, grid=(M // tm, N // tn), in_specs=in_specs, out_specs=out_specs,
                out_shape=out_shape)(*args)
    return out if emit_norm else out[0]


def mm_nt(name, a_list, b_list, *, relu2_of=None, out_dtype=F32, tm=512, tn=1536):
    a_list = [_as_cols(a) for a in a_list]
    b_list = [_as_cols(b) for b in b_list]
    M, N = a_list[0][0].shape[0], b_list[0][0].shape[0]
    tm, tn = min(tm, M), min(tn, N)
    na = len(a_list)

    def body(*refs):
        a_refs, b_refs = refs[:na], refs[na:2 * na]
        u_ref = refs[2 * na] if relu2_of is not None else None
        o_ref = refs[-1]
        acc = None
        for a_ref, b_ref in zip(a_refs, b_refs):
            d = _dot_nt(_mx(a_ref[...]), _mx(b_ref[...]))
            acc = d if acc is None else acc + d
        if u_ref is not None:
            acc = acc * (2.0 * jnp.maximum(u_ref[...], 0.0))
        o_ref[...] = acc.astype(o_ref.dtype)

    in_specs = [pl.BlockSpec((tm, w), functools.partial(lambda i, j, cb: (i, cb), cb=cb)) for (_, cb, w) in a_list]
    in_specs += [pl.BlockSpec((tn, w), functools.partial(lambda i, j, cb: (j, cb), cb=cb)) for (_, cb, w) in b_list]
    args = [a for (a, _, _) in a_list] + [b for (b, _, _) in b_list]
    if relu2_of is not None:
        in_specs.append(pl.BlockSpec((tm, tn), lambda i, j: (i, j)))
        args.append(relu2_of)
    return _call(body, name=name, grid=(M // tm, N // tn), in_specs=in_specs,
                 out_specs=pl.BlockSpec((tm, tn), lambda i, j: (i, j)),
                 out_shape=jax.ShapeDtypeStruct((M, N), out_dtype))(*args)


def mm_tn(name, a, b, *, act=None, tm=1024, tn=1024, tk=1024):
    a, acb, Ma = _as_cols(a)
    b, bcb, Nb = _as_cols(b)
    S = a.shape[0]
    tm, tn, tk = min(tm, Ma), min(tn, Nb), min(tk, S)
    a0, b0 = acb * (Ma // tm), bcb * (Nb // tn)

    def body(a_ref, b_ref, o_ref):
        @pl.when(pl.program_id(2) == 0)
        def _():
            o_ref[...] = jnp.zeros_like(o_ref)

        av = a_ref[...]
        if act == "relu2":
            av = jnp.maximum(av, 0.0)
            av = av * av
        o_ref[...] += _dot_tn(_mx(av), _mx(b_ref[...]))

    return _call(body, name=name, grid=(Ma // tm, Nb // tn, S // tk),
                 in_specs=[pl.BlockSpec((tk, tm), lambda i, j, k: (k, a0 + i)),
                           pl.BlockSpec((tk, tn), lambda i, j, k: (k, b0 + j))],
                 out_specs=pl.BlockSpec((tm, tn), lambda i, j, k: (i, j)),
                 out_shape=jax.ShapeDtypeStruct((Ma, Nb), F32))(a, b)


def rms_bwd(name, x, g, dh, dres, T=512):
    S, D = x.shape
    T = min(T, S)

    def body(x_ref, g_ref, dh_ref, dres_ref, dx_ref, dg_ref):
        @pl.when(pl.program_id(0) == 0)
        def _():
            dg_ref[...] = jnp.zeros_like(dg_ref)

        dx, dg = _rms_bwd(x_ref[...], g_ref[...], dh_ref[...])
        dx_ref[...] = dres_ref[...] + dx
        dg_ref[...] += dg

    row = pl.BlockSpec((T, D), lambda i: (i, 0))
    vec = pl.BlockSpec((1, D), lambda i: (0, 0))
    return _call(body, name=name, grid=(S // T,), in_specs=[row, vec, row, row], out_specs=[row, vec],
                 out_shape=[jax.ShapeDtypeStruct((S, D), F32), jax.ShapeDtypeStruct((1, D), F32)])(x, g, dh, dres)


def loss_head(x, g, target, T=512):
    S, D = x.shape
    T = min(T, S)

    def body(x_ref, g_ref, t_ref, loss_ref, dx_ref, dg_ref):
        @pl.when(pl.program_id(0) == 0)
        def _():
            dg_ref[...] = jnp.zeros_like(dg_ref)
            loss_ref[...] = jnp.zeros_like(loss_ref)

        x = x_ref[...]
        err = _rms(x, g_ref[...]) - t_ref[...]
        loss_ref[...] += 0.5 * jnp.sum(jnp.sum(err * err, axis=-1, keepdims=True) / D, axis=0, keepdims=True)
        dx, dg = _rms_bwd(x, g_ref[...], err / D)
        dx_ref[...] = dx
        dg_ref[...] += dg

    row = pl.BlockSpec((T, D), lambda i: (i, 0))
    vec = pl.BlockSpec((1, D), lambda i: (0, 0))
    return _call(body, name="loss_head", grid=(S // T,), in_specs=[row, vec, row],
                 out_specs=[pl.BlockSpec((1, LANES), lambda i: (0, 0)), row, vec],
                 out_shape=[jax.ShapeDtypeStruct((1, LANES), F32), jax.ShapeDtypeStruct((S, D), F32),
                            jax.ShapeDtypeStruct((1, D), F32)])(x, g, target)


def rope_tables(S, lo, half):
    inv = ROPE_THETA ** (-jnp.arange(half, dtype=F32) / half)
    ang = jnp.arange(S, dtype=F32)[:, None] * inv[None, :]
    cos, sin = jnp.cos(ang), jnp.sin(ang)
    one = lambda n: jnp.ones((S, n), F32)
    zero = lambda n: jnp.zeros((S, n), F32)
    hi = LANES - lo - 2 * half
    c = jnp.concatenate([one(lo), cos, cos, one(hi)], axis=1)
    sa = jnp.concatenate([zero(lo), -sin, zero(half), zero(hi)], axis=1)
    sb = jnp.concatenate([zero(lo), zero(half), sin, zero(hi)], axis=1)
    return c, sa, sb


SWA_BLOCK = 128
SWA_HALF = 32


def swa_prep(z, tabs, T=512):
    S = z.shape[0]
    T = min(T, S)

    def body(z_ref, c_ref, sa_ref, sb_ref, o_ref):
        c, sa, sb = c_ref[...], sa_ref[...], sb_ref[...]
        for s in range(10):
            sl = slice(s * LANES, (s + 1) * LANES)
            y = _rope(z_ref[:, sl], c, sa, sb, SWA_HALF)
            if s < 8:
                y = y * 0.125
            o_ref[:, sl] = y.astype(o_ref.dtype)
        o_ref[:, 1280:1536] = z_ref[:, 1280:1536].astype(o_ref.dtype)

    tab = pl.BlockSpec((T, LANES), lambda i: (i, 0))
    return _call(body, name="swa_prep", grid=(S // T,),
                 in_specs=[pl.BlockSpec((T, 1536), lambda i: (i, 0)), tab, tab, tab],
                 out_specs=pl.BlockSpec((T, 1536), lambda i: (i, 0)),
                 out_shape=jax.ShapeDtypeStruct((S, 1536), MXU_DTYPE))(z, *tabs)


SWA_GROUP = 4


def _swa_valid(n, S):
    B = SWA_BLOCK
    i = lax.broadcasted_iota(jnp.int32, (SWA_GROUP * B, 3 * B), 0) & (B - 1)
    j = lax.broadcasted_iota(jnp.int32, (SWA_GROUP * B, 3 * B), 1)
    kpos = j + (n - 1) * B
    return (jnp.abs(j - B - i) <= B) & (kpos >= 0) & (kpos < S)


def _swa_sink(sink_ref, hk):
    B = SWA_BLOCK
    row = lax.broadcasted_iota(jnp.int32, (SWA_GROUP * B, 1), 0)
    sk = jnp.full((SWA_GROUP * B, 1), sink_ref[hk * SWA_GROUP], F32)
    for g in range(1, SWA_GROUP):
        sk = jnp.where(row >= g * B, sink_ref[hk * SWA_GROUP + g], sk)
    return sk


def swa_fwd(qkv, sink):
    S = qkv.shape[0]
    B = SWA_BLOCK
    nb = S // B

    def body(sink_ref, q_ref, kp_ref, kc_ref, kn_ref, vp_ref, vc_ref, vn_ref, o_ref, st_ref):
        n = pl.program_id(0)
        valid = _swa_valid(n, S)
        lane = lax.broadcasted_iota(jnp.int32, (B, LANES), 1)
        st = jnp.zeros((B, LANES), F32)
        for hk in range(2):
            sl = slice(hk * LANES, (hk + 1) * LANES)
            k3 = jnp.concatenate([kp_ref[:, sl], kc_ref[:, sl], kn_ref[:, sl]], axis=0)
            v3 = jnp.concatenate([vp_ref[:, sl], vc_ref[:, sl], vn_ref[:, sl]], axis=0)
            heads = [hk * SWA_GROUP + g for g in range(SWA_GROUP)]
            q4 = jnp.concatenate([q_ref[:, h * LANES:(h + 1) * LANES] for h in heads], axis=0)
            s = jnp.where(valid, _dot_nt(q4, k3), -jnp.inf)
            sk = _swa_sink(sink_ref, hk)
            m = jnp.maximum(jnp.max(s, axis=-1, keepdims=True), sk)
            p = jnp.exp(s - m)
            den = jnp.sum(p, axis=-1, keepdims=True) + jnp.exp(sk - m)
            p = p / den
            o4 = _dot(_mx(p), v3)
            lse4 = m + jnp.log(den)
            for g, h in enumerate(heads):
                rows = slice(g * B, (g + 1) * B)
                o_ref[:, h * LANES:(h + 1) * LANES] = o4[rows].astype(o_ref.dtype)
                st = jnp.where(lane == h, lse4[rows], st)
        st_ref[...] = st

    kv = lambda cb, d: pl.BlockSpec((B, 2 * LANES), lambda n: (jnp.clip(n + d, 0, nb - 1), cb))
    return _call(body, name="swa_fwd", grid=(nb,),
                 in_specs=[pl.BlockSpec(memory_space=pltpu.SMEM),
                           pl.BlockSpec((B, 1024), lambda n: (n, 0)),
                           kv(4, -1), kv(4, 0), kv(4, 1), kv(5, -1), kv(5, 0), kv(5, 1)],
                 out_specs=[pl.BlockSpec((B, 1024), lambda n: (n, 0)), pl.BlockSpec((B, LANES), lambda n: (n, 0))],
                 out_shape=[jax.ShapeDtypeStruct((S, 1024), MXU_DTYPE), jax.ShapeDtypeStruct((S, LANES), F32)],
                 )(sink, qkv, qkv, qkv, qkv, qkv, qkv, qkv)


def swa_bwd(qkv, sink, lse, dycat, tabs):
    S = qkv.shape[0]
    B = SWA_BLOCK
    nb = S // B

    def body(sink_ref, q_ref, kp_ref, kc_ref, kn_ref, vp_ref, vc_ref, vn_ref, do_ref, st_ref,
             cq_ref, saq_ref, sbq_ref, ck_ref, sak_ref, sbk_ref,
             dq_ref, dk_ref, dv_ref, dsink_ref, dk_acc, dv_acc):
        n = pl.program_id(0)

        @pl.when(n == 0)
        def _():
            dk_acc[...] = jnp.zeros_like(dk_acc)
            dv_acc[...] = jnp.zeros_like(dv_acc)
            dsink_ref[...] = jnp.zeros_like(dsink_ref)

        @pl.when(n < nb)
        def _():
            valid = _swa_valid(n, S)
            lane = lax.broadcasted_iota(jnp.int32, (B, LANES), 1)
            lane1 = lax.broadcasted_iota(jnp.int32, (1, LANES), 1)
            st = st_ref[...]
            cq, saq, sbq = cq_ref[...], saq_ref[...], sbq_ref[...]
            dsink = jnp.zeros((1, LANES), F32)
            for hk in range(2):
                sl = slice(hk * LANES, (hk + 1) * LANES)
                k3 = jnp.concatenate([kp_ref[:, sl], kc_ref[:, sl], kn_ref[:, sl]], axis=0)
                v3 = jnp.concatenate([vp_ref[:, sl], vc_ref[:, sl], vn_ref[:, sl]], axis=0)
                heads = [hk * SWA_GROUP + g for g in range(SWA_GROUP)]
                q4 = jnp.concatenate([q_ref[:, h * LANES:(h + 1) * LANES] for h in heads], axis=0)
                do4 = jnp.concatenate([_mx(do_ref[:, h * LANES:(h + 1) * LANES]) for h in heads], axis=0)
                lse4 = jnp.concatenate([jnp.sum(jnp.where(lane == h, st, 0.0), axis=-1, keepdims=True) for h in heads],
                                       axis=0)
                p = jnp.where(valid, jnp.exp(_dot_nt(q4, k3) - lse4), 0.0)
                dp = _dot_nt(do4, v3)
                dsum = jnp.sum(p * dp, axis=-1, keepdims=True)
                ds = _mx(p * (dp - dsum))
                dq4 = _dot(ds, k3) * 0.125
                dk_acc[:, sl] += _dot_tn(ds, q4)
                dv_acc[:, sl] += _dot_tn(_mx(p), do4)
                dsk4 = jnp.exp(_swa_sink(sink_ref, hk) - lse4) * dsum
                for g, h in enumerate(heads):
                    rows = slice(g * B, (g + 1) * B)
                    dq_ref[:, h * LANES:(h + 1) * LANES] = _rope_t(dq4[rows], cq, saq, sbq, SWA_HALF)
                    dsink = jnp.where(lane1 == h, -jnp.sum(dsk4[rows], axis=0, keepdims=True), dsink)
            dsink_ref[...] += dsink

        ck, sak, sbk = ck_ref[...], sak_ref[...], sbk_ref[...]
        for hk in range(2):
            sl = slice(hk * LANES, (hk + 1) * LANES)
            dk_ref[:, sl] = _rope_t(dk_acc[0:B, sl], ck, sak, sbk, SWA_HALF)
        dv_ref[...] = dv_acc[0:B, :]
        for acc in (dk_acc, dv_acc):
            acc[0:B, :] = acc[B:2 * B, :]
            acc[B:2 * B, :] = acc[2 * B:3 * B, :]
            acc[2 * B:3 * B, :] = jnp.zeros((B, 2 * LANES), F32)

    qn = lambda n: jnp.minimum(n, nb - 1)
    kv = lambda cb, d: pl.BlockSpec((B, 2 * LANES), lambda n: (jnp.clip(qn(n) + d, 0, nb - 1), cb))
    qrow = lambda w: pl.BlockSpec((B, w), lambda n: (qn(n), 0))
    krow = lambda w: pl.BlockSpec((B, w), lambda n: (jnp.maximum(n - 1, 0), 0))
    return _call(body, name="swa_bwd", grid=(nb + 1,),
                 in_specs=[pl.BlockSpec(memory_space=pltpu.SMEM), qrow(1024),
                           kv(4, -1), kv(4, 0), kv(4, 1), kv(5, -1), kv(5, 0), kv(5, 1),
                           qrow(1024), qrow(LANES),
                           qrow(LANES), qrow(LANES), qrow(LANES), krow(LANES), krow(LANES), krow(LANES)],
                 out_specs=[qrow(1024), krow(2 * LANES), krow(2 * LANES), pl.BlockSpec((1, LANES), lambda n: (0, 0))],
                 out_shape=[jax.ShapeDtypeStruct((S, 1024), F32), jax.ShapeDtypeStruct((S, 2 * LANES), F32),
                            jax.ShapeDtypeStruct((S, 2 * LANES), F32), jax.ShapeDtypeStruct((1, LANES), F32)],
                 scratch=[pltpu.VMEM((3 * B, 2 * LANES), F32), pltpu.VMEM((3 * B, 2 * LANES), F32)],
                 )(sink, qkv, qkv, qkv, qkv, qkv, qkv, qkv, dycat, lse, *tabs, *tabs)


def _halo_specs(T, S, w, cb):
    r = T // HALO
    last = S // HALO - 1
    return [pl.BlockSpec((HALO, w), lambda i: (jnp.maximum(i * r - 1, 0), cb)),
            pl.BlockSpec((T, w), lambda i: (i, cb)),
            pl.BlockSpec((HALO, w), lambda i: (jnp.minimum((i + 1) * r, last), cb))]


def _fill_ext(ext, prev_ref, cur_ref, next_ref, i, nt, T):
    ext[0:HALO, :] = jnp.where(i > 0, prev_ref[...], 0.0).astype(F32)
    ext[HALO:HALO + T, :] = cur_ref[...].astype(F32)
    ext[HALO + T:2 * HALO + T, :] = jnp.where(i < nt - 1, next_ref[...], 0.0).astype(F32)


def _pool_cnt(t, half, S):
    return (jnp.clip(t + half, 0, S) - jnp.clip(t - half, 0, S)).astype(F32)


def pool_fwd(z, w_pool, scale, T=256):
    S = z.shape[0]
    T = min(T, S)
    nt = S // T

    def body(up_ref, uc_ref, un_ref, w_ref, sc_ref, o_ref, ext):
        i = pl.program_id(0)
        _fill_ext(ext, up_ref, uc_ref, un_ref, i, nt, T)
        t = i * T + lax.broadcasted_iota(jnp.int32, (T, 1), 0)
        for g, win in enumerate(POOL_WINDOWS):
            half = win // 2
            sl = slice(g * LANES, (g + 1) * LANES)
            acc = ext[pl.ds(HALO - half, T), sl]
            for off in range(-half + 1, half):
                acc = acc + ext[pl.ds(HALO + off, T), sl]
            d = acc / _pool_cnt(t, half, S) - ext[pl.ds(HALO, T), sl]
            o_ref[:, sl] = (_dot(_mx(d), w_ref[g]) * sc_ref[:, sl]).astype(o_ref.dtype)

    return _call(body, name="pool_fwd", grid=(nt,),
                 in_specs=_halo_specs(T, S, 512, 3) + [pl.BlockSpec((4, LANES, LANES), lambda i: (0, 0, 0)),
                                                      pl.BlockSpec((1, 512), lambda i: (0, 0))],
                 out_specs=pl.BlockSpec((T, 512), lambda i: (i, 0)),
                 out_shape=jax.ShapeDtypeStruct((S, 512), MXU_DTYPE),
                 scratch=[pltpu.VMEM((T + 2 * HALO, 512), F32)])(z, z, z, w_pool, scale)


def pool_bwd(z, dycat, w_pool, scale, T=256):
    S = z.shape[0]
    T = min(T, S)
    nt = S // T
    TE = T + 2 * HALO

    def body(up_ref, uc_ref, un_ref, yp_ref, yc_ref, yn_ref, w_ref, sc_ref, du_ref, dw_ref, dsc_ref, extu, exty, exte):
        i = pl.program_id(0)

        @pl.when(i == 0)
        def _():
            dw_ref[...] = jnp.zeros_like(dw_ref)
            dsc_ref[...] = jnp.zeros_like(dsc_ref)

        _fill_ext(extu, up_ref, uc_ref, un_ref, i, nt, T)
        _fill_ext(exty, yp_ref, yc_ref, yn_ref, i, nt, T)
        t = i * T + lax.broadcasted_iota(jnp.int32, (T, 1), 0)
        te = i * T - HALO + lax.broadcasted_iota(jnp.int32, (TE, 1), 0)
        for g, win in enumerate(POOL_WINDOWS):
            half = win // 2
            sl = slice(g * LANES, (g + 1) * LANES)
            w = w_ref[g]
            acc = extu[pl.ds(HALO - half, T), sl]
            for off in range(-half + 1, half):
                acc = acc + extu[pl.ds(HALO + off, T), sl]
            d = _mx(acc / _pool_cnt(t, half, S) - extu[pl.ds(HALO, T), sl])
            dy = exty[pl.ds(HALO, T), sl]
            dsc_ref[:, sl] += jnp.sum(dy * _dot(d, w), axis=0, keepdims=True)
            dw_ref[g] += _dot_tn(d, _mx(dy * sc_ref[:, sl]))
            dd = _dot_nt(_mx(exty[:, sl] * sc_ref[:, sl]), w)
            exte[:, sl] = dd / jnp.maximum(_pool_cnt(te, half, S), 1.0)
            acc = exte[pl.ds(HALO - half + 1, T), sl]
            for off in range(-half + 2, half + 1):
                acc = acc + exte[pl.ds(HALO + off, T), sl]
            du_ref[:, sl] = acc - dd[HALO:HALO + T, :]

    return _call(body, name="pool_bwd", grid=(nt,),
                 in_specs=_halo_specs(T, S, 512, 3) + _halo_specs(T, S, 512, 2)
                 + [pl.BlockSpec((4, LANES, LANES), lambda i: (0, 0, 0)), pl.BlockSpec((1, 512), lambda i: (0, 0))],
                 out_specs=[pl.BlockSpec((T, 512), lambda i: (i, 0)), pl.BlockSpec((4, LANES, LANES), lambda i: (0, 0, 0)),
                            pl.BlockSpec((1, 512), lambda i: (0, 0))],
                 out_shape=[jax.ShapeDtypeStruct((S, 512), F32), jax.ShapeDtypeStruct((4, LANES, LANES), F32),
                            jax.ShapeDtypeStruct((1, 512), F32)],
                 scratch=[pltpu.VMEM((TE, 512), F32)] * 3)(z, z, z, dycat, dycat, dycat, w_pool, scale)


MLA_HALF = 16
MLA_V_ONE = 64


def mla_prep(z, g_cq, g_ckv, w_uq, w_k, w_v, tabs_q, tabs_k, T=256):
    S = z.shape[0]
    T = min(T, S)

    def body(z_ref, gq_ref, gkv_ref, wq_ref, wk_ref, wv_ref, cq_ref, saq_ref, sbq_ref, ck_ref, sak_ref, sbk_ref,
             q_ref, k_ref, v_ref, nq_ref, nkv_ref):
        nq = _mx(_rms(z_ref[:, 0:256], gq_ref[...]))
        nkv = _mx(_rms(z_ref[:, 256:384], gkv_ref[...]))
        nq_ref[...] = nq
        nkv_ref[...] = nkv
        q = _dot(nq, wq_ref[...])
        kn = _dot(nkv, wk_ref[...])
        lane = lax.broadcasted_iota(jnp.int32, (T, 1024), 1)
        v_ref[...] = jnp.where(lane % LANES == MLA_V_ONE, 1.0, _dot(nkv, wv_ref[...])).astype(v_ref.dtype)
        kr = pltpu.roll(_rope(z_ref[:, 384:512], ck_ref[...], sak_ref[...], sbk_ref[...], MLA_HALF), 64, 1)
        cq, saq, sbq = cq_ref[...], saq_ref[...], sbq_ref[...]
        for h in range(8):
            hs = slice(h * LANES, (h + 1) * LANES)
            q_ref[:, hs] = (_rope(q[:, hs], cq, saq, sbq, MLA_HALF) * MLA_QSCALE).astype(q_ref.dtype)
            k_ref[:, hs] = (kn[:, hs] + kr).astype(k_ref.dtype)

    tab = pl.BlockSpec((T, LANES), lambda i: (i, 0))
    full = lambda a: pl.BlockSpec(a.shape, lambda i: (0, 0))
    row = lambda w: pl.BlockSpec((T, w), lambda i: (i, 0))
    sd = lambda w: jax.ShapeDtypeStruct((S, w), MXU_DTYPE)
    return _call(body, name="mla_prep", grid=(S // T,),
                 in_specs=[row(512), full(g_cq), full(g_ckv), full(w_uq), full(w_k), full(w_v)] + [tab] * 6,
                 out_specs=[row(1024), row(1024), row(1024), row(256), row(128)],
                 out_shape=[sd(1024), sd(1024), sd(1024), sd(256), sd(128)],
                 )(z, g_cq, g_ckv, w_uq, w_k, w_v, *tabs_q, *tabs_k)


def _col_to_row(c):
    return jnp.transpose(jnp.broadcast_to(c, (c.shape[0], LANES)))[0:1, :]


def mla_vt(v, TK=512):
    S = v.shape[0]
    TK = min(TK, S // 2)
    return v.reshape(S // TK, TK, 8, LANES).transpose(2, 0, 3, 1)


def mla_dq(dqt):
    _, nq, _, TQ = dqt.shape
    return dqt.transpose(1, 3, 0, 2).reshape(nq * TQ, 1024)


def mla_fwd(q, k, vt, TQ=512):
    S = q.shape[0]
    nk, TK = vt.shape[1], vt.shape[3]
    TQ = min(TQ, S)
    QB = min(256, TQ)
    assert nk % 2 == 0
    unroll = 8 if nk % 8 == 0 else 4 if nk % 4 == 0 else 2

    def body(q_ref, k_ref, vt_ref, o_ref, lse_ref, m_s, acc_s, s_a, s_b):
        m_s[...] = jnp.full_like(m_s, -jnp.inf)
        acc_s[...] = jnp.zeros_like(acc_s)
        q = q_ref[...]

        def scores(c):
            return _dot_nt(k_ref[pl.ds(pl.multiple_of(c * TK, TK), TK), :], q)

        def softmax_pv(s_ref, c):
            vt_c = vt_ref[0, c]
            for b in range(TQ // QB):
                cols = slice(b * QB, (b + 1) * QB)
                s = s_ref[:, cols]
                m_old = m_s[:, cols]
                m_new = jnp.maximum(m_old, jnp.max(s, axis=0, keepdims=True))
                alpha = jnp.exp2(m_old[0:1, :] - m_new[0:1, :])
                p = jnp.exp2(s - m_new[0:1, :])
                acc_s[:, cols] = alpha * acc_s[:, cols] + _dot(vt_c, _mx(p))
                m_s[:, cols] = m_new

        s_a[...] = scores(0)
        bufs = (s_a, s_b)

        def group(jj, carry):
            c = unroll * jj
            for u in range(unroll):
                nxt = c + u + 1 if u < unroll - 1 else jnp.minimum(c + unroll, nk - 1)
                bufs[(u + 1) % 2][...] = scores(nxt)
                softmax_pv(bufs[u % 2], c + u)
            return carry

        lax.fori_loop(0, nk // unroll, group, 0)
        acc = acc_s[...]
        den = acc[MLA_V_ONE:MLA_V_ONE + 1, :]
        sub = lax.broadcasted_iota(jnp.int32, (LANES, TQ), 0)
        o_ref[...] = jnp.transpose(jnp.where(sub < MLA_V_ONE, acc / den, 0.0))
        lse_ref[0] = m_s[0:1, :] + jnp.log(den) * LOG2E

    qs = pl.BlockSpec((TQ, LANES), lambda h, i: (i, h))
    return _call(body, name="mla_fwd", grid=(8, S // TQ),
                 in_specs=[qs, pl.BlockSpec((S, LANES), lambda h, i: (0, h)),
                           pl.BlockSpec((1, nk, LANES, TK), lambda h, i: (h, 0, 0, 0))],
                 out_specs=[qs, pl.BlockSpec((1, 1, TQ), lambda h, i: (h, 0, i))],
                 out_shape=[jax.ShapeDtypeStruct((S, 1024), F32), jax.ShapeDtypeStruct((8, 1, S), F32)],
                 scratch=[pltpu.VMEM((8, TQ), F32), pltpu.VMEM((LANES, TQ), F32),
                          pltpu.VMEM((TK, TQ), F32), pltpu.VMEM((TK, TQ), F32)],
                 )(q, k, vt)


def mla_delta(o, dycat, TQ=512):
    S = o.shape[0]
    TQ = min(TQ, S)

    def body(o_ref, do_ref, d_ref, dob_ref):
        for h in range(8):
            hs = slice(h * LANES, (h + 1) * LANES)
            do = do_ref[:, hs]
            d_ref[h] = _col_to_row(jnp.sum(o_ref[:, hs] * do, axis=-1, keepdims=True))
            dob_ref[:, hs] = do.astype(dob_ref.dtype)

    qs = pl.BlockSpec((TQ, 1024), lambda i: (i, 0))
    return _call(body, name="mla_delta", grid=(S // TQ,), in_specs=[qs, qs],
                 out_specs=[pl.BlockSpec((8, 1, TQ), lambda i: (0, 0, i)), qs],
                 out_shape=[jax.ShapeDtypeStruct((8, 1, S), F32), jax.ShapeDtypeStruct((S, 1024), MXU_DTYPE)])(o, dycat)


def mla_bwd(q, k, v, do, lse, delta, TQ=512, TK=512):
    S = q.shape[0]
    TQ, TK = min(TQ, S // 2), min(TK, S)
    nq, nk = S // TQ, S // TK
    assert nq % 2 == 0
    unroll = 8 if nq % 8 == 0 else 4 if nq % 4 == 0 else 2
    lse = lse.reshape(8, nq, 1, TQ)
    delta = delta.reshape(8, nq, 1, TQ)

    def body(q_ref, do_ref, k_ref, v_ref, lse_ref, d_ref, dqt_ref, dk_ref, dv_ref, dk_s, dv_s, s_a, p_a, s_b, p_b):
        @pl.when(pl.program_id(1) == 0)
        def _():
            dqt_ref[...] = jnp.zeros_like(dqt_ref)

        dk_s[...] = jnp.zeros_like(dk_s)
        dv_s[...] = jnp.zeros_like(dv_s)
        kk, vv = k_ref[...], v_ref[...]
        kt = _mx(jnp.transpose(kk.astype(F32)))

        def rows(c):
            return pl.ds(pl.multiple_of(c * TQ, TQ), TQ)

        def products(c, s_ref, p_ref):
            s_ref[...] = _dot_nt(kk, q_ref[rows(c), :])
            p_ref[...] = _dot_nt(vv, do_ref[rows(c), :])

        def consume(c, s_ref, p_ref):
            qc, doc = q_ref[rows(c), :], do_ref[rows(c), :]
            pt = jnp.exp2(s_ref[...] - lse_ref[0, c])
            dv_s[...] += _dot(_mx(pt), doc)
            ds = _mx(pt * (p_ref[...] - d_ref[0, c]))
            dk_s[...] += _dot(ds, qc)
            dqt_ref[0, c] += _dot(kt, ds)

        products(0, s_a, p_a)
        bufs = ((s_a, p_a), (s_b, p_b))

        def group(jj, carry):
            c = unroll * jj
            for u in range(unroll):
                nxt = c + u + 1 if u < unroll - 1 else jnp.minimum(c + unroll, nq - 1)
                products(nxt, *bufs[(u + 1) % 2])
                consume(c + u, *bufs[u % 2])
            return carry

        lax.fori_loop(0, nq // unroll, group, 0)
        dk_ref[...] = dk_s[...] * (1.0 / LOG2E)
        dv_ref[...] = dv_s[...]

    full = pl.BlockSpec((S, LANES), lambda h, j: (0, h))
    ks = pl.BlockSpec((TK, LANES), lambda h, j: (j, h))
    st = pl.BlockSpec((1, nq, 1, TQ), lambda h, j: (h, 0, 0, 0))
    sd = jax.ShapeDtypeStruct((S, 1024), F32)
    return _call(body, name="mla_bwd", grid=(8, nk), in_specs=[full, full, ks, ks, st, st],
                 out_specs=[pl.BlockSpec((1, nq, LANES, TQ), lambda h, j: (h, 0, 0, 0)), ks, ks],
                 out_shape=[jax.ShapeDtypeStruct((8, nq, LANES, TQ), F32), sd, sd],
                 scratch=[pltpu.VMEM((TK, LANES), F32), pltpu.VMEM((TK, LANES), F32)] + [pltpu.VMEM((TK, TQ), F32)] * 4,
                 )(q, do, k, v, lse, delta)


def mla_prep_bwd(z, g_cq, g_ckv, w_uq, w_k, w_v, dq, dk, dv, tabs_q, tabs_k, T=256):
    S = z.shape[0]
    T = min(T, S)

    def body(z_ref, gq_ref, gkv_ref, wq_ref, wk_ref, wv_ref, dq_ref, dk_ref, dv_ref,
             cq_ref, saq_ref, sbq_ref, ck_ref, sak_ref, sbk_ref, dz_ref, dqp_ref, dgq_ref, dgkv_ref):
        @pl.when(pl.program_id(0) == 0)
        def _():
            dgq_ref[...] = jnp.zeros_like(dgq_ref)
            dgkv_ref[...] = jnp.zeros_like(dgkv_ref)

        cq, saq, sbq = cq_ref[...], saq_ref[...], sbq_ref[...]
        dkr = jnp.zeros((T, LANES), F32)
        for h in range(8):
            hs = slice(h * LANES, (h + 1) * LANES)
            dqp_ref[:, hs] = _rope_t(dq_ref[:, hs] * MLA_SCALE, cq, saq, sbq, MLA_HALF).astype(dqp_ref.dtype)
            dkr = dkr + dk_ref[:, hs]
        lane = lax.broadcasted_iota(jnp.int32, (T, LANES), 1)
        dkr = jnp.where(lane < 2 * MLA_HALF, pltpu.roll(dkr, 64, 1), 0.0)
        dz_ref[:, 384:512] = _rope_t(dkr, ck_ref[...], sak_ref[...], sbk_ref[...], MLA_HALF)
        dnq = _dot_nt(dqp_ref[...], wq_ref[...])
        dx, dg = _rms_bwd(z_ref[:, 0:256], gq_ref[...], dnq)
        dz_ref[:, 0:256] = dx
        dgq_ref[...] += dg
        dnkv = _dot_nt(_mx(dk_ref[...]), wk_ref[...]) + _dot_nt(_mx(dv_ref[...]), wv_ref[...])
        dx, dg = _rms_bwd(z_ref[:, 256:384], gkv_ref[...], dnkv)
        dz_ref[:, 256:384] = dx
        dgkv_ref[...] += dg

    tab = pl.BlockSpec((T, LANES), lambda i: (i, 0))
    full = lambda a: pl.BlockSpec(a.shape, lambda i: (0, 0))
    row = lambda w: pl.BlockSpec((T, w), lambda i: (i, 0))
    return _call(body, name="mla_prep_bwd", grid=(S // T,),
                 in_specs=[row(512), full(g_cq), full(g_ckv), full(w_uq), full(w_k), full(w_v),
                           row(1024), row(1024), row(1024)] + [tab] * 6,
                 out_specs=[row(512), row(1024), full(g_cq), full(g_ckv)],
                 out_shape=[jax.ShapeDtypeStruct((S, 512), F32), jax.ShapeDtypeStruct((S, 1024), MXU_DTYPE),
                            jax.ShapeDtypeStruct(g_cq.shape, F32), jax.ShapeDtypeStruct(g_ckv.shape, F32)],
                 )(z, g_cq, g_ckv, w_uq, w_k, w_v, dq, dk, dv, *tabs_q, *tabs_k)


def _lru_gates(xc, w_ref, bias_ref, lam_ref):
    pre = _dot(_mx(xc), w_ref[...]) + bias_ref[...]
    out = []
    for d in range(2):
        r = _sigmoid(pre[:, d * 1024:d * 1024 + 512])
        ig = _sigmoid(pre[:, d * 1024 + 512:(d + 1) * 1024])
        log_a = -LRU_C * r * _softplus(-lam_ref[:, d * 512:(d + 1) * 512])
        out.append((r, ig, jnp.exp(log_a), jnp.sqrt(-_expm1(2.0 * log_a))))
    return out


def lru_pre(z, conv_w, conv_b, w_gate, b_gate, lam, T=256):
    S = z.shape[0]
    T = min(T, S)
    nt = S // T

    def body(xp_ref, xcur_ref, xn_ref, cw_ref, cb_ref, w_ref, bias_ref, lam_ref, xc_ref, a0_ref, b0_ref, a1_ref, b1_ref, ext):
        i = pl.program_id(0)
        _fill_ext(ext, xp_ref, xcur_ref, xn_ref, i, nt, T)
        xc = cb_ref[...] + cw_ref[0:1, :] * ext[pl.ds(HALO - 2, T), :]
        for j in range(1, 4):
            xc = xc + cw_ref[j:j + 1, :] * ext[pl.ds(HALO - 2 + j, T), :]
        xc_ref[...] = xc
        (_, i0, a0, m0), (_, i1, a1, m1) = _lru_gates(xc, w_ref, bias_ref, lam_ref)
        a0_ref[...] = a0
        b0_ref[...] = m0 * (i0 * xc)
        a1_ref[...] = a1
        b1_ref[...] = m1 * (i1 * xc)

    full = lambda a: pl.BlockSpec(a.shape, lambda i: (0, 0))
    row = pl.BlockSpec((T, 512), lambda i: (i, 0))
    sd = jax.ShapeDtypeStruct((S, 512), F32)
    return _call(body, name="lru_pre", grid=(nt,),
                 in_specs=_halo_specs(T, S, 512, 1) + [full(conv_w), full(conv_b), full(w_gate), full(b_gate), full(lam)],
                 out_specs=[row] * 5, out_shape=[sd] * 5,
                 scratch=[pltpu.VMEM((T + 2 * HALO, 512), F32)])(z, z, z, conv_w, conv_b, w_gate, b_gate, lam)


def lru_scan(name, af, bf, ar, br, *, adjoint, T=512):
    S, W = af.shape
    T = min(T, S)
    nt = S // T
    nc = T // 8

    def body(af_ref, bf_ref, ar_ref, br_ref, hf_ref, hr_ref, cf, cr):
        @pl.when(pl.program_id(0) == 0)
        def _():
            cf[...] = jnp.zeros_like(cf)
            cr[...] = jnp.zeros_like(cr)

        row = lax.broadcasted_iota(jnp.int32, (8, W), 0)

        def step(a, b, carry):
            if adjoint:
                val = b + carry
                return val, a * val
            val = a * carry + b
            return val, val

        def chunk(c, carry):
            hf, hr = carry
            of = pl.multiple_of(c * 8, 8)
            orv = pl.multiple_of((nc - 1 - c) * 8, 8)
            a8, b8 = af_ref[pl.ds(of, 8), :], bf_ref[pl.ds(of, 8), :]
            ra8, rb8 = ar_ref[pl.ds(orv, 8), :], br_ref[pl.ds(orv, 8), :]
            outf = jnp.zeros((8, W), F32)
            outr = jnp.zeros((8, W), F32)
            for k in range(8):
                val, hf = step(a8[k:k + 1, :], b8[k:k + 1, :], hf)
                outf = jnp.where(row == k, val, outf)
                kr = 7 - k
                val, hr = step(ra8[kr:kr + 1, :], rb8[kr:kr + 1, :], hr)
                outr = jnp.where(row == kr, val, outr)
            hf_ref[pl.ds(of, 8), :] = outf
            hr_ref[pl.ds(orv, 8), :] = outr
            return hf, hr

        hf, hr = lax.fori_loop(0, nc, chunk, (cf[0:1, :], cr[0:1, :]))
        cf[0:1, :] = hf
        cr[0:1, :] = hr

    fw = pl.BlockSpec((T, W), lambda i: (i, 0))
    rv = pl.BlockSpec((T, W), lambda i: (nt - 1 - i, 0))
    sd = jax.ShapeDtypeStruct((S, W), F32)
    return _call(body, name=name, grid=(nt,), in_specs=[fw, fw, rv, rv], out_specs=[fw, rv], out_shape=[sd, sd],
                 scratch=[pltpu.VMEM((8, W), F32), pltpu.VMEM((8, W), F32)])(af, bf, ar, br)


def lru_gate(h0, h1, z, T=512):
    S = z.shape[0]
    T = min(T, S)

    def body(h0_ref, h1_ref, xg_ref, y_ref):
        y_ref[...] = ((h0_ref[...] + h1_ref[...]) * _gelu(xg_ref[...])).astype(y_ref.dtype)

    row = pl.BlockSpec((T, 512), lambda i: (i, 0))
    return _call(body, name="lru_gate", grid=(S // T,), in_specs=[row, row, pl.BlockSpec((T, 512), lambda i: (i, 2))],
                 out_specs=row, out_shape=jax.ShapeDtypeStruct((S, 512), MXU_DTYPE))(h0, h1, z)


def lru_gate_bwd(h0, h1, z, dycat, T=512):
    S = z.shape[0]
    T = min(T, S)

    def body(h0_ref, h1_ref, xg_ref, dy_ref, dxg_ref, dh_ref):
        xg, dy = xg_ref[...], dy_ref[...]
        dxg_ref[...] = dy * (h0_ref[...] + h1_ref[...]) * _gelu_grad(xg)
        dh_ref[...] = dy * _gelu(xg)

    row = pl.BlockSpec((T, 512), lambda i: (i, 0))
    col2 = pl.BlockSpec((T, 512), lambda i: (i, 2))
    sd = jax.ShapeDtypeStruct((S, 512), F32)
    return _call(body, name="lru_gate_bwd", grid=(S // T,), in_specs=[row, row, col2, col2],
                 out_specs=[row, row], out_shape=[sd, sd])(h0, h1, z, dycat)


def lru_bwd_point(xc, h0, h1, g0, g1, w_gate, b_gate, lam, T=256):
    S = xc.shape[0]
    T = min(T, S)
    nt = S // T

    def body(xc_ref, h0p_ref, h0_ref, h0n_ref, h1p_ref, h1_ref, h1n_ref, g0_ref, g1_ref, w_ref, bias_ref, lam_ref,
             dxc_ref, dpre_ref, dbias_ref, dlam_ref, ext0, ext1):
        i = pl.program_id(0)

        @pl.when(i == 0)
        def _():
            dbias_ref[...] = jnp.zeros_like(dbias_ref)
            dlam_ref[...] = jnp.zeros_like(dlam_ref)

        _fill_ext(ext0, h0p_ref, h0_ref, h0n_ref, i, nt, T)
        _fill_ext(ext1, h1p_ref, h1_ref, h1n_ref, i, nt, T)
        xc = xc_ref[...]
        gates = _lru_gates(xc, w_ref, bias_ref, lam_ref)
        hshift = (ext0[pl.ds(HALO - 1, T), :], ext1[pl.ds(HALO + 1, T), :])
        gs = (g0_ref[...], g1_ref[...])
        dxc = jnp.zeros((T, 512), F32)
        for d in range(2):
            r, ig, a, mult = gates[d]
            db = gs[d]
            da = db * hshift[d]
            dmult = db * (ig * xc)
            di = db * (mult * xc)
            dxc = dxc + db * (mult * ig)
            dloga = da * a - dmult * (a * a / mult)
            lam_d = lam_ref[:, d * 512:(d + 1) * 512]
            dr = dloga * (-LRU_C * _softplus(-lam_d))
            dsp = jnp.sum(dloga * (-LRU_C * r), axis=0, keepdims=True)
            dlam_ref[:, d * 512:(d + 1) * 512] += dsp * (-_sigmoid(-lam_d))
            dpre_ref[:, d * 1024:d * 1024 + 512] = (dr * (r * (1.0 - r))).astype(dpre_ref.dtype)
            dpre_ref[:, d * 1024 + 512:(d + 1) * 1024] = (di * (ig * (1.0 - ig))).astype(dpre_ref.dtype)
            dbias_ref[:, d * 1024:d * 1024 + 512] += jnp.sum(dr * (r * (1.0 - r)), axis=0, keepdims=True)
            dbias_ref[:, d * 1024 + 512:(d + 1) * 1024] += jnp.sum(di * (ig * (1.0 - ig)), axis=0, keepdims=True)
        dxc_ref[...] = dxc + _dot_nt(dpre_ref[...], w_ref[...])

    full = lambda a: pl.BlockSpec(a.shape, lambda i: (0, 0))
    row = pl.BlockSpec((T, 512), lambda i: (i, 0))
    return _call(body, name="lru_bwd_point", grid=(nt,),
                 in_specs=[row] + _halo_specs(T, S, 512, 0) + _halo_specs(T, S, 512, 0) + [row, row, full(w_gate), full(b_gate), full(lam)],
                 out_specs=[row, pl.BlockSpec((T, 2048), lambda i: (i, 0)), full(b_gate), full(lam)],
                 out_shape=[jax.ShapeDtypeStruct((S, 512), F32), jax.ShapeDtypeStruct((S, 2048), MXU_DTYPE),
                            jax.ShapeDtypeStruct(b_gate.shape, F32), jax.ShapeDtypeStruct(lam.shape, F32)],
                 scratch=[pltpu.VMEM((T + 2 * HALO, 512), F32)] * 2,
                 )(xc, h0, h0, h0, h1, h1, h1, g0, g1, w_gate, b_gate, lam)


def conv_bwd(z, dxc, conv_w, T=512):
    S = z.shape[0]
    T = min(T, S)
    nt = S // T

    def body(xp_ref, xcur_ref, xn_ref, dp_ref, dcur_ref, dn_ref, cw_ref, dx_ref, dw_ref, db_ref, extx, extd):
        i = pl.program_id(0)

        @pl.when(i == 0)
        def _():
            dw_ref[...] = jnp.zeros_like(dw_ref)
            db_ref[...] = jnp.zeros_like(db_ref)

        _fill_ext(extx, xp_ref, xcur_ref, xn_ref, i, nt, T)
        _fill_ext(extd, dp_ref, dcur_ref, dn_ref, i, nt, T)
        d = extd[pl.ds(HALO, T), :]
        dx = cw_ref[0:1, :] * extd[pl.ds(HALO + 2, T), :]
        for j in range(1, 4):
            dx = dx + cw_ref[j:j + 1, :] * extd[pl.ds(HALO + 2 - j, T), :]
        dx_ref[...] = dx
        for j in range(4):
            dw_ref[j:j + 1, :] += jnp.sum(d * extx[pl.ds(HALO - 2 + j, T), :], axis=0, keepdims=True)
        db_ref[...] += jnp.sum(d, axis=0, keepdims=True)

    full = lambda a: pl.BlockSpec(a.shape, lambda i: (0, 0))
    row = pl.BlockSpec((T, 512), lambda i: (i, 0))
    return _call(body, name="conv_bwd", grid=(nt,),
                 in_specs=_halo_specs(T, S, 512, 1) + _halo_specs(T, S, 512, 0) + [full(conv_w)],
                 out_specs=[row, full(conv_w), pl.BlockSpec((1, 512), lambda i: (0, 0))],
                 out_shape=[jax.ShapeDtypeStruct((S, 512), F32), jax.ShapeDtypeStruct(conv_w.shape, F32),
                            jax.ShapeDtypeStruct((1, 512), F32)],
                 scratch=[pltpu.VMEM((T + 2 * HALO, 512), F32)] * 2)(z, z, z, dxc, dxc, dxc, conv_w)


MESH_ID = pl.DeviceIdType.MESH
ANY = pl.BlockSpec(memory_space=pl.ANY)


def _place():
    x, y, c = lax.axis_index("x"), lax.axis_index("y"), lax.axis_index("c")
    chips = [(1 - x, y), (x, 1 - y), (1 - x, 1 - y)]
    return x, y, c, chips


def _slot(px, py, pc):
    return 4 * px + 2 * py + pc


def all_gather(arrays):
    n = len(arrays)

    def body(*refs):
        ins, outs = refs[:n], refs[n:2 * n]
        send, recv, loc = refs[2 * n:]
        x, y, c, chips = _place()
        me, sibling = (x, y, c), (x, y, 1 - c)

        def copy(a, k, block, to, src=None):
            slot = outs[a].at[_slot(*block)]
            return pltpu.make_async_remote_copy(src_ref=slot if src is None else src, dst_ref=slot,
                                                send_sem=send.at[a * 7 + k], recv_sem=recv.at[a * 7 + k],
                                                device_id=to, device_id_type=MESH_ID)

        local = [pltpu.make_async_copy(ins[a], outs[a].at[_slot(*me)], loc.at[a]) for a in range(n)]
        for cp in local:
            cp.start()
        first = []
        for a in range(n):
            first.append(copy(a, 0, me, sibling, src=ins[a]))
            first += [copy(a, 1 + j, me, (*chip, c), src=ins[a]) for j, chip in enumerate(chips)]
        for cp in first:
            cp.start()
        passed = []
        for a in range(n):
            for j, chip in enumerate(chips):
                copy(a, 1 + j, (*chip, c), me).wait_recv()
                cp = copy(a, 4 + j, (*chip, c), sibling)
                cp.start()
                passed.append(cp)
        for a in range(n):
            copy(a, 0, sibling, me).wait_recv()
            for j, chip in enumerate(chips):
                copy(a, 4 + j, (*chip, 1 - c), me).wait_recv()
        for cp in first + passed:
            cp.wait_send()
        for cp in local:
            cp.wait()

    return pl.pallas_call(
        body, name="all_gather", in_specs=[ANY] * n, out_specs=[ANY] * n,
        out_shape=[jax.ShapeDtypeStruct((NDEV,) + a.shape, a.dtype) for a in arrays],
        scratch_shapes=[pltpu.SemaphoreType.DMA((n * 7,)), pltpu.SemaphoreType.DMA((n * 7,)),
                        pltpu.SemaphoreType.DMA((n,))],
    )(*arrays)


def grad_pair(gs):
    def body(gs_ref, o_ref, send, recv):
        x, y, c, _ = _place()
        cps = [pltpu.make_async_remote_copy(src_ref=gs_ref.at[2 * q + 1], dst_ref=o_ref.at[q], send_sem=send.at[q],
                                            recv_sem=recv.at[q], device_id=(x, y, 1 - c), device_id_type=MESH_ID)
               for q in range(4)]
        for cp in cps:
            cp.start()
        for cp in cps:
            cp.wait_recv()
        for cp in cps:
            cp.wait_send()

    return pl.pallas_call(
        body, name="grad_pair", in_specs=[ANY], out_specs=ANY,
        out_shape=jax.ShapeDtypeStruct((4,) + gs.shape[1:], gs.dtype),
        scratch_shapes=[pltpu.SemaphoreType.DMA((4,)), pltpu.SemaphoreType.DMA((4,))],
    )(gs)


def pair_add(gs, got, T=128):
    _, R, C = gs.shape
    T = min(T, R)

    def body(g0, g1, g2, g3, got_ref, own_ref, out_ref):
        own_ref[...] = g0[0] + got_ref[0]
        for q, g in enumerate((g1, g2, g3)):
            out_ref[q] = (g[0] + got_ref[q + 1]).astype(out_ref.dtype)

    even = lambda q: pl.BlockSpec((1, T, C), lambda i: (2 * q, i, 0))
    return _call(body, name="pair_add", grid=(R // T,),
                 in_specs=[even(0), even(1), even(2), even(3), pl.BlockSpec((4, T, C), lambda i: (0, i, 0))],
                 out_specs=[pl.BlockSpec((T, C), lambda i: (i, 0)), pl.BlockSpec((3, T, C), lambda i: (0, i, 0))],
                 out_shape=[jax.ShapeDtypeStruct((R, C), F32), jax.ShapeDtypeStruct((3, R, C), MXU_DTYPE)],
                 )(gs, gs, gs, gs, got)


def grad_cross(part, gr):
    def body(p_ref, gr_ref, o_ref, or_ref, send, recv, loc):
        x, y, c, chips = _place()
        peers = [(x, y, 1 - c)] + [(*chip, c) for chip in chips] + [(*chip, 1 - c) for chip in chips]
        local = pltpu.make_async_copy(gr_ref, or_ref.at[_slot(x, y, c)], loc.at[0])
        local.start()
        by_offset = [(x, 1 - y), (1 - x, y), (1 - x, 1 - y)]
        cps = [pltpu.make_async_remote_copy(src_ref=p_ref.at[j], dst_ref=o_ref.at[j], send_sem=send.at[j],
                                            recv_sem=recv.at[j], device_id=(*chip, c), device_id_type=MESH_ID)
               for j, chip in enumerate(by_offset)]
        cps += [pltpu.make_async_remote_copy(src_ref=gr_ref, dst_ref=or_ref.at[_slot(x, y, c)], send_sem=send.at[3 + k],
                                             recv_sem=recv.at[3 + k], device_id=peer, device_id_type=MESH_ID)
                for k, peer in enumerate(peers)]
        for cp in cps:
            cp.start()
        for cp in cps[:3]:
            cp.wait_recv()
        for k, peer in enumerate(peers):
            pltpu.make_async_remote_copy(src_ref=gr_ref, dst_ref=or_ref.at[_slot(*peer)], send_sem=send.at[3 + k],
                                         recv_sem=recv.at[3 + k], device_id=peer, device_id_type=MESH_ID).wait_recv()
        for cp in cps:
            cp.wait_send()
        local.wait()

    return pl.pallas_call(
        body, name="grad_cross", in_specs=[ANY, ANY], out_specs=[ANY, ANY],
        out_shape=[jax.ShapeDtypeStruct(part.shape, part.dtype), jax.ShapeDtypeStruct((NDEV,) + gr.shape, gr.dtype)],
        scratch_shapes=[pltpu.SemaphoreType.DMA((10,)), pltpu.SemaphoreType.DMA((10,)), pltpu.SemaphoreType.DMA((1,))],
    )(part, gr)


def adamw(name, gparts, w, m, v, T=128):
    R, C = w.shape
    T = min(T, R)
    ng = len(gparts)

    def body(*refs):
        g_refs = refs[:ng]
        w_ref, m_ref, v_ref, go_ref, d_ref, mo_ref, vo_ref = refs[ng:]
        g = None
        for g_ref in g_refs:
            for k in range(g_ref.shape[0]):
                t = g_ref[k].astype(F32)
                g = t if g is None else g + t
        mn = ADAM_B1 * m_ref[...] + (1.0 - ADAM_B1) * g
        vn = ADAM_B2 * v_ref[...] + (1.0 - ADAM_B2) * (g * g)
        m_hat = mn / (1.0 - ADAM_B1 ** ADAM_STEP)
        v_hat = vn / (1.0 - ADAM_B2 ** ADAM_STEP)
        go_ref[...] = g
        d_ref[...] = -ADAM_LR * (m_hat / (jnp.sqrt(v_hat) + ADAM_EPS) + ADAM_WD * w_ref[...])
        mo_ref[...] = mn
        vo_ref[...] = vn

    row = pl.BlockSpec((T, C), lambda i: (i, 0))
    sd = jax.ShapeDtypeStruct((R, C), F32)
    return _call(body, name=name, grid=(R // T,),
                 in_specs=[pl.BlockSpec((g.shape[0], T, C), lambda i: (0, i, 0)) for g in gparts] + [row, row, row],
                 out_specs=[row] * 4, out_shape=[sd] * 4)(*gparts, w, m, v)


PACK_W = 1024
PACK_TILE = 16
PACK_ROWS = 128


def _entry_rows(shape):
    n = 1
    for s in shape:
        n *= s
    return n, -(-n // (PACK_TILE * PACK_W)) * PACK_TILE


def _pack(arrays, dtype):
    mats, rows = [], 0
    for a in arrays:
        n, r = _entry_rows(a.shape)
        flat = a.astype(dtype).reshape(-1)
        if r * PACK_W != n:
            flat = jnp.pad(flat, (0, r * PACK_W - n))
        mats.append(flat.reshape(r, PACK_W))
        rows += r
    total = -(-rows // PACK_ROWS) * PACK_ROWS
    if total > rows:
        mats.append(jnp.zeros((total - rows, PACK_W), dtype))
    return jnp.concatenate(mats, axis=0)


def _unpack(buf, shapes, lead=()):
    out, off = [], 0
    for sh in shapes:
        n, r = _entry_rows(sh)
        piece = buf[..., off:off + r, :].reshape(lead + (r * PACK_W,))
        out.append(piece[..., :n].reshape(lead + tuple(sh)))
        off += r
    return out


def _unshard(parts, axis):
    return jnp.concatenate([parts[d] for d in range(NDEV)], axis=axis)


def _slab_cols(w, head, used, n_heads, slab=LANES):
    lead = w.shape[:-1]
    w = w.reshape(lead + (n_heads, head))[..., :used]
    w = jnp.pad(w, [(0, 0)] * len(lead) + [(0, 0), (0, slab - used)])
    return w.reshape(lead + (n_heads * slab,))


def _unslab_cols(g, used, n_heads, slab=LANES):
    lead = g.shape[:-1]
    return g.reshape(lead + (n_heads, slab))[..., :used].reshape(lead + (n_heads * used,))


def _block_diag(w):
    eye = jnp.eye(8, dtype=w.dtype)
    return jnp.einsum("nij,nm->nimj", w, eye).reshape(512, 512)


def _block_diag_t(g):
    g = g.reshape(8, 64, 8, 64)
    return jnp.stack([g[n, :, n, :] for n in range(8)])


BIG = (("e_w_in", 2), ("e_w_out", 1), ("o_w_in", 2), ("o_w_uq", 2), ("o_w_ukv", 2), ("o_w_out", 1),
       ("w_mlp1", 2), ("w_mlp2", 1))
SMALL = (("o_norm_mix", 1), ("o_g_cq", 1), ("o_conv_w", 2), ("o_conv_b", 1), ("o_lru_ba", 2), ("o_lru_bx", 2),
         ("o_lru_lambda", 2))
REPL = ("e_norm_mix", "e_sink", "e_w_pool", "e_pool_scale", "o_g_ckv", "o_lru_wa", "o_lru_wx", "norm_mlp",
        "final_norm")
WEIGHTS = ("e_norm_mix", "e_w_in", "e_sink", "e_w_pool", "e_pool_scale", "e_w_out", "o_norm_mix", "o_w_in", "o_g_cq",
           "o_w_uq", "o_g_ckv", "o_w_ukv", "o_conv_w", "o_conv_b", "o_lru_wa", "o_lru_ba", "o_lru_wx", "o_lru_bx",
           "o_lru_lambda", "o_w_out", "norm_mlp", "w_mlp1", "w_mlp2", "final_norm")


def _mlp_fwd(l, x, g, w1, w2):
    u1, hn = mm_nn(f"mlp1_{l}", [x], [w1], norm_g=g, emit_norm=True, out_dtype=MXU_DTYPE, tm=1024, tn=1024)
    x2 = mm_nn(f"mlp2_{l}", [u1], [w2], act="relu2", res=x, tm=512, tn=1024)
    return x2, (x, u1, hn)


def _mlp_bwd(l, saved, g, w1, w2, dx2):
    x, u1, hn = saved
    du1 = mm_nt(f"mlp2_dx_{l}", [dx2], [w2], relu2_of=u1, out_dtype=MXU_DTYPE, tm=1024, tn=1024)
    dw2 = mm_tn(f"mlp2_dw_{l}", u1, dx2, act="relu2", tm=1024, tn=1024)
    dhn = mm_nt(f"mlp1_dx_{l}", [du1], [w1], tm=512, tn=1024)
    dw1 = mm_tn(f"mlp1_dw_{l}", hn, du1, tm=1024, tn=1024)
    dx, dg = rms_bwd(f"mlp_norm_bwd_{l}", x, g, dhn, dx2)
    return dx, dg, dw1, dw2


def kernel(x, e_norm_mix, e_w_in, e_sink, e_w_pool, e_pool_scale, e_w_out, o_norm_mix, o_w_in, o_g_cq, o_w_uq, o_g_ckv, o_w_ukv, o_conv_w, o_conv_b, o_lru_wa, o_lru_ba, o_lru_wx, o_lru_bx, o_lru_lambda, o_w_out, norm_mlp, w_mlp1, w_mlp2, final_norm, loss_target, m_e_norm_mix, m_e_w_in, m_e_sink, m_e_w_pool, m_e_pool_scale, m_e_w_out, m_o_norm_mix, m_o_w_in, m_o_g_cq, m_o_w_uq, m_o_g_ckv, m_o_w_ukv, m_o_conv_w, m_o_conv_b, m_o_lru_wa, m_o_lru_ba, m_o_lru_wx, m_o_lru_bx, m_o_lru_lambda, m_o_w_out, m_norm_mlp, m_w_mlp1, m_w_mlp2, m_final_norm, v_e_norm_mix, v_e_w_in, v_e_sink, v_e_w_pool, v_e_pool_scale, v_e_w_out, v_o_norm_mix, v_o_w_in, v_o_g_cq, v_o_w_uq, v_o_g_ckv, v_o_w_ukv, v_o_conv_w, v_o_conv_b, v_o_lru_wa, v_o_lru_ba, v_o_lru_wx, v_o_lru_bx, v_o_lru_lambda, v_o_w_out, v_norm_mlp, v_w_mlp1, v_w_mlp2, v_final_norm):
    W = dict(e_norm_mix=e_norm_mix, e_w_in=e_w_in, e_sink=e_sink, e_w_pool=e_w_pool, e_pool_scale=e_pool_scale, e_w_out=e_w_out, o_norm_mix=o_norm_mix, o_w_in=o_w_in, o_g_cq=o_g_cq, o_w_uq=o_w_uq, o_g_ckv=o_g_ckv, o_w_ukv=o_w_ukv, o_conv_w=o_conv_w, o_conv_b=o_conv_b, o_lru_wa=o_lru_wa, o_lru_ba=o_lru_ba, o_lru_wx=o_lru_wx, o_lru_bx=o_lru_bx, o_lru_lambda=o_lru_lambda, o_w_out=o_w_out, norm_mlp=norm_mlp, w_mlp1=w_mlp1, w_mlp2=w_mlp2, final_norm=final_norm)
    Mo = dict(e_norm_mix=m_e_norm_mix, e_w_in=m_e_w_in, e_sink=m_e_sink, e_w_pool=m_e_w_pool, e_pool_scale=m_e_pool_scale, e_w_out=m_e_w_out, o_norm_mix=m_o_norm_mix, o_w_in=m_o_w_in, o_g_cq=m_o_g_cq, o_w_uq=m_o_w_uq, o_g_ckv=m_o_g_ckv, o_w_ukv=m_o_w_ukv, o_conv_w=m_o_conv_w, o_conv_b=m_o_conv_b, o_lru_wa=m_o_lru_wa, o_lru_ba=m_o_lru_ba, o_lru_wx=m_o_lru_wx, o_lru_bx=m_o_lru_bx, o_lru_lambda=m_o_lru_lambda, o_w_out=m_o_w_out, norm_mlp=m_norm_mlp, w_mlp1=m_w_mlp1, w_mlp2=m_w_mlp2, final_norm=m_final_norm)
    Vo = dict(e_norm_mix=v_e_norm_mix, e_w_in=v_e_w_in, e_sink=v_e_sink, e_w_pool=v_e_w_pool, e_pool_scale=v_e_pool_scale, e_w_out=v_e_w_out, o_norm_mix=v_o_norm_mix, o_w_in=v_o_w_in, o_g_cq=v_o_g_cq, o_w_uq=v_o_w_uq, o_g_ckv=v_o_g_ckv, o_w_ukv=v_o_w_ukv, o_conv_w=v_o_conv_w, o_conv_b=v_o_conv_b, o_lru_wa=v_o_lru_wa, o_lru_ba=v_o_lru_ba, o_lru_wx=v_o_lru_wx, o_lru_bx=v_o_lru_bx, o_lru_lambda=v_o_lru_lambda, o_w_out=v_o_w_out, norm_mlp=v_norm_mlp, w_mlp1=v_w_mlp1, w_mlp2=v_w_mlp2, final_norm=v_final_norm)

    S = x.shape[1]
    x0 = x[0]
    target = loss_target[0]

    big_g, small_g = all_gather([_pack([W[n] for n, _ in BIG], MXU_DTYPE), _pack([W[n] for n, _ in SMALL], F32)])
    full = {}
    for (n, ax), parts in zip(BIG, _unpack(big_g, [W[n].shape for n, _ in BIG], (NDEV,))):
        full[n] = _unshard(parts, ax)
    for (n, ax), parts in zip(SMALL, _unpack(small_g, [W[n].shape for n, _ in SMALL], (NDEV,))):
        full[n] = _unshard(parts, ax)

    def even_in(w):
        return jnp.concatenate([_slab_cols(w[:, 0:512], 64, 64, 8), _slab_cols(w[:, 512:640], 64, 64, 2),
                                _slab_cols(w[:, 640:768], 64, 64, 2), w[:, 768:1280]], axis=1)

    def even_in_t(g):
        return jnp.concatenate([_unslab_cols(g[:, 0:1024], 64, 8), _unslab_cols(g[:, 1024:1280], 64, 2),
                                _unslab_cols(g[:, 1280:1536], 64, 2), g[:, 1536:2048]], axis=1)

    def slab_rows(w, n_heads):
        return _slab_cols(w.T, 64, 64, n_heads).T

    def unslab_rows(g, n_heads):
        return _unslab_cols(g.T, 64, n_heads).T

    def odd_in(w):
        return jnp.concatenate([w[:, 0:384], jnp.pad(w[:, 384:416], ((0, 0), (0, 96))), w[:, 416:1440]], axis=1)

    def odd_in_t(g):
        return jnp.concatenate([g[:, 0:416], g[:, 512:1536]], axis=1)

    def uq(w):
        return _slab_cols(w, 96, 96, 8)

    def ukv(w):
        w = w.reshape(128, 8, 128)
        pad = lambda t: jnp.pad(t, ((0, 0), (0, 0), (0, 64))).reshape(128, 1024)
        return pad(w[:, :, :64]), pad(w[:, :, 64:])

    def ukv_t(gk, gv):
        gk = gk.reshape(128, 8, 128)[:, :, :64]
        gv = gv.reshape(128, 8, 128)[:, :, :64]
        return jnp.concatenate([gk, gv], axis=2).reshape(128, 1024)

    tabs_swa = rope_tables(S, 0, SWA_HALF)
    tabs_mq = rope_tables(S, 64, MLA_HALF)
    tabs_mk = rope_tables(S, 0, MLA_HALF)
    row = lambda v: v.reshape(1, -1)

    saved = []
    xcur = x0
    for l in range(4):
        j = l // 2
        if l % 2 == 0:
            w_in = even_in(full["e_w_in"][j])
            w_out = full["e_w_out"][j]
            w_out_a, w_out_b = slab_rows(w_out[0:512], 8), w_out[512:1024]
            w_pool = W["e_w_pool"][j].astype(MXU_DTYPE)
            z, h = mm_nn(f"e_in_{j}", [xcur], [w_in], norm_g=row(W["e_norm_mix"][j]), emit_norm=True, tm=512, tn=2048)
            qkv = swa_prep(z, tabs_swa)
            ya, lse = swa_fwd(qkv, W["e_sink"][j])
            yb = pool_fwd(z, w_pool, row(W["e_pool_scale"][j]))
            x1 = mm_nn(f"e_out_{j}", [ya, yb], [w_out_a, w_out_b], res=xcur)
            mix = (xcur, z, h, qkv, lse, ya, yb, w_in, w_out_a, w_out_b, w_pool)
        else:
            w_in = odd_in(full["o_w_in"][j])
            w_out = full["o_w_out"][j]
            w_out_a, w_out_b = slab_rows(w_out[0:512], 8), w_out[512:1024]
            w_uq = uq(full["o_w_uq"][j])
            w_k, w_v = ukv(full["o_w_ukv"][j])
            g_cq, g_ckv = row(full["o_g_cq"][j]), row(W["o_g_ckv"][j])
            w_gate = jnp.concatenate([_block_diag(W["o_lru_wa"][j, 0]), _block_diag(W["o_lru_wx"][j, 0]),
                                      _block_diag(W["o_lru_wa"][j, 1]), _block_diag(W["o_lru_wx"][j, 1])], axis=1).astype(MXU_DTYPE)
            b_gate = jnp.concatenate([full["o_lru_ba"][j, 0], full["o_lru_bx"][j, 0], full["o_lru_ba"][j, 1],
                                      full["o_lru_bx"][j, 1]]).reshape(1, 2048)
            lam = full["o_lru_lambda"][j].reshape(1, 1024)
            conv_w, conv_b = full["o_conv_w"][j], row(full["o_conv_b"][j])
            z, h = mm_nn(f"o_in_{j}", [xcur], [w_in], norm_g=row(full["o_norm_mix"][j]), emit_norm=True, tm=512, tn=1536)
            q, k, v, nq, nkv = mla_prep(z, g_cq, g_ckv, w_uq, w_k, w_v, tabs_mq, tabs_mk)
            yc, lse = mla_fwd(q, k, mla_vt(v), TQ=1024)
            xc, a0, b0, a1, b1 = lru_pre(z, conv_w, conv_b, w_gate, b_gate, lam)
            h0, h1 = lru_scan(f"lru_scan_fwd_{j}", a0, b0, a1, b1, adjoint=False)
            yd = lru_gate(h0, h1, z)
            x1 = mm_nn(f"o_out_{j}", [yc, yd], [w_out_a, w_out_b], res=xcur)
            mix = (xcur, z, h, q, k, v, nq, nkv, yc, lse, xc, a0, a1, h0, h1, yd, w_in, w_out_a, w_out_b, w_uq, w_k, w_v,
                   g_cq, g_ckv, w_gate, b_gate, lam, conv_w)
        xcur, mlp = _mlp_fwd(l, x1, row(W["norm_mlp"][l]), full["w_mlp1"][l], full["w_mlp2"][l])
        saved.append((mix, mlp))

    loss_row, dx, dg_final = loss_head(xcur, row(W["final_norm"]), target)

    G = {n: [None] * W[n].shape[0] for n in WEIGHTS if n != "final_norm"}
    G["final_norm"] = dg_final.reshape(-1)
    for l in reversed(range(4)):
        j = l // 2
        mix, mlp = saved[l]
        dx, dg, dw1, dw2 = _mlp_bwd(l, mlp, row(W["norm_mlp"][l]), full["w_mlp1"][l], full["w_mlp2"][l], dx)
        G["norm_mlp"][l], G["w_mlp1"][l], G["w_mlp2"][l] = dg.reshape(-1), dw1, dw2
        if l % 2 == 0:
            xin, z, h, qkv, lse, ya, yb, w_in, w_out_a, w_out_b, w_pool = mix
            w_out_cat = jnp.concatenate([w_out_a, w_out_b], axis=0)
            dycat = mm_nt(f"e_out_dx_{j}", [dx], [w_out_cat])
            G["e_w_out"][j] = jnp.concatenate([unslab_rows(mm_tn(f"e_out_dwa_{j}", ya, dx), 8),
                                               mm_tn(f"e_out_dwb_{j}", yb, dx)], axis=0)
            dq, dk, dv, dsink = swa_bwd(qkv, W["e_sink"][j], lse, dycat, tabs_swa)
            du, dwp, dsc = pool_bwd(z, dycat, w_pool, row(W["e_pool_scale"][j]))
            G["e_sink"][j], G["e_w_pool"][j], G["e_pool_scale"][j] = dsink[0, 0:8], dwp, dsc.reshape(-1)
            dh = mm_nt(f"e_in_dx_{j}", [dq, dk, dv, du], [(w_in, 0, 1024), (w_in, 4, 256), (w_in, 5, 256), (w_in, 3, 512)])
            G["e_w_in"][j] = even_in_t(jnp.concatenate(
                [mm_tn(f"e_in_dwq_{j}", h, dq), mm_tn(f"e_in_dwk_{j}", h, dk), mm_tn(f"e_in_dwv_{j}", h, dv),
                 mm_tn(f"e_in_dwu_{j}", h, du)], axis=1))
            dx, dg = rms_bwd(f"e_norm_bwd_{j}", xin, row(W["e_norm_mix"][j]), dh, dx)
            G["e_norm_mix"][j] = dg.reshape(-1)
        else:
            (xin, z, h, q, k, v, nq, nkv, yc, lse, xc, a0, a1, h0, h1, yd, w_in, w_out_a, w_out_b, w_uq, w_k, w_v,
             g_cq, g_ckv, w_gate, b_gate, lam, conv_w) = mix
            w_out_cat = jnp.concatenate([w_out_a, w_out_b], axis=0)
            dycat = mm_nt(f"o_out_dx_{j}", [dx], [w_out_cat])
            G["o_w_out"][j] = jnp.concatenate([unslab_rows(mm_tn(f"o_out_dwa_{j}", yc, dx), 8),
                                               mm_tn(f"o_out_dwb_{j}", yd, dx)], axis=0)
            delta, dob = mla_delta(yc, dycat)
            dqt, dk, dv = mla_bwd(q, k, v, dob, lse, delta)
            dq = mla_dq(dqt)
            dza, dqp, dgq, dgkv = mla_prep_bwd(z, g_cq, g_ckv, w_uq, w_k, w_v, dq, dk, dv, tabs_mq, tabs_mk)
            G["o_g_cq"][j], G["o_g_ckv"][j] = dgq.reshape(-1), dgkv.reshape(-1)
            G["o_w_uq"][j] = _unslab_cols(mm_tn(f"o_uq_dw_{j}", nq, dqp), 96, 8)
            G["o_w_ukv"][j] = ukv_t(mm_tn(f"o_uk_dw_{j}", nkv, dk), mm_tn(f"o_uv_dw_{j}", nkv, dv))
            dxg, dhh = lru_gate_bwd(h0, h1, z, dycat)
            g1, g0 = lru_scan(f"lru_scan_bwd_{j}", a1, dhh, a0, dhh, adjoint=True)
            dxc, dpre, dbias, dlam = lru_bwd_point(xc, h0, h1, g0, g1, w_gate, b_gate, lam)
            dwg = mm_tn(f"o_gate_dw_{j}", xc, dpre)
            G["o_lru_wa"][j] = jnp.stack([_block_diag_t(dwg[:, 0:512]), _block_diag_t(dwg[:, 1024:1536])])
            G["o_lru_wx"][j] = jnp.stack([_block_diag_t(dwg[:, 512:1024]), _block_diag_t(dwg[:, 1536:2048])])
            G["o_lru_ba"][j] = jnp.stack([dbias[0, 0:512], dbias[0, 1024:1536]])
            G["o_lru_bx"][j] = jnp.stack([dbias[0, 512:1024], dbias[0, 1536:2048]])
            G["o_lru_lambda"][j] = dlam.reshape(2, 512)
            dxr, dcw, dcb = conv_bwd(z, dxc, conv_w)
            G["o_conv_w"][j], G["o_conv_b"][j] = dcw, dcb.reshape(-1)
            dh = mm_nt(f"o_in_dx_{j}", [dza, dxr, dxg], [(w_in, 0, 512), (w_in, 1, 512), (w_in, 2, 512)])
            G["o_w_in"][j] = odd_in_t(jnp.concatenate(
                [mm_tn(f"o_in_dwa_{j}", h, dza), mm_tn(f"o_in_dwr_{j}", h, dxr), mm_tn(f"o_in_dwg_{j}", h, dxg)], axis=1))
            dx, dg = rms_bwd(f"o_norm_bwd_{j}", xin, row(full["o_norm_mix"][j]), dh, dx)
            G["o_norm_mix"][j] = dg.reshape(-1)
    grad_x = dx[None]
    G["final_norm"] = [G["final_norm"]]

    sharded = BIG + SMALL
    me = 4 * lax.axis_index("x") + 2 * lax.axis_index("y") + lax.axis_index("c")

    def layers(T, n):
        return [T[n]] if n == "final_norm" else [T[n][l] for l in range(T[n].shape[0])]

    def layer_shapes(names):
        return [a.shape for n in names for a in layers(W, n)]

    def shards_of(n, ax, dev):
        size = W[n].shape[ax]
        return [lax.dynamic_slice_in_dim(g, dev * size, size, axis=ax - 1) for g in G[n]]

    gs = jnp.stack([_pack([p for n, ax in sharded for p in shards_of(n, ax, me ^ r)], F32) for r in range(NDEV)])
    gr = _pack([g for n in REPL for g in G[n]] + [loss_row[0, 0:1]], F32)
    mine, others = pair_add(gs, grad_pair(gs))
    others, gr_all = grad_cross(others, gr)
    names_s = [n for n, _ in sharded]
    outs_s = adamw("adamw_sharded", [mine[None], others],
                   *[_pack([a for n in names_s for a in layers(T, n)], F32) for T in (W, Mo, Vo)])
    outs_r = adamw("adamw_replicated", [gr_all],
                   *[_pack([a for n in REPL for a in layers(T, n)] + [jnp.zeros((1,), F32)], F32) for T in (W, Mo, Vo)])
    res = [dict(), dict(), dict(), dict()]
    for kind in range(4):
        pieces = (_unpack(outs_s[kind], layer_shapes(names_s))
                  + _unpack(outs_r[kind], layer_shapes(REPL) + [(1,)]))
        for n in names_s + list(REPL):
            count = len(layers(W, n))
            got, pieces = pieces[:count], pieces[count:]
            res[kind][n] = got[0] if n == "final_norm" else jnp.stack(got)
        res[kind]["loss"] = pieces[0]
    loss = res[0]["loss"][0]
    return (loss, grad_x, *[res[0][n] for n in WEIGHTS], *[res[1][n] for n in WEIGHTS],
            *[res[2][n] for n in WEIGHTS], *[res[3][n] for n in WEIGHTS])
```

```python
import functools

import jax
import jax.numpy as jnp
from jax import lax
from jax.experimental import pallas as pl
from jax.experimental.pallas import tpu as pltpu

F32 = jnp.float32
MXU_DTYPE = jnp.bfloat16
EPS = 1e-6
ROPE_THETA = 10000.0
NDEV = 8
LANES = 128
VMEM_LIMIT = 48 * 1024 * 1024

D_MODEL = 1024
D_FF = 4096
LRU_C = 8.0
POOL_WINDOWS = (2, 4, 8, 16)
HALO = 8
MLA_SCALE = 96.0 ** -0.5
LOG2E = 1.4426950408889634
MLA_QSCALE = MLA_SCALE * LOG2E

ADAM_LR, ADAM_B1, ADAM_B2, ADAM_EPS, ADAM_WD, ADAM_STEP = 0.001, 0.9, 0.999, 1e-08, 0.01, 10


def _mx(v):
    return v.astype(MXU_DTYPE)


def _call(body, *, name, grid, in_specs, out_specs, out_shape, scratch=()):
    return pl.pallas_call(
        body, name=name, grid=grid, in_specs=in_specs, out_specs=out_specs, out_shape=out_shape,
        scratch_shapes=list(scratch),
        compiler_params=pltpu.CompilerParams(
            dimension_semantics=("arbitrary",) * len(grid), vmem_limit_bytes=VMEM_LIMIT),
    )


def _dot(a, b):
    return jnp.dot(a, b, preferred_element_type=F32)


def _dot_nt(a, b):
    return lax.dot_general(a, b, (((1,), (1,)), ((), ())), preferred_element_type=F32)


def _dot_tn(a, b):
    return lax.dot_general(a, b, (((0,), (0,)), ((), ())), preferred_element_type=F32)


def _rms(x, g):
    r = lax.rsqrt(jnp.mean(x * x, axis=-1, keepdims=True) + EPS)
    return (x * r) * g


def _rms_bwd(x, g, dy):
    r = lax.rsqrt(jnp.mean(x * x, axis=-1, keepdims=True) + EPS)
    xh = x * r
    dyg = dy * g
    dx = r * (dyg - xh * jnp.mean(dyg * xh, axis=-1, keepdims=True))
    return dx, jnp.sum(dy * xh, axis=0, keepdims=True)


def _sigmoid(x):
    return 1.0 / (1.0 + jnp.exp(-x))


def _log1p(e):
    u = 1.0 + e
    d = u - 1.0
    return jnp.where(d == 0.0, e, jnp.log(u) * (e / jnp.where(d == 0.0, 1.0, d)))


def _softplus(x):
    return jnp.maximum(x, 0.0) + _log1p(jnp.exp(-jnp.abs(x)))


def _expm1(x):
    u = jnp.exp(x)
    lu = jnp.log(u)
    safe = jnp.where((lu == 0.0) | (u == 0.0), 1.0, lu)
    return jnp.where(u == 1.0, x, jnp.where(u == 0.0, -1.0, (u - 1.0) * x / safe))


_GELU_K = 0.7978845608028654


def _gelu(x):
    return 0.5 * x * (1.0 + jnp.tanh(_GELU_K * (x + 0.044715 * x * x * x)))


def _gelu_grad(x):
    t = jnp.tanh(_GELU_K * (x + 0.044715 * x * x * x))
    return 0.5 * (1.0 + t) + 0.5 * x * (1.0 - t * t) * _GELU_K * (1.0 + 3.0 * 0.044715 * x * x)


def _rope(x, c, sa, sb, half):
    return x * c + pltpu.roll(x, LANES - half, 1) * sa + pltpu.roll(x, half, 1) * sb


def _rope_t(d, c, sa, sb, half):
    return d * c - pltpu.roll(d, LANES - half, 1) * sa - pltpu.roll(d, half, 1) * sb


def _as_cols(a):
    return a if isinstance(a, tuple) else (a, 0, a.shape[1])


def mm_nn(name, a_list, b_list, *, res=None, act=None, norm_g=None, emit_norm=False,
          out_dtype=F32, tm=512, tn=1024):
    a_list = [_as_cols(a) for a in a_list]
    M, N = a_list[0][0].shape[0], b_list[0].shape[1]
    tm, tn = min(tm, M), min(tn, N)
    na = len(a_list)

    def body(*refs):
        a_refs, b_refs = refs[:na], refs[na:2 * na]
        k = 2 * na
        g_ref = res_ref = hn_ref = None
        if norm_g is not None:
            g_ref = refs[k]
            k += 1
        if res is not None:
            res_ref = refs[k]
            k += 1
        o_ref = refs[k]
        if emit_norm:
            hn_ref = refs[k + 1]
        acc = None
        for a_ref, b_ref in zip(a_refs, b_refs):
            a = a_ref[...]
            if g_ref is not None:
                a = _rms(a.astype(F32), g_ref[...])
                if hn_ref is not None:
                    hn_ref[...] = a.astype(hn_ref.dtype)
            if act == "relu2":
                a = jnp.maximum(a, 0.0)
                a = a * a
            d = _dot(_mx(a), _mx(b_ref[...]))
            acc = d if acc is None else acc + d
        if res_ref is not None:
            acc = acc + res_ref[...]
        o_ref[...] = acc.astype(o_ref.dtype)

    in_specs = [pl.BlockSpec((tm, w), functools.partial(lambda i, j, cb: (i, cb), cb=cb)) for (_, cb, w) in a_list]
    in_specs += [pl.BlockSpec((b.shape[0], tn), lambda i, j: (0, j)) for b in b_list]
    args = [a for (a, _, _) in a_list] + list(b_list)
    if norm_g is not None:
        in_specs.append(pl.BlockSpec((1, norm_g.shape[1]), lambda i, j: (0, 0)))
        args.append(norm_g)
    if res is not None:
        in_specs.append(pl.BlockSpec((tm, tn), lambda i, j: (i, j)))
        args.append(res)
    out_specs = [pl.BlockSpec((tm, tn), lambda i, j: (i, j))]
    out_shape = [jax.ShapeDtypeStruct((M, N), out_dtype)]
    if emit_norm:
        K = a_list[0][2]
        out_specs.append(pl.BlockSpec((tm, K), lambda i, j: (i, 0)))
        out_shape.append(jax.ShapeDtypeStruct((M, K), MXU_DTYPE))
    out = _call(body, name=name, grid=(M // tm, N // tn), in_specs=in_specs, out_specs=out_specs,
                out_shape=out_shape)(*args)
    return out if emit_norm else out[0]


def mm_nt(name, a_list, b_list, *, relu2_of=None, out_dtype=F32, tm=512, tn=1536):
    a_list = [_as_cols(a) for a in a_list]
    b_list = [_as_cols(b) for b in b_list]
    M, N = a_list[0][0].shape[0], b_list[0][0].shape[0]
    tm, tn = min(tm, M), min(tn, N)
    na = len(a_list)

    def body(*refs):
        a_refs, b_refs = refs[:na], refs[na:2 * na]
        u_ref = refs[2 * na] if relu2_of is not None else None
        o_ref = refs[-1]
        acc = None
        for a_ref, b_ref in zip(a_refs, b_refs):
            d = _dot_nt(_mx(a_ref[...]), _mx(b_ref[...]))
            acc = d if acc is None else acc + d
        if u_ref is not None:
            acc = acc * (2.0 * jnp.maximum(u_ref[...], 0.0))
        o_ref[...] = acc.astype(o_ref.dtype)

    in_specs = [pl.BlockSpec((tm, w), functools.partial(lambda i, j, cb: (i, cb), cb=cb)) for (_, cb, w) in a_list]
    in_specs += [pl.BlockSpec((tn, w), functools.partial(lambda i, j, cb: (j, cb), cb=cb)) for (_, cb, w) in b_list]
    args = [a for (a, _, _) in a_list] + [b for (b, _, _) in b_list]
    if relu2_of is not None:
        in_specs.append(pl.BlockSpec((tm, tn), lambda i, j: (i, j)))
        args.append(relu2_of)
    return _call(body, name=name, grid=(M // tm, N // tn), in_specs=in_specs,
                 out_specs=pl.BlockSpec((tm, tn), lambda i, j: (i, j)),
                 out_shape=jax.ShapeDtypeStruct((M, N), out_dtype))(*args)


def mm_tn(name, a, b, *, act=None, tm=1024, tn=1024, tk=1024):
    a, acb, Ma = _as_cols(a)
    b, bcb, Nb = _as_cols(b)
    S = a.shape[0]
    tm, tn, tk = min(tm, Ma), min(tn, Nb), min(tk, S)
    a0, b0 = acb * (Ma // tm), bcb * (Nb // tn)

    def body(a_ref, b_ref, o_ref):
        @pl.when(pl.program_id(2) == 0)
        def _():
            o_ref[...] = jnp.zeros_like(o_ref)

        av = a_ref[...]
        if act == "relu2":
            av = jnp.maximum(av, 0.0)
            av = av * av
        o_ref[...] += _dot_tn(_mx(av), _mx(b_ref[...]))

    return _call(body, name=name, grid=(Ma // tm, Nb // tn, S // tk),
                 in_specs=[pl.BlockSpec((tk, tm), lambda i, j, k: (k, a0 + i)),
                           pl.BlockSpec((tk, tn), lambda i, j, k: (k, b0 + j))],
                 out_specs=pl.BlockSpec((tm, tn), lambda i, j, k: (i, j)),
                 out_shape=jax.ShapeDtypeStruct((Ma, Nb), F32))(a, b)


def rms_bwd(name, x, g, dh, dres, T=512):
    S, D = x.shape
    T = min(T, S)

    def body(x_ref, g_ref, dh_ref, dres_ref, dx_ref, dg_ref):
        @pl.when(pl.program_id(0) == 0)
        def _():
            dg_ref[...] = jnp.zeros_like(dg_ref)

        dx, dg = _rms_bwd(x_ref[...], g_ref[...], dh_ref[...])
        dx_ref[...] = dres_ref[...] + dx
        dg_ref[...] += dg

    row = pl.BlockSpec((T, D), lambda i: (i, 0))
    vec = pl.BlockSpec((1, D), lambda i: (0, 0))
    return _call(body, name=name, grid=(S // T,), in_specs=[row, vec, row, row], out_specs=[row, vec],
                 out_shape=[jax.ShapeDtypeStruct((S, D), F32), jax.ShapeDtypeStruct((1, D), F32)])(x, g, dh, dres)


def loss_head(x, g, target, T=512):
    S, D = x.shape
    T = min(T, S)

    def body(x_ref, g_ref, t_ref, loss_ref, dx_ref, dg_ref):
        @pl.when(pl.program_id(0) == 0)
        def _():
            dg_ref[...] = jnp.zeros_like(dg_ref)
            loss_ref[...] = jnp.zeros_like(loss_ref)

        x = x_ref[...]
        err = _rms(x, g_ref[...]) - t_ref[...]
        loss_ref[...] += 0.5 * jnp.sum(jnp.sum(err * err, axis=-1, keepdims=True) / D, axis=0, keepdims=True)
        dx, dg = _rms_bwd(x, g_ref[...], err / D)
        dx_ref[...] = dx
        dg_ref[...] += dg

    row = pl.BlockSpec((T, D), lambda i: (i, 0))
    vec = pl.BlockSpec((1, D), lambda i: (0, 0))
    return _call(body, name="loss_head", grid=(S // T,), in_specs=[row, vec, row],
                 out_specs=[pl.BlockSpec((1, LANES), lambda i: (0, 0)), row, vec],
                 out_shape=[jax.ShapeDtypeStruct((1, LANES), F32), jax.ShapeDtypeStruct((S, D), F32),
                            jax.ShapeDtypeStruct((1, D), F32)])(x, g, target)


def rope_tables(S, lo, half):
    inv = ROPE_THETA ** (-jnp.arange(half, dtype=F32) / half)
    ang = jnp.arange(S, dtype=F32)[:, None] * inv[None, :]
    cos, sin = jnp.cos(ang), jnp.sin(ang)
    one = lambda n: jnp.ones((S, n), F32)
    zero = lambda n: jnp.zeros((S, n), F32)
    hi = LANES - lo - 2 * half
    c = jnp.concatenate([one(lo), cos, cos, one(hi)], axis=1)
    sa = jnp.concatenate([zero(lo), -sin, zero(half), zero(hi)], axis=1)
    sb = jnp.concatenate([zero(lo), zero(half), sin, zero(hi)], axis=1)
    return c, sa, sb


SWA_BLOCK = 128
SWA_HALF = 32


def swa_prep(z, tabs, T=512):
    S = z.shape[0]
    T = min(T, S)

    def body(z_ref, c_ref, sa_ref, sb_ref, o_ref):
        c, sa, sb = c_ref[...], sa_ref[...], sb_ref[...]
        for s in range(10):
            sl = slice(s * LANES, (s + 1) * LANES)
            y = _rope(z_ref[:, sl], c, sa, sb, SWA_HALF)
            if s < 8:
                y = y * 0.125
            o_ref[:, sl] = y.astype(o_ref.dtype)
        o_ref[:, 1280:1536] = z_ref[:, 1280:1536].astype(o_ref.dtype)

    tab = pl.BlockSpec((T, LANES), lambda i: (i, 0))
    return _call(body, name="swa_prep", grid=(S // T,),
                 in_specs=[pl.BlockSpec((T, 1536), lambda i: (i, 0)), tab, tab, tab],
                 out_specs=pl.BlockSpec((T, 1536), lambda i: (i, 0)),
                 out_shape=jax.ShapeDtypeStruct((S, 1536), MXU_DTYPE))(z, *tabs)


SWA_GROUP = 4


def _swa_valid(n, S):
    B = SWA_BLOCK
    i = lax.broadcasted_iota(jnp.int32, (SWA_GROUP * B, 3 * B), 0) & (B - 1)
    j = lax.broadcasted_iota(jnp.int32, (SWA_GROUP * B, 3 * B), 1)
    kpos = j + (n - 1) * B
    return (jnp.abs(j - B - i) <= B) & (kpos >= 0) & (kpos < S)


def _swa_sink(sink_ref, hk):
    B = SWA_BLOCK
    row = lax.broadcasted_iota(jnp.int32, (SWA_GROUP * B, 1), 0)
    sk = jnp.full((SWA_GROUP * B, 1), sink_ref[hk * SWA_GROUP], F32)
    for g in range(1, SWA_GROUP):
        sk = jnp.where(row >= g * B, sink_ref[hk * SWA_GROUP + g], sk)
    return sk


def swa_fwd(qkv, sink):
    S = qkv.shape[0]
    B = SWA_BLOCK
    nb = S // B

    def body(sink_ref, q_ref, kp_ref, kc_ref, kn_ref, vp_ref, vc_ref, vn_ref, o_ref, st_ref):
        n = pl.program_id(0)
        valid = _swa_valid(n, S)
        lane = lax.broadcasted_iota(jnp.int32, (B, LANES), 1)
        st = jnp.zeros((B, LANES), F32)
        for hk in range(2):
            sl = slice(hk * LANES, (hk + 1) * LANES)
            k3 = jnp.concatenate([kp_ref[:, sl], kc_ref[:, sl], kn_ref[:, sl]], axis=0)
            v3 = jnp.concatenate([vp_ref[:, sl], vc_ref[:, sl], vn_ref[:, sl]], axis=0)
            heads = [hk * SWA_GROUP + g for g in range(SWA_GROUP)]
            q4 = jnp.concatenate([q_ref[:, h * LANES:(h + 1) * LANES] for h in heads], axis=0)
            s = jnp.where(valid, _dot_nt(q4, k3), -jnp.inf)
            sk = _swa_sink(sink_ref, hk)
            m = jnp.maximum(jnp.max(s, axis=-1, keepdims=True), sk)
            p = jnp.exp(s - m)
            den = jnp.sum(p, axis=-1, keepdims=True) + jnp.exp(sk - m)
            p = p / den
            o4 = _dot(_mx(p), v3)
            lse4 = m + jnp.log(den)
            for g, h in enumerate(heads):
                rows = slice(g * B, (g + 1) * B)
                o_ref[:, h * LANES:(h + 1) * LANES] = o4[rows].astype(o_ref.dtype)
                st = jnp.where(lane == h, lse4[rows], st)
        st_ref[...] = st

    kv = lambda cb, d: pl.BlockSpec((B, 2 * LANES), lambda n: (jnp.clip(n + d, 0, nb - 1), cb))
    return _call(body, name="swa_fwd", grid=(nb,),
                 in_specs=[pl.BlockSpec(memory_space=pltpu.SMEM),
                           pl.BlockSpec((B, 1024), lambda n: (n, 0)),
                           kv(4, -1), kv(4, 0), kv(4, 1), kv(5, -1), kv(5, 0), kv(5, 1)],
                 out_specs=[pl.BlockSpec((B, 1024), lambda n: (n, 0)), pl.BlockSpec((B, LANES), lambda n: (n, 0))],
                 out_shape=[jax.ShapeDtypeStruct((S, 1024), MXU_DTYPE), jax.ShapeDtypeStruct((S, LANES), F32)],
                 )(sink, qkv, qkv, qkv, qkv, qkv, qkv, qkv)


def swa_bwd(qkv, sink, lse, dycat, tabs):
    S = qkv.shape[0]
    B = SWA_BLOCK
    nb = S // B

    def body(sink_ref, q_ref, kp_ref, kc_ref, kn_ref, vp_ref, vc_ref, vn_ref, do_ref, st_ref,
             cq_ref, saq_ref, sbq_ref, ck_ref, sak_ref, sbk_ref,
             dq_ref, dk_ref, dv_ref, dsink_ref, dk_acc, dv_acc):
        n = pl.program_id(0)

        @pl.when(n == 0)
        def _():
            dk_acc[...] = jnp.zeros_like(dk_acc)
            dv_acc[...] = jnp.zeros_like(dv_acc)
            dsink_ref[...] = jnp.zeros_like(dsink_ref)

        @pl.when(n < nb)
        def _():
            valid = _swa_valid(n, S)
            lane = lax.broadcasted_iota(jnp.int32, (B, LANES), 1)
            lane1 = lax.broadcasted_iota(jnp.int32, (1, LANES), 1)
            st = st_ref[...]
            cq, saq, sbq = cq_ref[...], saq_ref[...], sbq_ref[...]
            dsink = jnp.zeros((1, LANES), F32)
            for hk in range(2):
                sl = slice(hk * LANES, (hk + 1) * LANES)
                k3 = jnp.concatenate([kp_ref[:, sl], kc_ref[:, sl], kn_ref[:, sl]], axis=0)
                v3 = jnp.concatenate([vp_ref[:, sl], vc_ref[:, sl], vn_ref[:, sl]], axis=0)
                heads = [hk * SWA_GROUP + g for g in range(SWA_GROUP)]
                q4 = jnp.concatenate([q_ref[:, h * LANES:(h + 1) * LANES] for h in heads], axis=0)
                do4 = jnp.concatenate([_mx(do_ref[:, h * LANES:(h + 1) * LANES]) for h in heads], axis=0)
                lse4 = jnp.concatenate([jnp.sum(jnp.where(lane == h, st, 0.0), axis=-1, keepdims=True) for h in heads],
                                       axis=0)
                p = jnp.where(valid, jnp.exp(_dot_nt(q4, k3) - lse4), 0.0)
                dp = _dot_nt(do4, v3)
                dsum = jnp.sum(p * dp, axis=-1, keepdims=True)
                ds = _mx(p * (dp - dsum))
                dq4 = _dot(ds, k3) * 0.125
                dk_acc[:, sl] += _dot_tn(ds, q4)
                dv_acc[:, sl] += _dot_tn(_mx(p), do4)
                dsk4 = jnp.exp(_swa_sink(sink_ref, hk) - lse4) * dsum
                for g, h in enumerate(heads):
                    rows = slice(g * B, (g + 1) * B)
                    dq_ref[:, h * LANES:(h + 1) * LANES] = _rope_t(dq4[rows], cq, saq, sbq, SWA_HALF)
                    dsink = jnp.where(lane1 == h, -jnp.sum(dsk4[rows], axis=0, keepdims=True), dsink)
            dsink_ref[...] += dsink

        ck, sak, sbk = ck_ref[...], sak_ref[...], sbk_ref[...]
        for hk in range(2):
            sl = slice(hk * LANES, (hk + 1) * LANES)
            dk_ref[:, sl] = _rope_t(dk_acc[0:B, sl], ck, sak, sbk, SWA_HALF)
        dv_ref[...] = dv_acc[0:B, :]
        for acc in (dk_acc, dv_acc):
            acc[0:B, :] = acc[B:2 * B, :]
            acc[B:2 * B, :] = acc[2 * B:3 * B, :]
            acc[2 * B:3 * B, :] = jnp.zeros((B, 2 * LANES), F32)

    qn = lambda n: jnp.minimum(n, nb - 1)
    kv = lambda cb, d: pl.BlockSpec((B, 2 * LANES), lambda n: (jnp.clip(qn(n) + d, 0, nb - 1), cb))
    qrow = lambda w: pl.BlockSpec((B, w), lambda n: (qn(n), 0))
    krow = lambda w: pl.BlockSpec((B, w), lambda n: (jnp.maximum(n - 1, 0), 0))
    return _call(body, name="swa_bwd", grid=(nb + 1,),
                 in_specs=[pl.BlockSpec(memory_space=pltpu.SMEM), qrow(1024),
                           kv(4, -1), kv(4, 0), kv(4, 1), kv(5, -1), kv(5, 0), kv(5, 1),
                           qrow(1024), qrow(LANES),
                           qrow(LANES), qrow(LANES), qrow(LANES), krow(LANES), krow(LANES), krow(LANES)],
                 out_specs=[qrow(1024), krow(2 * LANES), krow(2 * LANES), pl.BlockSpec((1, LANES), lambda n: (0, 0))],
                 out_shape=[jax.ShapeDtypeStruct((S, 1024), F32), jax.ShapeDtypeStruct((S, 2 * LANES), F32),
                            jax.ShapeDtypeStruct((S, 2 * LANES), F32), jax.ShapeDtypeStruct((1, LANES), F32)],
                 scratch=[pltpu.VMEM((3 * B, 2 * LANES), F32), pltpu.VMEM((3 * B, 2 * LANES), F32)],
                 )(sink, qkv, qkv, qkv, qkv, qkv, qkv, qkv, dycat, lse, *tabs, *tabs)


def _halo_specs(T, S, w, cb):
    r = T // HALO
    last = S // HALO - 1
    return [pl.BlockSpec((HALO, w), lambda i: (jnp.maximum(i * r - 1, 0), cb)),
            pl.BlockSpec((T, w), lambda i: (i, cb)),
            pl.BlockSpec((HALO, w), lambda i: (jnp.minimum((i + 1) * r, last), cb))]


def _fill_ext(ext, prev_ref, cur_ref, next_ref, i, nt, T):
    ext[0:HALO, :] = jnp.where(i > 0, prev_ref[...], 0.0).astype(F32)
    ext[HALO:HALO + T, :] = cur_ref[...].astype(F32)
    ext[HALO + T:2 * HALO + T, :] = jnp.where(i < nt - 1, next_ref[...], 0.0).astype(F32)


def _pool_cnt(t, half, S):
    return (jnp.clip(t + half, 0, S) - jnp.clip(t - half, 0, S)).astype(F32)


def pool_fwd(z, w_pool, scale, T=512):
    S = z.shape[0]
    T = min(T, S)
    nt = S // T

    def body(up_ref, uc_ref, un_ref, w_ref, sc_ref, o_ref, ext):
        i = pl.program_id(0)
        _fill_ext(ext, up_ref, uc_ref, un_ref, i, nt, T)
        t = i * T + lax.broadcasted_iota(jnp.int32, (T, 1), 0)
        for g, win in enumerate(POOL_WINDOWS):
            half = win // 2
            sl = slice(g * LANES, (g + 1) * LANES)
            acc = ext[pl.ds(HALO - half, T), sl]
            for off in range(-half + 1, half):
                acc = acc + ext[pl.ds(HALO + off, T), sl]
            d = acc / _pool_cnt(t, half, S) - ext[pl.ds(HALO, T), sl]
            o_ref[:, sl] = (_dot(_mx(d), w_ref[g]) * sc_ref[:, sl]).astype(o_ref.dtype)

    return _call(body, name="pool_fwd", grid=(nt,),
                 in_specs=_halo_specs(T, S, 512, 3) + [pl.BlockSpec((4, LANES, LANES), lambda i: (0, 0, 0)),
                                                      pl.BlockSpec((1, 512), lambda i: (0, 0))],
                 out_specs=pl.BlockSpec((T, 512), lambda i: (i, 0)),
                 out_shape=jax.ShapeDtypeStruct((S, 512), MXU_DTYPE),
                 scratch=[pltpu.VMEM((T + 2 * HALO, 512), F32)])(z, z, z, w_pool, scale)


def pool_bwd(z, dycat, w_pool, scale, T=512):
    S = z.shape[0]
    T = min(T, S)
    nt = S // T
    TE = T + 2 * HALO

    def body(up_ref, uc_ref, un_ref, yp_ref, yc_ref, yn_ref, w_ref, sc_ref, du_ref, dw_ref, dsc_ref, extu, exty, exte):
        i = pl.program_id(0)

        @pl.when(i == 0)
        def _():
            dw_ref[...] = jnp.zeros_like(dw_ref)
            dsc_ref[...] = jnp.zeros_like(dsc_ref)

        _fill_ext(extu, up_ref, uc_ref, un_ref, i, nt, T)
        _fill_ext(exty, yp_ref, yc_ref, yn_ref, i, nt, T)
        t = i * T + lax.broadcasted_iota(jnp.int32, (T, 1), 0)
        te = i * T - HALO + lax.broadcasted_iota(jnp.int32, (TE, 1), 0)
        for g, win in enumerate(POOL_WINDOWS):
            half = win // 2
            sl = slice(g * LANES, (g + 1) * LANES)
            w = w_ref[g]
            acc = extu[pl.ds(HALO - half, T), sl]
            for off in range(-half + 1, half):
                acc = acc + extu[pl.ds(HALO + off, T), sl]
            d = _mx(acc / _pool_cnt(t, half, S) - extu[pl.ds(HALO, T), sl])
            dy = exty[pl.ds(HALO, T), sl]
            dsc_ref[:, sl] += jnp.sum(dy * _dot(d, w), axis=0, keepdims=True)
            dw_ref[g] += _dot_tn(d, _mx(dy * sc_ref[:, sl]))
            dd = _dot_nt(_mx(exty[:, sl] * sc_ref[:, sl]), w)
            exte[:, sl] = dd / jnp.maximum(_pool_cnt(te, half, S), 1.0)
            acc = exte[pl.ds(HALO - half + 1, T), sl]
            for off in range(-half + 2, half + 1):
                acc = acc + exte[pl.ds(HALO + off, T), sl]
            du_ref[:, sl] = acc - dd[HALO:HALO + T, :]

    return _call(body, name="pool_bwd", grid=(nt,),
                 in_specs=_halo_specs(T, S, 512, 3) + _halo_specs(T, S, 512, 2)
                 + [pl.BlockSpec((4, LANES, LANES), lambda i: (0, 0, 0)), pl.BlockSpec((1, 512), lambda i: (0, 0))],
                 out_specs=[pl.BlockSpec((T, 512), lambda i: (i, 0)), pl.BlockSpec((4, LANES, LANES), lambda i: (0, 0, 0)),
                            pl.BlockSpec((1, 512), lambda i: (0, 0))],
                 out_shape=[jax.ShapeDtypeStruct((S, 512), F32), jax.ShapeDtypeStruct((4, LANES, LANES), F32),
                            jax.ShapeDtypeStruct((1, 512), F32)],
                 scratch=[pltpu.VMEM((TE, 512), F32)] * 3)(z, z, z, dycat, dycat, dycat, w_pool, scale)


MLA_HALF = 16
MLA_V_ONE = 64


def mla_prep(z, g_cq, g_ckv, w_uq, w_k, w_v, tabs_q, tabs_k, T=512):
    S = z.shape[0]
    T = min(T, S)

    def body(z_ref, gq_ref, gkv_ref, wq_ref, wk_ref, wv_ref, cq_ref, saq_ref, sbq_ref, ck_ref, sak_ref, sbk_ref,
             q_ref, k_ref, v_ref, nq_ref, nkv_ref):
        nq = _mx(_rms(z_ref[:, 0:256], gq_ref[...]))
        nkv = _mx(_rms(z_ref[:, 256:384], gkv_ref[...]))
        nq_ref[...] = nq
        nkv_ref[...] = nkv
        q = _dot(nq, wq_ref[...])
        kn = _dot(nkv, wk_ref[...])
        lane = lax.broadcasted_iota(jnp.int32, (T, 1024), 1)
        v_ref[...] = jnp.where(lane % LANES == MLA_V_ONE, 1.0, _dot(nkv, wv_ref[...])).astype(v_ref.dtype)
        kr = pltpu.roll(_rope(z_ref[:, 384:512], ck_ref[...], sak_ref[...], sbk_ref[...], MLA_HALF), 64, 1)
        cq, saq, sbq = cq_ref[...], saq_ref[...], sbq_ref[...]
        for h in range(8):
            hs = slice(h * LANES, (h + 1) * LANES)
            q_ref[:, hs] = (_rope(q[:, hs], cq, saq, sbq, MLA_HALF) * MLA_QSCALE).astype(q_ref.dtype)
            k_ref[:, hs] = (kn[:, hs] + kr).astype(k_ref.dtype)

    tab = pl.BlockSpec((T, LANES), lambda i: (i, 0))
    full = lambda a: pl.BlockSpec(a.shape, lambda i: (0, 0))
    row = lambda w: pl.BlockSpec((T, w), lambda i: (i, 0))
    sd = lambda w: jax.ShapeDtypeStruct((S, w), MXU_DTYPE)
    return _call(body, name="mla_prep", grid=(S // T,),
                 in_specs=[row(512), full(g_cq), full(g_ckv), full(w_uq), full(w_k), full(w_v)] + [tab] * 6,
                 out_specs=[row(1024), row(1024), row(1024), row(256), row(128)],
                 out_shape=[sd(1024), sd(1024), sd(1024), sd(256), sd(128)],
                 )(z, g_cq, g_ckv, w_uq, w_k, w_v, *tabs_q, *tabs_k)


def _col_to_row(c):
    return jnp.transpose(jnp.broadcast_to(c, (c.shape[0], LANES)))[0:1, :]


def mla_vt(v, TK=512):
    S = v.shape[0]
    TK = min(TK, S // 2)
    return v.reshape(S // TK, TK, 8, LANES).transpose(2, 0, 3, 1)


def mla_dq(dqt):
    _, nq, _, TQ = dqt.shape
    return dqt.transpose(1, 3, 0, 2).reshape(nq * TQ, 1024)


def mla_fwd(q, k, vt, TQ=512):
    S = q.shape[0]
    nk, TK = vt.shape[1], vt.shape[3]
    TQ = min(TQ, S)
    QB = min(256, TQ)
    assert nk % 2 == 0
    unroll = 8 if nk % 8 == 0 else 4 if nk % 4 == 0 else 2

    def body(q_ref, k_ref, vt_ref, o_ref, lse_ref, m_s, acc_s, s_a, s_b):
        m_s[...] = jnp.full_like(m_s, -jnp.inf)
        acc_s[...] = jnp.zeros_like(acc_s)
        q = q_ref[...]

        def scores(c):
            return _dot_nt(k_ref[pl.ds(pl.multiple_of(c * TK, TK), TK), :], q)

        def softmax_pv(s_ref, c):
            vt_c = vt_ref[0, c]
            for b in range(TQ // QB):
                cols = slice(b * QB, (b + 1) * QB)
                s = s_ref[:, cols]
                m_old = m_s[:, cols]
                m_new = jnp.maximum(m_old, jnp.max(s, axis=0, keepdims=True))
                alpha = jnp.exp2(m_old[0:1, :] - m_new[0:1, :])
                p = jnp.exp2(s - m_new[0:1, :])
                acc_s[:, cols] = alpha * acc_s[:, cols] + _dot(vt_c, _mx(p))
                m_s[:, cols] = m_new

        s_a[...] = scores(0)
        bufs = (s_a, s_b)

        def group(jj, carry):
            c = unroll * jj
            for u in range(unroll):
                nxt = c + u + 1 if u < unroll - 1 else jnp.minimum(c + unroll, nk - 1)
                bufs[(u + 1) % 2][...] = scores(nxt)
                softmax_pv(bufs[u % 2], c + u)
            return carry

        lax.fori_loop(0, nk // unroll, group, 0)
        acc = acc_s[...]
        den = acc[MLA_V_ONE:MLA_V_ONE + 1, :]
        sub = lax.broadcasted_iota(jnp.int32, (LANES, TQ), 0)
        o_ref[...] = jnp.transpose(jnp.where(sub < MLA_V_ONE, acc / den, 0.0))
        lse_ref[0] = m_s[0:1, :] + jnp.log(den) * LOG2E

    qs = pl.BlockSpec((TQ, LANES), lambda h, i: (i, h))
    return _call(body, name="mla_fwd", grid=(8, S // TQ),
                 in_specs=[qs, pl.BlockSpec((S, LANES), lambda h, i: (0, h)),
                           pl.BlockSpec((1, nk, LANES, TK), lambda h, i: (h, 0, 0, 0))],
                 out_specs=[qs, pl.BlockSpec((1, 1, TQ), lambda h, i: (h, 0, i))],
                 out_shape=[jax.ShapeDtypeStruct((S, 1024), F32), jax.ShapeDtypeStruct((8, 1, S), F32)],
                 scratch=[pltpu.VMEM((8, TQ), F32), pltpu.VMEM((LANES, TQ), F32),
                          pltpu.VMEM((TK, TQ), F32), pltpu.VMEM((TK, TQ), F32)],
                 )(q, k, vt)


def mla_delta(o, dycat, TQ=512):
    S = o.shape[0]
    TQ = min(TQ, S)

    def body(o_ref, do_ref, d_ref, dob_ref):
        for h in range(8):
            hs = slice(h * LANES, (h + 1) * LANES)
            do = do_ref[:, hs]
            d_ref[h] = _col_to_row(jnp.sum(o_ref[:, hs] * do, axis=-1, keepdims=True))
            dob_ref[:, hs] = do.astype(dob_ref.dtype)

    qs = pl.BlockSpec((TQ, 1024), lambda i: (i, 0))
    return _call(body, name="mla_delta", grid=(S // TQ,), in_specs=[qs, qs],
                 out_specs=[pl.BlockSpec((8, 1, TQ), lambda i: (0, 0, i)), qs],
                 out_shape=[jax.ShapeDtypeStruct((8, 1, S), F32), jax.ShapeDtypeStruct((S, 1024), MXU_DTYPE)])(o, dycat)


def mla_bwd(q, k, v, do, lse, delta, TQ=512, TK=512):
    S = q.shape[0]
    TQ, TK = min(TQ, S // 2), min(TK, S)
    nq, nk = S // TQ, S // TK
    assert nq % 2 == 0
    unroll = 8 if nq % 8 == 0 else 4 if nq % 4 == 0 else 2
    lse = lse.reshape(8, nq, 1, TQ)
    delta = delta.reshape(8, nq, 1, TQ)

    def body(q_ref, do_ref, k_ref, v_ref, lse_ref, d_ref, dqt_ref, dk_ref, dv_ref, dk_s, dv_s, s_a, p_a, s_b, p_b):
        @pl.when(pl.program_id(1) == 0)
        def _():
            dqt_ref[...] = jnp.zeros_like(dqt_ref)

        dk_s[...] = jnp.zeros_like(dk_s)
        dv_s[...] = jnp.zeros_like(dv_s)
        kk, vv = k_ref[...], v_ref[...]
        kt = _mx(jnp.transpose(kk.astype(F32)))

        def rows(c):
            return pl.ds(pl.multiple_of(c * TQ, TQ), TQ)

        def products(c, s_ref, p_ref):
            s_ref[...] = _dot_nt(kk, q_ref[rows(c), :])
            p_ref[...] = _dot_nt(vv, do_ref[rows(c), :])

        def consume(c, s_ref, p_ref):
            qc, doc = q_ref[rows(c), :], do_ref[rows(c), :]
            pt = jnp.exp2(s_ref[...] - lse_ref[0, c])
            dv_s[...] += _dot(_mx(pt), doc)
            ds = _mx(pt * (p_ref[...] - d_ref[0, c]))
            dk_s[...] += _dot(ds, qc)
            dqt_ref[0, c] += _dot(kt, ds)

        products(0, s_a, p_a)
        bufs = ((s_a, p_a), (s_b, p_b))

        def group(jj, carry):
            c = unroll * jj
            for u in range(unroll):
                nxt = c + u + 1 if u < unroll - 1 else jnp.minimum(c + unroll, nq - 1)
                products(nxt, *bufs[(u + 1) % 2])
                consume(c + u, *bufs[u % 2])
            return carry

        lax.fori_loop(0, nq // unroll, group, 0)
        dk_ref[...] = dk_s[...] * (1.0 / LOG2E)
        dv_ref[...] = dv_s[...]

    full = pl.BlockSpec((S, LANES), lambda h, j: (0, h))
    ks = pl.BlockSpec((TK, LANES), lambda h, j: (j, h))
    st = pl.BlockSpec((1, nq, 1, TQ), lambda h, j: (h, 0, 0, 0))
    sd = jax.ShapeDtypeStruct((S, 1024), F32)
    return _call(body, name="mla_bwd", grid=(8, nk), in_specs=[full, full, ks, ks, st, st],
                 out_specs=[pl.BlockSpec((1, nq, LANES, TQ), lambda h, j: (h, 0, 0, 0)), ks, ks],
                 out_shape=[jax.ShapeDtypeStruct((8, nq, LANES, TQ), F32), sd, sd],
                 scratch=[pltpu.VMEM((TK, LANES), F32), pltpu.VMEM((TK, LANES), F32)] + [pltpu.VMEM((TK, TQ), F32)] * 4,
                 )(q, do, k, v, lse, delta)


def mla_prep_bwd(z, g_cq, g_ckv, w_uq, w_k, w_v, dq, dk, dv, tabs_q, tabs_k, T=256):
    S = z.shape[0]
    T = min(T, S)

    def body(z_ref, gq_ref, gkv_ref, wq_ref, wk_ref, wv_ref, dq_ref, dk_ref, dv_ref,
             cq_ref, saq_ref, sbq_ref, ck_ref, sak_ref, sbk_ref, dz_ref, dqp_ref, dgq_ref, dgkv_ref):
        @pl.when(pl.program_id(0) == 0)
        def _():
            dgq_ref[...] = jnp.zeros_like(dgq_ref)
            dgkv_ref[...] = jnp.zeros_like(dgkv_ref)

        cq, saq, sbq = cq_ref[...], saq_ref[...], sbq_ref[...]
        dkr = jnp.zeros((T, LANES), F32)
        for h in range(8):
            hs = slice(h * LANES, (h + 1) * LANES)
            dqp_ref[:, hs] = _rope_t(dq_ref[:, hs] * MLA_SCALE, cq, saq, sbq, MLA_HALF).astype(dqp_ref.dtype)
            dkr = dkr + dk_ref[:, hs]
        lane = lax.broadcasted_iota(jnp.int32, (T, LANES), 1)
        dkr = jnp.where(lane < 2 * MLA_HALF, pltpu.roll(dkr, 64, 1), 0.0)
        dz_ref[:, 384:512] = _rope_t(dkr, ck_ref[...], sak_ref[...], sbk_ref[...], MLA_HALF)
        dnq = _dot_nt(dqp_ref[...], wq_ref[...])
        dx, dg = _rms_bwd(z_ref[:, 0:256], gq_ref[...], dnq)
        dz_ref[:, 0:256] = dx
        dgq_ref[...] += dg
        dnkv = _dot_nt(_mx(dk_ref[...]), wk_ref[...]) + _dot_nt(_mx(dv_ref[...]), wv_ref[...])
        dx, dg = _rms_bwd(z_ref[:, 256:384], gkv_ref[...], dnkv)
        dz_ref[:, 256:384] = dx
        dgkv_ref[...] += dg

    tab = pl.BlockSpec((T, LANES), lambda i: (i, 0))
    full = lambda a: pl.BlockSpec(a.shape, lambda i: (0, 0))
    row = lambda w: pl.BlockSpec((T, w), lambda i: (i, 0))
    return _call(body, name="mla_prep_bwd", grid=(S // T,),
                 in_specs=[row(512), full(g_cq), full(g_ckv), full(w_uq), full(w_k), full(w_v),
                           row(1024), row(1024), row(1024)] + [tab] * 6,
                 out_specs=[row(512), row(1024), full(g_cq), full(g_ckv)],
                 out_shape=[jax.ShapeDtypeStruct((S, 512), F32), jax.ShapeDtypeStruct((S, 1024), MXU_DTYPE),
                            jax.ShapeDtypeStruct(g_cq.shape, F32), jax.ShapeDtypeStruct(g_ckv.shape, F32)],
                 )(z, g_cq, g_ckv, w_uq, w_k, w_v, dq, dk, dv, *tabs_q, *tabs_k)


def _lru_gates(xc, w_ref, bias_ref, lam_ref):
    pre = _dot(_mx(xc), w_ref[...]) + bias_ref[...]
    out = []
    for d in range(2):
        r = _sigmoid(pre[:, d * 1024:d * 1024 + 512])
        ig = _sigmoid(pre[:, d * 1024 + 512:(d + 1) * 1024])
        log_a = -LRU_C * r * _softplus(-lam_ref[:, d * 512:(d + 1) * 512])
        out.append((r, ig, jnp.exp(log_a), jnp.sqrt(-_expm1(2.0 * log_a))))
    return out


def lru_pre(z, conv_w, conv_b, w_gate, b_gate, lam, T=256):
    S = z.shape[0]
    T = min(T, S)
    nt = S // T

    def body(xp_ref, xcur_ref, xn_ref, cw_ref, cb_ref, w_ref, bias_ref, lam_ref, xc_ref, a0_ref, b0_ref, a1_ref, b1_ref, ext):
        i = pl.program_id(0)
        _fill_ext(ext, xp_ref, xcur_ref, xn_ref, i, nt, T)
        xc = cb_ref[...] + cw_ref[0:1, :] * ext[pl.ds(HALO - 2, T), :]
        for j in range(1, 4):
            xc = xc + cw_ref[j:j + 1, :] * ext[pl.ds(HALO - 2 + j, T), :]
        xc_ref[...] = xc
        (_, i0, a0, m0), (_, i1, a1, m1) = _lru_gates(xc, w_ref, bias_ref, lam_ref)
        a0_ref[...] = a0
        b0_ref[...] = m0 * (i0 * xc)
        a1_ref[...] = a1
        b1_ref[...] = m1 * (i1 * xc)

    full = lambda a: pl.BlockSpec(a.shape, lambda i: (0, 0))
    row = pl.BlockSpec((T, 512), lambda i: (i, 0))
    sd = jax.ShapeDtypeStruct((S, 512), F32)
    return _call(body, name="lru_pre", grid=(nt,),
                 in_specs=_halo_specs(T, S, 512, 1) + [full(conv_w), full(conv_b), full(w_gate), full(b_gate), full(lam)],
                 out_specs=[row] * 5, out_shape=[sd] * 5,
                 scratch=[pltpu.VMEM((T + 2 * HALO, 512), F32)])(z, z, z, conv_w, conv_b, w_gate, b_gate, lam)


def lru_scan(name, af, bf, ar, br, *, adjoint, T=512):
    S, W = af.shape
    T = min(T, S)
    nt = S // T
    nc = T // 8

    def body(af_ref, bf_ref, ar_ref, br_ref, hf_ref, hr_ref, cf, cr):
        @pl.when(pl.program_id(0) == 0)
        def _():
            cf[...] = jnp.zeros_like(cf)
            cr[...] = jnp.zeros_like(cr)

        row = lax.broadcasted_iota(jnp.int32, (8, W), 0)

        def step(a, b, carry):
            if adjoint:
                val = b + carry
                return val, a * val
            val = a * carry + b
            return val, val

        def chunk(c, carry):
            hf, hr = carry
            of = pl.multiple_of(c * 8, 8)
            orv = pl.multiple_of((nc - 1 - c) * 8, 8)
            a8, b8 = af_ref[pl.ds(of, 8), :], bf_ref[pl.ds(of, 8), :]
            ra8, rb8 = ar_ref[pl.ds(orv, 8), :], br_ref[pl.ds(orv, 8), :]
            outf = jnp.zeros((8, W), F32)
            outr = jnp.zeros((8, W), F32)
            for k in range(8):
                val, hf = step(a8[k:k + 1, :], b8[k:k + 1, :], hf)
                outf = jnp.where(row == k, val, outf)
                kr = 7 - k
                val, hr = step(ra8[kr:kr + 1, :], rb8[kr:kr + 1, :], hr)
                outr = jnp.where(row == kr, val, outr)
            hf_ref[pl.ds(of, 8), :] = outf
            hr_ref[pl.ds(orv, 8), :] = outr
            return hf, hr

        hf, hr = lax.fori_loop(0, nc, chunk, (cf[0:1, :], cr[0:1, :]))
        cf[0:1, :] = hf
        cr[0:1, :] = hr

    fw = pl.BlockSpec((T, W), lambda i: (i, 0))
    rv = pl.BlockSpec((T, W), lambda i: (nt - 1 - i, 0))
    sd = jax.ShapeDtypeStruct((S, W), F32)
    return _call(body, name=name, grid=(nt,), in_specs=[fw, fw, rv, rv], out_specs=[fw, rv], out_shape=[sd, sd],
                 scratch=[pltpu.VMEM((8, W), F32), pltpu.VMEM((8, W), F32)])(af, bf, ar, br)


def lru_gate(h0, h1, z, T=512):
    S = z.shape[0]
    T = min(T, S)

    def body(h0_ref, h1_ref, xg_ref, y_ref):
        y_ref[...] = ((h0_ref[...] + h1_ref[...]) * _gelu(xg_ref[...])).astype(y_ref.dtype)

    row = pl.BlockSpec((T, 512), lambda i: (i, 0))
    return _call(body, name="lru_gate", grid=(S // T,), in_specs=[row, row, pl.BlockSpec((T, 512), lambda i: (i, 2))],
                 out_specs=row, out_shape=jax.ShapeDtypeStruct((S, 512), MXU_DTYPE))(h0, h1, z)


def lru_gate_bwd(h0, h1, z, dycat, T=512):
    S = z.shape[0]
    T = min(T, S)

    def body(h0_ref, h1_ref, xg_ref, dy_ref, dxg_ref, dh_ref):
        xg, dy = xg_ref[...], dy_ref[...]
        dxg_ref[...] = dy * (h0_ref[...] + h1_ref[...]) * _gelu_grad(xg)
        dh_ref[...] = dy * _gelu(xg)

    row = pl.BlockSpec((T, 512), lambda i: (i, 0))
    col2 = pl.BlockSpec((T, 512), lambda i: (i, 2))
    sd = jax.ShapeDtypeStruct((S, 512), F32)
    return _call(body, name="lru_gate_bwd", grid=(S // T,), in_specs=[row, row, col2, col2],
                 out_specs=[row, row], out_shape=[sd, sd])(h0, h1, z, dycat)


def lru_bwd_point(xc, h0, h1, g0, g1, w_gate, b_gate, lam, T=256):
    S = xc.shape[0]
    T = min(T, S)
    nt = S // T

    def body(xc_ref, h0p_ref, h0_ref, h0n_ref, h1p_ref, h1_ref, h1n_ref, g0_ref, g1_ref, w_ref, bias_ref, lam_ref,
             dxc_ref, dpre_ref, dbias_ref, dlam_ref, ext0, ext1):
        i = pl.program_id(0)

        @pl.when(i == 0)
        def _():
            dbias_ref[...] = jnp.zeros_like(dbias_ref)
            dlam_ref[...] = jnp.zeros_like(dlam_ref)

        _fill_ext(ext0, h0p_ref, h0_ref, h0n_ref, i, nt, T)
        _fill_ext(ext1, h1p_ref, h1_ref, h1n_ref, i, nt, T)
        xc = xc_ref[...]
        gates = _lru_gates(xc, w_ref, bias_ref, lam_ref)
        hshift = (ext0[pl.ds(HALO - 1, T), :], ext1[pl.ds(HALO + 1, T), :])
        gs = (g0_ref[...], g1_ref[...])
        dxc = jnp.zeros((T, 512), F32)
        for d in range(2):
            r, ig, a, mult = gates[d]
            db = gs[d]
            da = db * hshift[d]
            dmult = db * (ig * xc)
            di = db * (mult * xc)
            dxc = dxc + db * (mult * ig)
            dloga = da * a - dmult * (a * a / mult)
            lam_d = lam_ref[:, d * 512:(d + 1) * 512]
            dr = dloga * (-LRU_C * _softplus(-lam_d))
            dsp = jnp.sum(dloga * (-LRU_C * r), axis=0, keepdims=True)
            dlam_ref[:, d * 512:(d + 1) * 512] += dsp * (-_sigmoid(-lam_d))
            dpre_ref[:, d * 1024:d * 1024 + 512] = (dr * (r * (1.0 - r))).astype(dpre_ref.dtype)
            dpre_ref[:, d * 1024 + 512:(d + 1) * 1024] = (di * (ig * (1.0 - ig))).astype(dpre_ref.dtype)
            dbias_ref[:, d * 1024:d * 1024 + 512] += jnp.sum(dr * (r * (1.0 - r)), axis=0, keepdims=True)
            dbias_ref[:, d * 1024 + 512:(d + 1) * 1024] += jnp.sum(di * (ig * (1.0 - ig)), axis=0, keepdims=True)
        dxc_ref[...] = dxc + _dot_nt(dpre_ref[...], w_ref[...])

    full = lambda a: pl.BlockSpec(a.shape, lambda i: (0, 0))
    row = pl.BlockSpec((T, 512), lambda i: (i, 0))
    return _call(body, name="lru_bwd_point", grid=(nt,),
                 in_specs=[row] + _halo_specs(T, S, 512, 0) + _halo_specs(T, S, 512, 0) + [row, row, full(w_gate), full(b_gate), full(lam)],
                 out_specs=[row, pl.BlockSpec((T, 2048), lambda i: (i, 0)), full(b_gate), full(lam)],
                 out_shape=[jax.ShapeDtypeStruct((S, 512), F32), jax.ShapeDtypeStruct((S, 2048), MXU_DTYPE),
                            jax.ShapeDtypeStruct(b_gate.shape, F32), jax.ShapeDtypeStruct(lam.shape, F32)],
                 scratch=[pltpu.VMEM((T + 2 * HALO, 512), F32)] * 2,
                 )(xc, h0, h0, h0, h1, h1, h1, g0, g1, w_gate, b_gate, lam)


def conv_bwd(z, dxc, conv_w, T=512):
    S = z.shape[0]
    T = min(T, S)
    nt = S // T

    def body(xp_ref, xcur_ref, xn_ref, dp_ref, dcur_ref, dn_ref, cw_ref, dx_ref, dw_ref, db_ref, extx, extd):
        i = pl.program_id(0)

        @pl.when(i == 0)
        def _():
            dw_ref[...] = jnp.zeros_like(dw_ref)
            db_ref[...] = jnp.zeros_like(db_ref)

        _fill_ext(extx, xp_ref, xcur_ref, xn_ref, i, nt, T)
        _fill_ext(extd, dp_ref, dcur_ref, dn_ref, i, nt, T)
        d = extd[pl.ds(HALO, T), :]
        dx = cw_ref[0:1, :] * extd[pl.ds(HALO + 2, T), :]
        for j in range(1, 4):
            dx = dx + cw_ref[j:j + 1, :] * extd[pl.ds(HALO + 2 - j, T), :]
        dx_ref[...] = dx
        for j in range(4):
            dw_ref[j:j + 1, :] += jnp.sum(d * extx[pl.ds(HALO - 2 + j, T), :], axis=0, keepdims=True)
        db_ref[...] += jnp.sum(d, axis=0, keepdims=True)

    full = lambda a: pl.BlockSpec(a.shape, lambda i: (0, 0))
    row = pl.BlockSpec((T, 512), lambda i: (i, 0))
    return _call(body, name="conv_bwd", grid=(nt,),
                 in_specs=_halo_specs(T, S, 512, 1) + _halo_specs(T, S, 512, 0) + [full(conv_w)],
                 out_specs=[row, full(conv_w), pl.BlockSpec((1, 512), lambda i: (0, 0))],
                 out_shape=[jax.ShapeDtypeStruct((S, 512), F32), jax.ShapeDtypeStruct(conv_w.shape, F32),
                            jax.ShapeDtypeStruct((1, 512), F32)],
                 scratch=[pltpu.VMEM((T + 2 * HALO, 512), F32)] * 2)(z, z, z, dxc, dxc, dxc, conv_w)


MESH_ID = pl.DeviceIdType.MESH
ANY = pl.BlockSpec(memory_space=pl.ANY)


def _place():
    x, y, c = lax.axis_index("x"), lax.axis_index("y"), lax.axis_index("c")
    chips = [(1 - x, y), (x, 1 - y), (1 - x, 1 - y)]
    return x, y, c, chips


def _slot(px, py, pc):
    return 4 * px + 2 * py + pc


def all_gather(arrays):
    n = len(arrays)

    def body(*refs):
        ins, outs = refs[:n], refs[n:2 * n]
        send, recv, loc = refs[2 * n:]
        x, y, c, chips = _place()
        me, sibling = (x, y, c), (x, y, 1 - c)

        def copy(a, k, block, to, src=None):
            slot = outs[a].at[_slot(*block)]
            return pltpu.make_async_remote_copy(src_ref=slot if src is None else src, dst_ref=slot,
                                                send_sem=send.at[a * 7 + k], recv_sem=recv.at[a * 7 + k],
                                                device_id=to, device_id_type=MESH_ID)

        local = [pltpu.make_async_copy(ins[a], outs[a].at[_slot(*me)], loc.at[a]) for a in range(n)]
        for cp in local:
            cp.start()
        first = []
        for a in range(n):
            first.append(copy(a, 0, me, sibling, src=ins[a]))
            first += [copy(a, 1 + j, me, (*chip, c), src=ins[a]) for j, chip in enumerate(chips)]
        for cp in first:
            cp.start()
        passed = []
        for a in range(n):
            for j, chip in enumerate(chips):
                copy(a, 1 + j, (*chip, c), me).wait_recv()
                cp = copy(a, 4 + j, (*chip, c), sibling)
                cp.start()
                passed.append(cp)
        for a in range(n):
            copy(a, 0, sibling, me).wait_recv()
            for j, chip in enumerate(chips):
                copy(a, 4 + j, (*chip, 1 - c), me).wait_recv()
        for cp in first + passed:
            cp.wait_send()
        for cp in local:
            cp.wait()

    return pl.pallas_call(
        body, name="all_gather", in_specs=[ANY] * n, out_specs=[ANY] * n,
        out_shape=[jax.ShapeDtypeStruct((NDEV,) + a.shape, a.dtype) for a in arrays],
        scratch_shapes=[pltpu.SemaphoreType.DMA((n * 7,)), pltpu.SemaphoreType.DMA((n * 7,)),
                        pltpu.SemaphoreType.DMA((n,))],
    )(*arrays)


def grad_pair(gs):
    def body(gs_ref, o_ref, send, recv):
        x, y, c, _ = _place()
        cps = [pltpu.make_async_remote_copy(src_ref=gs_ref.at[2 * q + 1], dst_ref=o_ref.at[q], send_sem=send.at[q],
                                            recv_sem=recv.at[q], device_id=(x, y, 1 - c), device_id_type=MESH_ID)
               for q in range(4)]
        for cp in cps:
            cp.start()
        for cp in cps:
            cp.wait_recv()
        for cp in cps:
            cp.wait_send()

    return pl.pallas_call(
        body, name="grad_pair", in_specs=[ANY], out_specs=ANY,
        out_shape=jax.ShapeDtypeStruct((4,) + gs.shape[1:], gs.dtype),
        scratch_shapes=[pltpu.SemaphoreType.DMA((4,)), pltpu.SemaphoreType.DMA((4,))],
    )(gs)


def pair_add(gs, got, T=128):
    _, R, C = gs.shape
    T = min(T, R)

    def body(g0, g1, g2, g3, got_ref, own_ref, out_ref):
        own_ref[...] = g0[0] + got_ref[0]
        for q, g in enumerate((g1, g2, g3)):
            out_ref[q] = (g[0] + got_ref[q + 1]).astype(out_ref.dtype)

    even = lambda q: pl.BlockSpec((1, T, C), lambda i: (2 * q, i, 0))
    return _call(body, name="pair_add", grid=(R // T,),
                 in_specs=[even(0), even(1), even(2), even(3), pl.BlockSpec((4, T, C), lambda i: (0, i, 0))],
                 out_specs=[pl.BlockSpec((T, C), lambda i: (i, 0)), pl.BlockSpec((3, T, C), lambda i: (0, i, 0))],
                 out_shape=[jax.ShapeDtypeStruct((R, C), F32), jax.ShapeDtypeStruct((3, R, C), MXU_DTYPE)],
                 )(gs, gs, gs, gs, got)


def grad_cross(part, gr):
    def body(p_ref, gr_ref, o_ref, or_ref, send, recv, loc):
        x, y, c, chips = _place()
        peers = [(x, y, 1 - c)] + [(*chip, c) for chip in chips] + [(*chip, 1 - c) for chip in chips]
        local = pltpu.make_async_copy(gr_ref, or_ref.at[_slot(x, y, c)], loc.at[0])
        local.start()
        by_offset = [(x, 1 - y), (1 - x, y), (1 - x, 1 - y)]
        cps = [pltpu.make_async_remote_copy(src_ref=p_ref.at[j], dst_ref=o_ref.at[j], send_sem=send.at[j],
                                            recv_sem=recv.at[j], device_id=(*chip, c), device_id_type=MESH_ID)
               for j, chip in enumerate(by_offset)]
        cps += [pltpu.make_async_remote_copy(src_ref=gr_ref, dst_ref=or_ref.at[_slot(x, y, c)], send_sem=send.at[3 + k],
                                             recv_sem=recv.at[3 + k], device_id=peer, device_id_type=MESH_ID)
                for k, peer in enumerate(peers)]
        for cp in cps:
            cp.start()
        for cp in cps[:3]:
            cp.wait_recv()
        for k, peer in enumerate(peers):
            pltpu.make_async_remote_copy(src_ref=gr_ref, dst_ref=or_ref.at[_slot(*peer)], send_sem=send.at[3 + k],
                                         recv_sem=recv.at[3 + k], device_id=peer, device_id_type=MESH_ID).wait_recv()
        for cp in cps:
            cp.wait_send()
        local.wait()

    return pl.pallas_call(
        body, name="grad_cross", in_specs=[ANY, ANY], out_specs=[ANY, ANY],
        out_shape=[jax.ShapeDtypeStruct(part.shape, part.dtype), jax.ShapeDtypeStruct((NDEV,) + gr.shape, gr.dtype)],
        scratch_shapes=[pltpu.SemaphoreType.DMA((10,)), pltpu.SemaphoreType.DMA((10,)), pltpu.SemaphoreType.DMA((1,))],
    )(part, gr)


def adamw(name, gparts, w, m, v, T=128):
    R, C = w.shape
    T = min(T, R)
    ng = len(gparts)

    def body(*refs):
        g_refs = refs[:ng]
        w_ref, m_ref, v_ref, go_ref, d_ref, mo_ref, vo_ref = refs[ng:]
        g = None
        for g_ref in g_refs:
            for k in range(g_ref.shape[0]):
                t = g_ref[k].astype(F32)
                g = t if g is None else g + t
        mn = ADAM_B1 * m_ref[...] + (1.0 - ADAM_B1) * g
        vn = ADAM_B2 * v_ref[...] + (1.0 - ADAM_B2) * (g * g)
        m_hat = mn / (1.0 - ADAM_B1 ** ADAM_STEP)
        v_hat = vn / (1.0 - ADAM_B2 ** ADAM_STEP)
        go_ref[...] = g
        d_ref[...] = -ADAM_LR * (m_hat / (jnp.sqrt(v_hat) + ADAM_EPS) + ADAM_WD * w_ref[...])
        mo_ref[...] = mn
        vo_ref[...] = vn

    row = pl.BlockSpec((T, C), lambda i: (i, 0))
    sd = jax.ShapeDtypeStruct((R, C), F32)
    return _call(body, name=name, grid=(R // T,),
                 in_specs=[pl.BlockSpec((g.shape[0], T, C), lambda i: (0, i, 0)) for g in gparts] + [row, row, row],
                 out_specs=[row] * 4, out_shape=[sd] * 4)(*gparts, w, m, v)


PACK_W = 1024
PACK_TILE = 16
PACK_ROWS = 128


def _entry_rows(shape):
    n = 1
    for s in shape:
        n *= s
    return n, -(-n // (PACK_TILE * PACK_W)) * PACK_TILE


def _pack(arrays, dtype):
    mats, rows = [], 0
    for a in arrays:
        n, r = _entry_rows(a.shape)
        flat = a.astype(dtype).reshape(-1)
        if r * PACK_W != n:
            flat = jnp.pad(flat, (0, r * PACK_W - n))
        mats.append(flat.reshape(r, PACK_W))
        rows += r
    total = -(-rows // PACK_ROWS) * PACK_ROWS
    if total > rows:
        mats.append(jnp.zeros((total - rows, PACK_W), dtype))
    return jnp.concatenate(mats, axis=0)


def _unpack(buf, shapes, lead=()):
    out, off = [], 0
    for sh in shapes:
        n, r = _entry_rows(sh)
        piece = buf[..., off:off + r, :].reshape(lead + (r * PACK_W,))
        out.append(piece[..., :n].reshape(lead + tuple(sh)))
        off += r
    return out


def _unshard(parts, axis):
    return jnp.concatenate([parts[d] for d in range(NDEV)], axis=axis)


def _slab_cols(w, head, used, n_heads, slab=LANES):
    lead = w.shape[:-1]
    w = w.reshape(lead + (n_heads, head))[..., :used]
    w = jnp.pad(w, [(0, 0)] * len(lead) + [(0, 0), (0, slab - used)])
    return w.reshape(lead + (n_heads * slab,))


def _unslab_cols(g, used, n_heads, slab=LANES):
    lead = g.shape[:-1]
    return g.reshape(lead + (n_heads, slab))[..., :used].reshape(lead + (n_heads * used,))


def _block_diag(w):
    eye = jnp.eye(8, dtype=w.dtype)
    return jnp.einsum("nij,nm->nimj", w, eye).reshape(512, 512)


def _block_diag_t(g):
    g = g.reshape(8, 64, 8, 64)
    return jnp.stack([g[n, :, n, :] for n in range(8)])


BIG = (("e_w_in", 2), ("e_w_out", 1), ("o_w_in", 2), ("o_w_uq", 2), ("o_w_ukv", 2), ("o_w_out", 1),
       ("w_mlp1", 2), ("w_mlp2", 1))
SMALL = (("o_norm_mix", 1), ("o_g_cq", 1), ("o_conv_w", 2), ("o_conv_b", 1), ("o_lru_ba", 2), ("o_lru_bx", 2),
         ("o_lru_lambda", 2))
REPL = ("e_norm_mix", "e_sink", "e_w_pool", "e_pool_scale", "o_g_ckv", "o_lru_wa", "o_lru_wx", "norm_mlp",
        "final_norm")
WEIGHTS = ("e_norm_mix", "e_w_in", "e_sink", "e_w_pool", "e_pool_scale", "e_w_out", "o_norm_mix", "o_w_in", "o_g_cq",
           "o_w_uq", "o_g_ckv", "o_w_ukv", "o_conv_w", "o_conv_b", "o_lru_wa", "o_lru_ba", "o_lru_wx", "o_lru_bx",
           "o_lru_lambda", "o_w_out", "norm_mlp", "w_mlp1", "w_mlp2", "final_norm")


def _mlp_fwd(l, x, g, w1, w2):
    u1, hn = mm_nn(f"mlp1_{l}", [x], [w1], norm_g=g, emit_norm=True, out_dtype=MXU_DTYPE, tm=1024, tn=1024)
    x2 = mm_nn(f"mlp2_{l}", [u1], [w2], act="relu2", res=x, tm=512, tn=1024)
    return x2, (x, u1, hn)


def _mlp_bwd(l, saved, g, w1, w2, dx2):
    x, u1, hn = saved
    du1 = mm_nt(f"mlp2_dx_{l}", [dx2], [w2], relu2_of=u1, out_dtype=MXU_DTYPE, tm=1024, tn=1024)
    dw2 = mm_tn(f"mlp2_dw_{l}", u1, dx2, act="relu2", tm=1024, tn=1024)
    dhn = mm_nt(f"mlp1_dx_{l}", [du1], [w1], tm=512, tn=1024)
    dw1 = mm_tn(f"mlp1_dw_{l}", hn, du1, tm=1024, tn=1024)
    dx, dg = rms_bwd(f"mlp_norm_bwd_{l}", x, g, dhn, dx2)
    return dx, dg, dw1, dw2


def kernel(x, e_norm_mix, e_w_in, e_sink, e_w_pool, e_pool_scale, e_w_out, o_norm_mix, o_w_in, o_g_cq, o_w_uq, o_g_ckv, o_w_ukv, o_conv_w, o_conv_b, o_lru_wa, o_lru_ba, o_lru_wx, o_lru_bx, o_lru_lambda, o_w_out, norm_mlp, w_mlp1, w_mlp2, final_norm, loss_target, m_e_norm_mix, m_e_w_in, m_e_sink, m_e_w_pool, m_e_pool_scale, m_e_w_out, m_o_norm_mix, m_o_w_in, m_o_g_cq, m_o_w_uq, m_o_g_ckv, m_o_w_ukv, m_o_conv_w, m_o_conv_b, m_o_lru_wa, m_o_lru_ba, m_o_lru_wx, m_o_lru_bx, m_o_lru_lambda, m_o_w_out, m_norm_mlp, m_w_mlp1, m_w_mlp2, m_final_norm, v_e_norm_mix, v_e_w_in, v_e_sink, v_e_w_pool, v_e_pool_scale, v_e_w_out, v_o_norm_mix, v_o_w_in, v_o_g_cq, v_o_w_uq, v_o_g_ckv, v_o_w_ukv, v_o_conv_w, v_o_conv_b, v_o_lru_wa, v_o_lru_ba, v_o_lru_wx, v_o_lru_bx, v_o_lru_lambda, v_o_w_out, v_norm_mlp, v_w_mlp1, v_w_mlp2, v_final_norm):
    W = dict(e_norm_mix=e_norm_mix, e_w_in=e_w_in, e_sink=e_sink, e_w_pool=e_w_pool, e_pool_scale=e_pool_scale, e_w_out=e_w_out, o_norm_mix=o_norm_mix, o_w_in=o_w_in, o_g_cq=o_g_cq, o_w_uq=o_w_uq, o_g_ckv=o_g_ckv, o_w_ukv=o_w_ukv, o_conv_w=o_conv_w, o_conv_b=o_conv_b, o_lru_wa=o_lru_wa, o_lru_ba=o_lru_ba, o_lru_wx=o_lru_wx, o_lru_bx=o_lru_bx, o_lru_lambda=o_lru_lambda, o_w_out=o_w_out, norm_mlp=norm_mlp, w_mlp1=w_mlp1, w_mlp2=w_mlp2, final_norm=final_norm)
    Mo = dict(e_norm_mix=m_e_norm_mix, e_w_in=m_e_w_in, e_sink=m_e_sink, e_w_pool=m_e_w_pool, e_pool_scale=m_e_pool_scale, e_w_out=m_e_w_out, o_norm_mix=m_o_norm_mix, o_w_in=m_o_w_in, o_g_cq=m_o_g_cq, o_w_uq=m_o_w_uq, o_g_ckv=m_o_g_ckv, o_w_ukv=m_o_w_ukv, o_conv_w=m_o_conv_w, o_conv_b=m_o_conv_b, o_lru_wa=m_o_lru_wa, o_lru_ba=m_o_lru_ba, o_lru_wx=m_o_lru_wx, o_lru_bx=m_o_lru_bx, o_lru_lambda=m_o_lru_lambda, o_w_out=m_o_w_out, norm_mlp=m_norm_mlp, w_mlp1=m_w_mlp1, w_mlp2=m_w_mlp2, final_norm=m_final_norm)
    Vo = dict(e_norm_mix=v_e_norm_mix, e_w_in=v_e_w_in, e_sink=v_e_sink, e_w_pool=v_e_w_pool, e_pool_scale=v_e_pool_scale, e_w_out=v_e_w_out, o_norm_mix=v_o_norm_mix, o_w_in=v_o_w_in, o_g_cq=v_o_g_cq, o_w_uq=v_o_w_uq, o_g_ckv=v_o_g_ckv, o_w_ukv=v_o_w_ukv, o_conv_w=v_o_conv_w, o_conv_b=v_o_conv_b, o_lru_wa=v_o_lru_wa, o_lru_ba=v_o_lru_ba, o_lru_wx=v_o_lru_wx, o_lru_bx=v_o_lru_bx, o_lru_lambda=v_o_lru_lambda, o_w_out=v_o_w_out, norm_mlp=v_norm_mlp, w_mlp1=v_w_mlp1, w_mlp2=v_w_mlp2, final_norm=v_final_norm)

    S = x.shape[1]
    x0 = x[0]
    target = loss_target[0]

    big_g, small_g = all_gather([_pack([W[n] for n, _ in BIG], MXU_DTYPE), _pack([W[n] for n, _ in SMALL], F32)])
    full = {}
    for (n, ax), parts in zip(BIG, _unpack(big_g, [W[n].shape for n, _ in BIG], (NDEV,))):
        full[n] = _unshard(parts, ax)
    for (n, ax), parts in zip(SMALL, _unpack(small_g, [W[n].shape for n, _ in SMALL], (NDEV,))):
        full[n] = _unshard(parts, ax)

    def even_in(w):
        return jnp.concatenate([_slab_cols(w[:, 0:512], 64, 64, 8), _slab_cols(w[:, 512:640], 64, 64, 2),
                                _slab_cols(w[:, 640:768], 64, 64, 2), w[:, 768:1280]], axis=1)

    def even_in_t(g):
        return jnp.concatenate([_unslab_cols(g[:, 0:1024], 64, 8), _unslab_cols(g[:, 1024:1280], 64, 2),
                                _unslab_cols(g[:, 1280:1536], 64, 2), g[:, 1536:2048]], axis=1)

    def slab_rows(w, n_heads):
        return _slab_cols(w.T, 64, 64, n_heads).T

    def unslab_rows(g, n_heads):
        return _unslab_cols(g.T, 64, n_heads).T

    def odd_in(w):
        return jnp.concatenate([w[:, 0:384], jnp.pad(w[:, 384:416], ((0, 0), (0, 96))), w[:, 416:1440]], axis=1)

    def odd_in_t(g):
        return jnp.concatenate([g[:, 0:416], g[:, 512:1536]], axis=1)

    def uq(w):
        return _slab_cols(w, 96, 96, 8)

    def ukv(w):
        w = w.reshape(128, 8, 128)
        pad = lambda t: jnp.pad(t, ((0, 0), (0, 0), (0, 64))).reshape(128, 1024)
        return pad(w[:, :, :64]), pad(w[:, :, 64:])

    def ukv_t(gk, gv):
        gk = gk.reshape(128, 8, 128)[:, :, :64]
        gv = gv.reshape(128, 8, 128)[:, :, :64]
        return jnp.concatenate([gk, gv], axis=2).reshape(128, 1024)

    tabs_swa = rope_tables(S, 0, SWA_HALF)
    tabs_mq = rope_tables(S, 64, MLA_HALF)
    tabs_mk = rope_tables(S, 0, MLA_HALF)
    row = lambda v: v.reshape(1, -1)

    saved = []
    xcur = x0
    for l in range(4):
        j = l // 2
        if l % 2 == 0:
            w_in = even_in(full["e_w_in"][j])
            w_out = full["e_w_out"][j]
            w_out_a, w_out_b = slab_rows(w_out[0:512], 8), w_out[512:1024]
            w_pool = W["e_w_pool"][j].astype(MXU_DTYPE)
            z, h = mm_nn(f"e_in_{j}", [xcur], [w_in], norm_g=row(W["e_norm_mix"][j]), emit_norm=True, tm=512, tn=2048)
            qkv = swa_prep(z, tabs_swa)
            ya, lse = swa_fwd(qkv, W["e_sink"][j])
            yb = pool_fwd(z, w_pool, row(W["e_pool_scale"][j]))
            x1 = mm_nn(f"e_out_{j}", [ya, yb], [w_out_a, w_out_b], res=xcur)
            mix = (xcur, z, h, qkv, lse, ya, yb, w_in, w_out_a, w_out_b, w_pool)
        else:
            w_in = odd_in(full["o_w_in"][j])
            w_out = full["o_w_out"][j]
            w_out_a, w_out_b = slab_rows(w_out[0:512], 8), w_out[512:1024]
            w_uq = uq(full["o_w_uq"][j])
            w_k, w_v = ukv(full["o_w_ukv"][j])
            g_cq, g_ckv = row(full["o_g_cq"][j]), row(W["o_g_ckv"][j])
            w_gate = jnp.concatenate([_block_diag(W["o_lru_wa"][j, 0]), _block_diag(W["o_lru_wx"][j, 0]),
                                      _block_diag(W["o_lru_wa"][j, 1]), _block_diag(W["o_lru_wx"][j, 1])], axis=1).astype(MXU_DTYPE)
            b_gate = jnp.concatenate([full["o_lru_ba"][j, 0], full["o_lru_bx"][j, 0], full["o_lru_ba"][j, 1],
                                      full["o_lru_bx"][j, 1]]).reshape(1, 2048)
            lam = full["o_lru_lambda"][j].reshape(1, 1024)
            conv_w, conv_b = full["o_conv_w"][j], row(full["o_conv_b"][j])
            z, h = mm_nn(f"o_in_{j}", [xcur], [w_in], norm_g=row(full["o_norm_mix"][j]), emit_norm=True, tm=512, tn=1536)
            q, k, v, nq, nkv = mla_prep(z, g_cq, g_ckv, w_uq, w_k, w_v, tabs_mq, tabs_mk)
            yc, lse = mla_fwd(q, k, mla_vt(v), TQ=2048)
            xc, a0, b0, a1, b1 = lru_pre(z, conv_w, conv_b, w_gate, b_gate, lam)
            h0, h1 = lru_scan(f"lru_scan_fwd_{j}", a0, b0, a1, b1, adjoint=False)
            yd = lru_gate(h0, h1, z)
            x1 = mm_nn(f"o_out_{j}", [yc, yd], [w_out_a, w_out_b], res=xcur)
            mix = (xcur, z, h, q, k, v, nq, nkv, yc, lse, xc, a0, a1, h0, h1, yd, w_in, w_out_a, w_out_b, w_uq, w_k, w_v,
                   g_cq, g_ckv, w_gate, b_gate, lam, conv_w)
        xcur, mlp = _mlp_fwd(l, x1, row(W["norm_mlp"][l]), full["w_mlp1"][l], full["w_mlp2"][l])
        saved.append((mix, mlp))

    loss_row, dx, dg_final = loss_head(xcur, row(W["final_norm"]), target)

    G = {n: [None] * W[n].shape[0] for n in WEIGHTS if n != "final_norm"}
    G["final_norm"] = dg_final.reshape(-1)
    for l in reversed(range(4)):
        j = l // 2
        mix, mlp = saved[l]
        dx, dg, dw1, dw2 = _mlp_bwd(l, mlp, row(W["norm_mlp"][l]), full["w_mlp1"][l], full["w_mlp2"][l], dx)
        G["norm_mlp"][l], G["w_mlp1"][l], G["w_mlp2"][l] = dg.reshape(-1), dw1, dw2
        if l % 2 == 0:
            xin, z, h, qkv, lse, ya, yb, w_in, w_out_a, w_out_b, w_pool = mix
            w_out_cat = jnp.concatenate([w_out_a, w_out_b], axis=0)
            dycat = mm_nt(f"e_out_dx_{j}", [dx], [w_out_cat])
            G["e_w_out"][j] = jnp.concatenate([unslab_rows(mm_tn(f"e_out_dwa_{j}", ya, dx), 8),
                                               mm_tn(f"e_out_dwb_{j}", yb, dx)], axis=0)
            dq, dk, dv, dsink = swa_bwd(qkv, W["e_sink"][j], lse, dycat, tabs_swa)
            du, dwp, dsc = pool_bwd(z, dycat, w_pool, row(W["e_pool_scale"][j]))
            G["e_sink"][j], G["e_w_pool"][j], G["e_pool_scale"][j] = dsink[0, 0:8], dwp, dsc.reshape(-1)
            dh = mm_nt(f"e_in_dx_{j}", [dq, dk, dv, du], [(w_in, 0, 1024), (w_in, 4, 256), (w_in, 5, 256), (w_in, 3, 512)])
            G["e_w_in"][j] = even_in_t(jnp.concatenate(
                [mm_tn(f"e_in_dwq_{j}", h, dq), mm_tn(f"e_in_dwk_{j}", h, dk), mm_tn(f"e_in_dwv_{j}", h, dv),
                 mm_tn(f"e_in_dwu_{j}", h, du)], axis=1))
            dx, dg = rms_bwd(f"e_norm_bwd_{j}", xin, row(W["e_norm_mix"][j]), dh, dx)
            G["e_norm_mix"][j] = dg.reshape(-1)
        else:
            (xin, z, h, q, k, v, nq, nkv, yc, lse, xc, a0, a1, h0, h1, yd, w_in, w_out_a, w_out_b, w_uq, w_k, w_v,
             g_cq, g_ckv, w_gate, b_gate, lam, conv_w) = mix
            w_out_cat = jnp.concatenate([w_out_a, w_out_b], axis=0)
            dycat = mm_nt(f"o_out_dx_{j}", [dx], [w_out_cat])
            G["o_w_out"][j] = jnp.concatenate([unslab_rows(mm_tn(f"o_out_dwa_{j}", yc, dx), 8),
                                               mm_tn(f"o_out_dwb_{j}", yd, dx)], axis=0)
            delta, dob = mla_delta(yc, dycat)
            dqt, dk, dv = mla_bwd(q, k, v, dob, lse, delta)
            dq = mla_dq(dqt)
            dza, dqp, dgq, dgkv = mla_prep_bwd(z, g_cq, g_ckv, w_uq, w_k, w_v, dq, dk, dv, tabs_mq, tabs_mk)
            G["o_g_cq"][j], G["o_g_ckv"][j] = dgq.reshape(-1), dgkv.reshape(-1)
            G["o_w_uq"][j] = _unslab_cols(mm_tn(f"o_uq_dw_{j}", nq, dqp), 96, 8)
            G["o_w_ukv"][j] = ukv_t(mm_tn(f"o_uk_dw_{j}", nkv, dk), mm_tn(f"o_uv_dw_{j}", nkv, dv))
            dxg, dhh = lru_gate_bwd(h0, h1, z, dycat)
            g1, g0 = lru_scan(f"lru_scan_bwd_{j}", a1, dhh, a0, dhh, adjoint=True)
            dxc, dpre, dbias, dlam = lru_bwd_point(xc, h0, h1, g0, g1, w_gate, b_gate, lam)
            dwg = mm_tn(f"o_gate_dw_{j}", xc, dpre)
            G["o_lru_wa"][j] = jnp.stack([_block_diag_t(dwg[:, 0:512]), _block_diag_t(dwg[:, 1024:1536])])
            G["o_lru_wx"][j] = jnp.stack([_block_diag_t(dwg[:, 512:1024]), _block_diag_t(dwg[:, 1536:2048])])
            G["o_lru_ba"][j] = jnp.stack([dbias[0, 0:512], dbias[0, 1024:1536]])
            G["o_lru_bx"][j] = jnp.stack([dbias[0, 512:1024], dbias[0, 1536:2048]])
            G["o_lru_lambda"][j] = dlam.reshape(2, 512)
            dxr, dcw, dcb = conv_bwd(z, dxc, conv_w)
            G["o_conv_w"][j], G["o_conv_b"][j] = dcw, dcb.reshape(-1)
            dh = mm_nt(f"o_in_dx_{j}", [dza, dxr, dxg], [(w_in, 0, 512), (w_in, 1, 512), (w_in, 2, 512)])
            G["o_w_in"][j] = odd_in_t(jnp.concatenate(
                [mm_tn(f"o_in_dwa_{j}", h, dza), mm_tn(f"o_in_dwr_{j}", h, dxr), mm_tn(f"o_in_dwg_{j}", h, dxg)], axis=1))
            dx, dg = rms_bwd(f"o_norm_bwd_{j}", xin, row(full["o_norm_mix"][j]), dh, dx)
            G["o_norm_mix"][j] = dg.reshape(-1)
    grad_x = dx[None]
    G["final_norm"] = [G["final_norm"]]

    sharded = BIG + SMALL
    me = 4 * lax.axis_index("x") + 2 * lax.axis_index("y") + lax.axis_index("c")

    def layers(T, n):
        return [T[n]] if n == "final_norm" else [T[n][l] for l in range(T[n].shape[0])]

    def layer_shapes(names):
        return [a.shape for n in names for a in layers(W, n)]

    def shards_of(n, ax, dev):
        size = W[n].shape[ax]
        return [lax.dynamic_slice_in_dim(g, dev * size, size, axis=ax - 1) for g in G[n]]

    gs = jnp.stack([_pack([p for n, ax in sharded for p in shards_of(n, ax, me ^ r)], F32) for r in range(NDEV)])
    gr = _pack([g for n in REPL for g in G[n]] + [loss_row[0, 0:1]], F32)
    mine, others = pair_add(gs, grad_pair(gs))
    others, gr_all = grad_cross(others, gr)
    names_s = [n for n, _ in sharded]
    outs_s = adamw("adamw_sharded", [mine[None], others],
                   *[_pack([a for n in names_s for a in layers(T, n)], F32) for T in (W, Mo, Vo)])
    outs_r = adamw("adamw_replicated", [gr_all],
                   *[_pack([a for n in REPL for a in layers(T, n)] + [jnp.zeros((1,), F32)], F32) for T in (W, Mo, Vo)])
    res = [dict(), dict(), dict(), dict()]
    for kind in range(4):
        pieces = (_unpack(outs_s[kind], layer_shapes(names_s))
                  + _unpack(outs_r[kind], layer_shapes(REPL) + [(1,)]))
        for n in names_s + list(REPL):
            count = len(layers(W, n))
            got, pieces = pieces[:count], pieces[count:]
            res[kind][n] = got[0] if n == "final_norm" else jnp.stack(got)
        res[kind]["loss"] = pieces[0]
    loss = res[0]["loss"][0]
    return (loss, grad_x, *[res[0][n] for n in WEIGHTS], *[res[1][n] for n in WEIGHTS],
            *[res[2][n] for n in WEIGHTS], *[res[3][n] for n in WEIGHTS])
```

```python
import functools

import jax
import jax.numpy as jnp
from jax import lax
from jax.experimental import pallas as pl
from jax.experimental.pallas import tpu as pltpu

F32 = jnp.float32
MXU_DTYPE = jnp.bfloat16
EPS = 1e-6
ROPE_THETA = 10000.0
NDEV = 8
LANES = 128
VMEM_LIMIT = 48 * 1024 * 1024

D_MODEL = 1024
D_FF = 4096
LRU_C = 8.0
POOL_WINDOWS = (2, 4, 8, 16)
HALO = 8
MLA_SCALE = 96.0 ** -0.5
LOG2E = 1.4426950408889634
MLA_QSCALE = MLA_SCALE * LOG2E

ADAM_LR, ADAM_B1, ADAM_B2, ADAM_EPS, ADAM_WD, ADAM_STEP = 0.001, 0.9, 0.999, 1e-08, 0.01, 10


def _mx(v):
    return v.astype(MXU_DTYPE)


def _call(body, *, name, grid, in_specs, out_specs, out_shape, scratch=()):
    return pl.pallas_call(
        body, name=name, grid=grid, in_specs=in_specs, out_specs=out_specs, out_shape=out_shape,
        scratch_shapes=list(scratch),
        compiler_params=pltpu.CompilerParams(
            dimension_semantics=("arbitrary",) * len(grid), vmem_limit_bytes=VMEM_LIMIT),
    )


def _dot(a, b):
    return jnp.dot(a, b, preferred_element_type=F32)


def _dot_nt(a, b):
    return lax.dot_general(a, b, (((1,), (1,)), ((), ())), preferred_element_type=F32)


def _dot_tn(a, b):
    return lax.dot_general(a, b, (((0,), (0,)), ((), ())), preferred_element_type=F32)


def _rms(x, g):
    r = lax.rsqrt(jnp.mean(x * x, axis=-1, keepdims=True) + EPS)
    return (x * r) * g


def _rms_bwd(x, g, dy):
    r = lax.rsqrt(jnp.mean(x * x, axis=-1, keepdims=True) + EPS)
    xh = x * r
    dyg = dy * g
    dx = r * (dyg - xh * jnp.mean(dyg * xh, axis=-1, keepdims=True))
    return dx, jnp.sum(dy * xh, axis=0, keepdims=True)


def _sigmoid(x):
    return 1.0 / (1.0 + jnp.exp(-x))


def _log1p(e):
    u = 1.0 + e
    d = u - 1.0
    return jnp.where(d == 0.0, e, jnp.log(u) * (e / jnp.where(d == 0.0, 1.0, d)))


def _softplus(x):
    return jnp.maximum(x, 0.0) + _log1p(jnp.exp(-jnp.abs(x)))


def _expm1(x):
    u = jnp.exp(x)
    lu = jnp.log(u)
    safe = jnp.where((lu == 0.0) | (u == 0.0), 1.0, lu)
    return jnp.where(u == 1.0, x, jnp.where(u == 0.0, -1.0, (u - 1.0) * x / safe))


_GELU_K = 0.7978845608028654


def _gelu(x):
    return 0.5 * x * (1.0 + jnp.tanh(_GELU_K * (x + 0.044715 * x * x * x)))


def _gelu_grad(x):
    t = jnp.tanh(_GELU_K * (x + 0.044715 * x * x * x))
    return 0.5 * (1.0 + t) + 0.5 * x * (1.0 - t * t) * _GELU_K * (1.0 + 3.0 * 0.044715 * x * x)


def _rope(x, c, sa, sb, half):
    return x * c + pltpu.roll(x, LANES - half, 1) * sa + pltpu.roll(x, half, 1) * sb


def _rope_t(d, c, sa, sb, half):
    return d * c - pltpu.roll(d, LANES - half, 1) * sa - pltpu.roll(d, half, 1) * sb


def _as_cols(a):
    return a if isinstance(a, tuple) else (a, 0, a.shape[1])


def mm_nn(name, a_list, b_list, *, res=None, act=None, norm_g=None, emit_norm=False,
          out_dtype=F32, tm=512, tn=1024):
    a_list = [_as_cols(a) for a in a_list]
    M, N = a_list[0][0].shape[0], b_list[0].shape[1]
    tm, tn = min(tm, M), min(tn, N)
    na = len(a_list)

    def body(*refs):
        a_refs, b_refs = refs[:na], refs[na:2 * na]
        k = 2 * na
        g_ref = res_ref = hn_ref = None
        if norm_g is not None:
            g_ref = refs[k]
            k += 1
        if res is not None:
            res_ref = refs[k]
            k += 1
        o_ref = refs[k]
        if emit_norm:
            hn_ref = refs[k + 1]
        acc = None
        for a_ref, b_ref in zip(a_refs, b_refs):
            a = a_ref[...]
            if g_ref is not None:
                a = _rms(a.astype(F32), g_ref[...])
                if hn_ref is not None:
                    hn_ref[...] = a.astype(hn_ref.dtype)
            if act == "relu2":
                a = jnp.maximum(a, 0.0)
                a = a * a
            d = _dot(_mx(a), _mx(b_ref[...]))
            acc = d if acc is None else acc + d
        if res_ref is not None:
            acc = acc + res_ref[...]
        o_ref[...] = acc.astype(o_ref.dtype)

    in_specs = [pl.BlockSpec((tm, w), functools.partial(lambda i, j, cb: (i, cb), cb=cb)) for (_, cb, w) in a_list]
    in_specs += [pl.BlockSpec((b.shape[0], tn), lambda i, j: (0, j)) for b in b_list]
    args = [a for (a, _, _) in a_list] + list(b_list)
    if norm_g is not None:
        in_specs.append(pl.BlockSpec((1, norm_g.shape[1]), lambda i, j: (0, 0)))
        args.append(norm_g)
    if res is not None:
        in_specs.append(pl.BlockSpec((tm, tn), lambda i, j: (i, j)))
        args.append(res)
    out_specs = [pl.BlockSpec((tm, tn), lambda i, j: (i, j))]
    out_shape = [jax.ShapeDtypeStruct((M, N), out_dtype)]
    if emit_norm:
        K = a_list[0][2]
        out_specs.append(pl.BlockSpec((tm, K), lambda i, j: (i, 0)))
        out_shape.append(jax.ShapeDtypeStruct((M, K), MXU_DTYPE))
    out = _call(body, name=name, grid=(M // tm, N // tn), in_specs=in_specs, out_specs=out_specs,
                out_shape=out_shape)(*args)
    return out if emit_norm else out[0]


def mm_nt(name, a_list, b_list, *, relu2_of=None, rms=None, out_dtype=F32, tm=512, tn=1536):
    a_list = [_as_cols(a) for a in a_list]
    b_list = [_as_cols(b) for b in b_list]
    M, N = a_list[0][0].shape[0], b_list[0][0].shape[0]
    tm, tn = min(tm, M), min(tn, N)
    na = len(a_list)
    assert rms is None or (tn == N and relu2_of is None)

    def body(*refs):
        a_refs, b_refs = refs[:na], refs[na:2 * na]
        u_ref = refs[2 * na] if relu2_of is not None else None
        acc = None
        for a_ref, b_ref in zip(a_refs, b_refs):
            d = _dot_nt(_mx(a_ref[...]), _mx(b_ref[...]))
            acc = d if acc is None else acc + d
        if u_ref is not None:
            acc = acc * (2.0 * jnp.maximum(u_ref[...], 0.0))
        if rms is None:
            refs[-1][...] = acc.astype(refs[-1].dtype)
            return
        x_ref, g_ref, dres_ref, o_ref, dg_ref = refs[2 * na:]

        @pl.when(pl.program_id(0) == 0)
        def _():
            dg_ref[...] = jnp.zeros_like(dg_ref)

        dx, dg = _rms_bwd(x_ref[...], g_ref[...], acc)
        o_ref[...] = dres_ref[...] + dx
        dg_ref[...] += dg

    in_specs = [pl.BlockSpec((tm, w), functools.partial(lambda i, j, cb: (i, cb), cb=cb)) for (_, cb, w) in a_list]
    in_specs += [pl.BlockSpec((tn, w), functools.partial(lambda i, j, cb: (j, cb), cb=cb)) for (_, cb, w) in b_list]
    args = [a for (a, _, _) in a_list] + [b for (b, _, _) in b_list]
    tile = pl.BlockSpec((tm, tn), lambda i, j: (i, j))
    if relu2_of is not None:
        in_specs.append(tile)
        args.append(relu2_of)
    if rms is not None:
        vec = pl.BlockSpec((1, N), lambda i, j: (0, 0))
        return _call(body, name=name, grid=(M // tm, 1), in_specs=in_specs + [tile, vec, tile],
                     out_specs=[tile, vec],
                     out_shape=[jax.ShapeDtypeStruct((M, N), F32), jax.ShapeDtypeStruct((1, N), F32)])(*args, *rms)
    return _call(body, name=name, grid=(M // tm, N // tn), in_specs=in_specs, out_specs=tile,
                 out_shape=jax.ShapeDtypeStruct((M, N), out_dtype))(*args)


def mm_tn(name, a, b, *, act=None, tm=1024, tn=1024, tk=1024):
    a, acb, Ma = _as_cols(a)
    b, bcb, Nb = _as_cols(b)
    S = a.shape[0]
    tm, tn, tk = min(tm, Ma), min(tn, Nb), min(tk, S)
    a0, b0 = acb * (Ma // tm), bcb * (Nb // tn)

    def body(a_ref, b_ref, o_ref):
        @pl.when(pl.program_id(2) == 0)
        def _():
            o_ref[...] = jnp.zeros_like(o_ref)

        av = a_ref[...]
        if act == "relu2":
            av = jnp.maximum(av, 0.0)
            av = av * av
        o_ref[...] += _dot_tn(_mx(av), _mx(b_ref[...]))

    return _call(body, name=name, grid=(Ma // tm, Nb // tn, S // tk),
                 in_specs=[pl.BlockSpec((tk, tm), lambda i, j, k: (k, a0 + i)),
                           pl.BlockSpec((tk, tn), lambda i, j, k: (k, b0 + j))],
                 out_specs=pl.BlockSpec((tm, tn), lambda i, j, k: (i, j)),
                 out_shape=jax.ShapeDtypeStruct((Ma, Nb), F32))(a, b)


def rms_bwd(name, x, g, dh, dres, T=512):
    S, D = x.shape
    T = min(T, S)

    def body(x_ref, g_ref, dh_ref, dres_ref, dx_ref, dg_ref):
        @pl.when(pl.program_id(0) == 0)
        def _():
            dg_ref[...] = jnp.zeros_like(dg_ref)

        dx, dg = _rms_bwd(x_ref[...], g_ref[...], dh_ref[...])
        dx_ref[...] = dres_ref[...] + dx
        dg_ref[...] += dg

    row = pl.BlockSpec((T, D), lambda i: (i, 0))
    vec = pl.BlockSpec((1, D), lambda i: (0, 0))
    return _call(body, name=name, grid=(S // T,), in_specs=[row, vec, row, row], out_specs=[row, vec],
                 out_shape=[jax.ShapeDtypeStruct((S, D), F32), jax.ShapeDtypeStruct((1, D), F32)])(x, g, dh, dres)


def loss_head(x, g, target, T=512):
    S, D = x.shape
    T = min(T, S)

    def body(x_ref, g_ref, t_ref, loss_ref, dx_ref, dg_ref):
        @pl.when(pl.program_id(0) == 0)
        def _():
            dg_ref[...] = jnp.zeros_like(dg_ref)
            loss_ref[...] = jnp.zeros_like(loss_ref)

        x = x_ref[...]
        err = _rms(x, g_ref[...]) - t_ref[...]
        loss_ref[...] += 0.5 * jnp.sum(jnp.sum(err * err, axis=-1, keepdims=True) / D, axis=0, keepdims=True)
        dx, dg = _rms_bwd(x, g_ref[...], err / D)
        dx_ref[...] = dx
        dg_ref[...] += dg

    row = pl.BlockSpec((T, D), lambda i: (i, 0))
    vec = pl.BlockSpec((1, D), lambda i: (0, 0))
    return _call(body, name="loss_head", grid=(S // T,), in_specs=[row, vec, row],
                 out_specs=[pl.BlockSpec((1, LANES), lambda i: (0, 0)), row, vec],
                 out_shape=[jax.ShapeDtypeStruct((1, LANES), F32), jax.ShapeDtypeStruct((S, D), F32),
                            jax.ShapeDtypeStruct((1, D), F32)])(x, g, target)


def rope_tables(S, lo, half):
    inv = ROPE_THETA ** (-jnp.arange(half, dtype=F32) / half)
    ang = jnp.arange(S, dtype=F32)[:, None] * inv[None, :]
    cos, sin = jnp.cos(ang), jnp.sin(ang)
    one = lambda n: jnp.ones((S, n), F32)
    zero = lambda n: jnp.zeros((S, n), F32)
    hi = LANES - lo - 2 * half
    c = jnp.concatenate([one(lo), cos, cos, one(hi)], axis=1)
    sa = jnp.concatenate([zero(lo), -sin, zero(half), zero(hi)], axis=1)
    sb = jnp.concatenate([zero(lo), zero(half), sin, zero(hi)], axis=1)
    return c, sa, sb


SWA_BLOCK = 128
SWA_HALF = 32


def swa_prep(z, tabs, T=512):
    S = z.shape[0]
    T = min(T, S)

    def body(z_ref, c_ref, sa_ref, sb_ref, o_ref):
        c, sa, sb = c_ref[...], sa_ref[...], sb_ref[...]
        for s in range(10):
            sl = slice(s * LANES, (s + 1) * LANES)
            y = _rope(z_ref[:, sl], c, sa, sb, SWA_HALF)
            if s < 8:
                y = y * 0.125
            o_ref[:, sl] = y.astype(o_ref.dtype)
        o_ref[:, 1280:1536] = z_ref[:, 1280:1536].astype(o_ref.dtype)

    tab = pl.BlockSpec((T, LANES), lambda i: (i, 0))
    return _call(body, name="swa_prep", grid=(S // T,),
                 in_specs=[pl.BlockSpec((T, 1536), lambda i: (i, 0)), tab, tab, tab],
                 out_specs=pl.BlockSpec((T, 1536), lambda i: (i, 0)),
                 out_shape=jax.ShapeDtypeStruct((S, 1536), MXU_DTYPE))(z, *tabs)


SWA_GROUP = 4


def _swa_valid(n, S):
    B = SWA_BLOCK
    i = lax.broadcasted_iota(jnp.int32, (SWA_GROUP * B, 3 * B), 0) & (B - 1)
    j = lax.broadcasted_iota(jnp.int32, (SWA_GROUP * B, 3 * B), 1)
    kpos = j + (n - 1) * B
    return (jnp.abs(j - B - i) <= B) & (kpos >= 0) & (kpos < S)


def _swa_sink(sink_ref, hk):
    B = SWA_BLOCK
    row = lax.broadcasted_iota(jnp.int32, (SWA_GROUP * B, 1), 0)
    sk = jnp.full((SWA_GROUP * B, 1), sink_ref[hk * SWA_GROUP], F32)
    for g in range(1, SWA_GROUP):
        sk = jnp.where(row >= g * B, sink_ref[hk * SWA_GROUP + g], sk)
    return sk


def swa_fwd(qkv, sink):
    S = qkv.shape[0]
    B = SWA_BLOCK
    nb = S // B

    def body(sink_ref, q_ref, kp_ref, kc_ref, kn_ref, vp_ref, vc_ref, vn_ref, o_ref, st_ref):
        n = pl.program_id(0)
        valid = _swa_valid(n, S)
        lane = lax.broadcasted_iota(jnp.int32, (B, LANES), 1)
        st = jnp.zeros((B, LANES), F32)
        for hk in range(2):
            sl = slice(hk * LANES, (hk + 1) * LANES)
            k3 = jnp.concatenate([kp_ref[:, sl], kc_ref[:, sl], kn_ref[:, sl]], axis=0)
            v3 = jnp.concatenate([vp_ref[:, sl], vc_ref[:, sl], vn_ref[:, sl]], axis=0)
            heads = [hk * SWA_GROUP + g for g in range(SWA_GROUP)]
            q4 = jnp.concatenate([q_ref[:, h * LANES:(h + 1) * LANES] for h in heads], axis=0)
            s = jnp.where(valid, _dot_nt(q4, k3), -jnp.inf)
            sk = _swa_sink(sink_ref, hk)
            m = jnp.maximum(jnp.max(s, axis=-1, keepdims=True), sk)
            p = jnp.exp(s - m)
            den = jnp.sum(p, axis=-1, keepdims=True) + jnp.exp(sk - m)
            p = p / den
            o4 = _dot(_mx(p), v3)
            lse4 = m + jnp.log(den)
            for g, h in enumerate(heads):
                rows = slice(g * B, (g + 1) * B)
                o_ref[:, h * LANES:(h + 1) * LANES] = o4[rows].astype(o_ref.dtype)
                st = jnp.where(lane == h, lse4[rows], st)
        st_ref[...] = st

    kv = lambda cb, d: pl.BlockSpec((B, 2 * LANES), lambda n: (jnp.clip(n + d, 0, nb - 1), cb))
    return _call(body, name="swa_fwd", grid=(nb,),
                 in_specs=[pl.BlockSpec(memory_space=pltpu.SMEM),
                           pl.BlockSpec((B, 1024), lambda n: (n, 0)),
                           kv(4, -1), kv(4, 0), kv(4, 1), kv(5, -1), kv(5, 0), kv(5, 1)],
                 out_specs=[pl.BlockSpec((B, 1024), lambda n: (n, 0)), pl.BlockSpec((B, LANES), lambda n: (n, 0))],
                 out_shape=[jax.ShapeDtypeStruct((S, 1024), MXU_DTYPE), jax.ShapeDtypeStruct((S, LANES), F32)],
                 )(sink, qkv, qkv, qkv, qkv, qkv, qkv, qkv)


def swa_bwd(qkv, sink, lse, dycat, tabs):
    S = qkv.shape[0]
    B = SWA_BLOCK
    nb = S // B

    def body(sink_ref, q_ref, kp_ref, kc_ref, kn_ref, vp_ref, vc_ref, vn_ref, do_ref, st_ref,
             cq_ref, saq_ref, sbq_ref, ck_ref, sak_ref, sbk_ref,
             dq_ref, dk_ref, dv_ref, dsink_ref, dk_acc, dv_acc):
        n = pl.program_id(0)

        @pl.when(n == 0)
        def _():
            dk_acc[...] = jnp.zeros_like(dk_acc)
            dv_acc[...] = jnp.zeros_like(dv_acc)
            dsink_ref[...] = jnp.zeros_like(dsink_ref)

        @pl.when(n < nb)
        def _():
            valid = _swa_valid(n, S)
            lane = lax.broadcasted_iota(jnp.int32, (B, LANES), 1)
            lane1 = lax.broadcasted_iota(jnp.int32, (1, LANES), 1)
            st = st_ref[...]
            cq, saq, sbq = cq_ref[...], saq_ref[...], sbq_ref[...]
            dsink = jnp.zeros((1, LANES), F32)
            for hk in range(2):
                sl = slice(hk * LANES, (hk + 1) * LANES)
                k3 = jnp.concatenate([kp_ref[:, sl], kc_ref[:, sl], kn_ref[:, sl]], axis=0)
                v3 = jnp.concatenate([vp_ref[:, sl], vc_ref[:, sl], vn_ref[:, sl]], axis=0)
                heads = [hk * SWA_GROUP + g for g in range(SWA_GROUP)]
                q4 = jnp.concatenate([q_ref[:, h * LANES:(h + 1) * LANES] for h in heads], axis=0)
                do4 = jnp.concatenate([_mx(do_ref[:, h * LANES:(h + 1) * LANES]) for h in heads], axis=0)
                lse4 = jnp.concatenate([jnp.sum(jnp.where(lane == h, st, 0.0), axis=-1, keepdims=True) for h in heads],
                                       axis=0)
                p = jnp.where(valid, jnp.exp(_dot_nt(q4, k3) - lse4), 0.0)
                dp = _dot_nt(do4, v3)
                dsum = jnp.sum(p * dp, axis=-1, keepdims=True)
                ds = _mx(p * (dp - dsum))
                dq4 = _dot(ds, k3) * 0.125
                dk_acc[:, sl] += _dot_tn(ds, q4)
                dv_acc[:, sl] += _dot_tn(_mx(p), do4)
                dsk4 = jnp.exp(_swa_sink(sink_ref, hk) - lse4) * dsum
                for g, h in enumerate(heads):
                    rows = slice(g * B, (g + 1) * B)
                    dq_ref[:, h * LANES:(h + 1) * LANES] = _rope_t(dq4[rows], cq, saq, sbq, SWA_HALF)
                    dsink = jnp.where(lane1 == h, -jnp.sum(dsk4[rows], axis=0, keepdims=True), dsink)
            dsink_ref[...] += dsink

        ck, sak, sbk = ck_ref[...], sak_ref[...], sbk_ref[...]
        for hk in range(2):
            sl = slice(hk * LANES, (hk + 1) * LANES)
            dk_ref[:, sl] = _rope_t(dk_acc[0:B, sl], ck, sak, sbk, SWA_HALF)
        dv_ref[...] = dv_acc[0:B, :]
        for acc in (dk_acc, dv_acc):
            acc[0:B, :] = acc[B:2 * B, :]
            acc[B:2 * B, :] = acc[2 * B:3 * B, :]
            acc[2 * B:3 * B, :] = jnp.zeros((B, 2 * LANES), F32)

    qn = lambda n: jnp.minimum(n, nb - 1)
    kv = lambda cb, d: pl.BlockSpec((B, 2 * LANES), lambda n: (jnp.clip(qn(n) + d, 0, nb - 1), cb))
    qrow = lambda w: pl.BlockSpec((B, w), lambda n: (qn(n), 0))
    krow = lambda w: pl.BlockSpec((B, w), lambda n: (jnp.maximum(n - 1, 0), 0))
    return _call(body, name="swa_bwd", grid=(nb + 1,),
                 in_specs=[pl.BlockSpec(memory_space=pltpu.SMEM), qrow(1024),
                           kv(4, -1), kv(4, 0), kv(4, 1), kv(5, -1), kv(5, 0), kv(5, 1),
                           qrow(1024), qrow(LANES),
                           qrow(LANES), qrow(LANES), qrow(LANES), krow(LANES), krow(LANES), krow(LANES)],
                 out_specs=[qrow(1024), krow(2 * LANES), krow(2 * LANES), pl.BlockSpec((1, LANES), lambda n: (0, 0))],
                 out_shape=[jax.ShapeDtypeStruct((S, 1024), F32), jax.ShapeDtypeStruct((S, 2 * LANES), F32),
                            jax.ShapeDtypeStruct((S, 2 * LANES), F32), jax.ShapeDtypeStruct((1, LANES), F32)],
                 scratch=[pltpu.VMEM((3 * B, 2 * LANES), F32), pltpu.VMEM((3 * B, 2 * LANES), F32)],
                 )(sink, qkv, qkv, qkv, qkv, qkv, qkv, qkv, dycat, lse, *tabs, *tabs)


def _halo_specs(T, S, w, cb):
    r = T // HALO
    last = S // HALO - 1
    return [pl.BlockSpec((HALO, w), lambda i: (jnp.maximum(i * r - 1, 0), cb)),
            pl.BlockSpec((T, w), lambda i: (i, cb)),
            pl.BlockSpec((HALO, w), lambda i: (jnp.minimum((i + 1) * r, last), cb))]


def _fill_ext(ext, prev_ref, cur_ref, next_ref, i, nt, T):
    ext[0:HALO, :] = jnp.where(i > 0, prev_ref[...], 0.0).astype(F32)
    ext[HALO:HALO + T, :] = cur_ref[...].astype(F32)
    ext[HALO + T:2 * HALO + T, :] = jnp.where(i < nt - 1, next_ref[...], 0.0).astype(F32)


def _pool_cnt(t, half, S):
    return (jnp.clip(t + half, 0, S) - jnp.clip(t - half, 0, S)).astype(F32)


def pool_fwd(z, w_pool, scale, T=256):
    S = z.shape[0]
    T = min(T, S)
    nt = S // T

    def body(up_ref, uc_ref, un_ref, w_ref, sc_ref, o_ref, ext):
        i = pl.program_id(0)
        _fill_ext(ext, up_ref, uc_ref, un_ref, i, nt, T)
        t = i * T + lax.broadcasted_iota(jnp.int32, (T, 1), 0)
        for g, win in enumerate(POOL_WINDOWS):
            half = win // 2
            sl = slice(g * LANES, (g + 1) * LANES)
            acc = ext[pl.ds(HALO - half, T), sl]
            for off in range(-half + 1, half):
                acc = acc + ext[pl.ds(HALO + off, T), sl]
            d = acc / _pool_cnt(t, half, S) - ext[pl.ds(HALO, T), sl]
            o_ref[:, sl] = (_dot(_mx(d), w_ref[g]) * sc_ref[:, sl]).astype(o_ref.dtype)

    return _call(body, name="pool_fwd", grid=(nt,),
                 in_specs=_halo_specs(T, S, 512, 3) + [pl.BlockSpec((4, LANES, LANES), lambda i: (0, 0, 0)),
                                                      pl.BlockSpec((1, 512), lambda i: (0, 0))],
                 out_specs=pl.BlockSpec((T, 512), lambda i: (i, 0)),
                 out_shape=jax.ShapeDtypeStruct((S, 512), MXU_DTYPE),
                 scratch=[pltpu.VMEM((T + 2 * HALO, 512), F32)])(z, z, z, w_pool, scale)


def pool_bwd(z, dycat, w_pool, scale, T=256):
    S = z.shape[0]
    T = min(T, S)
    nt = S // T
    TE = T + 2 * HALO

    def body(up_ref, uc_ref, un_ref, yp_ref, yc_ref, yn_ref, w_ref, sc_ref, du_ref, dw_ref, dsc_ref, extu, exty, exte):
        i = pl.program_id(0)

        @pl.when(i == 0)
        def _():
            dw_ref[...] = jnp.zeros_like(dw_ref)
            dsc_ref[...] = jnp.zeros_like(dsc_ref)

        _fill_ext(extu, up_ref, uc_ref, un_ref, i, nt, T)
        _fill_ext(exty, yp_ref, yc_ref, yn_ref, i, nt, T)
        t = i * T + lax.broadcasted_iota(jnp.int32, (T, 1), 0)
        te = i * T - HALO + lax.broadcasted_iota(jnp.int32, (TE, 1), 0)
        for g, win in enumerate(POOL_WINDOWS):
            half = win // 2
            sl = slice(g * LANES, (g + 1) * LANES)
            w = w_ref[g]
            acc = extu[pl.ds(HALO - half, T), sl]
            for off in range(-half + 1, half):
                acc = acc + extu[pl.ds(HALO + off, T), sl]
            d = _mx(acc / _pool_cnt(t, half, S) - extu[pl.ds(HALO, T), sl])
            dy = exty[pl.ds(HALO, T), sl]
            dsc_ref[:, sl] += jnp.sum(dy * _dot(d, w), axis=0, keepdims=True)
            dw_ref[g] += _dot_tn(d, _mx(dy * sc_ref[:, sl]))
            dd = _dot_nt(_mx(exty[:, sl] * sc_ref[:, sl]), w)
            exte[:, sl] = dd / jnp.maximum(_pool_cnt(te, half, S), 1.0)
            acc = exte[pl.ds(HALO - half + 1, T), sl]
            for off in range(-half + 2, half + 1):
                acc = acc + exte[pl.ds(HALO + off, T), sl]
            du_ref[:, sl] = acc - dd[HALO:HALO + T, :]

    return _call(body, name="pool_bwd", grid=(nt,),
                 in_specs=_halo_specs(T, S, 512, 3) + _halo_specs(T, S, 512, 2)
                 + [pl.BlockSpec((4, LANES, LANES), lambda i: (0, 0, 0)), pl.BlockSpec((1, 512), lambda i: (0, 0))],
                 out_specs=[pl.BlockSpec((T, 512), lambda i: (i, 0)), pl.BlockSpec((4, LANES, LANES), lambda i: (0, 0, 0)),
                            pl.BlockSpec((1, 512), lambda i: (0, 0))],
                 out_shape=[jax.ShapeDtypeStruct((S, 512), F32), jax.ShapeDtypeStruct((4, LANES, LANES), F32),
                            jax.ShapeDtypeStruct((1, 512), F32)],
                 scratch=[pltpu.VMEM((TE, 512), F32)] * 3)(z, z, z, dycat, dycat, dycat, w_pool, scale)


MLA_HALF = 16
MLA_V_ONE = 64


def mla_prep(z, g_cq, g_ckv, w_uq, w_k, w_v, tabs_q, tabs_k, T=256):
    S = z.shape[0]
    T = min(T, S)

    def body(z_ref, gq_ref, gkv_ref, wq_ref, wk_ref, wv_ref, cq_ref, saq_ref, sbq_ref, ck_ref, sak_ref, sbk_ref,
             q_ref, k_ref, v_ref, nq_ref, nkv_ref):
        nq = _mx(_rms(z_ref[:, 0:256], gq_ref[...]))
        nkv = _mx(_rms(z_ref[:, 256:384], gkv_ref[...]))
        nq_ref[...] = nq
        nkv_ref[...] = nkv
        q = _dot(nq, wq_ref[...])
        kn = _dot(nkv, wk_ref[...])
        lane = lax.broadcasted_iota(jnp.int32, (T, 1024), 1)
        v_ref[...] = jnp.where(lane % LANES == MLA_V_ONE, 1.0, _dot(nkv, wv_ref[...])).astype(v_ref.dtype)
        kr = pltpu.roll(_rope(z_ref[:, 384:512], ck_ref[...], sak_ref[...], sbk_ref[...], MLA_HALF), 64, 1)
        cq, saq, sbq = cq_ref[...], saq_ref[...], sbq_ref[...]
        for h in range(8):
            hs = slice(h * LANES, (h + 1) * LANES)
            q_ref[:, hs] = (_rope(q[:, hs], cq, saq, sbq, MLA_HALF) * MLA_QSCALE).astype(q_ref.dtype)
            k_ref[:, hs] = (kn[:, hs] + kr).astype(k_ref.dtype)

    tab = pl.BlockSpec((T, LANES), lambda i: (i, 0))
    full = lambda a: pl.BlockSpec(a.shape, lambda i: (0, 0))
    row = lambda w: pl.BlockSpec((T, w), lambda i: (i, 0))
    sd = lambda w: jax.ShapeDtypeStruct((S, w), MXU_DTYPE)
    return _call(body, name="mla_prep", grid=(S // T,),
                 in_specs=[row(512), full(g_cq), full(g_ckv), full(w_uq), full(w_k), full(w_v)] + [tab] * 6,
                 out_specs=[row(1024), row(1024), row(1024), row(256), row(128)],
                 out_shape=[sd(1024), sd(1024), sd(1024), sd(256), sd(128)],
                 )(z, g_cq, g_ckv, w_uq, w_k, w_v, *tabs_q, *tabs_k)


def _col_to_row(c):
    return jnp.transpose(jnp.broadcast_to(c, (c.shape[0], LANES)))[0:1, :]


def mla_vt(v, TK=512):
    S = v.shape[0]
    TK = min(TK, S // 2)
    return v.reshape(S // TK, TK, 8, LANES).transpose(2, 0, 3, 1)


def mla_dq(dqt):
    _, nq, _, TQ = dqt.shape
    return dqt.transpose(1, 3, 0, 2).reshape(nq * TQ, 1024)


def mla_fwd(q, k, vt, TQ=512):
    S = q.shape[0]
    nk, TK = vt.shape[1], vt.shape[3]
    TQ = min(TQ, S)
    QB = min(256, TQ)
    assert nk % 2 == 0
    unroll = 8 if nk % 8 == 0 else 4 if nk % 4 == 0 else 2

    def body(q_ref, k_ref, vt_ref, o_ref, lse_ref, m_s, acc_s, s_a, s_b):
        m_s[...] = jnp.full_like(m_s, -jnp.inf)
        acc_s[...] = jnp.zeros_like(acc_s)
        q = q_ref[...]

        def scores(c):
            return _dot_nt(k_ref[pl.ds(pl.multiple_of(c * TK, TK), TK), :], q)

        def softmax_pv(s_ref, c):
            vt_c = vt_ref[0, c]
            for b in range(TQ // QB):
                cols = slice(b * QB, (b + 1) * QB)
                s = s_ref[:, cols]
                m_old = m_s[:, cols]
                m_new = jnp.maximum(m_old, jnp.max(s, axis=0, keepdims=True))
                alpha = jnp.exp2(m_old[0:1, :] - m_new[0:1, :])
                p = jnp.exp2(s - m_new[0:1, :])
                acc_s[:, cols] = alpha * acc_s[:, cols] + _dot(vt_c, _mx(p))
                m_s[:, cols] = m_new

        s_a[...] = scores(0)
        bufs = (s_a, s_b)

        def group(jj, carry):
            c = unroll * jj
            for u in range(unroll):
                nxt = c + u + 1 if u < unroll - 1 else jnp.minimum(c + unroll, nk - 1)
                bufs[(u + 1) % 2][...] = scores(nxt)
                softmax_pv(bufs[u % 2], c + u)
            return carry

        lax.fori_loop(0, nk // unroll, group, 0)
        acc = acc_s[...]
        den = acc[MLA_V_ONE:MLA_V_ONE + 1, :]
        sub = lax.broadcasted_iota(jnp.int32, (LANES, TQ), 0)
        o_ref[...] = jnp.transpose(jnp.where(sub < MLA_V_ONE, acc / den, 0.0))
        lse_ref[0] = m_s[0:1, :] + jnp.log(den) * LOG2E

    qs = pl.BlockSpec((TQ, LANES), lambda h, i: (i, h))
    return _call(body, name="mla_fwd", grid=(8, S // TQ),
                 in_specs=[qs, pl.BlockSpec((S, LANES), lambda h, i: (0, h)),
                           pl.BlockSpec((1, nk, LANES, TK), lambda h, i: (h, 0, 0, 0))],
                 out_specs=[qs, pl.BlockSpec((1, 1, TQ), lambda h, i: (h, 0, i))],
                 out_shape=[jax.ShapeDtypeStruct((S, 1024), F32), jax.ShapeDtypeStruct((8, 1, S), F32)],
                 scratch=[pltpu.VMEM((8, TQ), F32), pltpu.VMEM((LANES, TQ), F32),
                          pltpu.VMEM((TK, TQ), F32), pltpu.VMEM((TK, TQ), F32)],
                 )(q, k, vt)


def mla_delta(o, dycat, TQ=512):
    S = o.shape[0]
    TQ = min(TQ, S)

    def body(o_ref, do_ref, d_ref, dob_ref):
        for h in range(8):
            hs = slice(h * LANES, (h + 1) * LANES)
            do = do_ref[:, hs]
            d_ref[h] = _col_to_row(jnp.sum(o_ref[:, hs] * do, axis=-1, keepdims=True))
            dob_ref[:, hs] = do.astype(dob_ref.dtype)

    qs = pl.BlockSpec((TQ, 1024), lambda i: (i, 0))
    return _call(body, name="mla_delta", grid=(S // TQ,), in_specs=[qs, qs],
                 out_specs=[pl.BlockSpec((8, 1, TQ), lambda i: (0, 0, i)), qs],
                 out_shape=[jax.ShapeDtypeStruct((8, 1, S), F32), jax.ShapeDtypeStruct((S, 1024), MXU_DTYPE)])(o, dycat)


def mla_bwd(q, k, v, do, lse, delta, TQ=512, TK=512):
    S = q.shape[0]
    TQ, TK = min(TQ, S // 2), min(TK, S)
    nq, nk = S // TQ, S // TK
    assert nq % 2 == 0
    unroll = 8 if nq % 8 == 0 else 4 if nq % 4 == 0 else 2
    lse = lse.reshape(8, nq, 1, TQ)
    delta = delta.reshape(8, nq, 1, TQ)

    def body(q_ref, do_ref, k_ref, v_ref, lse_ref, d_ref, dqt_ref, dk_ref, dv_ref, dk_s, dv_s, s_a, p_a, s_b, p_b):
        @pl.when(pl.program_id(1) == 0)
        def _():
            dqt_ref[...] = jnp.zeros_like(dqt_ref)

        dk_s[...] = jnp.zeros_like(dk_s)
        dv_s[...] = jnp.zeros_like(dv_s)
        kk, vv = k_ref[...], v_ref[...]
        kt = _mx(jnp.transpose(kk.astype(F32)))

        def rows(c):
            return pl.ds(pl.multiple_of(c * TQ, TQ), TQ)

        def products(c, s_ref, p_ref):
            s_ref[...] = _dot_nt(kk, q_ref[rows(c), :])
            p_ref[...] = _dot_nt(vv, do_ref[rows(c), :])

        def consume(c, s_ref, p_ref):
            qc, doc = q_ref[rows(c), :], do_ref[rows(c), :]
            pt = jnp.exp2(s_ref[...] - lse_ref[0, c])
            dv_s[...] += _dot(_mx(pt), doc)
            ds = _mx(pt * (p_ref[...] - d_ref[0, c]))
            dk_s[...] += _dot(ds, qc)
            dqt_ref[0, c] += _dot(kt, ds)

        products(0, s_a, p_a)
        bufs = ((s_a, p_a), (s_b, p_b))

        def group(jj, carry):
            c = unroll * jj
            for u in range(unroll):
                nxt = c + u + 1 if u < unroll - 1 else jnp.minimum(c + unroll, nq - 1)
                products(nxt, *bufs[(u + 1) % 2])
                consume(c + u, *bufs[u % 2])
            return carry

        lax.fori_loop(0, nq // unroll, group, 0)
        dk_ref[...] = dk_s[...] * (1.0 / LOG2E)
        dv_ref[...] = dv_s[...]

    full = pl.BlockSpec((S, LANES), lambda h, j: (0, h))
    ks = pl.BlockSpec((TK, LANES), lambda h, j: (j, h))
    st = pl.BlockSpec((1, nq, 1, TQ), lambda h, j: (h, 0, 0, 0))
    sd = jax.ShapeDtypeStruct((S, 1024), F32)
    return _call(body, name="mla_bwd", grid=(8, nk), in_specs=[full, full, ks, ks, st, st],
                 out_specs=[pl.BlockSpec((1, nq, LANES, TQ), lambda h, j: (h, 0, 0, 0)), ks, ks],
                 out_shape=[jax.ShapeDtypeStruct((8, nq, LANES, TQ), F32), sd, sd],
                 scratch=[pltpu.VMEM((TK, LANES), F32), pltpu.VMEM((TK, LANES), F32)] + [pltpu.VMEM((TK, TQ), F32)] * 4,
                 )(q, do, k, v, lse, delta)


def mla_prep_bwd(z, g_cq, g_ckv, w_uq, w_k, w_v, dq, dk, dv, tabs_q, tabs_k, T=256):
    S = z.shape[0]
    T = min(T, S)

    def body(z_ref, gq_ref, gkv_ref, wq_ref, wk_ref, wv_ref, dq_ref, dk_ref, dv_ref,
             cq_ref, saq_ref, sbq_ref, ck_ref, sak_ref, sbk_ref, dz_ref, dqp_ref, dgq_ref, dgkv_ref):
        @pl.when(pl.program_id(0) == 0)
        def _():
            dgq_ref[...] = jnp.zeros_like(dgq_ref)
            dgkv_ref[...] = jnp.zeros_like(dgkv_ref)

        cq, saq, sbq = cq_ref[...], saq_ref[...], sbq_ref[...]
        dkr = jnp.zeros((T, LANES), F32)
        for h in range(8):
            hs = slice(h * LANES, (h + 1) * LANES)
            dqp_ref[:, hs] = _rope_t(dq_ref[:, hs] * MLA_SCALE, cq, saq, sbq, MLA_HALF).astype(dqp_ref.dtype)
            dkr = dkr + dk_ref[:, hs]
        lane = lax.broadcasted_iota(jnp.int32, (T, LANES), 1)
        dkr = jnp.where(lane < 2 * MLA_HALF, pltpu.roll(dkr, 64, 1), 0.0)
        dz_ref[:, 384:512] = _rope_t(dkr, ck_ref[...], sak_ref[...], sbk_ref[...], MLA_HALF)
        dnq = _dot_nt(dqp_ref[...], wq_ref[...])
        dx, dg = _rms_bwd(z_ref[:, 0:256], gq_ref[...], dnq)
        dz_ref[:, 0:256] = dx
        dgq_ref[...] += dg
        dnkv = _dot_nt(_mx(dk_ref[...]), wk_ref[...]) + _dot_nt(_mx(dv_ref[...]), wv_ref[...])
        dx, dg = _rms_bwd(z_ref[:, 256:384], gkv_ref[...], dnkv)
        dz_ref[:, 256:384] = dx
        dgkv_ref[...] += dg

    tab = pl.BlockSpec((T, LANES), lambda i: (i, 0))
    full = lambda a: pl.BlockSpec(a.shape, lambda i: (0, 0))
    row = lambda w: pl.BlockSpec((T, w), lambda i: (i, 0))
    return _call(body, name="mla_prep_bwd", grid=(S // T,),
                 in_specs=[row(512), full(g_cq), full(g_ckv), full(w_uq), full(w_k), full(w_v),
                           row(1024), row(1024), row(1024)] + [tab] * 6,
                 out_specs=[row(512), row(1024), full(g_cq), full(g_ckv)],
                 out_shape=[jax.ShapeDtypeStruct((S, 512), F32), jax.ShapeDtypeStruct((S, 1024), MXU_DTYPE),
                            jax.ShapeDtypeStruct(g_cq.shape, F32), jax.ShapeDtypeStruct(g_ckv.shape, F32)],
                 )(z, g_cq, g_ckv, w_uq, w_k, w_v, dq, dk, dv, *tabs_q, *tabs_k)


def _lru_gates(xc, w_ref, bias_ref, lam_ref):
    pre = _dot(_mx(xc), w_ref[...]) + bias_ref[...]
    out = []
    for d in range(2):
        r = _sigmoid(pre[:, d * 1024:d * 1024 + 512])
        ig = _sigmoid(pre[:, d * 1024 + 512:(d + 1) * 1024])
        log_a = -LRU_C * r * _softplus(-lam_ref[:, d * 512:(d + 1) * 512])
        out.append((r, ig, jnp.exp(log_a), jnp.sqrt(-_expm1(2.0 * log_a))))
    return out


def lru_pre(z, conv_w, conv_b, w_gate, b_gate, lam, T=256):
    S = z.shape[0]
    T = min(T, S)
    nt = S // T

    def body(xp_ref, xcur_ref, xn_ref, cw_ref, cb_ref, w_ref, bias_ref, lam_ref, xc_ref, a0_ref, b0_ref, a1_ref, b1_ref, ext):
        i = pl.program_id(0)
        _fill_ext(ext, xp_ref, xcur_ref, xn_ref, i, nt, T)
        xc = cb_ref[...] + cw_ref[0:1, :] * ext[pl.ds(HALO - 2, T), :]
        for j in range(1, 4):
            xc = xc + cw_ref[j:j + 1, :] * ext[pl.ds(HALO - 2 + j, T), :]
        xc_ref[...] = xc
        (_, i0, a0, m0), (_, i1, a1, m1) = _lru_gates(xc, w_ref, bias_ref, lam_ref)
        a0_ref[...] = a0
        b0_ref[...] = m0 * (i0 * xc)
        a1_ref[...] = a1
        b1_ref[...] = m1 * (i1 * xc)

    full = lambda a: pl.BlockSpec(a.shape, lambda i: (0, 0))
    row = pl.BlockSpec((T, 512), lambda i: (i, 0))
    sd = jax.ShapeDtypeStruct((S, 512), F32)
    return _call(body, name="lru_pre", grid=(nt,),
                 in_specs=_halo_specs(T, S, 512, 1) + [full(conv_w), full(conv_b), full(w_gate), full(b_gate), full(lam)],
                 out_specs=[row] * 5, out_shape=[sd] * 5,
                 scratch=[pltpu.VMEM((T + 2 * HALO, 512), F32)])(z, z, z, conv_w, conv_b, w_gate, b_gate, lam)


def lru_scan(name, af, bf, ar, br, *, adjoint, T=512):
    S, W = af.shape
    T = min(T, S)
    nt = S // T
    nc = T // 8

    def body(af_ref, bf_ref, ar_ref, br_ref, hf_ref, hr_ref, cf, cr):
        @pl.when(pl.program_id(0) == 0)
        def _():
            cf[...] = jnp.zeros_like(cf)
            cr[...] = jnp.zeros_like(cr)

        row = lax.broadcasted_iota(jnp.int32, (8, W), 0)

        def step(a, b, carry):
            if adjoint:
                val = b + carry
                return val, a * val
            val = a * carry + b
            return val, val

        def chunk(c, carry):
            hf, hr = carry
            of = pl.multiple_of(c * 8, 8)
            orv = pl.multiple_of((nc - 1 - c) * 8, 8)
            a8, b8 = af_ref[pl.ds(of, 8), :], bf_ref[pl.ds(of, 8), :]
            ra8, rb8 = ar_ref[pl.ds(orv, 8), :], br_ref[pl.ds(orv, 8), :]
            outf = jnp.zeros((8, W), F32)
            outr = jnp.zeros((8, W), F32)
            for k in range(8):
                val, hf = step(a8[k:k + 1, :], b8[k:k + 1, :], hf)
                outf = jnp.where(row == k, val, outf)
                kr = 7 - k
                val, hr = step(ra8[kr:kr + 1, :], rb8[kr:kr + 1, :], hr)
                outr = jnp.where(row == kr, val, outr)
            hf_ref[pl.ds(of, 8), :] = outf
            hr_ref[pl.ds(orv, 8), :] = outr
            return hf, hr

        hf, hr = lax.fori_loop(0, nc, chunk, (cf[0:1, :], cr[0:1, :]))
        cf[0:1, :] = hf
        cr[0:1, :] = hr

    fw = pl.BlockSpec((T, W), lambda i: (i, 0))
    rv = pl.BlockSpec((T, W), lambda i: (nt - 1 - i, 0))
    sd = jax.ShapeDtypeStruct((S, W), F32)
    return _call(body, name=name, grid=(nt,), in_specs=[fw, fw, rv, rv], out_specs=[fw, rv], out_shape=[sd, sd],
                 scratch=[pltpu.VMEM((8, W), F32), pltpu.VMEM((8, W), F32)])(af, bf, ar, br)


def lru_gate(h0, h1, z, T=512):
    S = z.shape[0]
    T = min(T, S)

    def body(h0_ref, h1_ref, xg_ref, y_ref):
        y_ref[...] = ((h0_ref[...] + h1_ref[...]) * _gelu(xg_ref[...])).astype(y_ref.dtype)

    row = pl.BlockSpec((T, 512), lambda i: (i, 0))
    return _call(body, name="lru_gate", grid=(S // T,), in_specs=[row, row, pl.BlockSpec((T, 512), lambda i: (i, 2))],
                 out_specs=row, out_shape=jax.ShapeDtypeStruct((S, 512), MXU_DTYPE))(h0, h1, z)


def lru_gate_bwd(h0, h1, z, dycat, T=512):
    S = z.shape[0]
    T = min(T, S)

    def body(h0_ref, h1_ref, xg_ref, dy_ref, dxg_ref, dh_ref):
        xg, dy = xg_ref[...], dy_ref[...]
        dxg_ref[...] = dy * (h0_ref[...] + h1_ref[...]) * _gelu_grad(xg)
        dh_ref[...] = dy * _gelu(xg)

    row = pl.BlockSpec((T, 512), lambda i: (i, 0))
    col2 = pl.BlockSpec((T, 512), lambda i: (i, 2))
    sd = jax.ShapeDtypeStruct((S, 512), F32)
    return _call(body, name="lru_gate_bwd", grid=(S // T,), in_specs=[row, row, col2, col2],
                 out_specs=[row, row], out_shape=[sd, sd])(h0, h1, z, dycat)


def lru_bwd_point(xc, h0, h1, g0, g1, w_gate, b_gate, lam, T=256):
    S = xc.shape[0]
    T = min(T, S)
    nt = S // T

    def body(xc_ref, h0p_ref, h0_ref, h0n_ref, h1p_ref, h1_ref, h1n_ref, g0_ref, g1_ref, w_ref, bias_ref, lam_ref,
             dxc_ref, dpre_ref, dbias_ref, dlam_ref, ext0, ext1):
        i = pl.program_id(0)

        @pl.when(i == 0)
        def _():
            dbias_ref[...] = jnp.zeros_like(dbias_ref)
            dlam_ref[...] = jnp.zeros_like(dlam_ref)

        _fill_ext(ext0, h0p_ref, h0_ref, h0n_ref, i, nt, T)
        _fill_ext(ext1, h1p_ref, h1_ref, h1n_ref, i, nt, T)
        xc = xc_ref[...]
        gates = _lru_gates(xc, w_ref, bias_ref, lam_ref)
        hshift = (ext0[pl.ds(HALO - 1, T), :], ext1[pl.ds(HALO + 1, T), :])
        gs = (g0_ref[...], g1_ref[...])
        dxc = jnp.zeros((T, 512), F32)
        for d in range(2):
            r, ig, a, mult = gates[d]
            db = gs[d]
            da = db * hshift[d]
            dmult = db * (ig * xc)
            di = db * (mult * xc)
            dxc = dxc + db * (mult * ig)
            dloga = da * a - dmult * (a * a / mult)
            lam_d = lam_ref[:, d * 512:(d + 1) * 512]
            dr = dloga * (-LRU_C * _softplus(-lam_d))
            dsp = jnp.sum(dloga * (-LRU_C * r), axis=0, keepdims=True)
            dlam_ref[:, d * 512:(d + 1) * 512] += dsp * (-_sigmoid(-lam_d))
            dpre_ref[:, d * 1024:d * 1024 + 512] = (dr * (r * (1.0 - r))).astype(dpre_ref.dtype)
            dpre_ref[:, d * 1024 + 512:(d + 1) * 1024] = (di * (ig * (1.0 - ig))).astype(dpre_ref.dtype)
            dbias_ref[:, d * 1024:d * 1024 + 512] += jnp.sum(dr * (r * (1.0 - r)), axis=0, keepdims=True)
            dbias_ref[:, d * 1024 + 512:(d + 1) * 1024] += jnp.sum(di * (ig * (1.0 - ig)), axis=0, keepdims=True)
        dxc_ref[...] = dxc + _dot_nt(dpre_ref[...], w_ref[...])

    full = lambda a: pl.BlockSpec(a.shape, lambda i: (0, 0))
    row = pl.BlockSpec((T, 512), lambda i: (i, 0))
    return _call(body, name="lru_bwd_point", grid=(nt,),
                 in_specs=[row] + _halo_specs(T, S, 512, 0) + _halo_specs(T, S, 512, 0) + [row, row, full(w_gate), full(b_gate), full(lam)],
                 out_specs=[row, pl.BlockSpec((T, 2048), lambda i: (i, 0)), full(b_gate), full(lam)],
                 out_shape=[jax.ShapeDtypeStruct((S, 512), F32), jax.ShapeDtypeStruct((S, 2048), MXU_DTYPE),
                            jax.ShapeDtypeStruct(b_gate.shape, F32), jax.ShapeDtypeStruct(lam.shape, F32)],
                 scratch=[pltpu.VMEM((T + 2 * HALO, 512), F32)] * 2,
                 )(xc, h0, h0, h0, h1, h1, h1, g0, g1, w_gate, b_gate, lam)


def conv_bwd(z, dxc, conv_w, T=512):
    S = z.shape[0]
    T = min(T, S)
    nt = S // T

    def body(xp_ref, xcur_ref, xn_ref, dp_ref, dcur_ref, dn_ref, cw_ref, dx_ref, dw_ref, db_ref, extx, extd):
        i = pl.program_id(0)

        @pl.when(i == 0)
        def _():
            dw_ref[...] = jnp.zeros_like(dw_ref)
            db_ref[...] = jnp.zeros_like(db_ref)

        _fill_ext(extx, xp_ref, xcur_ref, xn_ref, i, nt, T)
        _fill_ext(extd, dp_ref, dcur_ref, dn_ref, i, nt, T)
        d = extd[pl.ds(HALO, T), :]
        dx = cw_ref[0:1, :] * extd[pl.ds(HALO + 2, T), :]
        for j in range(1, 4):
            dx = dx + cw_ref[j:j + 1, :] * extd[pl.ds(HALO + 2 - j, T), :]
        dx_ref[...] = dx
        for j in range(4):
            dw_ref[j:j + 1, :] += jnp.sum(d * extx[pl.ds(HALO - 2 + j, T), :], axis=0, keepdims=True)
        db_ref[...] += jnp.sum(d, axis=0, keepdims=True)

    full = lambda a: pl.BlockSpec(a.shape, lambda i: (0, 0))
    row = pl.BlockSpec((T, 512), lambda i: (i, 0))
    return _call(body, name="conv_bwd", grid=(nt,),
                 in_specs=_halo_specs(T, S, 512, 1) + _halo_specs(T, S, 512, 0) + [full(conv_w)],
                 out_specs=[row, full(conv_w), pl.BlockSpec((1, 512), lambda i: (0, 0))],
                 out_shape=[jax.ShapeDtypeStruct((S, 512), F32), jax.ShapeDtypeStruct(conv_w.shape, F32),
                            jax.ShapeDtypeStruct((1, 512), F32)],
                 scratch=[pltpu.VMEM((T + 2 * HALO, 512), F32)] * 2)(z, z, z, dxc, dxc, dxc, conv_w)


MESH_ID = pl.DeviceIdType.MESH
ANY = pl.BlockSpec(memory_space=pl.ANY)


def _place():
    x, y, c = lax.axis_index("x"), lax.axis_index("y"), lax.axis_index("c")
    chips = [(1 - x, y), (x, 1 - y), (1 - x, 1 - y)]
    return x, y, c, chips


def _slot(px, py, pc):
    return 4 * px + 2 * py + pc


def all_gather(arrays):
    n = len(arrays)

    def body(*refs):
        ins, outs = refs[:n], refs[n:2 * n]
        send, recv, loc = refs[2 * n:]
        x, y, c, chips = _place()
        me, sibling = (x, y, c), (x, y, 1 - c)

        def copy(a, k, block, to, src=None):
            slot = outs[a].at[_slot(*block)]
            return pltpu.make_async_remote_copy(src_ref=slot if src is None else src, dst_ref=slot,
                                                send_sem=send.at[a * 7 + k], recv_sem=recv.at[a * 7 + k],
                                                device_id=to, device_id_type=MESH_ID)

        local = [pltpu.make_async_copy(ins[a], outs[a].at[_slot(*me)], loc.at[a]) for a in range(n)]
        for cp in local:
            cp.start()
        first = []
        for a in range(n):
            first.append(copy(a, 0, me, sibling, src=ins[a]))
            first += [copy(a, 1 + j, me, (*chip, c), src=ins[a]) for j, chip in enumerate(chips)]
        for cp in first:
            cp.start()
        passed = []
        for a in range(n):
            for j, chip in enumerate(chips):
                copy(a, 1 + j, (*chip, c), me).wait_recv()
                cp = copy(a, 4 + j, (*chip, c), sibling)
                cp.start()
                passed.append(cp)
        for a in range(n):
            copy(a, 0, sibling, me).wait_recv()
            for j, chip in enumerate(chips):
                copy(a, 4 + j, (*chip, 1 - c), me).wait_recv()
        for cp in first + passed:
            cp.wait_send()
        for cp in local:
            cp.wait()

    return pl.pallas_call(
        body, name="all_gather", in_specs=[ANY] * n, out_specs=[ANY] * n,
        out_shape=[jax.ShapeDtypeStruct((NDEV,) + a.shape, a.dtype) for a in arrays],
        scratch_shapes=[pltpu.SemaphoreType.DMA((n * 7,)), pltpu.SemaphoreType.DMA((n * 7,)),
                        pltpu.SemaphoreType.DMA((n,))],
    )(*arrays)


def grad_pair(gs):
    def body(gs_ref, o_ref, send, recv):
        x, y, c, _ = _place()
        cps = [pltpu.make_async_remote_copy(src_ref=gs_ref.at[2 * q + 1], dst_ref=o_ref.at[q], send_sem=send.at[q],
                                            recv_sem=recv.at[q], device_id=(x, y, 1 - c), device_id_type=MESH_ID)
               for q in range(4)]
        for cp in cps:
            cp.start()
        for cp in cps:
            cp.wait_recv()
        for cp in cps:
            cp.wait_send()

    return pl.pallas_call(
        body, name="grad_pair", in_specs=[ANY], out_specs=ANY,
        out_shape=jax.ShapeDtypeStruct((4,) + gs.shape[1:], gs.dtype),
        scratch_shapes=[pltpu.SemaphoreType.DMA((4,)), pltpu.SemaphoreType.DMA((4,))],
    )(gs)


def pair_add(gs, got, T=128):
    _, R, C = gs.shape
    T = min(T, R)

    def body(g0, g1, g2, g3, got_ref, own_ref, out_ref):
        own_ref[...] = g0[0] + got_ref[0]
        for q, g in enumerate((g1, g2, g3)):
            out_ref[q] = (g[0] + got_ref[q + 1]).astype(out_ref.dtype)

    even = lambda q: pl.BlockSpec((1, T, C), lambda i: (2 * q, i, 0))
    return _call(body, name="pair_add", grid=(R // T,),
                 in_specs=[even(0), even(1), even(2), even(3), pl.BlockSpec((4, T, C), lambda i: (0, i, 0))],
                 out_specs=[pl.BlockSpec((T, C), lambda i: (i, 0)), pl.BlockSpec((3, T, C), lambda i: (0, i, 0))],
                 out_shape=[jax.ShapeDtypeStruct((R, C), F32), jax.ShapeDtypeStruct((3, R, C), MXU_DTYPE)],
                 )(gs, gs, gs, gs, got)


def grad_cross(part, gr):
    def body(p_ref, gr_ref, o_ref, or_ref, send, recv, loc):
        x, y, c, chips = _place()
        peers = [(x, y, 1 - c)] + [(*chip, c) for chip in chips] + [(*chip, 1 - c) for chip in chips]
        local = pltpu.make_async_copy(gr_ref, or_ref.at[_slot(x, y, c)], loc.at[0])
        local.start()
        by_offset = [(x, 1 - y), (1 - x, y), (1 - x, 1 - y)]
        cps = [pltpu.make_async_remote_copy(src_ref=p_ref.at[j], dst_ref=o_ref.at[j], send_sem=send.at[j],
                                            recv_sem=recv.at[j], device_id=(*chip, c), device_id_type=MESH_ID)
               for j, chip in enumerate(by_offset)]
        cps += [pltpu.make_async_remote_copy(src_ref=gr_ref, dst_ref=or_ref.at[_slot(x, y, c)], send_sem=send.at[3 + k],
                                             recv_sem=recv.at[3 + k], device_id=peer, device_id_type=MESH_ID)
                for k, peer in enumerate(peers)]
        for cp in cps:
            cp.start()
        for cp in cps[:3]:
            cp.wait_recv()
        for k, peer in enumerate(peers):
            pltpu.make_async_remote_copy(src_ref=gr_ref, dst_ref=or_ref.at[_slot(*peer)], send_sem=send.at[3 + k],
                                         recv_sem=recv.at[3 + k], device_id=peer, device_id_type=MESH_ID).wait_recv()
        for cp in cps:
            cp.wait_send()
        local.wait()

    return pl.pallas_call(
        body, name="grad_cross", in_specs=[ANY, ANY], out_specs=[ANY, ANY],
        out_shape=[jax.ShapeDtypeStruct(part.shape, part.dtype), jax.ShapeDtypeStruct((NDEV,) + gr.shape, gr.dtype)],
        scratch_shapes=[pltpu.SemaphoreType.DMA((10,)), pltpu.SemaphoreType.DMA((10,)), pltpu.SemaphoreType.DMA((1,))],
    )(part, gr)


def adamw(name, gparts, w, m, v, T=128):
    R, C = w.shape
    T = min(T, R)
    ng = len(gparts)

    def body(*refs):
        g_refs = refs[:ng]
        w_ref, m_ref, v_ref, go_ref, d_ref, mo_ref, vo_ref = refs[ng:]
        g = None
        for g_ref in g_refs:
            for k in range(g_ref.shape[0]):
                t = g_ref[k].astype(F32)
                g = t if g is None else g + t
        mn = ADAM_B1 * m_ref[...] + (1.0 - ADAM_B1) * g
        vn = ADAM_B2 * v_ref[...] + (1.0 - ADAM_B2) * (g * g)
        m_hat = mn / (1.0 - ADAM_B1 ** ADAM_STEP)
        v_hat = vn / (1.0 - ADAM_B2 ** ADAM_STEP)
        go_ref[...] = g
        d_ref[...] = -ADAM_LR * (m_hat / (jnp.sqrt(v_hat) + ADAM_EPS) + ADAM_WD * w_ref[...])
        mo_ref[...] = mn
        vo_ref[...] = vn

    row = pl.BlockSpec((T, C), lambda i: (i, 0))
    sd = jax.ShapeDtypeStruct((R, C), F32)
    return _call(body, name=name, grid=(R // T,),
                 in_specs=[pl.BlockSpec((g.shape[0], T, C), lambda i: (0, i, 0)) for g in gparts] + [row, row, row],
                 out_specs=[row] * 4, out_shape=[sd] * 4)(*gparts, w, m, v)


PACK_W = 1024
PACK_TILE = 16
PACK_ROWS = 128


def _entry_rows(shape):
    n = 1
    for s in shape:
        n *= s
    return n, -(-n // (PACK_TILE * PACK_W)) * PACK_TILE


def _pack(arrays, dtype):
    mats, rows = [], 0
    for a in arrays:
        n, r = _entry_rows(a.shape)
        flat = a.astype(dtype).reshape(-1)
        if r * PACK_W != n:
            flat = jnp.pad(flat, (0, r * PACK_W - n))
        mats.append(flat.reshape(r, PACK_W))
        rows += r
    total = -(-rows // PACK_ROWS) * PACK_ROWS
    if total > rows:
        mats.append(jnp.zeros((total - rows, PACK_W), dtype))
    return jnp.concatenate(mats, axis=0)


def _unpack(buf, shapes, lead=()):
    out, off = [], 0
    for sh in shapes:
        n, r = _entry_rows(sh)
        piece = buf[..., off:off + r, :].reshape(lead + (r * PACK_W,))
        out.append(piece[..., :n].reshape(lead + tuple(sh)))
        off += r
    return out


def _unshard(parts, axis):
    return jnp.concatenate([parts[d] for d in range(NDEV)], axis=axis)


def _slab_cols(w, head, used, n_heads, slab=LANES):
    lead = w.shape[:-1]
    w = w.reshape(lead + (n_heads, head))[..., :used]
    w = jnp.pad(w, [(0, 0)] * len(lead) + [(0, 0), (0, slab - used)])
    return w.reshape(lead + (n_heads * slab,))


def _unslab_cols(g, used, n_heads, slab=LANES):
    lead = g.shape[:-1]
    return g.reshape(lead + (n_heads, slab))[..., :used].reshape(lead + (n_heads * used,))


def _block_diag(w):
    eye = jnp.eye(8, dtype=w.dtype)
    return jnp.einsum("nij,nm->nimj", w, eye).reshape(512, 512)


def _block_diag_t(g):
    g = g.reshape(8, 64, 8, 64)
    return jnp.stack([g[n, :, n, :] for n in range(8)])


BIG = (("e_w_in", 2), ("e_w_out", 1), ("o_w_in", 2), ("o_w_uq", 2), ("o_w_ukv", 2), ("o_w_out", 1),
       ("w_mlp1", 2), ("w_mlp2", 1))
SMALL = (("o_norm_mix", 1), ("o_g_cq", 1), ("o_conv_w", 2), ("o_conv_b", 1), ("o_lru_ba", 2), ("o_lru_bx", 2),
         ("o_lru_lambda", 2))
REPL = ("e_norm_mix", "e_sink", "e_w_pool", "e_pool_scale", "o_g_ckv", "o_lru_wa", "o_lru_wx", "norm_mlp",
        "final_norm")
WEIGHTS = ("e_norm_mix", "e_w_in", "e_sink", "e_w_pool", "e_pool_scale", "e_w_out", "o_norm_mix", "o_w_in", "o_g_cq",
           "o_w_uq", "o_g_ckv", "o_w_ukv", "o_conv_w", "o_conv_b", "o_lru_wa", "o_lru_ba", "o_lru_wx", "o_lru_bx",
           "o_lru_lambda", "o_w_out", "norm_mlp", "w_mlp1", "w_mlp2", "final_norm")


def _mlp_fwd(l, x, g, w1, w2):
    u1, hn = mm_nn(f"mlp1_{l}", [x], [w1], norm_g=g, emit_norm=True, out_dtype=MXU_DTYPE, tm=1024, tn=1024)
    x2 = mm_nn(f"mlp2_{l}", [u1], [w2], act="relu2", res=x, tm=512, tn=1024)
    return x2, (x, u1, hn)


def _mlp_bwd(l, saved, g, w1, w2, dx2):
    x, u1, hn = saved
    du1 = mm_nt(f"mlp2_dx_{l}", [dx2], [w2], relu2_of=u1, out_dtype=MXU_DTYPE, tm=1024, tn=1024)
    dw2 = mm_tn(f"mlp2_dw_{l}", u1, dx2, act="relu2", tm=1024, tn=1024)
    dx, dg = mm_nt(f"mlp1_dx_{l}", [du1], [w1], rms=(x, g, dx2), tm=512, tn=1024)
    dw1 = mm_tn(f"mlp1_dw_{l}", hn, du1, tm=1024, tn=1024)
    return dx, dg, dw1, dw2


def kernel(x, e_norm_mix, e_w_in, e_sink, e_w_pool, e_pool_scale, e_w_out, o_norm_mix, o_w_in, o_g_cq, o_w_uq, o_g_ckv, o_w_ukv, o_conv_w, o_conv_b, o_lru_wa, o_lru_ba, o_lru_wx, o_lru_bx, o_lru_lambda, o_w_out, norm_mlp, w_mlp1, w_mlp2, final_norm, loss_target, m_e_norm_mix, m_e_w_in, m_e_sink, m_e_w_pool, m_e_pool_scale, m_e_w_out, m_o_norm_mix, m_o_w_in, m_o_g_cq, m_o_w_uq, m_o_g_ckv, m_o_w_ukv, m_o_conv_w, m_o_conv_b, m_o_lru_wa, m_o_lru_ba, m_o_lru_wx, m_o_lru_bx, m_o_lru_lambda, m_o_w_out, m_norm_mlp, m_w_mlp1, m_w_mlp2, m_final_norm, v_e_norm_mix, v_e_w_in, v_e_sink, v_e_w_pool, v_e_pool_scale, v_e_w_out, v_o_norm_mix, v_o_w_in, v_o_g_cq, v_o_w_uq, v_o_g_ckv, v_o_w_ukv, v_o_conv_w, v_o_conv_b, v_o_lru_wa, v_o_lru_ba, v_o_lru_wx, v_o_lru_bx, v_o_lru_lambda, v_o_w_out, v_norm_mlp, v_w_mlp1, v_w_mlp2, v_final_norm):
    W = dict(e_norm_mix=e_norm_mix, e_w_in=e_w_in, e_sink=e_sink, e_w_pool=e_w_pool, e_pool_scale=e_pool_scale, e_w_out=e_w_out, o_norm_mix=o_norm_mix, o_w_in=o_w_in, o_g_cq=o_g_cq, o_w_uq=o_w_uq, o_g_ckv=o_g_ckv, o_w_ukv=o_w_ukv, o_conv_w=o_conv_w, o_conv_b=o_conv_b, o_lru_wa=o_lru_wa, o_lru_ba=o_lru_ba, o_lru_wx=o_lru_wx, o_lru_bx=o_lru_bx, o_lru_lambda=o_lru_lambda, o_w_out=o_w_out, norm_mlp=norm_mlp, w_mlp1=w_mlp1, w_mlp2=w_mlp2, final_norm=final_norm)
    Mo = dict(e_norm_mix=m_e_norm_mix, e_w_in=m_e_w_in, e_sink=m_e_sink, e_w_pool=m_e_w_pool, e_pool_scale=m_e_pool_scale, e_w_out=m_e_w_out, o_norm_mix=m_o_norm_mix, o_w_in=m_o_w_in, o_g_cq=m_o_g_cq, o_w_uq=m_o_w_uq, o_g_ckv=m_o_g_ckv, o_w_ukv=m_o_w_ukv, o_conv_w=m_o_conv_w, o_conv_b=m_o_conv_b, o_lru_wa=m_o_lru_wa, o_lru_ba=m_o_lru_ba, o_lru_wx=m_o_lru_wx, o_lru_bx=m_o_lru_bx, o_lru_lambda=m_o_lru_lambda, o_w_out=m_o_w_out, norm_mlp=m_norm_mlp, w_mlp1=m_w_mlp1, w_mlp2=m_w_mlp2, final_norm=m_final_norm)
    Vo = dict(e_norm_mix=v_e_norm_mix, e_w_in=v_e_w_in, e_sink=v_e_sink, e_w_pool=v_e_w_pool, e_pool_scale=v_e_pool_scale, e_w_out=v_e_w_out, o_norm_mix=v_o_norm_mix, o_w_in=v_o_w_in, o_g_cq=v_o_g_cq, o_w_uq=v_o_w_uq, o_g_ckv=v_o_g_ckv, o_w_ukv=v_o_w_ukv, o_conv_w=v_o_conv_w, o_conv_b=v_o_conv_b, o_lru_wa=v_o_lru_wa, o_lru_ba=v_o_lru_ba, o_lru_wx=v_o_lru_wx, o_lru_bx=v_o_lru_bx, o_lru_lambda=v_o_lru_lambda, o_w_out=v_o_w_out, norm_mlp=v_norm_mlp, w_mlp1=v_w_mlp1, w_mlp2=v_w_mlp2, final_norm=v_final_norm)

    S = x.shape[1]
    x0 = x[0]
    target = loss_target[0]

    big_g, small_g = all_gather([_pack([W[n] for n, _ in BIG], MXU_DTYPE), _pack([W[n] for n, _ in SMALL], F32)])
    full = {}
    for (n, ax), parts in zip(BIG, _unpack(big_g, [W[n].shape for n, _ in BIG], (NDEV,))):
        full[n] = _unshard(parts, ax)
    for (n, ax), parts in zip(SMALL, _unpack(small_g, [W[n].shape for n, _ in SMALL], (NDEV,))):
        full[n] = _unshard(parts, ax)

    def even_in(w):
        return jnp.concatenate([_slab_cols(w[:, 0:512], 64, 64, 8), _slab_cols(w[:, 512:640], 64, 64, 2),
                                _slab_cols(w[:, 640:768], 64, 64, 2), w[:, 768:1280]], axis=1)

    def even_in_t(g):
        return jnp.concatenate([_unslab_cols(g[:, 0:1024], 64, 8), _unslab_cols(g[:, 1024:1280], 64, 2),
                                _unslab_cols(g[:, 1280:1536], 64, 2), g[:, 1536:2048]], axis=1)

    def slab_rows(w, n_heads):
        return _slab_cols(w.T, 64, 64, n_heads).T

    def unslab_rows(g, n_heads):
        return _unslab_cols(g.T, 64, n_heads).T

    def odd_in(w):
        return jnp.concatenate([w[:, 0:384], jnp.pad(w[:, 384:416], ((0, 0), (0, 96))), w[:, 416:1440]], axis=1)

    def odd_in_t(g):
        return jnp.concatenate([g[:, 0:416], g[:, 512:1536]], axis=1)

    def uq(w):
        return _slab_cols(w, 96, 96, 8)

    def ukv(w):
        w = w.reshape(128, 8, 128)
        pad = lambda t: jnp.pad(t, ((0, 0), (0, 0), (0, 64))).reshape(128, 1024)
        return pad(w[:, :, :64]), pad(w[:, :, 64:])

    def ukv_t(gk, gv):
        gk = gk.reshape(128, 8, 128)[:, :, :64]
        gv = gv.reshape(128, 8, 128)[:, :, :64]
        return jnp.concatenate([gk, gv], axis=2).reshape(128, 1024)

    tabs_swa = rope_tables(S, 0, SWA_HALF)
    tabs_mq = rope_tables(S, 64, MLA_HALF)
    tabs_mk = rope_tables(S, 0, MLA_HALF)
    row = lambda v: v.reshape(1, -1)

    saved = []
    xcur = x0
    for l in range(4):
        j = l // 2
        if l % 2 == 0:
            w_in = even_in(full["e_w_in"][j])
            w_out = full["e_w_out"][j]
            w_out_a, w_out_b = slab_rows(w_out[0:512], 8), w_out[512:1024]
            w_pool = W["e_w_pool"][j].astype(MXU_DTYPE)
            z, h = mm_nn(f"e_in_{j}", [xcur], [w_in], norm_g=row(W["e_norm_mix"][j]), emit_norm=True, tm=512, tn=2048)
            qkv = swa_prep(z, tabs_swa)
            ya, lse = swa_fwd(qkv, W["e_sink"][j])
            yb = pool_fwd(z, w_pool, row(W["e_pool_scale"][j]))
            x1 = mm_nn(f"e_out_{j}", [ya, yb], [w_out_a, w_out_b], res=xcur)
            mix = (xcur, z, h, qkv, lse, ya, yb, w_in, w_out_a, w_out_b, w_pool)
        else:
            w_in = odd_in(full["o_w_in"][j])
            w_out = full["o_w_out"][j]
            w_out_a, w_out_b = slab_rows(w_out[0:512], 8), w_out[512:1024]
            w_uq = uq(full["o_w_uq"][j])
            w_k, w_v = ukv(full["o_w_ukv"][j])
            g_cq, g_ckv = row(full["o_g_cq"][j]), row(W["o_g_ckv"][j])
            w_gate = jnp.concatenate([_block_diag(W["o_lru_wa"][j, 0]), _block_diag(W["o_lru_wx"][j, 0]),
                                      _block_diag(W["o_lru_wa"][j, 1]), _block_diag(W["o_lru_wx"][j, 1])], axis=1).astype(MXU_DTYPE)
            b_gate = jnp.concatenate([full["o_lru_ba"][j, 0], full["o_lru_bx"][j, 0], full["o_lru_ba"][j, 1],
                                      full["o_lru_bx"][j, 1]]).reshape(1, 2048)
            lam = full["o_lru_lambda"][j].reshape(1, 1024)
            conv_w, conv_b = full["o_conv_w"][j], row(full["o_conv_b"][j])
            z, h = mm_nn(f"o_in_{j}", [xcur], [w_in], norm_g=row(full["o_norm_mix"][j]), emit_norm=True, tm=512, tn=1536)
            q, k, v, nq, nkv = mla_prep(z, g_cq, g_ckv, w_uq, w_k, w_v, tabs_mq, tabs_mk)
            yc, lse = mla_fwd(q, k, mla_vt(v), TQ=1024)
            xc, a0, b0, a1, b1 = lru_pre(z, conv_w, conv_b, w_gate, b_gate, lam)
            h0, h1 = lru_scan(f"lru_scan_fwd_{j}", a0, b0, a1, b1, adjoint=False)
            yd = lru_gate(h0, h1, z)
            x1 = mm_nn(f"o_out_{j}", [yc, yd], [w_out_a, w_out_b], res=xcur)
            mix = (xcur, z, h, q, k, v, nq, nkv, yc, lse, xc, a0, a1, h0, h1, yd, w_in, w_out_a, w_out_b, w_uq, w_k, w_v,
                   g_cq, g_ckv, w_gate, b_gate, lam, conv_w)
        xcur, mlp = _mlp_fwd(l, x1, row(W["norm_mlp"][l]), full["w_mlp1"][l], full["w_mlp2"][l])
        saved.append((mix, mlp))

    loss_row, dx, dg_final = loss_head(xcur, row(W["final_norm"]), target)

    G = {n: [None] * W[n].shape[0] for n in WEIGHTS if n != "final_norm"}
    G["final_norm"] = dg_final.reshape(-1)
    for l in reversed(range(4)):
        j = l // 2
        mix, mlp = saved[l]
        dx, dg, dw1, dw2 = _mlp_bwd(l, mlp, row(W["norm_mlp"][l]), full["w_mlp1"][l], full["w_mlp2"][l], dx)
        G["norm_mlp"][l], G["w_mlp1"][l], G["w_mlp2"][l] = dg.reshape(-1), dw1, dw2
        if l % 2 == 0:
            xin, z, h, qkv, lse, ya, yb, w_in, w_out_a, w_out_b, w_pool = mix
            w_out_cat = jnp.concatenate([w_out_a, w_out_b], axis=0)
            dycat = mm_nt(f"e_out_dx_{j}", [dx], [w_out_cat])
            G["e_w_out"][j] = jnp.concatenate([unslab_rows(mm_tn(f"e_out_dwa_{j}", ya, dx), 8),
                                               mm_tn(f"e_out_dwb_{j}", yb, dx)], axis=0)
            dq, dk, dv, dsink = swa_bwd(qkv, W["e_sink"][j], lse, dycat, tabs_swa)
            du, dwp, dsc = pool_bwd(z, dycat, w_pool, row(W["e_pool_scale"][j]))
            G["e_sink"][j], G["e_w_pool"][j], G["e_pool_scale"][j] = dsink[0, 0:8], dwp, dsc.reshape(-1)
            dxin, dg = mm_nt(f"e_in_dx_{j}", [dq, dk, dv, du], [(w_in, 0, 1024), (w_in, 4, 256), (w_in, 5, 256), (w_in, 3, 512)],
                             rms=(xin, row(W["e_norm_mix"][j]), dx))
            G["e_w_in"][j] = even_in_t(jnp.concatenate(
                [mm_tn(f"e_in_dwq_{j}", h, dq), mm_tn(f"e_in_dwk_{j}", h, dk), mm_tn(f"e_in_dwv_{j}", h, dv),
                 mm_tn(f"e_in_dwu_{j}", h, du)], axis=1))
            dx = dxin
            G["e_norm_mix"][j] = dg.reshape(-1)
        else:
            (xin, z, h, q, k, v, nq, nkv, yc, lse, xc, a0, a1, h0, h1, yd, w_in, w_out_a, w_out_b, w_uq, w_k, w_v,
             g_cq, g_ckv, w_gate, b_gate, lam, conv_w) = mix
            w_out_cat = jnp.concatenate([w_out_a, w_out_b], axis=0)
            dycat = mm_nt(f"o_out_dx_{j}", [dx], [w_out_cat])
            G["o_w_out"][j] = jnp.concatenate([unslab_rows(mm_tn(f"o_out_dwa_{j}", yc, dx), 8),
                                               mm_tn(f"o_out_dwb_{j}", yd, dx)], axis=0)
            delta, dob = mla_delta(yc, dycat)
            dqt, dk, dv = mla_bwd(q, k, v, dob, lse, delta)
            dq = mla_dq(dqt)
            dza, dqp, dgq, dgkv = mla_prep_bwd(z, g_cq, g_ckv, w_uq, w_k, w_v, dq, dk, dv, tabs_mq, tabs_mk)
            G["o_g_cq"][j], G["o_g_ckv"][j] = dgq.reshape(-1), dgkv.reshape(-1)
            G["o_w_uq"][j] = _unslab_cols(mm_tn(f"o_uq_dw_{j}", nq, dqp), 96, 8)
            G["o_w_ukv"][j] = ukv_t(mm_tn(f"o_uk_dw_{j}", nkv, dk), mm_tn(f"o_uv_dw_{j}", nkv, dv))
            dxg, dhh = lru_gate_bwd(h0, h1, z, dycat)
            g1, g0 = lru_scan(f"lru_scan_bwd_{j}", a1, dhh, a0, dhh, adjoint=True)
            dxc, dpre, dbias, dlam = lru_bwd_point(xc, h0, h1, g0, g1, w_gate, b_gate, lam)
            dwg = mm_tn(f"o_gate_dw_{j}", xc, dpre)
            G["o_lru_wa"][j] = jnp.stack([_block_diag_t(dwg[:, 0:512]), _block_diag_t(dwg[:, 1024:1536])])
            G["o_lru_wx"][j] = jnp.stack([_block_diag_t(dwg[:, 512:1024]), _block_diag_t(dwg[:, 1536:2048])])
            G["o_lru_ba"][j] = jnp.stack([dbias[0, 0:512], dbias[0, 1024:1536]])
            G["o_lru_bx"][j] = jnp.stack([dbias[0, 512:1024], dbias[0, 1536:2048]])
            G["o_lru_lambda"][j] = dlam.reshape(2, 512)
            dxr, dcw, dcb = conv_bwd(z, dxc, conv_w)
            G["o_conv_w"][j], G["o_conv_b"][j] = dcw, dcb.reshape(-1)
            dxin, dg = mm_nt(f"o_in_dx_{j}", [dza, dxr, dxg], [(w_in, 0, 512), (w_in, 1, 512), (w_in, 2, 512)],
                             rms=(xin, row(full["o_norm_mix"][j]), dx))
            G["o_w_in"][j] = odd_in_t(jnp.concatenate(
                [mm_tn(f"o_in_dwa_{j}", h, dza), mm_tn(f"o_in_dwr_{j}", h, dxr), mm_tn(f"o_in_dwg_{j}", h, dxg)], axis=1))
            dx = dxin
            G["o_norm_mix"][j] = dg.reshape(-1)
    grad_x = dx[None]
    G["final_norm"] = [G["final_norm"]]

    sharded = BIG + SMALL
    me = 4 * lax.axis_index("x") + 2 * lax.axis_index("y") + lax.axis_index("c")

    def layers(T, n):
        return [T[n]] if n == "final_norm" else [T[n][l] for l in range(T[n].shape[0])]

    def layer_shapes(names):
        return [a.shape for n in names for a in layers(W, n)]

    def shards_of(n, ax, dev):
        size = W[n].shape[ax]
        return [lax.dynamic_slice_in_dim(g, dev * size, size, axis=ax - 1) for g in G[n]]

    gs = jnp.stack([_pack([p for n, ax in sharded for p in shards_of(n, ax, me ^ r)], F32) for r in range(NDEV)])
    gr = _pack([g for n in REPL for g in G[n]] + [loss_row[0, 0:1]], F32)
    mine, others = pair_add(gs, grad_pair(gs))
    others, gr_all = grad_cross(others, gr)
    names_s = [n for n, _ in sharded]
    outs_s = adamw("adamw_sharded", [mine[None], others],
                   *[_pack([a for n in names_s for a in layers(T, n)], F32) for T in (W, Mo, Vo)])
    outs_r = adamw("adamw_replicated", [gr_all],
                   *[_pack([a for n in REPL for a in layers(T, n)] + [jnp.zeros((1,), F32)], F32) for T in (W, Mo, Vo)])
    res = [dict(), dict(), dict(), dict()]
    for kind in range(4):
        pieces = (_unpack(outs_s[kind], layer_shapes(names_s))
                  + _unpack(outs_r[kind], layer_shapes(REPL) + [(1,)]))
        for n in names_s + list(REPL):
            count = len(layers(W, n))
            got, pieces = pieces[:count], pieces[count:]
            res[kind][n] = got[0] if n == "final_norm" else jnp.stack(got)
        res[kind]["loss"] = pieces[0]
    loss = res[0]["loss"][0]
    return (loss, grad_x, *[res[0][n] for n in WEIGHTS], *[res[1][n] for n in WEIGHTS],
            *[res[2][n] for n in WEIGHTS], *[res[3][n] for n in WEIGHTS])
```
